```python
import math
import jax, jax.numpy as jnp
from jax import lax
import numpy as np

D_MODEL = 2048
BATCH = 2
SEQ = 16384
DEPTH = 1

A_HEADS = 16
A_HEAD_DIM = 128
A_PATTERNS = ((128, 1), (512, 4), (2048, 16))
A_BLOCK = 128
A_WIDTH = A_HEADS * A_HEAD_DIM
B_HEADS = 16
Q_LORA = 512
KV_LORA = 512
QK_NOPE = 128
QK_ROPE = 64
V_HEAD = 128
ROPE_THETA = 10000.0
Q_BLOCK = 128
N_KEYS = 128
N_EXPERTS = N_KEYS * N_KEYS
PEER_HEADS = 8
PEER_QDIM = 256
PEER_TOPK = 16
PEER_CHUNK = 128
LN_EPS = 1e-5
RMS_EPS = 1e-6
ALPHA = (2.0 * DEPTH) ** 0.25
BETA = (8.0 * DEPTH) ** -0.25
NEG = -1e30
IN_SIZES = (A_WIDTH, A_WIDTH, A_WIDTH, Q_LORA, KV_LORA, QK_ROPE, 2 * D_MODEL)
IN_WIDTH = sum(IN_SIZES)
IN_SPLITS = tuple(int(v) for v in np.cumsum(IN_SIZES)[:-1])

kernel_name = 'hybrid_dilated_mla_peer_deepnorm'


def layer_norm(x, g, b):
    xf = x.astype(jnp.float32)
    mu = jnp.mean(xf, axis=-1, keepdims=True)
    var = jnp.mean(jnp.square(xf - mu), axis=-1, keepdims=True)
    return ((xf - mu) * lax.rsqrt(var + LN_EPS) * g + b).astype(x.dtype)


def rms_norm(x, g):
    xf = x.astype(jnp.float32)
    ms = jnp.mean(jnp.square(xf), axis=-1, keepdims=True)
    return (xf * lax.rsqrt(ms + RMS_EPS) * g).astype(x.dtype)


def apply_rope(t, positions):
    half = t.shape[-1] // 2
    inv_freq = ROPE_THETA ** (-jnp.arange(half, dtype=jnp.float32) / half)
    ang = positions.astype(jnp.float32)[:, None] * inv_freq[None, :]
    cos = jnp.cos(ang)[:, None, :].astype(t.dtype)
    sin = jnp.sin(ang)[:, None, :].astype(t.dtype)
    t1, t2 = t[..., :half], t[..., half:]
    return jnp.concatenate([t1 * cos - t2 * sin, t1 * sin + t2 * cos], axis=-1)


def alibi_slopes(n_heads):
    return jnp.power(2.0, -8.0 * jnp.arange(1, n_heads + 1, dtype=jnp.float32) / n_heads)


def dilated_window_attention(q, k, v, window, dilation, slopes):
    B, S, H, Dh = q.shape
    steps = window // dilation
    L = S // dilation
    nb = -(-L // A_BLOCK)
    Lp = nb * A_BLOCK

    def to_sub(t):
        t = t.reshape(B, L, dilation, H, Dh).transpose(0, 2, 3, 1, 4)
        t = jnp.pad(t, ((0, 0), (0, 0), (0, 0), (0, Lp - L), (0, 0)))
        return t.reshape(B, dilation, H, nb, A_BLOCK, Dh)

    def with_prev(t):
        prev = jnp.pad(t, ((0, 0), (0, 0), (0, 0), (1, 0), (0, 0), (0, 0)))[:, :, :, :-1]
        return jnp.concatenate([prev, t], axis=4)

    qb = to_sub(q)
    kw = with_prev(to_sub(k))
    vw = with_prev(to_sub(v))
    s = jnp.einsum('bdhnqc,bdhnkc->bdhnqk', qb, kw).astype(jnp.float32) * (Dh ** -0.5)

    i = jnp.arange(A_BLOCK)[:, None]
    c = jnp.arange(2 * A_BLOCK)[None, :]
    rel = A_BLOCK + i - c
    band = (rel >= 0) & (rel <= steps)
    blk = jnp.arange(nb)[:, None, None]
    valid = band[None] & (blk * A_BLOCK + c[None] >= A_BLOCK)
    bias = -slopes[:, None, None] * (dilation * rel).astype(jnp.float32)
    logits = jnp.where(valid, s + bias[None, None, :, None], NEG)

    m = jnp.max(logits, axis=-1, keepdims=True)
    p = jnp.exp(logits - m)
    z = jnp.sum(p, axis=-1, keepdims=True)
    o = jnp.einsum('bdhnqk,bdhnkc->bdhnqc', p / z, vw).astype(q.dtype)
    lse = (m + jnp.log(z))[..., 0]

    o = o.reshape(B, dilation, H, Lp, Dh)[:, :, :, :L].transpose(0, 3, 1, 2, 4).reshape(B, S, H, Dh)
    lse = lse.reshape(B, dilation, H, Lp)[:, :, :, :L].transpose(0, 3, 1, 2).reshape(B, S, H)
    return o, lse


def mla_attention(q_nope, q_rope, k_nope, k_rope, v):
    B, S, H, _ = q_nope.shape
    nq = S // Q_BLOCK
    scale = (QK_NOPE + QK_ROPE) ** -0.5
    kpos = jnp.arange(S)

    def block(args):
        qn, qr, start = args
        s = (jnp.einsum('bqhc,bkhc->bhqk', qn, k_nope)
             + jnp.einsum('bqhr,bkr->bhqk', qr, k_rope)).astype(jnp.float32) * scale
        qpos = start + jnp.arange(Q_BLOCK)
        s = jnp.where(kpos[None, :] <= qpos[:, None], s, NEG)
        p = jax.nn.softmax(s, axis=-1)
        return jnp.einsum('bhqk,bkhc->bqhc', p, v).astype(v.dtype)

    qn_b = q_nope.reshape(B, nq, Q_BLOCK, H, QK_NOPE).transpose(1, 0, 2, 3, 4)
    qr_b = q_rope.reshape(B, nq, Q_BLOCK, H, QK_ROPE).transpose(1, 0, 2, 3, 4)
    starts = jnp.arange(nq, dtype=jnp.int32) * Q_BLOCK
    out = lax.map(block, (qn_b, qr_b, starts))
    return out.transpose(1, 0, 2, 3, 4).reshape(B, S, H, V_HEAD)


def peer(x, w_query, sub_keys_1, sub_keys_2, expert_down, expert_up):
    B, S, D = x.shape
    q = (x @ w_query).reshape(B, S, PEER_HEADS, 2, PEER_QDIM // 2)
    s1 = jnp.einsum('bshc,kc->bshk', q[..., 0, :], sub_keys_1).astype(jnp.float32)
    s2 = jnp.einsum('bshc,kc->bshk', q[..., 1, :], sub_keys_2).astype(jnp.float32)
    v1, i1 = lax.top_k(s1, PEER_TOPK)
    v2, i2 = lax.top_k(s2, PEER_TOPK)
    cand_s = (v1[..., :, None] + v2[..., None, :]).reshape(B, S, PEER_HEADS, PEER_TOPK * PEER_TOPK)
    cand_i = (i1[..., :, None] * N_KEYS + i2[..., None, :]).reshape(B, S, PEER_HEADS, PEER_TOPK * PEER_TOPK)
    top_s, pos = lax.top_k(cand_s, PEER_TOPK)
    expert_idx = jnp.take_along_axis(cand_i, pos, axis=-1)
    gates = jax.nn.softmax(top_s, axis=-1)

    T = B * S
    nc = T // PEER_CHUNK
    E = PEER_HEADS * PEER_TOPK
    xc = x.reshape(nc, PEER_CHUNK, D)
    ec = expert_idx.reshape(nc, PEER_CHUNK, E)
    gc = gates.reshape(nc, PEER_CHUNK, E).astype(x.dtype)

    def chunk(args):
        xt, et, gt = args
        u = expert_down[et]
        h = jax.nn.gelu(jnp.einsum('cd,ced->ce', xt, u), approximate=False)
        vv = expert_up[et]
        return jnp.einsum('ce,ced->cd', gt * h, vv)

    y = lax.map(chunk, (xc, ec, gc))
    return y.reshape(B, S, D)


def setup_inputs(seed: int = 0) -> dict:
    key = jax.random.key(seed)
    ks = jax.random.split(key, 24)
    f32 = jnp.float32

    def normal(k, shape, scale):
        return jax.random.normal(k, shape, f32) * scale

    x = normal(ks[0], (BATCH, SEQ, D_MODEL), 1.0)
    in_scale = jnp.concatenate([jnp.ones((2 * A_WIDTH,), f32), jnp.full((A_WIDTH,), BETA, f32),
                                jnp.ones((IN_WIDTH - 3 * A_WIDTH,), f32)])
    w_in = normal(ks[1], (DEPTH, D_MODEL, IN_WIDTH), D_MODEL ** -0.5) * in_scale
    b_gates = normal(ks[2], (DEPTH, 2 * D_MODEL), 0.02)
    a_w_out = normal(ks[3], (DEPTH, A_WIDTH, D_MODEL), A_WIDTH ** -0.5)
    mla_q_norm = 1.0 + normal(ks[4], (DEPTH, Q_LORA), 0.02)
    mla_w_uq = normal(ks[5], (DEPTH, Q_LORA, B_HEADS * (QK_NOPE + QK_ROPE)), Q_LORA ** -0.5)
    mla_kv_norm = 1.0 + normal(ks[6], (DEPTH, KV_LORA), 0.02)
    ukv_scale = jnp.concatenate([jnp.ones((QK_NOPE,), f32), jnp.full((V_HEAD,), BETA, f32)])
    mla_w_ukv = (normal(ks[7], (DEPTH, KV_LORA, B_HEADS, QK_NOPE + V_HEAD), KV_LORA ** -0.5)
                 * ukv_scale).reshape(DEPTH, KV_LORA, B_HEADS * (QK_NOPE + V_HEAD))
    mla_w_out = normal(ks[8], (DEPTH, B_HEADS * V_HEAD, D_MODEL), (B_HEADS * V_HEAD) ** -0.5)
    w_out = normal(ks[9], (DEPTH, D_MODEL, D_MODEL), BETA * D_MODEL ** -0.5)
    ln1_g = 1.0 + normal(ks[10], (DEPTH, D_MODEL), 0.02)
    ln1_b = normal(ks[11], (DEPTH, D_MODEL), 0.02)
    peer_w_query = normal(ks[12], (DEPTH, D_MODEL, PEER_HEADS * PEER_QDIM), D_MODEL ** -0.5)
    peer_sub_keys_1 = normal(ks[13], (DEPTH, N_KEYS, PEER_QDIM // 2), (PEER_QDIM // 2) ** -0.5)
    peer_sub_keys_2 = normal(ks[14], (DEPTH, N_KEYS, PEER_QDIM // 2), (PEER_QDIM // 2) ** -0.5)
    peer_expert_down = normal(ks[15], (DEPTH, N_EXPERTS, D_MODEL), D_MODEL ** -0.5)
    peer_expert_up = normal(ks[16], (DEPTH, N_EXPERTS, D_MODEL), BETA)
    ln2_g = 1.0 + normal(ks[17], (DEPTH, D_MODEL), 0.02)
    ln2_b = normal(ks[18], (DEPTH, D_MODEL), 0.02)
    return {'x': x, 'w_in': w_in, 'b_gates': b_gates, 'a_w_out': a_w_out,
            'mla_q_norm': mla_q_norm, 'mla_w_uq': mla_w_uq, 'mla_kv_norm': mla_kv_norm,
            'mla_w_ukv': mla_w_ukv, 'mla_w_out': mla_w_out, 'w_out': w_out,
            'ln1_g': ln1_g, 'ln1_b': ln1_b, 'peer_w_query': peer_w_query,
            'peer_sub_keys_1': peer_sub_keys_1, 'peer_sub_keys_2': peer_sub_keys_2,
            'peer_expert_down': peer_expert_down, 'peer_expert_up': peer_expert_up,
            'ln2_g': ln2_g, 'ln2_b': ln2_b}


def reference(x, w_in, b_gates, a_w_out, mla_q_norm, mla_w_uq, mla_kv_norm, mla_w_ukv,
              mla_w_out, w_out, ln1_g, ln1_b, peer_w_query, peer_sub_keys_1, peer_sub_keys_2,
              peer_expert_down, peer_expert_up, ln2_g, ln2_b):
    B, S, D = x.shape
    positions = jnp.arange(S, dtype=jnp.int32)
    slopes = alibi_slopes(A_HEADS)
    h = x
    for l in range(DEPTH):
        proj = h @ w_in[l]
        qa, ka, va, cq, ckv, kr, gate_pre = jnp.split(proj, IN_SPLITS, axis=-1)
        gates = jax.nn.sigmoid(gate_pre + b_gates[l])
        g_a, g_b = gates[..., :D_MODEL], gates[..., D_MODEL:]

        qa = qa.reshape(B, S, A_HEADS, A_HEAD_DIM)
        ka = ka.reshape(B, S, A_HEADS, A_HEAD_DIM)
        va = va.reshape(B, S, A_HEADS, A_HEAD_DIM)
        outs, lses = [], []
        for window, dilation in A_PATTERNS:
            o_g, lse_g = dilated_window_attention(qa, ka, va, window, dilation, slopes)
            outs.append(o_g)
            lses.append(lse_g)
        w_pat = jax.nn.softmax(jnp.stack(lses, axis=0), axis=0).astype(h.dtype)
        y_a = jnp.einsum('gbsh,gbshc->bshc', w_pat, jnp.stack(outs, axis=0)).reshape(B, S, A_WIDTH)
        y_a = y_a @ a_w_out[l]

        cq = rms_norm(cq, mla_q_norm[l])
        q_b = (cq @ mla_w_uq[l]).reshape(B, S, B_HEADS, QK_NOPE + QK_ROPE)
        q_nope = q_b[..., :QK_NOPE]
        q_rope = apply_rope(q_b[..., QK_NOPE:], positions)
        ckv = rms_norm(ckv, mla_kv_norm[l])
        kv = (ckv @ mla_w_ukv[l]).reshape(B, S, B_HEADS, QK_NOPE + V_HEAD)
        k_nope, v_b = kv[..., :QK_NOPE], kv[..., QK_NOPE:]
        k_rope = apply_rope(kr[:, :, None, :], positions)[:, :, 0]
        y_b = mla_attention(q_nope, q_rope, k_nope, k_rope, v_b).reshape(B, S, B_HEADS * V_HEAD)
        y_b = y_b @ mla_w_out[l]

        mix = (g_a * y_a + g_b * y_b) @ w_out[l]
        h = layer_norm(ALPHA * h + mix, ln1_g[l], ln1_b[l])

        y_p = peer(h, peer_w_query[l], peer_sub_keys_1[l], peer_sub_keys_2[l],
                   peer_expert_down[l], peer_expert_up[l])
        h = layer_norm(ALPHA * h + y_p, ln2_g[l], ln2_b[l])
    return h
```

```python
import functools
import math

import jax
import jax.numpy as jnp
from jax import lax
from jax.experimental import pallas as pl
from jax.experimental.pallas import tpu as pltpu

F32 = jnp.float32
BF16 = jnp.bfloat16

D_MODEL = 2048
A_HEADS = 16
A_HEAD_DIM = 128
A_PATTERNS = ((128, 1), (512, 4), (2048, 16))
A_BLOCK = 128
A_WIDTH = A_HEADS * A_HEAD_DIM
B_HEADS = 16
Q_LORA = 512
KV_LORA = 512
QK_NOPE = 128
QK_ROPE = 64
V_HEAD = 128
ROPE_THETA = 10000.0
N_KEYS = 128
PEER_HEADS = 8
PEER_QDIM = 256
PEER_TOPK = 16
N_EXPERTS = N_KEYS * N_KEYS
N_SLOTS = PEER_HEADS * PEER_TOPK
LN_EPS = 1e-5
RMS_EPS = 1e-6
DEPTH = 1
ALPHA = (2.0 * DEPTH) ** 0.25
NEG = -1e30

LANE = 128
MLA_HEAD_PAD = 256
VMEM_LIMIT = 56 * 1024 * 1024

COL_Q, COL_K, COL_V = 0, A_WIDTH, 2 * A_WIDTH
COL_GA = 3 * A_WIDTH
COL_GB = COL_GA + D_MODEL
COL_CQ = COL_GB + D_MODEL
COL_CKV = COL_CQ + Q_LORA
COL_KR = COL_CKV + KV_LORA
PROJ_WIDTH = COL_CQ + 2048

_NT = (((1,), (1,)), ((), ()))


def _cparams(*sem):
    return pltpu.CompilerParams(dimension_semantics=sem, vmem_limit_bytes=VMEM_LIMIT)


def _layer_norm(z, g, b):
    mu = jnp.mean(z, axis=-1, keepdims=True)
    zc = z - mu
    var = jnp.mean(zc * zc, axis=-1, keepdims=True)
    return zc * lax.rsqrt(var + LN_EPS) * g + b


def _inproj_kernel(x_ref, w_ref, b_ref, o_ref, *, gate_lo, gate_hi):
    j = pl.program_id(1)
    acc = jnp.dot(x_ref[...].astype(BF16), w_ref[...], preferred_element_type=F32)
    is_gate = jnp.logical_and(j >= gate_lo, j < gate_hi)

    @pl.when(is_gate)
    def _():
        o_ref[...] = jax.nn.sigmoid(acc + b_ref[...]).astype(o_ref.dtype)

    @pl.when(jnp.logical_not(is_gate))
    def _():
        o_ref[...] = acc.astype(o_ref.dtype)


def _in_projection(x2, w_all, b_all):
    T = x2.shape[0]
    tm, tn = min(1024, T), 1024
    kern = functools.partial(_inproj_kernel, gate_lo=COL_GA // tn, gate_hi=COL_CQ // tn)
    return pl.pallas_call(
        kern,
        out_shape=jax.ShapeDtypeStruct((T, PROJ_WIDTH), BF16),
        grid=(T // tm, PROJ_WIDTH // tn),
        in_specs=[
            pl.BlockSpec((tm, D_MODEL), lambda i, j: (i, 0)),
            pl.BlockSpec((D_MODEL, tn), lambda i, j: (0, j)),
            pl.BlockSpec((1, tn), lambda i, j: (0, j)),
        ],
        out_specs=pl.BlockSpec((tm, tn), lambda i, j: (i, j)),
        compiler_params=_cparams("parallel", "arbitrary"),
        name="in_projection",
    )(x2, w_all, b_all)


def _dilated_kernel(q_ref, kp_ref, ko_ref, vp_ref, vo_ref, o_ref, lse_ref, *, dilation, steps):
    n = pl.program_id(2)
    blk = A_BLOCK
    i = lax.broadcasted_iota(jnp.int32, (blk, 2 * blk), 0)
    c = lax.broadcasted_iota(jnp.int32, (blk, 2 * blk), 1)
    rel = blk + i - c
    valid = (rel >= 0) & (rel <= steps) & ((c >= blk) | (n > 0))
    dist = (dilation * rel).astype(F32)
    lane = lax.broadcasted_iota(jnp.int32, (blk, LANE), 1)
    lse_all = jnp.zeros((blk, LANE), F32)
    for h in range(A_HEADS):
        sl = slice(h * A_HEAD_DIM, (h + 1) * A_HEAD_DIM)
        q = q_ref[:, sl]
        k = jnp.concatenate([kp_ref[:, sl], ko_ref[:, sl]], axis=0)
        v = jnp.concatenate([vp_ref[:, sl], vo_ref[:, sl]], axis=0)
        s = lax.dot_general(q, k, _NT, preferred_element_type=F32) * (A_HEAD_DIM ** -0.5)
        slope = 2.0 ** (-8.0 * (h + 1) / A_HEADS)
        logits = jnp.where(valid, s - slope * dist, NEG)
        m = jnp.max(logits, axis=-1, keepdims=True)
        p = jnp.exp(logits - m)
        z = jnp.sum(p, axis=-1, keepdims=True)
        o = jnp.dot(p.astype(BF16), v, preferred_element_type=F32) / z
        o_ref[:, sl] = o.astype(o_ref.dtype)
        lse_all = jnp.where(lane == h, m + jnp.log(z), lse_all)
    lse_ref[...] = lse_all


def _dilated_attention(qkv_r, window, dilation):
    B, d, L, _ = qkv_r.shape
    assert d == dilation and L % A_BLOCK == 0
    nb = L // A_BLOCK
    blk = (None, None, A_BLOCK, A_WIDTH)
    kern = functools.partial(_dilated_kernel, dilation=dilation, steps=window // dilation)
    return pl.pallas_call(
        kern,
        out_shape=(
            jax.ShapeDtypeStruct((B, d, L, A_WIDTH), BF16),
            jax.ShapeDtypeStruct((B, d, L, LANE), F32),
        ),
        grid=(B, d, nb),
        in_specs=[
            pl.BlockSpec(blk, lambda b, r, n: (b, r, n, 0)),
            pl.BlockSpec(blk, lambda b, r, n: (b, r, jnp.maximum(n - 1, 0), 1)),
            pl.BlockSpec(blk, lambda b, r, n: (b, r, n, 1)),
            pl.BlockSpec(blk, lambda b, r, n: (b, r, jnp.maximum(n - 1, 0), 2)),
            pl.BlockSpec(blk, lambda b, r, n: (b, r, n, 2)),
        ],
        out_specs=(
            pl.BlockSpec(blk, lambda b, r, n: (b, r, n, 0)),
            pl.BlockSpec((None, None, A_BLOCK, LANE), lambda b, r, n: (b, r, n, 0)),
        ),
        compiler_params=_cparams("parallel", "parallel", "arbitrary"),
        name=f"dilated_attention_d{dilation}",
    )(qkv_r, qkv_r, qkv_r, qkv_r, qkv_r)


def _combine_kernel(o1_ref, o2_ref, o3_ref, l1_ref, l2_ref, l3_ref, y_ref):
    a, b, c = l1_ref[...], l2_ref[...], l3_ref[...]
    m = jnp.maximum(jnp.maximum(a, b), c)
    ea, eb, ec = jnp.exp(a - m), jnp.exp(b - m), jnp.exp(c - m)
    inv = 1.0 / (ea + eb + ec)
    wa, wb, wc = ea * inv, eb * inv, ec * inv
    for h in range(A_HEADS):
        sl = slice(h * A_HEAD_DIM, (h + 1) * A_HEAD_DIM)
        y = (wa[:, h:h + 1] * o1_ref[:, sl].astype(F32)
             + wb[:, h:h + 1] * o2_ref[:, sl].astype(F32)
             + wc[:, h:h + 1] * o3_ref[:, sl].astype(F32))
        y_ref[:, sl] = y.astype(y_ref.dtype)


def _combine_patterns(outs, lses):
    T = outs[0].shape[0]
    tm = min(512, T)
    ospec = pl.BlockSpec((tm, A_WIDTH), lambda i: (i, 0))
    lspec = pl.BlockSpec((tm, LANE), lambda i: (i, 0))
    return pl.pallas_call(
        _combine_kernel,
        out_shape=jax.ShapeDtypeStruct((T, A_WIDTH), BF16),
        grid=(T // tm,),
        in_specs=[ospec, ospec, ospec, lspec, lspec, lspec],
        out_specs=ospec,
        compiler_params=_cparams("parallel"),
        name="combine_patterns",
    )(*outs, *lses)


def _rope_lanes(t, cos, sin):
    lane = lax.broadcasted_iota(jnp.int32, t.shape, 1)
    half = QK_ROPE // 2
    rot = jnp.where(lane < half, pltpu.roll(t, LANE - half, 1), pltpu.roll(t, half, 1))
    return t * cos + rot * sin


def _rms_norm(x, g):
    ms = jnp.mean(x * x, axis=-1, keepdims=True)
    return x * lax.rsqrt(ms + RMS_EPS) * g


def _latent_kernel(cq_ref, ckv_ref, kr_ref, gq_ref, gkv_ref, cos_ref, sin_ref, cqn_ref, ckvn_ref, krope_ref):
    cqn_ref[...] = _rms_norm(cq_ref[...].astype(F32), gq_ref[...]).astype(cqn_ref.dtype)
    ckvn_ref[...] = _rms_norm(ckv_ref[...].astype(F32), gkv_ref[...]).astype(ckvn_ref.dtype)
    krope_ref[...] = _rope_lanes(kr_ref[...].astype(F32), cos_ref[...], sin_ref[...]).astype(krope_ref.dtype)


def _latent_prep(proj, gq, gkv, cos_t, sin_t, S):
    T = proj.shape[0]
    tm = min(512, S)
    ns = S // tm
    return pl.pallas_call(
        _latent_kernel,
        out_shape=(
            jax.ShapeDtypeStruct((T, Q_LORA), BF16),
            jax.ShapeDtypeStruct((T, KV_LORA), BF16),
            jax.ShapeDtypeStruct((T, LANE), BF16),
        ),
        grid=(T // tm,),
        in_specs=[
            pl.BlockSpec((tm, Q_LORA), lambda i: (i, COL_CQ // Q_LORA)),
            pl.BlockSpec((tm, KV_LORA), lambda i: (i, COL_CKV // KV_LORA)),
            pl.BlockSpec((tm, LANE), lambda i: (i, COL_KR // LANE)),
            pl.BlockSpec((1, Q_LORA), lambda i: (0, 0)),
            pl.BlockSpec((1, KV_LORA), lambda i: (0, 0)),
            pl.BlockSpec((tm, LANE), lambda i: (i % ns, 0)),
            pl.BlockSpec((tm, LANE), lambda i: (i % ns, 0)),
        ],
        out_specs=(
            pl.BlockSpec((tm, Q_LORA), lambda i: (i, 0)),
            pl.BlockSpec((tm, KV_LORA), lambda i: (i, 0)),
            pl.BlockSpec((tm, LANE), lambda i: (i, 0)),
        ),
        compiler_params=_cparams("parallel"),
        name="latent_prep",
    )(proj, proj, proj, gq, gkv, cos_t, sin_t)


def _qup_kernel(c_ref, w_ref, cos_ref, sin_ref, o_ref, *, scale):
    acc = jnp.dot(c_ref[...], w_ref[...], preferred_element_type=F32) * scale
    cos, sin = cos_ref[...], sin_ref[...]
    for hb in range(acc.shape[1] // MLA_HEAD_PAD):
        lo = hb * MLA_HEAD_PAD
        o_ref[:, lo:lo + QK_NOPE] = acc[:, lo:lo + QK_NOPE].astype(o_ref.dtype)
        r = _rope_lanes(acc[:, lo + QK_NOPE:lo + MLA_HEAD_PAD], cos, sin)
        o_ref[:, lo + QK_NOPE:lo + MLA_HEAD_PAD] = r.astype(o_ref.dtype)


def _q_up(cqn, w_uq_p, cos_t, sin_t, S):
    T = cqn.shape[0]
    N = w_uq_p.shape[1]
    tm, tn = min(512, S), 1024
    ns = S // tm
    kern = functools.partial(_qup_kernel, scale=(QK_NOPE + QK_ROPE) ** -0.5)
    return pl.pallas_call(
        kern,
        out_shape=jax.ShapeDtypeStruct((T, N), BF16),
        grid=(T // tm, N // tn),
        in_specs=[
            pl.BlockSpec((tm, Q_LORA), lambda i, j: (i, 0)),
            pl.BlockSpec((Q_LORA, tn), lambda i, j: (0, j)),
            pl.BlockSpec((tm, LANE), lambda i, j: (i % ns, 0)),
            pl.BlockSpec((tm, LANE), lambda i, j: (i % ns, 0)),
        ],
        out_specs=pl.BlockSpec((tm, tn), lambda i, j: (i, j)),
        compiler_params=_cparams("parallel", "arbitrary"),
        name="mla_q_up",
    )(cqn, w_uq_p, cos_t, sin_t)


def _mm_kernel(a_ref, w_ref, o_ref):
    o_ref[...] = jnp.dot(a_ref[...], w_ref[...], preferred_element_type=F32).astype(o_ref.dtype)


def _matmul(a, w, name, tm=512, tn=1024):
    M, K = a.shape
    N = w.shape[1]
    tm, tn = min(tm, M), min(tn, N)
    return pl.pallas_call(
        _mm_kernel,
        out_shape=jax.ShapeDtypeStruct((M, N), BF16),
        grid=(M // tm, N // tn),
        in_specs=[
            pl.BlockSpec((tm, K), lambda i, j: (i, 0)),
            pl.BlockSpec((K, tn), lambda i, j: (0, j)),
        ],
        out_specs=pl.BlockSpec((tm, tn), lambda i, j: (i, j)),
        compiler_params=_cparams("parallel", "arbitrary"),
        name=name,
    )(a, w)


def _mla_kernel(q_ref, kv_ref, kr_ref, o_ref, m_ref, l_ref, acc_ref, *, tq):
    qi = pl.program_id(2)
    q = q_ref[...]
    m_ref[...] = jnp.full(m_ref.shape, NEG, F32)
    l_ref[...] = jnp.zeros(l_ref.shape, F32)
    acc_ref[...] = jnp.zeros(acc_ref.shape, F32)

    def step(kc, masked):
        start = pl.multiple_of(kc * tq, tq)
        k = jnp.concatenate([kv_ref[pl.ds(start, tq), :QK_NOPE], kr_ref[pl.ds(start, tq), :]], axis=1)
        v = kv_ref[pl.ds(start, tq), QK_NOPE:]
        s = lax.dot_general(q, k, _NT, preferred_element_type=F32)
        if masked:
            row = lax.broadcasted_iota(jnp.int32, s.shape, 0)
            col = lax.broadcasted_iota(jnp.int32, s.shape, 1)
            s = jnp.where(col <= row, s, NEG)
        m_prev = m_ref[...]
        m_new = jnp.maximum(m_prev, jnp.max(s, axis=-1, keepdims=True))
        a = jnp.exp(m_prev - m_new)
        p = jnp.exp(s - m_new)
        l_ref[...] = a * l_ref[...] + jnp.sum(p, axis=-1, keepdims=True)
        acc_ref[...] = a * acc_ref[...] + jnp.dot(p.astype(BF16), v, preferred_element_type=F32)
        m_ref[...] = m_new

    def body(kc, carry):
        step(kc, False)
        return carry

    lax.fori_loop(0, qi, body, 0)
    step(qi, True)
    o_ref[...] = (acc_ref[...] / l_ref[...]).astype(o_ref.dtype)


def _mla_attention(q, kv, krope, B, S):
    T = q.shape[0]
    tq = min(512, S)
    nq = S // tq
    kern = functools.partial(_mla_kernel, tq=tq)
    return pl.pallas_call(
        kern,
        out_shape=jax.ShapeDtypeStruct((T, B_HEADS * V_HEAD), BF16),
        grid=(B, B_HEADS, nq),
        in_specs=[
            pl.BlockSpec((tq, MLA_HEAD_PAD), lambda b, h, i: (b * nq + i, h)),
            pl.BlockSpec((S, MLA_HEAD_PAD), lambda b, h, i: (b, h)),
            pl.BlockSpec((S, LANE), lambda b, h, i: (b, 0)),
        ],
        out_specs=pl.BlockSpec((tq, V_HEAD), lambda b, h, i: (b * nq + i, h)),
        scratch_shapes=[
            pltpu.VMEM((tq, 1), F32),
            pltpu.VMEM((tq, 1), F32),
            pltpu.VMEM((tq, V_HEAD), F32),
        ],
        compiler_params=_cparams("parallel", "parallel", "arbitrary"),
        name="mla_attention",
    )(q, kv, krope)


def _branch_kernel(ya_ref, yb_ref, wa_ref, wb_ref, ga_ref, gb_ref, o_ref):
    pa = jnp.dot(ya_ref[...], wa_ref[...], preferred_element_type=F32)
    pb = jnp.dot(yb_ref[...], wb_ref[...], preferred_element_type=F32)
    u = ga_ref[...].astype(F32) * pa + gb_ref[...].astype(F32) * pb
    o_ref[...] = u.astype(o_ref.dtype)


def _branch_mix(ya, yb, wa, wb, proj):
    T = ya.shape[0]
    tm, tn = min(512, T), 1024
    return pl.pallas_call(
        _branch_kernel,
        out_shape=jax.ShapeDtypeStruct((T, D_MODEL), BF16),
        grid=(T // tm, D_MODEL // tn),
        in_specs=[
            pl.BlockSpec((tm, A_WIDTH), lambda i, j: (i, 0)),
            pl.BlockSpec((tm, B_HEADS * V_HEAD), lambda i, j: (i, 0)),
            pl.BlockSpec((A_WIDTH, tn), lambda i, j: (0, j)),
            pl.BlockSpec((B_HEADS * V_HEAD, tn), lambda i, j: (0, j)),
            pl.BlockSpec((tm, tn), lambda i, j: (i, COL_GA // tn + j)),
            pl.BlockSpec((tm, tn), lambda i, j: (i, COL_GB // tn + j)),
        ],
        out_specs=pl.BlockSpec((tm, tn), lambda i, j: (i, j)),
        compiler_params=_cparams("parallel", "arbitrary"),
        name="branch_mix",
    )(ya, yb, wa, wb, proj, proj)


def _outln_kernel(u_ref, w_ref, x_ref, g_ref, b_ref, h_ref, hb_ref):
    mix = jnp.dot(u_ref[...], w_ref[...], preferred_element_type=F32)
    h = _layer_norm(ALPHA * x_ref[...] + mix, g_ref[...], b_ref[...])
    h_ref[...] = h
    hb_ref[...] = h.astype(hb_ref.dtype)


def _out_projection_ln(u, w_out, x2, g, b):
    T = u.shape[0]
    tm = min(256, T)
    row = pl.BlockSpec((tm, D_MODEL), lambda i: (i, 0))
    vec = pl.BlockSpec((1, D_MODEL), lambda i: (0, 0))
    return pl.pallas_call(
        _outln_kernel,
        out_shape=(
            jax.ShapeDtypeStruct((T, D_MODEL), F32),
            jax.ShapeDtypeStruct((T, D_MODEL), BF16),
        ),
        grid=(T // tm,),
        in_specs=[row, pl.BlockSpec((D_MODEL, D_MODEL), lambda i: (0, 0)), row, vec, vec],
        out_specs=(row, row),
        compiler_params=_cparams("parallel"),
        name="out_projection_ln",
    )(u, w_out, x2, g, b)


def _topk_axis0(s, k):
    n = s.shape[0]
    iota = lax.broadcasted_iota(jnp.int32, s.shape, 0)
    vals, idxs = [], []
    for _ in range(k):
        m = jnp.max(s, axis=0, keepdims=True)
        idx = jnp.min(jnp.where(s == m, iota, n), axis=0, keepdims=True)
        vals.append(m)
        idxs.append(idx)
        s = jnp.where(iota == idx, -jnp.inf, s)
    return vals, idxs


def _select_rows(rows, sel):
    out = jnp.zeros_like(rows[0])
    for a, r in enumerate(rows):
        out = jnp.where(sel == a, r, out)
    return out


def _route_kernel(q_ref, k1_ref, k2_ref, i1_ref, i2_ref, g_ref):
    half = PEER_QDIM // 2
    i1_all, i2_all, g_all = [], [], []
    for h in range(PEER_HEADS):
        q1 = q_ref[:, h * PEER_QDIM:h * PEER_QDIM + half]
        q2 = q_ref[:, h * PEER_QDIM + half:(h + 1) * PEER_QDIM]
        s1 = lax.dot_general(k1_ref[...], q1, _NT, preferred_element_type=F32)
        s2 = lax.dot_general(k2_ref[...], q2, _NT, preferred_element_type=F32)
        v1, i1 = _topk_axis0(s1, PEER_TOPK)
        v2, i2 = _topk_axis0(s2, PEER_TOPK)
        v2m = jnp.concatenate(v2, axis=0)
        cand = jnp.concatenate([v1[a] + v2m for a in range(PEER_TOPK)], axis=0)
        ts, pos = _topk_axis0(cand, PEER_TOPK)
        top = jnp.concatenate(ts, axis=0)
        e = jnp.exp(top - ts[0])
        g_all.append(e / jnp.sum(e, axis=0, keepdims=True))
        for p in pos:
            i1_all.append(_select_rows(i1, p // PEER_TOPK))
            i2_all.append(_select_rows(i2, p % PEER_TOPK))
    i1_ref[...] = jnp.concatenate(i1_all, axis=0).T
    i2_ref[...] = jnp.concatenate(i2_all, axis=0).T
    g_ref[...] = jnp.concatenate(g_all, axis=0).T


def _peer_route(qp, k1, k2):
    T = qp.shape[0]
    tm = min(256, T)
    slot = pl.BlockSpec((tm, N_SLOTS), lambda i: (i, 0))
    keys = pl.BlockSpec((N_KEYS, PEER_QDIM // 2), lambda i: (0, 0))
    return pl.pallas_call(
        _route_kernel,
        out_shape=(
            jax.ShapeDtypeStruct((T, N_SLOTS), jnp.int32),
            jax.ShapeDtypeStruct((T, N_SLOTS), jnp.int32),
            jax.ShapeDtypeStruct((T, N_SLOTS), F32),
        ),
        grid=(T // tm,),
        in_specs=[pl.BlockSpec((tm, PEER_HEADS * PEER_QDIM), lambda i: (i, 0)), keys, keys],
        out_specs=(slot, slot, slot),
        compiler_params=_cparams("parallel"),
        name="peer_route",
    )(qp, k1, k2)


def _gate_matrix_kernel(i1_ref, i2_ref, g_ref, w_ref):
    key = lax.broadcasted_iota(jnp.int32, (N_KEYS, N_SLOTS), 0)

    def body(t, carry):
        r1 = i1_ref[pl.ds(t, 1), :]
        r2 = i2_ref[pl.ds(t, 1), :]
        g = g_ref[pl.ds(t, 1), :]
        a = jnp.where(key == r1, 1.0, 0.0).astype(BF16)
        b = jnp.where(key == r2, g, 0.0).astype(BF16)
        w_ref[t] = lax.dot_general(a, b, _NT, preferred_element_type=F32).astype(w_ref.dtype)
        return carry

    lax.fori_loop(0, w_ref.shape[0], body, 0)


def _gate_matrix(i1, i2, g):
    T = i1.shape[0]
    tb = min(128, T)
    slot = pl.BlockSpec((tb, N_SLOTS), lambda i: (i, 0))
    return pl.pallas_call(
        _gate_matrix_kernel,
        out_shape=jax.ShapeDtypeStruct((T, N_KEYS, N_KEYS), BF16),
        grid=(T // tb,),
        in_specs=[slot, slot, slot],
        out_specs=pl.BlockSpec((tb, N_KEYS, N_KEYS), lambda i: (i, 0, 0)),
        compiler_params=_cparams("parallel"),
        name="peer_gate_matrix",
    )(i1, i2, g)


def _experts_kernel(hb_ref, dn_ref, up_ref, w_ref, h_ref, g_ref, b_ref, o_ref, acc_ref):
    j = pl.program_id(1)

    @pl.when(j == 0)
    def _():
        acc_ref[...] = jnp.zeros(acc_ref.shape, F32)

    pre = jnp.dot(hb_ref[...], dn_ref[...], preferred_element_type=F32)
    act = 0.5 * pre * (1.0 + lax.erf(pre * (2.0 ** -0.5)))
    act = act * w_ref[...].astype(F32)
    acc_ref[...] += jnp.dot(act.astype(BF16), up_ref[...], preferred_element_type=F32)

    @pl.when(j == pl.num_programs(1) - 1)
    def _():
        o_ref[...] = _layer_norm(ALPHA * h_ref[...] + acc_ref[...], g_ref[...], b_ref[...])


def _peer_experts(hb, down_t, up, w2, h, g, b):
    T = hb.shape[0]
    tm, te = min(512, T), 1024
    row = pl.BlockSpec((tm, D_MODEL), lambda i, j: (i, 0))
    vec = pl.BlockSpec((1, D_MODEL), lambda i, j: (0, 0))
    return pl.pallas_call(
        _experts_kernel,
        out_shape=jax.ShapeDtypeStruct((T, D_MODEL), F32),
        grid=(T // tm, N_EXPERTS // te),
        in_specs=[
            row,
            pl.BlockSpec((D_MODEL, te), lambda i, j: (0, j)),
            pl.BlockSpec((te, D_MODEL), lambda i, j: (j, 0)),
            pl.BlockSpec((tm, te), lambda i, j: (i, j)),
            row, vec, vec,
        ],
        out_specs=row,
        scratch_shapes=[pltpu.VMEM((tm, D_MODEL), F32)],
        compiler_params=_cparams("parallel", "arbitrary"),
        name="peer_experts",
    )(hb, down_t, up, w2, h, g, b)


def _rope_tables(S):
    half = QK_ROPE // 2
    inv_freq = ROPE_THETA ** (-jnp.arange(half, dtype=F32) / half)
    ang = jnp.arange(S, dtype=jnp.int32).astype(F32)[:, None] * inv_freq[None, :]
    cos, sin = jnp.cos(ang), jnp.sin(ang)
    zeros = jnp.zeros((S, LANE - QK_ROPE), F32)
    return (jnp.concatenate([cos, cos, zeros], axis=1), jnp.concatenate([-sin, sin, zeros], axis=1))


def _pack_input_weights(w_in, b_gates):
    wq, wk, wv, wcq, wckv, wkr, wg = jnp.split(
        w_in, (A_WIDTH, 2 * A_WIDTH, 3 * A_WIDTH, 3 * A_WIDTH + Q_LORA,
               3 * A_WIDTH + Q_LORA + KV_LORA, 3 * A_WIDTH + Q_LORA + KV_LORA + QK_ROPE), axis=1)
    pad = jnp.zeros((D_MODEL, PROJ_WIDTH - COL_KR - QK_ROPE), w_in.dtype)
    w_all = jnp.concatenate([wq, wk, wv, wg, wcq, wckv, wkr, pad], axis=1).astype(BF16)
    b_all = jnp.zeros((1, PROJ_WIDTH), F32).at[0, COL_GA:COL_CQ].set(b_gates)
    return w_all, b_all


def _pack_uq(w_uq):
    w = w_uq.reshape(Q_LORA, B_HEADS, QK_NOPE + QK_ROPE)
    w = jnp.pad(w, ((0, 0), (0, 0), (0, MLA_HEAD_PAD - QK_NOPE - QK_ROPE)))
    return w.reshape(Q_LORA, B_HEADS * MLA_HEAD_PAD).astype(BF16)


def kernel(x, w_in, b_gates, a_w_out, mla_q_norm, mla_w_uq, mla_kv_norm, mla_w_ukv, mla_w_out, w_out,
           ln1_g, ln1_b, peer_w_query, peer_sub_keys_1, peer_sub_keys_2, peer_expert_down,
           peer_expert_up, ln2_g, ln2_b):
    B, S, D = x.shape
    assert D == D_MODEL and w_in.shape[0] == DEPTH
    T = B * S
    cos_t, sin_t = _rope_tables(S)
    h = x.reshape(T, D)
    for l in range(DEPTH):
        w_all, b_all = _pack_input_weights(w_in[l], b_gates[l])
        proj = _in_projection(h, w_all, b_all)

        qkv = proj[:, :3 * A_WIDTH].reshape(B, S, 3 * A_WIDTH)
        outs, lses = [], []
        for window, d in A_PATTERNS:
            L = S // d
            qkv_r = qkv.reshape(B, L, d, 3 * A_WIDTH).transpose(0, 2, 1, 3)
            o_r, lse_r = _dilated_attention(qkv_r, window, d)
            outs.append(o_r.transpose(0, 2, 1, 3).reshape(T, A_WIDTH))
            lses.append(lse_r.transpose(0, 2, 1, 3).reshape(T, LANE))
        ya = _combine_patterns(outs, lses)

        cqn, ckvn, krope = _latent_prep(proj, mla_q_norm[l][None], mla_kv_norm[l][None], cos_t, sin_t, S)
        q = _q_up(cqn, _pack_uq(mla_w_uq[l]), cos_t, sin_t, S)
        kv = _matmul(ckvn, mla_w_ukv[l].astype(BF16), "mla_kv_up")
        yb = _mla_attention(q, kv, krope, B, S)

        u = _branch_mix(ya, yb, a_w_out[l].astype(BF16), mla_w_out[l].astype(BF16), proj)
        h1, h1b = _out_projection_ln(u, w_out[l].astype(BF16), h, ln1_g[l][None], ln1_b[l][None])

        qp = _matmul(h1b, peer_w_query[l].astype(BF16), "peer_query")
        i1, i2, g = _peer_route(qp, peer_sub_keys_1[l].astype(BF16), peer_sub_keys_2[l].astype(BF16))
        w2 = _gate_matrix(i1, i2, g).reshape(T, N_EXPERTS)
        h = _peer_experts(h1b, peer_expert_down[l].T.astype(BF16), peer_expert_up[l].astype(BF16),
                          w2, h1, ln2_g[l][None], ln2_b[l][None])
    return h.reshape(B, S, D)
```

```python
import functools
import math

import jax
import jax.numpy as jnp
from jax import lax
from jax.experimental import pallas as pl
from jax.experimental.pallas import tpu as pltpu

F32 = jnp.float32
BF16 = jnp.bfloat16

D_MODEL = 2048
A_HEADS = 16
A_HEAD_DIM = 128
A_PATTERNS = ((128, 1), (512, 4), (2048, 16))
A_BLOCK = 128
A_WIDTH = A_HEADS * A_HEAD_DIM
B_HEADS = 16
Q_LORA = 512
KV_LORA = 512
QK_NOPE = 128
QK_ROPE = 64
V_HEAD = 128
ROPE_THETA = 10000.0
N_KEYS = 128
PEER_HEADS = 8
PEER_QDIM = 256
PEER_TOPK = 16
N_EXPERTS = N_KEYS * N_KEYS
N_SLOTS = PEER_HEADS * PEER_TOPK
LN_EPS = 1e-5
RMS_EPS = 1e-6
DEPTH = 1
ALPHA = (2.0 * DEPTH) ** 0.25
NEG = -1e30

LANE = 128
MLA_HEAD_PAD = 256
VMEM_LIMIT = 56 * 1024 * 1024

COL_Q, COL_K, COL_V = 0, A_WIDTH, 2 * A_WIDTH
COL_GA = 3 * A_WIDTH
COL_GB = COL_GA + D_MODEL
COL_CQ = COL_GB + D_MODEL
COL_CKV = COL_CQ + Q_LORA
COL_KR = COL_CKV + KV_LORA
PROJ_WIDTH = COL_CQ + 2048

_NT = (((1,), (1,)), ((), ()))


def _cparams(*sem):
    return pltpu.CompilerParams(dimension_semantics=sem, vmem_limit_bytes=VMEM_LIMIT)


def _layer_norm(z, g, b):
    mu = jnp.mean(z, axis=-1, keepdims=True)
    zc = z - mu
    var = jnp.mean(zc * zc, axis=-1, keepdims=True)
    return zc * lax.rsqrt(var + LN_EPS) * g + b


def _inproj_kernel(x_ref, w_ref, b_ref, o_ref, *, gate_lo, gate_hi):
    j = pl.program_id(1)
    acc = jnp.dot(x_ref[...].astype(BF16), w_ref[...], preferred_element_type=F32)
    is_gate = jnp.logical_and(j >= gate_lo, j < gate_hi)

    @pl.when(is_gate)
    def _():
        o_ref[...] = jax.nn.sigmoid(acc + b_ref[...]).astype(o_ref.dtype)

    @pl.when(jnp.logical_not(is_gate))
    def _():
        o_ref[...] = acc.astype(o_ref.dtype)


def _in_projection(x2, w_all, b_all):
    T = x2.shape[0]
    tm, tn = min(1024, T), 1024
    kern = functools.partial(_inproj_kernel, gate_lo=COL_GA // tn, gate_hi=COL_CQ // tn)
    return pl.pallas_call(
        kern,
        out_shape=jax.ShapeDtypeStruct((T, PROJ_WIDTH), BF16),
        grid=(T // tm, PROJ_WIDTH // tn),
        in_specs=[
            pl.BlockSpec((tm, D_MODEL), lambda i, j: (i, 0)),
            pl.BlockSpec((D_MODEL, tn), lambda i, j: (0, j)),
            pl.BlockSpec((1, tn), lambda i, j: (0, j)),
        ],
        out_specs=pl.BlockSpec((tm, tn), lambda i, j: (i, j)),
        compiler_params=_cparams("parallel", "arbitrary"),
        name="in_projection",
    )(x2, w_all, b_all)


def _dilated_kernel(q_ref, kp_ref, ko_ref, vp_ref, vo_ref, o_ref, lse_ref, *, dilation, steps):
    n = pl.program_id(2)
    blk = A_BLOCK
    i = lax.broadcasted_iota(jnp.int32, (blk, 2 * blk), 0)
    c = lax.broadcasted_iota(jnp.int32, (blk, 2 * blk), 1)
    rel = blk + i - c
    valid = (rel >= 0) & (rel <= steps) & ((c >= blk) | (n > 0))
    dist = (dilation * rel).astype(F32)
    lane = lax.broadcasted_iota(jnp.int32, (blk, LANE), 1)
    lse_all = jnp.zeros((blk, LANE), F32)
    for h in range(A_HEADS):
        sl = slice(h * A_HEAD_DIM, (h + 1) * A_HEAD_DIM)
        q = q_ref[:, sl]
        k = jnp.concatenate([kp_ref[:, sl], ko_ref[:, sl]], axis=0)
        v = jnp.concatenate([vp_ref[:, sl], vo_ref[:, sl]], axis=0)
        s = lax.dot_general(q, k, _NT, preferred_element_type=F32) * (A_HEAD_DIM ** -0.5)
        slope = 2.0 ** (-8.0 * (h + 1) / A_HEADS)
        logits = jnp.where(valid, s - slope * dist, NEG)
        m = jnp.max(logits, axis=-1, keepdims=True)
        p = jnp.exp(logits - m)
        z = jnp.sum(p, axis=-1, keepdims=True)
        o = jnp.dot(p.astype(BF16), v, preferred_element_type=F32) / z
        o_ref[:, sl] = o.astype(o_ref.dtype)
        lse_all = jnp.where(lane == h, m + jnp.log(z), lse_all)
    lse_ref[...] = lse_all


def _dilated_attention(qkv_r, window, dilation):
    B, d, L, _ = qkv_r.shape
    assert d == dilation and L % A_BLOCK == 0
    nb = L // A_BLOCK
    blk = (None, None, A_BLOCK, A_WIDTH)
    kern = functools.partial(_dilated_kernel, dilation=dilation, steps=window // dilation)
    return pl.pallas_call(
        kern,
        out_shape=(
            jax.ShapeDtypeStruct((B, d, L, A_WIDTH), BF16),
            jax.ShapeDtypeStruct((B, d, L, LANE), F32),
        ),
        grid=(B, d, nb),
        in_specs=[
            pl.BlockSpec(blk, lambda b, r, n: (b, r, n, 0)),
            pl.BlockSpec(blk, lambda b, r, n: (b, r, jnp.maximum(n - 1, 0), 1)),
            pl.BlockSpec(blk, lambda b, r, n: (b, r, n, 1)),
            pl.BlockSpec(blk, lambda b, r, n: (b, r, jnp.maximum(n - 1, 0), 2)),
            pl.BlockSpec(blk, lambda b, r, n: (b, r, n, 2)),
        ],
        out_specs=(
            pl.BlockSpec(blk, lambda b, r, n: (b, r, n, 0)),
            pl.BlockSpec((None, None, A_BLOCK, LANE), lambda b, r, n: (b, r, n, 0)),
        ),
        compiler_params=_cparams("parallel", "parallel", "arbitrary"),
        name=f"dilated_attention_d{dilation}",
    )(qkv_r, qkv_r, qkv_r, qkv_r, qkv_r)


def _combine_kernel(o1_ref, o2_ref, o3_ref, l1_ref, l2_ref, l3_ref, y_ref):
    a, b, c = l1_ref[...], l2_ref[...], l3_ref[...]
    m = jnp.maximum(jnp.maximum(a, b), c)
    ea, eb, ec = jnp.exp(a - m), jnp.exp(b - m), jnp.exp(c - m)
    inv = 1.0 / (ea + eb + ec)
    wa, wb, wc = ea * inv, eb * inv, ec * inv
    for h in range(A_HEADS):
        sl = slice(h * A_HEAD_DIM, (h + 1) * A_HEAD_DIM)
        y = (wa[:, h:h + 1] * o1_ref[:, sl].astype(F32)
             + wb[:, h:h + 1] * o2_ref[:, sl].astype(F32)
             + wc[:, h:h + 1] * o3_ref[:, sl].astype(F32))
        y_ref[:, sl] = y.astype(y_ref.dtype)


def _combine_patterns(outs, lses):
    T = outs[0].shape[0]
    tm = min(512, T)
    ospec = pl.BlockSpec((tm, A_WIDTH), lambda i: (i, 0))
    lspec = pl.BlockSpec((tm, LANE), lambda i: (i, 0))
    return pl.pallas_call(
        _combine_kernel,
        out_shape=jax.ShapeDtypeStruct((T, A_WIDTH), BF16),
        grid=(T // tm,),
        in_specs=[ospec, ospec, ospec, lspec, lspec, lspec],
        out_specs=ospec,
        compiler_params=_cparams("parallel"),
        name="combine_patterns",
    )(*outs, *lses)


def _rope_lanes(t, cos, sin):
    lane = lax.broadcasted_iota(jnp.int32, t.shape, 1)
    half = QK_ROPE // 2
    rot = jnp.where(lane < half, pltpu.roll(t, LANE - half, 1), pltpu.roll(t, half, 1))
    return t * cos + rot * sin


def _rms_norm(x, g):
    ms = jnp.mean(x * x, axis=-1, keepdims=True)
    return x * lax.rsqrt(ms + RMS_EPS) * g


def _latent_kernel(cq_ref, ckv_ref, kr_ref, gq_ref, gkv_ref, cos_ref, sin_ref, cqn_ref, ckvn_ref, krope_ref):
    cqn_ref[...] = _rms_norm(cq_ref[...].astype(F32), gq_ref[...]).astype(cqn_ref.dtype)
    ckvn_ref[...] = _rms_norm(ckv_ref[...].astype(F32), gkv_ref[...]).astype(ckvn_ref.dtype)
    krope_ref[...] = _rope_lanes(kr_ref[...].astype(F32), cos_ref[...], sin_ref[...]).astype(krope_ref.dtype)


def _latent_prep(proj, gq, gkv, cos_t, sin_t, S):
    T = proj.shape[0]
    tm = min(512, S)
    ns = S // tm
    return pl.pallas_call(
        _latent_kernel,
        out_shape=(
            jax.ShapeDtypeStruct((T, Q_LORA), BF16),
            jax.ShapeDtypeStruct((T, KV_LORA), BF16),
            jax.ShapeDtypeStruct((T, LANE), BF16),
        ),
        grid=(T // tm,),
        in_specs=[
            pl.BlockSpec((tm, Q_LORA), lambda i: (i, COL_CQ // Q_LORA)),
            pl.BlockSpec((tm, KV_LORA), lambda i: (i, COL_CKV // KV_LORA)),
            pl.BlockSpec((tm, LANE), lambda i: (i, COL_KR // LANE)),
            pl.BlockSpec((1, Q_LORA), lambda i: (0, 0)),
            pl.BlockSpec((1, KV_LORA), lambda i: (0, 0)),
            pl.BlockSpec((tm, LANE), lambda i: (i % ns, 0)),
            pl.BlockSpec((tm, LANE), lambda i: (i % ns, 0)),
        ],
        out_specs=(
            pl.BlockSpec((tm, Q_LORA), lambda i: (i, 0)),
            pl.BlockSpec((tm, KV_LORA), lambda i: (i, 0)),
            pl.BlockSpec((tm, LANE), lambda i: (i, 0)),
        ),
        compiler_params=_cparams("parallel"),
        name="latent_prep",
    )(proj, proj, proj, gq, gkv, cos_t, sin_t)


def _qup_kernel(c_ref, w_ref, cos_ref, sin_ref, o_ref, *, scale):
    acc = jnp.dot(c_ref[...], w_ref[...], preferred_element_type=F32) * scale
    cos, sin = cos_ref[...], sin_ref[...]
    for hb in range(acc.shape[1] // MLA_HEAD_PAD):
        lo = hb * MLA_HEAD_PAD
        o_ref[:, lo:lo + QK_NOPE] = acc[:, lo:lo + QK_NOPE].astype(o_ref.dtype)
        r = _rope_lanes(acc[:, lo + QK_NOPE:lo + MLA_HEAD_PAD], cos, sin)
        o_ref[:, lo + QK_NOPE:lo + MLA_HEAD_PAD] = r.astype(o_ref.dtype)


def _q_up(cqn, w_uq_p, cos_t, sin_t, S):
    T = cqn.shape[0]
    N = w_uq_p.shape[1]
    tm, tn = min(512, S), 1024
    ns = S // tm
    kern = functools.partial(_qup_kernel, scale=(QK_NOPE + QK_ROPE) ** -0.5 * math.log2(math.e))
    return pl.pallas_call(
        kern,
        out_shape=jax.ShapeDtypeStruct((T, N), BF16),
        grid=(T // tm, N // tn),
        in_specs=[
            pl.BlockSpec((tm, Q_LORA), lambda i, j: (i, 0)),
            pl.BlockSpec((Q_LORA, tn), lambda i, j: (0, j)),
            pl.BlockSpec((tm, LANE), lambda i, j: (i % ns, 0)),
            pl.BlockSpec((tm, LANE), lambda i, j: (i % ns, 0)),
        ],
        out_specs=pl.BlockSpec((tm, tn), lambda i, j: (i, j)),
        compiler_params=_cparams("parallel", "arbitrary"),
        name="mla_q_up",
    )(cqn, w_uq_p, cos_t, sin_t)


def _mm_kernel(a_ref, w_ref, o_ref):
    o_ref[...] = jnp.dot(a_ref[...], w_ref[...], preferred_element_type=F32).astype(o_ref.dtype)


def _matmul(a, w, name, tm=512, tn=1024):
    M, K = a.shape
    N = w.shape[1]
    tm, tn = min(tm, M), min(tn, N)
    return pl.pallas_call(
        _mm_kernel,
        out_shape=jax.ShapeDtypeStruct((M, N), BF16),
        grid=(M // tm, N // tn),
        in_specs=[
            pl.BlockSpec((tm, K), lambda i, j: (i, 0)),
            pl.BlockSpec((K, tn), lambda i, j: (0, j)),
        ],
        out_specs=pl.BlockSpec((tm, tn), lambda i, j: (i, j)),
        compiler_params=_cparams("parallel", "arbitrary"),
        name=name,
    )(a, w)


def _kvup_kernel(c_ref, wk_ref, wvt_ref, kn_ref, vt_ref):
    c = c_ref[...]
    kn_ref[...] = jnp.dot(c, wk_ref[...], preferred_element_type=F32).astype(kn_ref.dtype)
    vt_ref[...] = lax.dot_general(wvt_ref[...], c, _NT, preferred_element_type=F32).astype(vt_ref.dtype)


def _kv_up(ckvn, wk, wvt):
    T = ckvn.shape[0]
    tm = min(512, T)
    n = B_HEADS * QK_NOPE
    return pl.pallas_call(
        _kvup_kernel,
        out_shape=(jax.ShapeDtypeStruct((T, n), BF16), jax.ShapeDtypeStruct((B_HEADS * V_HEAD, T), BF16)),
        grid=(T // tm,),
        in_specs=[
            pl.BlockSpec((tm, KV_LORA), lambda i: (i, 0)),
            pl.BlockSpec((KV_LORA, n), lambda i: (0, 0)),
            pl.BlockSpec((B_HEADS * V_HEAD, KV_LORA), lambda i: (0, 0)),
        ],
        out_specs=(pl.BlockSpec((tm, n), lambda i: (i, 0)), pl.BlockSpec((B_HEADS * V_HEAD, tm), lambda i: (0, i))),
        compiler_params=_cparams("parallel"),
        name="mla_kv_up",
    )(ckvn, wk, wvt)


def _mla_kernel(q_ref, kn_ref, kr_ref, vt_ref, o_ref, sa_ref, sb_ref, m_ref, l_ref, acc_ref, *, tq):
    qi = pl.program_id(2)
    q = q_ref[...]
    m_ref[...] = jnp.full(m_ref.shape, NEG, F32)
    l_ref[...] = jnp.zeros(l_ref.shape, F32)
    acc_ref[...] = jnp.zeros(acc_ref.shape, F32)

    def scores(c, s_ref):
        start = pl.multiple_of(c * tq, tq)
        k = jnp.concatenate([kn_ref[pl.ds(start, tq), :], kr_ref[pl.ds(start, tq), :]], axis=1)
        s_ref[...] = lax.dot_general(k, q, _NT, preferred_element_type=F32)

    def update(c, s_ref, masked):
        start = pl.multiple_of(c * tq, tq)
        st = s_ref[...]
        if masked:
            key = lax.broadcasted_iota(jnp.int32, st.shape, 0)
            qry = lax.broadcasted_iota(jnp.int32, st.shape, 1)
            st = jnp.where(key <= qry, st, NEG)
        m_prev = m_ref[...]
        m_new = jnp.maximum(m_prev, jnp.max(st, axis=0, keepdims=True))
        a = jnp.exp2(m_prev - m_new)
        p = jnp.exp2(st - m_new)
        l_ref[...] = a * l_ref[...] + jnp.sum(p, axis=0, keepdims=True)
        pv = jnp.dot(vt_ref[:, pl.ds(start, tq)], p.astype(BF16), preferred_element_type=F32)
        acc_ref[...] = a * acc_ref[...] + pv
        m_ref[...] = m_new

    scores(0, sa_ref)

    def pair(i, carry):
        c = 2 * i
        scores(c + 1, sb_ref)
        update(c, sa_ref, False)
        scores(c + 2, sa_ref)
        update(c + 1, sb_ref, False)
        return carry

    npair = qi // 2
    lax.fori_loop(0, npair, pair, 0)
    c0 = 2 * npair

    @pl.when(c0 < qi)
    def _():
        scores(qi, sb_ref)
        update(c0, sa_ref, False)
        update(qi, sb_ref, True)

    @pl.when(c0 == qi)
    def _():
        update(qi, sa_ref, True)

    o_ref[...] = (acc_ref[...] / l_ref[...]).T.astype(o_ref.dtype)


def _mla_attention(q, kn, krope, vt, B, S):
    T = q.shape[0]
    tq = min(512, S)
    nq = S // tq
    kern = functools.partial(_mla_kernel, tq=tq)
    return pl.pallas_call(
        kern,
        out_shape=jax.ShapeDtypeStruct((T, B_HEADS * V_HEAD), BF16),
        grid=(B, B_HEADS, nq),
        in_specs=[
            pl.BlockSpec((tq, MLA_HEAD_PAD), lambda b, h, i: (b * nq + i, h)),
            pl.BlockSpec((S, QK_NOPE), lambda b, h, i: (b, h)),
            pl.BlockSpec((S, LANE), lambda b, h, i: (b, 0)),
            pl.BlockSpec((V_HEAD, S), lambda b, h, i: (h, b)),
        ],
        out_specs=pl.BlockSpec((tq, V_HEAD), lambda b, h, i: (b * nq + i, h)),
        scratch_shapes=[
            pltpu.VMEM((tq, tq), F32),
            pltpu.VMEM((tq, tq), F32),
            pltpu.VMEM((1, tq), F32),
            pltpu.VMEM((1, tq), F32),
            pltpu.VMEM((V_HEAD, tq), F32),
        ],
        compiler_params=_cparams("parallel", "parallel", "arbitrary"),
        name="mla_attention",
    )(q, kn, krope, vt)


def _branch_kernel(ya_ref, yb_ref, wa_ref, wb_ref, ga_ref, gb_ref, o_ref):
    pa = jnp.dot(ya_ref[...], wa_ref[...], preferred_element_type=F32)
    pb = jnp.dot(yb_ref[...], wb_ref[...], preferred_element_type=F32)
    u = ga_ref[...].astype(F32) * pa + gb_ref[...].astype(F32) * pb
    o_ref[...] = u.astype(o_ref.dtype)


def _branch_mix(ya, yb, wa, wb, proj):
    T = ya.shape[0]
    tm, tn = min(512, T), 1024
    return pl.pallas_call(
        _branch_kernel,
        out_shape=jax.ShapeDtypeStruct((T, D_MODEL), BF16),
        grid=(T // tm, D_MODEL // tn),
        in_specs=[
            pl.BlockSpec((tm, A_WIDTH), lambda i, j: (i, 0)),
            pl.BlockSpec((tm, B_HEADS * V_HEAD), lambda i, j: (i, 0)),
            pl.BlockSpec((A_WIDTH, tn), lambda i, j: (0, j)),
            pl.BlockSpec((B_HEADS * V_HEAD, tn), lambda i, j: (0, j)),
            pl.BlockSpec((tm, tn), lambda i, j: (i, COL_GA // tn + j)),
            pl.BlockSpec((tm, tn), lambda i, j: (i, COL_GB // tn + j)),
        ],
        out_specs=pl.BlockSpec((tm, tn), lambda i, j: (i, j)),
        compiler_params=_cparams("parallel", "arbitrary"),
        name="branch_mix",
    )(ya, yb, wa, wb, proj, proj)


def _outln_kernel(u_ref, w_ref, x_ref, g_ref, b_ref, h_ref, hb_ref):
    mix = jnp.dot(u_ref[...], w_ref[...], preferred_element_type=F32)
    h = _layer_norm(ALPHA * x_ref[...] + mix, g_ref[...], b_ref[...])
    h_ref[...] = h
    hb_ref[...] = h.astype(hb_ref.dtype)


def _out_projection_ln(u, w_out, x2, g, b):
    T = u.shape[0]
    tm = min(256, T)
    row = pl.BlockSpec((tm, D_MODEL), lambda i: (i, 0))
    vec = pl.BlockSpec((1, D_MODEL), lambda i: (0, 0))
    return pl.pallas_call(
        _outln_kernel,
        out_shape=(
            jax.ShapeDtypeStruct((T, D_MODEL), F32),
            jax.ShapeDtypeStruct((T, D_MODEL), BF16),
        ),
        grid=(T // tm,),
        in_specs=[row, pl.BlockSpec((D_MODEL, D_MODEL), lambda i: (0, 0)), row, vec, vec],
        out_specs=(row, row),
        compiler_params=_cparams("parallel"),
        name="out_projection_ln",
    )(u, w_out, x2, g, b)


def _topk_axis0(s, k):
    n = s.shape[0]
    iota = lax.broadcasted_iota(jnp.int32, s.shape, 0)
    vals, idxs = [], []
    for _ in range(k):
        m = jnp.max(s, axis=0, keepdims=True)
        idx = jnp.min(jnp.where(s == m, iota, n), axis=0, keepdims=True)
        vals.append(m)
        idxs.append(idx)
        s = jnp.where(iota == idx, -jnp.inf, s)
    return vals, idxs


def _select_rows(rows, sel):
    out = jnp.zeros_like(rows[0])
    for a, r in enumerate(rows):
        out = jnp.where(sel == a, r, out)
    return out


def _route_kernel(q_ref, k1_ref, k2_ref, i1_ref, i2_ref, g_ref):
    half = PEER_QDIM // 2
    i1_all, i2_all, g_all = [], [], []
    for h in range(PEER_HEADS):
        q1 = q_ref[:, h * PEER_QDIM:h * PEER_QDIM + half]
        q2 = q_ref[:, h * PEER_QDIM + half:(h + 1) * PEER_QDIM]
        s1 = lax.dot_general(k1_ref[...], q1, _NT, preferred_element_type=F32)
        s2 = lax.dot_general(k2_ref[...], q2, _NT, preferred_element_type=F32)
        v1, i1 = _topk_axis0(s1, PEER_TOPK)
        v2, i2 = _topk_axis0(s2, PEER_TOPK)
        v2m = jnp.concatenate(v2, axis=0)
        cand = jnp.concatenate([v1[a] + v2m for a in range(PEER_TOPK)], axis=0)
        ts, pos = _topk_axis0(cand, PEER_TOPK)
        top = jnp.concatenate(ts, axis=0)
        e = jnp.exp(top - ts[0])
        g_all.append(e / jnp.sum(e, axis=0, keepdims=True))
        for p in pos:
            i1_all.append(_select_rows(i1, p // PEER_TOPK))
            i2_all.append(_select_rows(i2, p % PEER_TOPK))
    i1_ref[...] = jnp.concatenate(i1_all, axis=0).T
    i2_ref[...] = jnp.concatenate(i2_all, axis=0).T
    g_ref[...] = jnp.concatenate(g_all, axis=0).T


def _peer_route(qp, k1, k2):
    T = qp.shape[0]
    tm = min(256, T)
    slot = pl.BlockSpec((tm, N_SLOTS), lambda i: (i, 0))
    keys = pl.BlockSpec((N_KEYS, PEER_QDIM // 2), lambda i: (0, 0))
    return pl.pallas_call(
        _route_kernel,
        out_shape=(
            jax.ShapeDtypeStruct((T, N_SLOTS), jnp.int32),
            jax.ShapeDtypeStruct((T, N_SLOTS), jnp.int32),
            jax.ShapeDtypeStruct((T, N_SLOTS), F32),
        ),
        grid=(T // tm,),
        in_specs=[pl.BlockSpec((tm, PEER_HEADS * PEER_QDIM), lambda i: (i, 0)), keys, keys],
        out_specs=(slot, slot, slot),
        compiler_params=_cparams("parallel"),
        name="peer_route",
    )(qp, k1, k2)


GATE_GROUP = 16


def _gate_matrix_kernel(i1_ref, i2_ref, g_ref, w_ref):
    key = lax.broadcasted_iota(jnp.int32, (N_KEYS, N_SLOTS), 0)

    def body(tg, carry):
        t0 = pl.multiple_of(tg * GATE_GROUP, GATE_GROUP)
        per_token = []
        for u in range(GATE_GROUP):
            r1 = i1_ref[pl.ds(t0 + u, 1), :]
            r2 = i2_ref[pl.ds(t0 + u, 1), :]
            g = g_ref[pl.ds(t0 + u, 1), :]
            a = jnp.where(key == r1, 1.0, 0.0).astype(BF16)
            b = jnp.where(key == r2, g, 0.0).astype(BF16)
            per_token.append(lax.dot_general(a, b, _NT, preferred_element_type=F32))
        w = pltpu.einshape("tid->itd", jnp.stack(per_token, axis=0))
        w_ref[:, pl.ds(t0, GATE_GROUP), :] = w.astype(w_ref.dtype)
        return carry

    lax.fori_loop(0, w_ref.shape[1] // GATE_GROUP, body, 0)


def _gate_matrix(i1, i2, g):
    T = i1.shape[0]
    tb = min(128, T)
    slot = pl.BlockSpec((tb, N_SLOTS), lambda i: (i, 0))
    return pl.pallas_call(
        _gate_matrix_kernel,
        out_shape=jax.ShapeDtypeStruct((N_KEYS, T, N_KEYS), BF16),
        grid=(T // tb,),
        in_specs=[slot, slot, slot],
        out_specs=pl.BlockSpec((N_KEYS, tb, N_KEYS), lambda i: (0, i, 0)),
        compiler_params=_cparams("parallel"),
        name="peer_gate_matrix",
    )(i1, i2, g)


def _experts_kernel(hb_ref, dn_ref, up_ref, w_ref, h_ref, g_ref, b_ref, o_ref, acc_ref):
    j = pl.program_id(1)

    @pl.when(j == 0)
    def _():
        acc_ref[...] = jnp.zeros(acc_ref.shape, F32)

    pre = jnp.dot(hb_ref[...], dn_ref[...], preferred_element_type=F32)
    act = 0.5 * pre * (1.0 + lax.erf(pre * (2.0 ** -0.5)))
    gates = jnp.concatenate([w_ref[m] for m in range(w_ref.shape[0])], axis=1)
    act = act * gates.astype(F32)
    acc_ref[...] += jnp.dot(act.astype(BF16), up_ref[...], preferred_element_type=F32)

    @pl.when(j == pl.num_programs(1) - 1)
    def _():
        o_ref[...] = _layer_norm(ALPHA * h_ref[...] + acc_ref[...], g_ref[...], b_ref[...])


def _peer_experts(hb, down_t, up, w3, h, g, b):
    T = hb.shape[0]
    tm, te = min(512, T), 1024
    row = pl.BlockSpec((tm, D_MODEL), lambda i, j: (i, 0))
    vec = pl.BlockSpec((1, D_MODEL), lambda i, j: (0, 0))
    return pl.pallas_call(
        _experts_kernel,
        out_shape=jax.ShapeDtypeStruct((T, D_MODEL), F32),
        grid=(T // tm, N_EXPERTS // te),
        in_specs=[
            row,
            pl.BlockSpec((D_MODEL, te), lambda i, j: (0, j)),
            pl.BlockSpec((te, D_MODEL), lambda i, j: (j, 0)),
            pl.BlockSpec((te // N_KEYS, tm, N_KEYS), lambda i, j: (j, i, 0)),
            row, vec, vec,
        ],
        out_specs=row,
        scratch_shapes=[pltpu.VMEM((tm, D_MODEL), F32)],
        compiler_params=_cparams("parallel", "arbitrary"),
        name="peer_experts",
    )(hb, down_t, up, w3, h, g, b)


def _rope_tables(S):
    half = QK_ROPE // 2
    inv_freq = ROPE_THETA ** (-jnp.arange(half, dtype=F32) / half)
    ang = jnp.arange(S, dtype=jnp.int32).astype(F32)[:, None] * inv_freq[None, :]
    cos, sin = jnp.cos(ang), jnp.sin(ang)
    zeros = jnp.zeros((S, LANE - QK_ROPE), F32)
    return (jnp.concatenate([cos, cos, zeros], axis=1), jnp.concatenate([-sin, sin, zeros], axis=1))


def _pack_input_weights(w_in, b_gates):
    wq, wk, wv, wcq, wckv, wkr, wg = jnp.split(
        w_in, (A_WIDTH, 2 * A_WIDTH, 3 * A_WIDTH, 3 * A_WIDTH + Q_LORA,
               3 * A_WIDTH + Q_LORA + KV_LORA, 3 * A_WIDTH + Q_LORA + KV_LORA + QK_ROPE), axis=1)
    pad = jnp.zeros((D_MODEL, PROJ_WIDTH - COL_KR - QK_ROPE), w_in.dtype)
    w_all = jnp.concatenate([wq, wk, wv, wg, wcq, wckv, wkr, pad], axis=1).astype(BF16)
    b_all = jnp.zeros((1, PROJ_WIDTH), F32).at[0, COL_GA:COL_CQ].set(b_gates)
    return w_all, b_all


def _pack_uq(w_uq):
    w = w_uq.reshape(Q_LORA, B_HEADS, QK_NOPE + QK_ROPE)
    w = jnp.pad(w, ((0, 0), (0, 0), (0, MLA_HEAD_PAD - QK_NOPE - QK_ROPE)))
    return w.reshape(Q_LORA, B_HEADS * MLA_HEAD_PAD).astype(BF16)


def kernel(x, w_in, b_gates, a_w_out, mla_q_norm, mla_w_uq, mla_kv_norm, mla_w_ukv, mla_w_out, w_out,
           ln1_g, ln1_b, peer_w_query, peer_sub_keys_1, peer_sub_keys_2, peer_expert_down,
           peer_expert_up, ln2_g, ln2_b):
    B, S, D = x.shape
    assert D == D_MODEL and w_in.shape[0] == DEPTH
    T = B * S
    cos_t, sin_t = _rope_tables(S)
    h = x.reshape(T, D)
    for l in range(DEPTH):
        w_all, b_all = _pack_input_weights(w_in[l], b_gates[l])
        proj = _in_projection(h, w_all, b_all)

        qkv = proj[:, :3 * A_WIDTH].reshape(B, S, 3 * A_WIDTH)
        outs, lses = [], []
        for window, d in A_PATTERNS:
            L = S // d
            qkv_r = qkv.reshape(B, L, d, 3 * A_WIDTH).transpose(0, 2, 1, 3)
            o_r, lse_r = _dilated_attention(qkv_r, window, d)
            outs.append(o_r.transpose(0, 2, 1, 3).reshape(T, A_WIDTH))
            lses.append(lse_r.transpose(0, 2, 1, 3).reshape(T, LANE))
        ya = _combine_patterns(outs, lses)

        cqn, ckvn, krope = _latent_prep(proj, mla_q_norm[l][None], mla_kv_norm[l][None], cos_t, sin_t, S)
        q = _q_up(cqn, _pack_uq(mla_w_uq[l]), cos_t, sin_t, S)
        w_ukv = mla_w_ukv[l].reshape(KV_LORA, B_HEADS, QK_NOPE + V_HEAD)
        wk = w_ukv[:, :, :QK_NOPE].reshape(KV_LORA, B_HEADS * QK_NOPE).astype(BF16)
        wvt = w_ukv[:, :, QK_NOPE:].reshape(KV_LORA, B_HEADS * V_HEAD).T.astype(BF16)
        kn, vt = _kv_up(ckvn, wk, wvt)
        yb = _mla_attention(q, kn, krope, vt, B, S)

        u = _branch_mix(ya, yb, a_w_out[l].astype(BF16), mla_w_out[l].astype(BF16), proj)
        h1, h1b = _out_projection_ln(u, w_out[l].astype(BF16), h, ln1_g[l][None], ln1_b[l][None])

        qp = _matmul(h1b, peer_w_query[l].astype(BF16), "peer_query")
        i1, i2, g = _peer_route(qp, peer_sub_keys_1[l].astype(BF16), peer_sub_keys_2[l].astype(BF16))
        w3 = _gate_matrix(i1, i2, g)
        h = _peer_experts(h1b, peer_expert_down[l].T.astype(BF16), peer_expert_up[l].astype(BF16),
                          w3, h1, ln2_g[l][None], ln2_b[l][None])
    return h.reshape(B, S, D)
```

```python
import functools
import math

import jax
import jax.numpy as jnp
from jax import lax
from jax.experimental import pallas as pl
from jax.experimental.pallas import tpu as pltpu

F32 = jnp.float32
BF16 = jnp.bfloat16

D_MODEL = 2048
A_HEADS = 16
A_HEAD_DIM = 128
A_PATTERNS = ((128, 1), (512, 4), (2048, 16))
A_BLOCK = 128
A_WIDTH = A_HEADS * A_HEAD_DIM
B_HEADS = 16
Q_LORA = 512
KV_LORA = 512
QK_NOPE = 128
QK_ROPE = 64
V_HEAD = 128
ROPE_THETA = 10000.0
N_KEYS = 128
PEER_HEADS = 8
PEER_QDIM = 256
PEER_TOPK = 16
N_EXPERTS = N_KEYS * N_KEYS
N_SLOTS = PEER_HEADS * PEER_TOPK
LN_EPS = 1e-5
RMS_EPS = 1e-6
DEPTH = 1
ALPHA = (2.0 * DEPTH) ** 0.25
NEG = -1e30

LANE = 128
MLA_HEAD_PAD = 256
VMEM_LIMIT = 56 * 1024 * 1024

COL_Q, COL_K, COL_V = 0, A_WIDTH, 2 * A_WIDTH
COL_GA = 3 * A_WIDTH
COL_GB = COL_GA + D_MODEL
COL_CQ = COL_GB + D_MODEL
COL_CKV = COL_CQ + Q_LORA
COL_KR = COL_CKV + KV_LORA
PROJ_WIDTH = COL_CQ + 2048

_NT = (((1,), (1,)), ((), ()))


def _cparams(*sem):
    return pltpu.CompilerParams(dimension_semantics=sem, vmem_limit_bytes=VMEM_LIMIT)


def _layer_norm(z, g, b):
    mu = jnp.mean(z, axis=-1, keepdims=True)
    zc = z - mu
    var = jnp.mean(zc * zc, axis=-1, keepdims=True)
    return zc * lax.rsqrt(var + LN_EPS) * g + b


def _inproj_kernel(x_ref, w_ref, b_ref, o_ref, *, gate_lo, gate_hi):
    j = pl.program_id(1)
    acc = jnp.dot(x_ref[...].astype(BF16), w_ref[...], preferred_element_type=F32)
    is_gate = jnp.logical_and(j >= gate_lo, j < gate_hi)

    @pl.when(is_gate)
    def _():
        o_ref[...] = jax.nn.sigmoid(acc + b_ref[...]).astype(o_ref.dtype)

    @pl.when(jnp.logical_not(is_gate))
    def _():
        o_ref[...] = acc.astype(o_ref.dtype)


def _in_projection(x2, w_all, b_all):
    T = x2.shape[0]
    tm, tn = min(1024, T), 1024
    kern = functools.partial(_inproj_kernel, gate_lo=COL_GA // tn, gate_hi=COL_CQ // tn)
    return pl.pallas_call(
        kern,
        out_shape=jax.ShapeDtypeStruct((T, PROJ_WIDTH), BF16),
        grid=(T // tm, PROJ_WIDTH // tn),
        in_specs=[
            pl.BlockSpec((tm, D_MODEL), lambda i, j: (i, 0)),
            pl.BlockSpec((D_MODEL, tn), lambda i, j: (0, j)),
            pl.BlockSpec((1, tn), lambda i, j: (0, j)),
        ],
        out_specs=pl.BlockSpec((tm, tn), lambda i, j: (i, j)),
        compiler_params=_cparams("parallel", "arbitrary"),
        name="in_projection",
    )(x2, w_all, b_all)


def _block_pos(idx, groups):
    if groups == 1:
        return idx
    per = A_BLOCK // groups
    return groups * (idx % per) + idx // per


def _dilated_kernel(q_ref, kp_ref, ko_ref, vp_ref, vo_ref, o_ref, lse_ref, *, dilation, steps, groups):
    n = pl.program_id(2)
    blk = A_BLOCK
    i = lax.broadcasted_iota(jnp.int32, (blk, 2 * blk), 0)
    c = lax.broadcasted_iota(jnp.int32, (blk, 2 * blk), 1)
    rel = blk + _block_pos(i, groups) - (blk * (c // blk) + _block_pos(c % blk, groups))
    valid = (rel >= 0) & (rel <= steps) & ((c >= blk) | (n > 0))
    dist = (dilation * rel).astype(F32)
    lane = lax.broadcasted_iota(jnp.int32, (blk, LANE), 1)
    lse_all = jnp.zeros((blk, LANE), F32)

    def rows(ref, sl):
        return ref[..., sl].reshape(blk, A_HEAD_DIM)

    for h in range(A_HEADS):
        sl = slice(h * A_HEAD_DIM, (h + 1) * A_HEAD_DIM)
        q = rows(q_ref, sl)
        k = jnp.concatenate([rows(kp_ref, sl), rows(ko_ref, sl)], axis=0)
        v = jnp.concatenate([rows(vp_ref, sl), rows(vo_ref, sl)], axis=0)
        s = lax.dot_general(q, k, _NT, preferred_element_type=F32) * (A_HEAD_DIM ** -0.5)
        slope = 2.0 ** (-8.0 * (h + 1) / A_HEADS)
        logits = jnp.where(valid, s - slope * dist, NEG)
        m = jnp.max(logits, axis=-1, keepdims=True)
        p = jnp.exp(logits - m)
        z = jnp.sum(p, axis=-1, keepdims=True)
        o = jnp.dot(p.astype(BF16), v, preferred_element_type=F32) / z
        o_ref[..., sl] = o.astype(o_ref.dtype).reshape(o_ref.shape[:-1] + (A_HEAD_DIM,))
        lse_all = jnp.where(lane == h, m + jnp.log(z), lse_all)
    lse_ref[...] = lse_all.reshape(lse_ref.shape)


STREAMS = max(d for _, d in A_PATTERNS)


def _stream_of_residue(r):
    return (r % 4) * 4 + r // 4


def _dilated_dense(proj, B, S, window):
    T = B * S
    nb = S // A_BLOCK
    blk = (A_BLOCK, A_WIDTH)
    kern = functools.partial(_dilated_kernel, dilation=1, steps=window, groups=1)
    return pl.pallas_call(
        kern,
        out_shape=(jax.ShapeDtypeStruct((T, A_WIDTH), BF16), jax.ShapeDtypeStruct((T, LANE), F32)),
        grid=(B, 1, nb),
        in_specs=[
            pl.BlockSpec(blk, lambda b, r, n: (b * nb + n, COL_Q // A_WIDTH)),
            pl.BlockSpec(blk, lambda b, r, n: (b * nb + jnp.maximum(n - 1, 0), COL_K // A_WIDTH)),
            pl.BlockSpec(blk, lambda b, r, n: (b * nb + n, COL_K // A_WIDTH)),
            pl.BlockSpec(blk, lambda b, r, n: (b * nb + jnp.maximum(n - 1, 0), COL_V // A_WIDTH)),
            pl.BlockSpec(blk, lambda b, r, n: (b * nb + n, COL_V // A_WIDTH)),
        ],
        out_specs=(
            pl.BlockSpec(blk, lambda b, r, n: (b * nb + n, 0)),
            pl.BlockSpec((A_BLOCK, LANE), lambda b, r, n: (b * nb + n, 0)),
        ),
        compiler_params=_cparams("parallel", "parallel", "arbitrary"),
        name="dilated_attention_d1",
    )(proj, proj, proj, proj, proj)


def _dilated_streams(qkv_s, window, dilation):
    B, ns, Ls, _ = qkv_s.shape
    groups = STREAMS // dilation
    per = A_BLOCK // groups
    assert ns == STREAMS and Ls % per == 0
    blk = (None, groups, per, A_WIDTH)
    kern = functools.partial(_dilated_kernel, dilation=dilation, steps=window // dilation, groups=groups)
    return pl.pallas_call(
        kern,
        out_shape=(
            jax.ShapeDtypeStruct((B, STREAMS, Ls, A_WIDTH), BF16),
            jax.ShapeDtypeStruct((B, STREAMS, Ls, LANE), F32),
        ),
        grid=(B, STREAMS // groups, Ls // per),
        in_specs=[
            pl.BlockSpec(blk, lambda b, r, n: (b, r, n, 0)),
            pl.BlockSpec(blk, lambda b, r, n: (b, r, jnp.maximum(n - 1, 0), 1)),
            pl.BlockSpec(blk, lambda b, r, n: (b, r, n, 1)),
            pl.BlockSpec(blk, lambda b, r, n: (b, r, jnp.maximum(n - 1, 0), 2)),
            pl.BlockSpec(blk, lambda b, r, n: (b, r, n, 2)),
        ],
        out_specs=(
            pl.BlockSpec(blk, lambda b, r, n: (b, r, n, 0)),
            pl.BlockSpec((None, groups, per, LANE), lambda b, r, n: (b, r, n, 0)),
        ),
        compiler_params=_cparams("parallel", "parallel", "arbitrary"),
        name=f"dilated_attention_d{dilation}",
    )(qkv_s, qkv_s, qkv_s, qkv_s, qkv_s)


COMBINE_STEPS = 16


def _combine_kernel(o1_ref, o2_ref, o3_ref, l1_ref, l2_ref, l3_ref, y_ref, ob_ref, oc_ref, lb_ref, lc_ref):
    for r in range(STREAMS):
        s = _stream_of_residue(r)
        tok = pl.ds(r, COMBINE_STEPS, stride=STREAMS)
        lb_ref[tok, :] = l2_ref[s]
        lc_ref[tok, :] = l3_ref[s]
        for h in range(A_HEADS):
            sl = slice(h * A_HEAD_DIM, (h + 1) * A_HEAD_DIM)
            ob_ref[h, tok, :] = o2_ref[s, :, sl].astype(F32)
            oc_ref[h, tok, :] = o3_ref[s, :, sl].astype(F32)
    a, b, c = l1_ref[...], lb_ref[...], lc_ref[...]
    m = jnp.maximum(jnp.maximum(a, b), c)
    ea, eb, ec = jnp.exp(a - m), jnp.exp(b - m), jnp.exp(c - m)
    inv = 1.0 / (ea + eb + ec)
    wa, wb, wc = ea * inv, eb * inv, ec * inv
    for h in range(A_HEADS):
        sl = slice(h * A_HEAD_DIM, (h + 1) * A_HEAD_DIM)
        y = (wa[:, h:h + 1] * o1_ref[:, sl].astype(F32)
             + wb[:, h:h + 1] * ob_ref[h]
             + wc[:, h:h + 1] * oc_ref[h])
        y_ref[:, sl] = y.astype(y_ref.dtype)


def _combine_patterns(o1, l1, o4, l4, o16, l16, B, S):
    T = B * S
    tm = COMBINE_STEPS * STREAMS
    nt = S // tm
    tok_o = pl.BlockSpec((tm, A_WIDTH), lambda b, i: (b * nt + i, 0))
    tok_l = pl.BlockSpec((tm, LANE), lambda b, i: (b * nt + i, 0))
    str_o = pl.BlockSpec((None, STREAMS, COMBINE_STEPS, A_WIDTH), lambda b, i: (b, 0, i, 0))
    str_l = pl.BlockSpec((None, STREAMS, COMBINE_STEPS, LANE), lambda b, i: (b, 0, i, 0))
    return pl.pallas_call(
        _combine_kernel,
        out_shape=jax.ShapeDtypeStruct((T, A_WIDTH), BF16),
        grid=(B, nt),
        in_specs=[tok_o, str_o, str_o, tok_l, str_l, str_l],
        out_specs=tok_o,
        scratch_shapes=[
            pltpu.VMEM((A_HEADS, tm, A_HEAD_DIM), F32),
            pltpu.VMEM((A_HEADS, tm, A_HEAD_DIM), F32),
            pltpu.VMEM((tm, LANE), F32),
            pltpu.VMEM((tm, LANE), F32),
        ],
        compiler_params=_cparams("parallel", "parallel"),
        name="combine_patterns",
    )(o1, o4, o16, l1, l4, l16)


def _rope_lanes(t, cos, sin):
    lane = lax.broadcasted_iota(jnp.int32, t.shape, 1)
    half = QK_ROPE // 2
    rot = jnp.where(lane < half, pltpu.roll(t, LANE - half, 1), pltpu.roll(t, half, 1))
    return t * cos + rot * sin


def _rms_norm(x, g):
    ms = jnp.mean(x * x, axis=-1, keepdims=True)
    return x * lax.rsqrt(ms + RMS_EPS) * g


def _latent_kernel(cq_ref, ckv_ref, kr_ref, gq_ref, gkv_ref, cos_ref, sin_ref, cqn_ref, ckvn_ref, krope_ref):
    cqn_ref[...] = _rms_norm(cq_ref[...].astype(F32), gq_ref[...]).astype(cqn_ref.dtype)
    ckvn_ref[...] = _rms_norm(ckv_ref[...].astype(F32), gkv_ref[...]).astype(ckvn_ref.dtype)
    krope_ref[...] = _rope_lanes(kr_ref[...].astype(F32), cos_ref[...], sin_ref[...]).astype(krope_ref.dtype)


def _latent_prep(proj, gq, gkv, cos_t, sin_t, S):
    T = proj.shape[0]
    tm = min(512, S)
    ns = S // tm
    return pl.pallas_call(
        _latent_kernel,
        out_shape=(
            jax.ShapeDtypeStruct((T, Q_LORA), BF16),
            jax.ShapeDtypeStruct((T, KV_LORA), BF16),
            jax.ShapeDtypeStruct((T, LANE), BF16),
        ),
        grid=(T // tm,),
        in_specs=[
            pl.BlockSpec((tm, Q_LORA), lambda i: (i, COL_CQ // Q_LORA)),
            pl.BlockSpec((tm, KV_LORA), lambda i: (i, COL_CKV // KV_LORA)),
            pl.BlockSpec((tm, LANE), lambda i: (i, COL_KR // LANE)),
            pl.BlockSpec((1, Q_LORA), lambda i: (0, 0)),
            pl.BlockSpec((1, KV_LORA), lambda i: (0, 0)),
            pl.BlockSpec((tm, LANE), lambda i: (i % ns, 0)),
            pl.BlockSpec((tm, LANE), lambda i: (i % ns, 0)),
        ],
        out_specs=(
            pl.BlockSpec((tm, Q_LORA), lambda i: (i, 0)),
            pl.BlockSpec((tm, KV_LORA), lambda i: (i, 0)),
            pl.BlockSpec((tm, LANE), lambda i: (i, 0)),
        ),
        compiler_params=_cparams("parallel"),
        name="latent_prep",
    )(proj, proj, proj, gq, gkv, cos_t, sin_t)


def _qup_kernel(c_ref, w_ref, cos_ref, sin_ref, o_ref, *, scale):
    acc = jnp.dot(c_ref[...], w_ref[...], preferred_element_type=F32) * scale
    cos, sin = cos_ref[...], sin_ref[...]
    for hb in range(acc.shape[1] // MLA_HEAD_PAD):
        lo = hb * MLA_HEAD_PAD
        o_ref[:, lo:lo + QK_NOPE] = acc[:, lo:lo + QK_NOPE].astype(o_ref.dtype)
        r = _rope_lanes(acc[:, lo + QK_NOPE:lo + MLA_HEAD_PAD], cos, sin)
        o_ref[:, lo + QK_NOPE:lo + MLA_HEAD_PAD] = r.astype(o_ref.dtype)


def _q_up(cqn, w_uq_p, cos_t, sin_t, S):
    T = cqn.shape[0]
    N = w_uq_p.shape[1]
    tm, tn = min(512, S), 1024
    ns = S // tm
    kern = functools.partial(_qup_kernel, scale=(QK_NOPE + QK_ROPE) ** -0.5 * math.log2(math.e))
    return pl.pallas_call(
        kern,
        out_shape=jax.ShapeDtypeStruct((T, N), BF16),
        grid=(T // tm, N // tn),
        in_specs=[
            pl.BlockSpec((tm, Q_LORA), lambda i, j: (i, 0)),
            pl.BlockSpec((Q_LORA, tn), lambda i, j: (0, j)),
            pl.BlockSpec((tm, LANE), lambda i, j: (i % ns, 0)),
            pl.BlockSpec((tm, LANE), lambda i, j: (i % ns, 0)),
        ],
        out_specs=pl.BlockSpec((tm, tn), lambda i, j: (i, j)),
        compiler_params=_cparams("parallel", "arbitrary"),
        name="mla_q_up",
    )(cqn, w_uq_p, cos_t, sin_t)


def _mm_kernel(a_ref, w_ref, o_ref):
    o_ref[...] = jnp.dot(a_ref[...], w_ref[...], preferred_element_type=F32).astype(o_ref.dtype)


def _matmul(a, w, name, tm=512, tn=1024):
    M, K = a.shape
    N = w.shape[1]
    tm, tn = min(tm, M), min(tn, N)
    return pl.pallas_call(
        _mm_kernel,
        out_shape=jax.ShapeDtypeStruct((M, N), BF16),
        grid=(M // tm, N // tn),
        in_specs=[
            pl.BlockSpec((tm, K), lambda i, j: (i, 0)),
            pl.BlockSpec((K, tn), lambda i, j: (0, j)),
        ],
        out_specs=pl.BlockSpec((tm, tn), lambda i, j: (i, j)),
        compiler_params=_cparams("parallel", "arbitrary"),
        name=name,
    )(a, w)


def _kvup_kernel(c_ref, wk_ref, wvt_ref, kn_ref, vt_ref):
    c = c_ref[...]
    kn_ref[...] = jnp.dot(c, wk_ref[...], preferred_element_type=F32).astype(kn_ref.dtype)
    vt_ref[...] = lax.dot_general(wvt_ref[...], c, _NT, preferred_element_type=F32).astype(vt_ref.dtype)


def _kv_up(ckvn, wk, wvt):
    T = ckvn.shape[0]
    tm = min(512, T)
    n = B_HEADS * QK_NOPE
    return pl.pallas_call(
        _kvup_kernel,
        out_shape=(jax.ShapeDtypeStruct((T, n), BF16), jax.ShapeDtypeStruct((B_HEADS * V_HEAD, T), BF16)),
        grid=(T // tm,),
        in_specs=[
            pl.BlockSpec((tm, KV_LORA), lambda i: (i, 0)),
            pl.BlockSpec((KV_LORA, n), lambda i: (0, 0)),
            pl.BlockSpec((B_HEADS * V_HEAD, KV_LORA), lambda i: (0, 0)),
        ],
        out_specs=(pl.BlockSpec((tm, n), lambda i: (i, 0)), pl.BlockSpec((B_HEADS * V_HEAD, tm), lambda i: (0, i))),
        compiler_params=_cparams("parallel"),
        name="mla_kv_up",
    )(ckvn, wk, wvt)


def _mla_kernel(q_ref, kn_ref, kr_ref, vt_ref, o_ref, sa_ref, sb_ref, m_ref, l_ref, acc_ref, *, tq):
    qi = pl.program_id(2)
    tk = tq // 2
    q = q_ref[...]
    m_ref[...] = jnp.full(m_ref.shape, NEG, F32)
    l_ref[...] = jnp.zeros(l_ref.shape, F32)
    acc_ref[...] = jnp.zeros(acc_ref.shape, F32)

    def scores(c, s_ref):
        start = pl.multiple_of(c * tk, tk)
        k = jnp.concatenate([kn_ref[pl.ds(start, tk), :], kr_ref[pl.ds(start, tk), :]], axis=1)
        s_ref[...] = lax.dot_general(k, q, _NT, preferred_element_type=F32)

    def update(c, s_ref, masked):
        start = pl.multiple_of(c * tk, tk)
        st = s_ref[...]
        if masked:
            key = lax.broadcasted_iota(jnp.int32, st.shape, 0) + (c * tk - qi * tq)
            qry = lax.broadcasted_iota(jnp.int32, st.shape, 1)
            st = jnp.where(key <= qry, st, NEG)
        m_prev = m_ref[...]
        m_new = jnp.maximum(m_prev, jnp.max(st, axis=0, keepdims=True))
        a = jnp.exp2(m_prev - m_new)
        p = jnp.exp2(st - m_new)
        l_ref[...] = a * l_ref[...] + jnp.sum(p, axis=0, keepdims=True)
        pv = jnp.dot(vt_ref[:, pl.ds(start, tk)], p.astype(BF16), preferred_element_type=F32)
        acc_ref[...] = a * acc_ref[...] + pv
        m_ref[...] = m_new

    scores(0, sa_ref)

    def pair(i, carry):
        c = 2 * i
        scores(c + 1, sb_ref)
        update(c, sa_ref, False)
        scores(c + 2, sa_ref)
        update(c + 1, sb_ref, False)
        return carry

    lax.fori_loop(0, qi, pair, 0)
    c = 2 * qi
    scores(c + 1, sb_ref)
    update(c, sa_ref, True)
    update(c + 1, sb_ref, True)
    o_ref[...] = (acc_ref[...] / l_ref[...]).T.astype(o_ref.dtype)


def _mla_attention(q, kn, krope, vt, B, S):
    T = q.shape[0]
    tq = min(1024, S)
    nq = S // tq
    kern = functools.partial(_mla_kernel, tq=tq)
    return pl.pallas_call(
        kern,
        out_shape=jax.ShapeDtypeStruct((T, B_HEADS * V_HEAD), BF16),
        grid=(B, B_HEADS, nq),
        in_specs=[
            pl.BlockSpec((tq, MLA_HEAD_PAD), lambda b, h, i: (b * nq + i, h)),
            pl.BlockSpec((S, QK_NOPE), lambda b, h, i: (b, h)),
            pl.BlockSpec((S, LANE), lambda b, h, i: (b, 0)),
            pl.BlockSpec((V_HEAD, S), lambda b, h, i: (h, b)),
        ],
        out_specs=pl.BlockSpec((tq, V_HEAD), lambda b, h, i: (b * nq + i, h)),
        scratch_shapes=[
            pltpu.VMEM((tq // 2, tq), F32),
            pltpu.VMEM((tq // 2, tq), F32),
            pltpu.VMEM((1, tq), F32),
            pltpu.VMEM((1, tq), F32),
            pltpu.VMEM((V_HEAD, tq), F32),
        ],
        compiler_params=_cparams("parallel", "parallel", "arbitrary"),
        name="mla_attention",
    )(q, kn, krope, vt)


def _branch_kernel(ya_ref, yb_ref, wa_ref, wb_ref, ga_ref, gb_ref, o_ref):
    pa = jnp.dot(ya_ref[...], wa_ref[...], preferred_element_type=F32)
    pb = jnp.dot(yb_ref[...], wb_ref[...], preferred_element_type=F32)
    u = ga_ref[...].astype(F32) * pa + gb_ref[...].astype(F32) * pb
    o_ref[...] = u.astype(o_ref.dtype)


def _branch_mix(ya, yb, wa, wb, proj):
    T = ya.shape[0]
    tm, tn = min(512, T), 1024
    return pl.pallas_call(
        _branch_kernel,
        out_shape=jax.ShapeDtypeStruct((T, D_MODEL), BF16),
        grid=(T // tm, D_MODEL // tn),
        in_specs=[
            pl.BlockSpec((tm, A_WIDTH), lambda i, j: (i, 0)),
            pl.BlockSpec((tm, B_HEADS * V_HEAD), lambda i, j: (i, 0)),
            pl.BlockSpec((A_WIDTH, tn), lambda i, j: (0, j)),
            pl.BlockSpec((B_HEADS * V_HEAD, tn), lambda i, j: (0, j)),
            pl.BlockSpec((tm, tn), lambda i, j: (i, COL_GA // tn + j)),
            pl.BlockSpec((tm, tn), lambda i, j: (i, COL_GB // tn + j)),
        ],
        out_specs=pl.BlockSpec((tm, tn), lambda i, j: (i, j)),
        compiler_params=_cparams("parallel", "arbitrary"),
        name="branch_mix",
    )(ya, yb, wa, wb, proj, proj)


def _outln_kernel(u_ref, w_ref, x_ref, g_ref, b_ref, h_ref, hb_ref):
    mix = jnp.dot(u_ref[...], w_ref[...], preferred_element_type=F32)
    h = _layer_norm(ALPHA * x_ref[...] + mix, g_ref[...], b_ref[...])
    h_ref[...] = h
    hb_ref[...] = h.astype(hb_ref.dtype)


def _out_projection_ln(u, w_out, x2, g, b):
    T = u.shape[0]
    tm = min(256, T)
    row = pl.BlockSpec((tm, D_MODEL), lambda i: (i, 0))
    vec = pl.BlockSpec((1, D_MODEL), lambda i: (0, 0))
    return pl.pallas_call(
        _outln_kernel,
        out_shape=(
            jax.ShapeDtypeStruct((T, D_MODEL), F32),
            jax.ShapeDtypeStruct((T, D_MODEL), BF16),
        ),
        grid=(T // tm,),
        in_specs=[row, pl.BlockSpec((D_MODEL, D_MODEL), lambda i: (0, 0)), row, vec, vec],
        out_specs=(row, row),
        compiler_params=_cparams("parallel"),
        name="out_projection_ln",
    )(u, w_out, x2, g, b)


def _topk_axis0(s, ids, k):
    big = jnp.int32(2 ** 30)
    vals, idxs = [], []
    for _ in range(k):
        m = jnp.max(s, axis=0, keepdims=True)
        idx = jnp.min(jnp.where(s == m, ids, big), axis=0, keepdims=True)
        vals.append(m)
        idxs.append(idx)
        s = jnp.where(ids == idx, -jnp.inf, s)
    return vals, idxs


def _select_rows(rows, sel):
    out = jnp.zeros(sel.shape, rows[0].dtype)
    for a, r in enumerate(rows):
        out = jnp.where(sel == a, r, out)
    return out


def _route_kernel(q_ref, k1_ref, k2_ref, i1_ref, i2_ref, g_ref):
    half = PEER_QDIM // 2
    k, tm = PEER_TOPK, q_ref.shape[0]
    key_id = lax.broadcasted_iota(jnp.int32, (N_KEYS, tm), 0)
    sub = lax.broadcasted_iota(jnp.int32, (k // 2, tm), 0)
    cand_pos = jnp.concatenate([a * k + sub for a in range(k // 2)] + [k // 2 + sub, (k // 2 + sub) * k], axis=0)
    i1_all, i2_all, g_all = [], [], []
    for h in range(PEER_HEADS):
        q1 = q_ref[:, h * PEER_QDIM:h * PEER_QDIM + half]
        q2 = q_ref[:, h * PEER_QDIM + half:(h + 1) * PEER_QDIM]
        s1 = lax.dot_general(k1_ref[...], q1, _NT, preferred_element_type=F32)
        s2 = lax.dot_general(k2_ref[...], q2, _NT, preferred_element_type=F32)
        v1, i1 = _topk_axis0(s1, key_id, k)
        v2, i2 = _topk_axis0(s2, key_id, k)
        v1m = jnp.concatenate(v1, axis=0)
        v2m = jnp.concatenate(v2, axis=0)
        cand = jnp.concatenate(
            [v1[a] + v2m[:k // 2] for a in range(k // 2)] + [v1[0] + v2m[k // 2:], v1m[k // 2:] + v2[0]], axis=0)
        ts, pos = _topk_axis0(cand, cand_pos, k)
        top = jnp.concatenate(ts, axis=0)
        e = jnp.exp(top - ts[0])
        g_all.append(e / jnp.sum(e, axis=0, keepdims=True))
        posm = jnp.concatenate(pos, axis=0)
        i1_all.append(_select_rows(i1, posm >> int(math.log2(k))))
        i2_all.append(_select_rows(i2, posm & (k - 1)))
    i1_ref[...] = jnp.concatenate(i1_all, axis=0).T
    i2_ref[...] = jnp.concatenate(i2_all, axis=0).T
    g_ref[...] = jnp.concatenate(g_all, axis=0).T


def _peer_route(qp, k1, k2):
    T = qp.shape[0]
    tm = min(256, T)
    slot = pl.BlockSpec((tm, N_SLOTS), lambda i: (i, 0))
    keys = pl.BlockSpec((N_KEYS, PEER_QDIM // 2), lambda i: (0, 0))
    return pl.pallas_call(
        _route_kernel,
        out_shape=(
            jax.ShapeDtypeStruct((T, N_SLOTS), jnp.int32),
            jax.ShapeDtypeStruct((T, N_SLOTS), jnp.int32),
            jax.ShapeDtypeStruct((T, N_SLOTS), F32),
        ),
        grid=(T // tm,),
        in_specs=[pl.BlockSpec((tm, PEER_HEADS * PEER_QDIM), lambda i: (i, 0)), keys, keys],
        out_specs=(slot, slot, slot),
        compiler_params=_cparams("parallel"),
        name="peer_route",
    )(qp, k1, k2)


GATE_GROUP = 16


def _gate_matrix_kernel(i1_ref, i2_ref, g_ref, w_ref):
    key = lax.broadcasted_iota(jnp.int32, (N_KEYS, N_SLOTS), 0)

    def body(tg, carry):
        t0 = pl.multiple_of(tg * GATE_GROUP, GATE_GROUP)
        per_token = []
        for u in range(GATE_GROUP):
            r1 = i1_ref[pl.ds(t0 + u, 1), :]
            r2 = i2_ref[pl.ds(t0 + u, 1), :]
            g = g_ref[pl.ds(t0 + u, 1), :]
            a = jnp.where(key == r1, 1.0, 0.0).astype(BF16)
            b = jnp.where(key == r2, g, 0.0).astype(BF16)
            per_token.append(lax.dot_general(a, b, _NT, preferred_element_type=F32))
        w = pltpu.einshape("tid->itd", jnp.stack(per_token, axis=0))
        w_ref[:, pl.ds(t0, GATE_GROUP), :] = w.astype(w_ref.dtype)
        return carry

    lax.fori_loop(0, w_ref.shape[1] // GATE_GROUP, body, 0)


def _gate_matrix(i1, i2, g):
    T = i1.shape[0]
    tb = min(128, T)
    slot = pl.BlockSpec((tb, N_SLOTS), lambda i: (i, 0))
    return pl.pallas_call(
        _gate_matrix_kernel,
        out_shape=jax.ShapeDtypeStruct((N_KEYS, T, N_KEYS), BF16),
        grid=(T // tb,),
        in_specs=[slot, slot, slot],
        out_specs=pl.BlockSpec((N_KEYS, tb, N_KEYS), lambda i: (0, i, 0)),
        compiler_params=_cparams("parallel"),
        name="peer_gate_matrix",
    )(i1, i2, g)


def _experts_kernel(hb_ref, dn_ref, up_ref, w_ref, h_ref, g_ref, b_ref, o_ref, acc_ref):
    j = pl.program_id(1)

    @pl.when(j == 0)
    def _():
        acc_ref[...] = jnp.zeros(acc_ref.shape, F32)

    pre = jnp.dot(hb_ref[...], dn_ref[...], preferred_element_type=F32)
    act = 0.5 * pre * (1.0 + lax.erf(pre * (2.0 ** -0.5)))
    gates = jnp.concatenate([w_ref[m] for m in range(w_ref.shape[0])], axis=1)
    act = act * gates.astype(F32)
    acc_ref[...] += jnp.dot(act.astype(BF16), up_ref[...], preferred_element_type=F32)

    @pl.when(j == pl.num_programs(1) - 1)
    def _():
        o_ref[...] = _layer_norm(ALPHA * h_ref[...] + acc_ref[...], g_ref[...], b_ref[...])


def _peer_experts(hb, down_t, up, w3, h, g, b):
    T = hb.shape[0]
    tm, te = min(512, T), 1024
    row = pl.BlockSpec((tm, D_MODEL), lambda i, j: (i, 0))
    vec = pl.BlockSpec((1, D_MODEL), lambda i, j: (0, 0))
    return pl.pallas_call(
        _experts_kernel,
        out_shape=jax.ShapeDtypeStruct((T, D_MODEL), F32),
        grid=(T // tm, N_EXPERTS // te),
        in_specs=[
            row,
            pl.BlockSpec((D_MODEL, te), lambda i, j: (0, j)),
            pl.BlockSpec((te, D_MODEL), lambda i, j: (j, 0)),
            pl.BlockSpec((te // N_KEYS, tm, N_KEYS), lambda i, j: (j, i, 0)),
            row, vec, vec,
        ],
        out_specs=row,
        scratch_shapes=[pltpu.VMEM((tm, D_MODEL), F32)],
        compiler_params=_cparams("parallel", "arbitrary"),
        name="peer_experts",
    )(hb, down_t, up, w3, h, g, b)


def _rope_tables(S):
    half = QK_ROPE // 2
    inv_freq = ROPE_THETA ** (-jnp.arange(half, dtype=F32) / half)
    ang = jnp.arange(S, dtype=jnp.int32).astype(F32)[:, None] * inv_freq[None, :]
    cos, sin = jnp.cos(ang), jnp.sin(ang)
    zeros = jnp.zeros((S, LANE - QK_ROPE), F32)
    return (jnp.concatenate([cos, cos, zeros], axis=1), jnp.concatenate([-sin, sin, zeros], axis=1))


def _pack_input_weights(w_in, b_gates):
    wq, wk, wv, wcq, wckv, wkr, wg = jnp.split(
        w_in, (A_WIDTH, 2 * A_WIDTH, 3 * A_WIDTH, 3 * A_WIDTH + Q_LORA,
               3 * A_WIDTH + Q_LORA + KV_LORA, 3 * A_WIDTH + Q_LORA + KV_LORA + QK_ROPE), axis=1)
    pad = jnp.zeros((D_MODEL, PROJ_WIDTH - COL_KR - QK_ROPE), w_in.dtype)
    w_all = jnp.concatenate([wq, wk, wv, wg, wcq, wckv, wkr, pad], axis=1).astype(BF16)
    b_all = jnp.zeros((1, PROJ_WIDTH), F32).at[0, COL_GA:COL_CQ].set(b_gates)
    return w_all, b_all


def _pack_uq(w_uq):
    w = w_uq.reshape(Q_LORA, B_HEADS, QK_NOPE + QK_ROPE)
    w = jnp.pad(w, ((0, 0), (0, 0), (0, MLA_HEAD_PAD - QK_NOPE - QK_ROPE)))
    return w.reshape(Q_LORA, B_HEADS * MLA_HEAD_PAD).astype(BF16)


def kernel(x, w_in, b_gates, a_w_out, mla_q_norm, mla_w_uq, mla_kv_norm, mla_w_ukv, mla_w_out, w_out,
           ln1_g, ln1_b, peer_w_query, peer_sub_keys_1, peer_sub_keys_2, peer_expert_down,
           peer_expert_up, ln2_g, ln2_b):
    B, S, D = x.shape
    assert D == D_MODEL and w_in.shape[0] == DEPTH
    T = B * S
    cos_t, sin_t = _rope_tables(S)
    h = x.reshape(T, D)
    for l in range(DEPTH):
        w_all, b_all = _pack_input_weights(w_in[l], b_gates[l])
        proj = _in_projection(h, w_all, b_all)

        (w1, d1), (w4, d4), (w16, d16) = A_PATTERNS
        assert (d1, d4, d16) == (1, 4, STREAMS)
        qkv_s = proj[:, :3 * A_WIDTH].reshape(B, S // STREAMS, 4, 4, 3 * A_WIDTH)
        qkv_s = qkv_s.transpose(0, 3, 2, 1, 4).reshape(B, STREAMS, S // STREAMS, 3 * A_WIDTH)
        o1, l1 = _dilated_dense(proj, B, S, w1)
        o4, l4 = _dilated_streams(qkv_s, w4, d4)
        o16, l16 = _dilated_streams(qkv_s, w16, d16)
        ya = _combine_patterns(o1, l1, o4, l4, o16, l16, B, S)

        cqn, ckvn, krope = _latent_prep(proj, mla_q_norm[l][None], mla_kv_norm[l][None], cos_t, sin_t, S)
        q = _q_up(cqn, _pack_uq(mla_w_uq[l]), cos_t, sin_t, S)
        w_ukv = mla_w_ukv[l].reshape(KV_LORA, B_HEADS, QK_NOPE + V_HEAD)
        wk = w_ukv[:, :, :QK_NOPE].reshape(KV_LORA, B_HEADS * QK_NOPE).astype(BF16)
        wvt = w_ukv[:, :, QK_NOPE:].reshape(KV_LORA, B_HEADS * V_HEAD).T.astype(BF16)
        kn, vt = _kv_up(ckvn, wk, wvt)
        yb = _mla_attention(q, kn, krope, vt, B, S)

        u = _branch_mix(ya, yb, a_w_out[l].astype(BF16), mla_w_out[l].astype(BF16), proj)
        h1, h1b = _out_projection_ln(u, w_out[l].astype(BF16), h, ln1_g[l][None], ln1_b[l][None])

        qp = _matmul(h1b, peer_w_query[l].astype(BF16), "peer_query")
        i1, i2, g = _peer_route(qp, peer_sub_keys_1[l].astype(BF16), peer_sub_keys_2[l].astype(BF16))
        w3 = _gate_matrix(i1, i2, g)
        h = _peer_experts(h1b, peer_expert_down[l].T.astype(BF16), peer_expert_up[l].astype(BF16),
                          w3, h1, ln2_g[l][None], ln2_b[l][None])
    return h.reshape(B, S, D)
```

```python
import functools
import math

import jax
import jax.numpy as jnp
from jax import lax
from jax.experimental import pallas as pl
from jax.experimental.pallas import tpu as pltpu

F32 = jnp.float32
BF16 = jnp.bfloat16

D_MODEL = 2048
A_HEADS = 16
A_HEAD_DIM = 128
A_PATTERNS = ((128, 1), (512, 4), (2048, 16))
A_BLOCK = 128
A_WIDTH = A_HEADS * A_HEAD_DIM
B_HEADS = 16
Q_LORA = 512
KV_LORA = 512
QK_NOPE = 128
QK_ROPE = 64
V_HEAD = 128
ROPE_THETA = 10000.0
N_KEYS = 128
PEER_HEADS = 8
PEER_QDIM = 256
PEER_TOPK = 16
N_EXPERTS = N_KEYS * N_KEYS
N_SLOTS = PEER_HEADS * PEER_TOPK
LN_EPS = 1e-5
RMS_EPS = 1e-6
DEPTH = 1
ALPHA = (2.0 * DEPTH) ** 0.25
NEG = -1e30

LANE = 128
MLA_HEAD_PAD = 256
VMEM_LIMIT = 56 * 1024 * 1024
PITCH_PAD = LANE

COL_Q, COL_K, COL_V = 0, A_WIDTH, 2 * A_WIDTH
COL_GA = 3 * A_WIDTH
COL_GB = COL_GA + D_MODEL
COL_CQ = COL_GB + D_MODEL
COL_CKV = COL_CQ + Q_LORA
COL_KR = COL_CKV + KV_LORA
PROJ_WIDTH = COL_CQ + 2048

_NT = (((1,), (1,)), ((), ()))


def _cparams(*sem):
    return pltpu.CompilerParams(dimension_semantics=sem, vmem_limit_bytes=VMEM_LIMIT)


def _layer_norm(z, g, b):
    mu = jnp.mean(z, axis=-1, keepdims=True)
    zc = z - mu
    var = jnp.mean(zc * zc, axis=-1, keepdims=True)
    return zc * lax.rsqrt(var + LN_EPS) * g + b


def _inproj_kernel(x_ref, w_ref, b_ref, o_ref, qs_ref, slab_ref, *, qkv_hi, gate_lo, gate_hi):
    j = pl.program_id(1)
    acc = jnp.dot(x_ref[...].astype(BF16), w_ref[...], preferred_element_type=F32)
    is_gate = jnp.logical_and(j >= gate_lo, j < gate_hi)

    @pl.when(is_gate)
    def _():
        o_ref[...] = jax.nn.sigmoid(acc + b_ref[...]).astype(o_ref.dtype)

    @pl.when(jnp.logical_not(is_gate))
    def _():
        o_ref[...] = acc.astype(o_ref.dtype)

    @pl.when(j < qkv_hi)
    def _():
        steps = qs_ref.shape[1]
        for c in range(slab_ref.shape[0]):
            slab_ref[c] = acc[:, c * LANE:(c + 1) * LANE]
        for r in range(STREAMS):
            for c in range(slab_ref.shape[0]):
                rows = slab_ref[c, pl.ds(r, steps, stride=STREAMS), :]
                qs_ref[_stream_of_residue(r), :, c * LANE:(c + 1) * LANE] = rows.astype(qs_ref.dtype)


def _in_projection(x2, w_all, b_all, B, S):
    T = x2.shape[0]
    tm, tn = min(1024, S), 1024
    nt = S // tm
    qkv_hi = COL_GA // tn
    kern = functools.partial(_inproj_kernel, qkv_hi=qkv_hi, gate_lo=COL_GA // tn, gate_hi=COL_CQ // tn)
    return pl.pallas_call(
        kern,
        out_shape=(
            jax.ShapeDtypeStruct((T, PROJ_WIDTH), BF16),
            jax.ShapeDtypeStruct((B, STREAMS, S // STREAMS, 3 * A_WIDTH), BF16),
        ),
        grid=(T // tm, PROJ_WIDTH // tn),
        in_specs=[
            pl.BlockSpec((tm, D_MODEL), lambda i, j: (i, 0)),
            pl.BlockSpec((D_MODEL, tn), lambda i, j: (0, j)),
            pl.BlockSpec((1, tn), lambda i, j: (0, j)),
        ],
        out_specs=(
            pl.BlockSpec((tm, tn), lambda i, j: (i, j)),
            pl.BlockSpec((None, STREAMS, tm // STREAMS, tn),
                         lambda i, j: (i // nt, 0, i % nt, jnp.minimum(j, qkv_hi - 1))),
        ),
        scratch_shapes=[pltpu.VMEM((tn // LANE, tm, LANE), F32)],
        compiler_params=_cparams("parallel", "arbitrary"),
        name="in_projection",
    )(x2, w_all, b_all)


def _block_pos(idx, groups):
    if groups == 1:
        return idx
    per = A_BLOCK // groups
    return groups * (idx % per) + idx // per


def _dilated_kernel(q_ref, kp_ref, ko_ref, vp_ref, vo_ref, o_ref, lse_ref, *, dilation, steps, groups):
    n = pl.program_id(2)
    blk = A_BLOCK
    i = lax.broadcasted_iota(jnp.int32, (blk, 2 * blk), 0)
    c = lax.broadcasted_iota(jnp.int32, (blk, 2 * blk), 1)
    rel = blk + _block_pos(i, groups) - (blk * (c // blk) + _block_pos(c % blk, groups))
    valid = (rel >= 0) & (rel <= steps) & ((c >= blk) | (n > 0))
    dist = (dilation * rel).astype(F32)
    lane = lax.broadcasted_iota(jnp.int32, (blk, LANE), 1)
    lse_all = jnp.zeros((blk, LANE), F32)

    def rows(ref, sl):
        return ref[..., sl].reshape(blk, A_HEAD_DIM)

    for h in range(A_HEADS):
        sl = slice(h * A_HEAD_DIM, (h + 1) * A_HEAD_DIM)
        q = rows(q_ref, sl)
        k = jnp.concatenate([rows(kp_ref, sl), rows(ko_ref, sl)], axis=0)
        v = jnp.concatenate([rows(vp_ref, sl), rows(vo_ref, sl)], axis=0)
        s = lax.dot_general(q, k, _NT, preferred_element_type=F32) * (A_HEAD_DIM ** -0.5)
        slope = 2.0 ** (-8.0 * (h + 1) / A_HEADS)
        logits = jnp.where(valid, s - slope * dist, NEG)
        m = jnp.max(logits, axis=-1, keepdims=True)
        p = jnp.exp(logits - m)
        z = jnp.sum(p, axis=-1, keepdims=True)
        o = jnp.dot(p.astype(BF16), v, preferred_element_type=F32) / z
        o_ref[..., sl] = o.astype(o_ref.dtype).reshape(o_ref.shape[:-1] + (A_HEAD_DIM,))
        lse_all = jnp.where(lane == h, m + jnp.log(z), lse_all)
    lse_ref[...] = lse_all.reshape(lse_ref.shape)


STREAMS = max(d for _, d in A_PATTERNS)


def _stream_of_residue(r):
    return (r % 4) * 4 + r // 4


def _dilated_dense(proj, B, S, window):
    T = B * S
    nb = S // A_BLOCK
    blk = (A_BLOCK, A_WIDTH)
    kern = functools.partial(_dilated_kernel, dilation=1, steps=window, groups=1)
    return pl.pallas_call(
        kern,
        out_shape=(jax.ShapeDtypeStruct((T, A_WIDTH), BF16), jax.ShapeDtypeStruct((T, LANE), F32)),
        grid=(B, 1, nb),
        in_specs=[
            pl.BlockSpec(blk, lambda b, r, n: (b * nb + n, COL_Q // A_WIDTH)),
            pl.BlockSpec(blk, lambda b, r, n: (b * nb + jnp.maximum(n - 1, 0), COL_K // A_WIDTH)),
            pl.BlockSpec(blk, lambda b, r, n: (b * nb + n, COL_K // A_WIDTH)),
            pl.BlockSpec(blk, lambda b, r, n: (b * nb + jnp.maximum(n - 1, 0), COL_V // A_WIDTH)),
            pl.BlockSpec(blk, lambda b, r, n: (b * nb + n, COL_V // A_WIDTH)),
        ],
        out_specs=(
            pl.BlockSpec(blk, lambda b, r, n: (b * nb + n, 0)),
            pl.BlockSpec((A_BLOCK, LANE), lambda b, r, n: (b * nb + n, 0)),
        ),
        compiler_params=_cparams("parallel", "parallel", "arbitrary"),
        name="dilated_attention_d1",
    )(proj, proj, proj, proj, proj)


def _dilated_streams(qkv_s, window, dilation):
    B, ns, Ls, _ = qkv_s.shape
    groups = STREAMS // dilation
    per = A_BLOCK // groups
    assert ns == STREAMS and Ls % per == 0
    blk = (None, groups, per, A_WIDTH)
    kern = functools.partial(_dilated_kernel, dilation=dilation, steps=window // dilation, groups=groups)
    return pl.pallas_call(
        kern,
        out_shape=(
            jax.ShapeDtypeStruct((B, STREAMS, Ls, A_WIDTH), BF16),
            jax.ShapeDtypeStruct((B, STREAMS, Ls, LANE), F32),
        ),
        grid=(B, STREAMS // groups, Ls // per),
        in_specs=[
            pl.BlockSpec(blk, lambda b, r, n: (b, r, n, 0)),
            pl.BlockSpec(blk, lambda b, r, n: (b, r, jnp.maximum(n - 1, 0), 1)),
            pl.BlockSpec(blk, lambda b, r, n: (b, r, n, 1)),
            pl.BlockSpec(blk, lambda b, r, n: (b, r, jnp.maximum(n - 1, 0), 2)),
            pl.BlockSpec(blk, lambda b, r, n: (b, r, n, 2)),
        ],
        out_specs=(
            pl.BlockSpec(blk, lambda b, r, n: (b, r, n, 0)),
            pl.BlockSpec((None, groups, per, LANE), lambda b, r, n: (b, r, n, 0)),
        ),
        compiler_params=_cparams("parallel", "parallel", "arbitrary"),
        name=f"dilated_attention_d{dilation}",
    )(qkv_s, qkv_s, qkv_s, qkv_s, qkv_s)


COMBINE_STEPS = 16


def _combine_kernel(o1_ref, o2_ref, o3_ref, l1_ref, l2_ref, l3_ref, y_ref, ob_ref, oc_ref, lb_ref, lc_ref):
    for r in range(STREAMS):
        s = _stream_of_residue(r)
        tok = pl.ds(r, COMBINE_STEPS, stride=STREAMS)
        lb_ref[tok, :] = l2_ref[s]
        lc_ref[tok, :] = l3_ref[s]
        for h in range(A_HEADS):
            sl = slice(h * A_HEAD_DIM, (h + 1) * A_HEAD_DIM)
            ob_ref[h, tok, :] = o2_ref[s, :, sl].astype(F32)
            oc_ref[h, tok, :] = o3_ref[s, :, sl].astype(F32)
    a, b, c = l1_ref[...], lb_ref[...], lc_ref[...]
    m = jnp.maximum(jnp.maximum(a, b), c)
    ea, eb, ec = jnp.exp(a - m), jnp.exp(b - m), jnp.exp(c - m)
    inv = 1.0 / (ea + eb + ec)
    wa, wb, wc = ea * inv, eb * inv, ec * inv
    for h in range(A_HEADS):
        sl = slice(h * A_HEAD_DIM, (h + 1) * A_HEAD_DIM)
        y = (wa[:, h:h + 1] * o1_ref[:, sl].astype(F32)
             + wb[:, h:h + 1] * ob_ref[h]
             + wc[:, h:h + 1] * oc_ref[h])
        y_ref[:, sl] = y.astype(y_ref.dtype)


def _combine_patterns(o1, l1, o4, l4, o16, l16, B, S):
    T = B * S
    tm = COMBINE_STEPS * STREAMS
    nt = S // tm
    tok_o = pl.BlockSpec((tm, A_WIDTH), lambda b, i: (b * nt + i, 0))
    tok_l = pl.BlockSpec((tm, LANE), lambda b, i: (b * nt + i, 0))
    str_o = pl.BlockSpec((None, STREAMS, COMBINE_STEPS, A_WIDTH), lambda b, i: (b, 0, i, 0))
    str_l = pl.BlockSpec((None, STREAMS, COMBINE_STEPS, LANE), lambda b, i: (b, 0, i, 0))
    return pl.pallas_call(
        _combine_kernel,
        out_shape=jax.ShapeDtypeStruct((T, A_WIDTH), BF16),
        grid=(B, nt),
        in_specs=[tok_o, str_o, str_o, tok_l, str_l, str_l],
        out_specs=tok_o,
        scratch_shapes=[
            pltpu.VMEM((A_HEADS, tm, A_HEAD_DIM), F32),
            pltpu.VMEM((A_HEADS, tm, A_HEAD_DIM), F32),
            pltpu.VMEM((tm, LANE), F32),
            pltpu.VMEM((tm, LANE), F32),
        ],
        compiler_params=_cparams("parallel", "parallel"),
        name="combine_patterns",
    )(o1, o4, o16, l1, l4, l16)


def _rope_lanes(t, cos, sin):
    lane = lax.broadcasted_iota(jnp.int32, t.shape, 1)
    half = QK_ROPE // 2
    rot = jnp.where(lane < half, pltpu.roll(t, LANE - half, 1), pltpu.roll(t, half, 1))
    return t * cos + rot * sin


def _rms_norm(x, g):
    ms = jnp.mean(x * x, axis=-1, keepdims=True)
    return x * lax.rsqrt(ms + RMS_EPS) * g


def _latent_kernel(cq_ref, ckv_ref, kr_ref, gq_ref, gkv_ref, cos_ref, sin_ref, cqn_ref, ckvn_ref, krope_ref):
    cqn_ref[...] = _rms_norm(cq_ref[...].astype(F32), gq_ref[...]).astype(cqn_ref.dtype)
    ckvn_ref[...] = _rms_norm(ckv_ref[...].astype(F32), gkv_ref[...]).astype(ckvn_ref.dtype)
    krope_ref[...] = _rope_lanes(kr_ref[...].astype(F32), cos_ref[...], sin_ref[...]).astype(krope_ref.dtype)


def _latent_prep(proj, gq, gkv, cos_t, sin_t, S):
    T = proj.shape[0]
    tm = min(512, S)
    ns = S // tm
    return pl.pallas_call(
        _latent_kernel,
        out_shape=(
            jax.ShapeDtypeStruct((T, Q_LORA), BF16),
            jax.ShapeDtypeStruct((T, KV_LORA), BF16),
            jax.ShapeDtypeStruct((T, LANE), BF16),
        ),
        grid=(T // tm,),
        in_specs=[
            pl.BlockSpec((tm, Q_LORA), lambda i: (i, COL_CQ // Q_LORA)),
            pl.BlockSpec((tm, KV_LORA), lambda i: (i, COL_CKV // KV_LORA)),
            pl.BlockSpec((tm, LANE), lambda i: (i, COL_KR // LANE)),
            pl.BlockSpec((1, Q_LORA), lambda i: (0, 0)),
            pl.BlockSpec((1, KV_LORA), lambda i: (0, 0)),
            pl.BlockSpec((tm, LANE), lambda i: (i % ns, 0)),
            pl.BlockSpec((tm, LANE), lambda i: (i % ns, 0)),
        ],
        out_specs=(
            pl.BlockSpec((tm, Q_LORA), lambda i: (i, 0)),
            pl.BlockSpec((tm, KV_LORA), lambda i: (i, 0)),
            pl.BlockSpec((tm, LANE), lambda i: (i, 0)),
        ),
        compiler_params=_cparams("parallel"),
        name="latent_prep",
    )(proj, proj, proj, gq, gkv, cos_t, sin_t)


def _qup_kernel(c_ref, w_ref, cos_ref, sin_ref, o_ref, *, scale):
    acc = jnp.dot(c_ref[...], w_ref[...], preferred_element_type=F32) * scale
    cos, sin = cos_ref[...], sin_ref[...]
    for hb in range(acc.shape[1] // MLA_HEAD_PAD):
        lo = hb * MLA_HEAD_PAD
        o_ref[:, lo:lo + QK_NOPE] = acc[:, lo:lo + QK_NOPE].astype(o_ref.dtype)
        r = _rope_lanes(acc[:, lo + QK_NOPE:lo + MLA_HEAD_PAD], cos, sin)
        o_ref[:, lo + QK_NOPE:lo + MLA_HEAD_PAD] = r.astype(o_ref.dtype)


def _q_up(cqn, w_uq_p, cos_t, sin_t, S):
    T = cqn.shape[0]
    N = w_uq_p.shape[1]
    tm, tn = min(512, S), 1024
    ns = S // tm
    kern = functools.partial(_qup_kernel, scale=(QK_NOPE + QK_ROPE) ** -0.5 * math.log2(math.e))
    return pl.pallas_call(
        kern,
        out_shape=jax.ShapeDtypeStruct((T, N), BF16),
        grid=(T // tm, N // tn),
        in_specs=[
            pl.BlockSpec((tm, Q_LORA), lambda i, j: (i, 0)),
            pl.BlockSpec((Q_LORA, tn), lambda i, j: (0, j)),
            pl.BlockSpec((tm, LANE), lambda i, j: (i % ns, 0)),
            pl.BlockSpec((tm, LANE), lambda i, j: (i % ns, 0)),
        ],
        out_specs=pl.BlockSpec((tm, tn), lambda i, j: (i, j)),
        compiler_params=_cparams("parallel", "arbitrary"),
        name="mla_q_up",
    )(cqn, w_uq_p, cos_t, sin_t)


def _mm_kernel(a_ref, w_ref, o_ref):
    o_ref[...] = jnp.dot(a_ref[...], w_ref[...], preferred_element_type=F32).astype(o_ref.dtype)


def _matmul(a, w, name, tm=512, tn=1024):
    M, K = a.shape
    N = w.shape[1]
    tm, tn = min(tm, M), min(tn, N)
    return pl.pallas_call(
        _mm_kernel,
        out_shape=jax.ShapeDtypeStruct((M, N), BF16),
        grid=(M // tm, N // tn),
        in_specs=[
            pl.BlockSpec((tm, K), lambda i, j: (i, 0)),
            pl.BlockSpec((K, tn), lambda i, j: (0, j)),
        ],
        out_specs=pl.BlockSpec((tm, tn), lambda i, j: (i, j)),
        compiler_params=_cparams("parallel", "arbitrary"),
        name=name,
    )(a, w)


def _kvup_kernel(c_ref, wk_ref, wvt_ref, kn_ref, vt_ref):
    c = c_ref[...]
    kn_ref[...] = jnp.dot(c, wk_ref[...], preferred_element_type=F32).astype(kn_ref.dtype)
    vt_ref[...] = lax.dot_general(wvt_ref[...], c, _NT, preferred_element_type=F32).astype(vt_ref.dtype)


def _kv_up(ckvn, wk, wvt):
    T = ckvn.shape[0]
    tm = min(512, T)
    n = B_HEADS * QK_NOPE
    return pl.pallas_call(
        _kvup_kernel,
        out_shape=(jax.ShapeDtypeStruct((T, n), BF16), jax.ShapeDtypeStruct((B_HEADS * V_HEAD, T), BF16)),
        grid=(T // tm,),
        in_specs=[
            pl.BlockSpec((tm, KV_LORA), lambda i: (i, 0)),
            pl.BlockSpec((KV_LORA, n), lambda i: (0, 0)),
            pl.BlockSpec((B_HEADS * V_HEAD, KV_LORA), lambda i: (0, 0)),
        ],
        out_specs=(pl.BlockSpec((tm, n), lambda i: (i, 0)), pl.BlockSpec((B_HEADS * V_HEAD, tm), lambda i: (0, i))),
        compiler_params=_cparams("parallel"),
        name="mla_kv_up",
    )(ckvn, wk, wvt)


def _mla_kernel(q_ref, kn_ref, kr_ref, vt_ref, o_ref, sa_ref, sb_ref, m_ref, l_ref, acc_ref, *, tq):
    qi = pl.program_id(2)
    tk = tq // 2
    q = q_ref[...]
    sa_ref, sb_ref, acc_ref = (r.at[:, pl.ds(0, tq)] for r in (sa_ref, sb_ref, acc_ref))
    m_ref[...] = jnp.full(m_ref.shape, NEG, F32)
    l_ref[...] = jnp.zeros(l_ref.shape, F32)
    acc_ref[...] = jnp.zeros(acc_ref.shape, F32)

    def scores(c, s_ref):
        start = pl.multiple_of(c * tk, tk)
        k = jnp.concatenate([kn_ref[pl.ds(start, tk), :], kr_ref[pl.ds(start, tk), :]], axis=1)
        s_ref[...] = lax.dot_general(k, q, _NT, preferred_element_type=F32)

    def update(c, s_ref, masked):
        start = pl.multiple_of(c * tk, tk)
        st = s_ref[...]
        if masked:
            key = lax.broadcasted_iota(jnp.int32, st.shape, 0) + (c * tk - qi * tq)
            qry = lax.broadcasted_iota(jnp.int32, st.shape, 1)
            st = jnp.where(key <= qry, st, NEG)
        m_prev = m_ref[...]
        m_new = jnp.maximum(m_prev, jnp.max(st, axis=0, keepdims=True))
        a = jnp.exp2(m_prev - m_new)
        p = jnp.exp2(st - m_new)
        l_ref[...] = a * l_ref[...] + jnp.sum(p, axis=0, keepdims=True)
        pv = jnp.dot(vt_ref[:, pl.ds(start, tk)], p.astype(BF16), preferred_element_type=F32)
        acc_ref[...] = a * acc_ref[...] + pv
        m_ref[...] = m_new

    scores(0, sa_ref)

    def pair(i, carry):
        c = 2 * i
        scores(c + 1, sb_ref)
        update(c, sa_ref, False)
        scores(c + 2, sa_ref)
        update(c + 1, sb_ref, False)
        return carry

    lax.fori_loop(0, qi, pair, 0)
    c = 2 * qi
    scores(c + 1, sb_ref)
    update(c, sa_ref, True)
    update(c + 1, sb_ref, True)
    o_ref[...] = (acc_ref[...] / l_ref[...]).T.astype(o_ref.dtype)


def _mla_attention(q, kn, krope, vt, B, S):
    T = q.shape[0]
    tq = min(1024, S)
    nq = S // tq
    kern = functools.partial(_mla_kernel, tq=tq)
    return pl.pallas_call(
        kern,
        out_shape=jax.ShapeDtypeStruct((T, B_HEADS * V_HEAD), BF16),
        grid=(B, B_HEADS, nq),
        in_specs=[
            pl.BlockSpec((tq, MLA_HEAD_PAD), lambda b, h, i: (b * nq + i, h)),
            pl.BlockSpec((S, QK_NOPE), lambda b, h, i: (b, h)),
            pl.BlockSpec((S, LANE), lambda b, h, i: (b, 0)),
            pl.BlockSpec((V_HEAD, S), lambda b, h, i: (h, b)),
        ],
        out_specs=pl.BlockSpec((tq, V_HEAD), lambda b, h, i: (b * nq + i, h)),
        scratch_shapes=[
            pltpu.VMEM((tq // 2, tq + PITCH_PAD), F32),
            pltpu.VMEM((tq // 2, tq + PITCH_PAD), F32),
            pltpu.VMEM((1, tq), F32),
            pltpu.VMEM((1, tq), F32),
            pltpu.VMEM((V_HEAD, tq + PITCH_PAD), F32),
        ],
        compiler_params=_cparams("parallel", "parallel", "arbitrary"),
        name="mla_attention",
    )(q, kn, krope, vt)


def _branch_kernel(ya_ref, yb_ref, wa_ref, wb_ref, ga_ref, gb_ref, o_ref):
    pa = jnp.dot(ya_ref[...], wa_ref[...], preferred_element_type=F32)
    pb = jnp.dot(yb_ref[...], wb_ref[...], preferred_element_type=F32)
    u = ga_ref[...].astype(F32) * pa + gb_ref[...].astype(F32) * pb
    o_ref[...] = u.astype(o_ref.dtype)


def _branch_mix(ya, yb, wa, wb, proj):
    T = ya.shape[0]
    tm, tn = min(512, T), 1024
    return pl.pallas_call(
        _branch_kernel,
        out_shape=jax.ShapeDtypeStruct((T, D_MODEL), BF16),
        grid=(T // tm, D_MODEL // tn),
        in_specs=[
            pl.BlockSpec((tm, A_WIDTH), lambda i, j: (i, 0)),
            pl.BlockSpec((tm, B_HEADS * V_HEAD), lambda i, j: (i, 0)),
            pl.BlockSpec((A_WIDTH, tn), lambda i, j: (0, j)),
            pl.BlockSpec((B_HEADS * V_HEAD, tn), lambda i, j: (0, j)),
            pl.BlockSpec((tm, tn), lambda i, j: (i, COL_GA // tn + j)),
            pl.BlockSpec((tm, tn), lambda i, j: (i, COL_GB // tn + j)),
        ],
        out_specs=pl.BlockSpec((tm, tn), lambda i, j: (i, j)),
        compiler_params=_cparams("parallel", "arbitrary"),
        name="branch_mix",
    )(ya, yb, wa, wb, proj, proj)


def _outln_kernel(u_ref, w_ref, x_ref, g_ref, b_ref, h_ref, hb_ref):
    mix = jnp.dot(u_ref[...], w_ref[...], preferred_element_type=F32)
    h = _layer_norm(ALPHA * x_ref[...] + mix, g_ref[...], b_ref[...])
    h_ref[...] = h
    hb_ref[...] = h.astype(hb_ref.dtype)


def _out_projection_ln(u, w_out, x2, g, b):
    T = u.shape[0]
    tm = min(256, T)
    row = pl.BlockSpec((tm, D_MODEL), lambda i: (i, 0))
    vec = pl.BlockSpec((1, D_MODEL), lambda i: (0, 0))
    return pl.pallas_call(
        _outln_kernel,
        out_shape=(
            jax.ShapeDtypeStruct((T, D_MODEL), F32),
            jax.ShapeDtypeStruct((T, D_MODEL), BF16),
        ),
        grid=(T // tm,),
        in_specs=[row, pl.BlockSpec((D_MODEL, D_MODEL), lambda i: (0, 0)), row, vec, vec],
        out_specs=(row, row),
        compiler_params=_cparams("parallel"),
        name="out_projection_ln",
    )(u, w_out, x2, g, b)


def _topk_axis0(s, ids, k):
    big = jnp.int32(2 ** 30)
    vals, idxs = [], []
    for _ in range(k):
        m = jnp.max(s, axis=0, keepdims=True)
        idx = jnp.min(jnp.where(s == m, ids, big), axis=0, keepdims=True)
        vals.append(m)
        idxs.append(idx)
        s = jnp.where(ids == idx, -jnp.inf, s)
    return vals, idxs


def _select_rows(rows, sel):
    out = jnp.zeros(sel.shape, rows[0].dtype)
    for a, r in enumerate(rows):
        out = jnp.where(sel == a, r, out)
    return out


def _route_kernel(q_ref, k1_ref, k2_ref, i1_ref, i2_ref, g_ref):
    half = PEER_QDIM // 2
    k, tm = PEER_TOPK, q_ref.shape[0]
    key_id = lax.broadcasted_iota(jnp.int32, (N_KEYS, tm), 0)
    sub = lax.broadcasted_iota(jnp.int32, (k // 2, tm), 0)
    cand_pos = jnp.concatenate([a * k + sub for a in range(k // 2)] + [k // 2 + sub, (k // 2 + sub) * k], axis=0)
    i1_all, i2_all, g_all = [], [], []
    for h in range(PEER_HEADS):
        q1 = q_ref[:, h * PEER_QDIM:h * PEER_QDIM + half]
        q2 = q_ref[:, h * PEER_QDIM + half:(h + 1) * PEER_QDIM]
        s1 = lax.dot_general(k1_ref[...], q1, _NT, preferred_element_type=F32)
        s2 = lax.dot_general(k2_ref[...], q2, _NT, preferred_element_type=F32)
        v1, i1 = _topk_axis0(s1, key_id, k)
        v2, i2 = _topk_axis0(s2, key_id, k)
        v1m = jnp.concatenate(v1, axis=0)
        v2m = jnp.concatenate(v2, axis=0)
        cand = jnp.concatenate(
            [v1[a] + v2m[:k // 2] for a in range(k // 2)] + [v1[0] + v2m[k // 2:], v1m[k // 2:] + v2[0]], axis=0)
        ts, pos = _topk_axis0(cand, cand_pos, k)
        top = jnp.concatenate(ts, axis=0)
        e = jnp.exp(top - ts[0])
        g_all.append(e / jnp.sum(e, axis=0, keepdims=True))
        posm = jnp.concatenate(pos, axis=0)
        i1_all.append(_select_rows(i1, posm >> int(math.log2(k))))
        i2_all.append(_select_rows(i2, posm & (k - 1)))
    i1_ref[...] = jnp.concatenate(i1_all, axis=0).T
    i2_ref[...] = jnp.concatenate(i2_all, axis=0).T
    g_ref[...] = jnp.concatenate(g_all, axis=0).T


def _peer_route(qp, k1, k2):
    T = qp.shape[0]
    tm = min(256, T)
    slot = pl.BlockSpec((tm, N_SLOTS), lambda i: (i, 0))
    keys = pl.BlockSpec((N_KEYS, PEER_QDIM // 2), lambda i: (0, 0))
    return pl.pallas_call(
        _route_kernel,
        out_shape=(
            jax.ShapeDtypeStruct((T, N_SLOTS), jnp.int32),
            jax.ShapeDtypeStruct((T, N_SLOTS), jnp.int32),
            jax.ShapeDtypeStruct((T, N_SLOTS), F32),
        ),
        grid=(T // tm,),
        in_specs=[pl.BlockSpec((tm, PEER_HEADS * PEER_QDIM), lambda i: (i, 0)), keys, keys],
        out_specs=(slot, slot, slot),
        compiler_params=_cparams("parallel"),
        name="peer_route",
    )(qp, k1, k2)


GATE_GROUP = 16


def _gate_matrix_kernel(i1_ref, i2_ref, g_ref, w_ref):
    key = lax.broadcasted_iota(jnp.int32, (N_KEYS, N_SLOTS), 0)

    def body(tg, carry):
        t0 = pl.multiple_of(tg * GATE_GROUP, GATE_GROUP)
        per_token = []
        for u in range(GATE_GROUP):
            r1 = i1_ref[pl.ds(t0 + u, 1), :]
            r2 = i2_ref[pl.ds(t0 + u, 1), :]
            g = g_ref[pl.ds(t0 + u, 1), :]
            a = jnp.where(key == r1, 1.0, 0.0).astype(BF16)
            b = jnp.where(key == r2, g, 0.0).astype(BF16)
            per_token.append(lax.dot_general(a, b, _NT, preferred_element_type=F32))
        w = pltpu.einshape("tid->itd", jnp.stack(per_token, axis=0))
        w_ref[:, pl.ds(t0, GATE_GROUP), :] = w.astype(w_ref.dtype)
        return carry

    lax.fori_loop(0, w_ref.shape[1] // GATE_GROUP, body, 0)


def _gate_matrix(i1, i2, g):
    T = i1.shape[0]
    tb = min(128, T)
    slot = pl.BlockSpec((tb, N_SLOTS), lambda i: (i, 0))
    return pl.pallas_call(
        _gate_matrix_kernel,
        out_shape=jax.ShapeDtypeStruct((N_KEYS, T, N_KEYS), BF16),
        grid=(T // tb,),
        in_specs=[slot, slot, slot],
        out_specs=pl.BlockSpec((N_KEYS, tb, N_KEYS), lambda i: (0, i, 0)),
        compiler_params=_cparams("parallel"),
        name="peer_gate_matrix",
    )(i1, i2, g)


def _experts_kernel(hb_ref, dn_ref, up_ref, w_ref, h_ref, g_ref, b_ref, o_ref, acc_ref):
    j = pl.program_id(1)
    acc_ref = acc_ref.at[:, pl.ds(0, D_MODEL)]

    @pl.when(j == 0)
    def _():
        acc_ref[...] = jnp.zeros(acc_ref.shape, F32)

    pre = jnp.dot(hb_ref[...], dn_ref[...], preferred_element_type=F32)
    act = 0.5 * pre * (1.0 + lax.erf(pre * (2.0 ** -0.5)))
    gates = jnp.concatenate([w_ref[m] for m in range(w_ref.shape[0])], axis=1)
    act = act * gates.astype(F32)
    acc_ref[...] += jnp.dot(act.astype(BF16), up_ref[...], preferred_element_type=F32)

    @pl.when(j == pl.num_programs(1) - 1)
    def _():
        o_ref[...] = _layer_norm(ALPHA * h_ref[...] + acc_ref[...], g_ref[...], b_ref[...])


def _peer_experts(hb, down_t, up, w3, h, g, b):
    T = hb.shape[0]
    tm, te = min(512, T), 1024
    row = pl.BlockSpec((tm, D_MODEL), lambda i, j: (i, 0))
    vec = pl.BlockSpec((1, D_MODEL), lambda i, j: (0, 0))
    return pl.pallas_call(
        _experts_kernel,
        out_shape=jax.ShapeDtypeStruct((T, D_MODEL), F32),
        grid=(T // tm, N_EXPERTS // te),
        in_specs=[
            row,
            pl.BlockSpec((D_MODEL, te), lambda i, j: (0, j)),
            pl.BlockSpec((te, D_MODEL), lambda i, j: (j, 0)),
            pl.BlockSpec((te // N_KEYS, tm, N_KEYS), lambda i, j: (j, i, 0)),
            row, vec, vec,
        ],
        out_specs=row,
        scratch_shapes=[pltpu.VMEM((tm, D_MODEL + PITCH_PAD), F32)],
        compiler_params=_cparams("parallel", "arbitrary"),
        name="peer_experts",
    )(hb, down_t, up, w3, h, g, b)


def _rope_tables(S):
    half = QK_ROPE // 2
    inv_freq = ROPE_THETA ** (-jnp.arange(half, dtype=F32) / half)
    ang = jnp.arange(S, dtype=jnp.int32).astype(F32)[:, None] * inv_freq[None, :]
    cos, sin = jnp.cos(ang), jnp.sin(ang)
    zeros = jnp.zeros((S, LANE - QK_ROPE), F32)
    return (jnp.concatenate([cos, cos, zeros], axis=1), jnp.concatenate([-sin, sin, zeros], axis=1))


def _pack_input_weights(w_in, b_gates):
    wq, wk, wv, wcq, wckv, wkr, wg = jnp.split(
        w_in, (A_WIDTH, 2 * A_WIDTH, 3 * A_WIDTH, 3 * A_WIDTH + Q_LORA,
               3 * A_WIDTH + Q_LORA + KV_LORA, 3 * A_WIDTH + Q_LORA + KV_LORA + QK_ROPE), axis=1)
    pad = jnp.zeros((D_MODEL, PROJ_WIDTH - COL_KR - QK_ROPE), w_in.dtype)
    w_all = jnp.concatenate([wq, wk, wv, wg, wcq, wckv, wkr, pad], axis=1).astype(BF16)
    b_all = jnp.zeros((1, PROJ_WIDTH), F32).at[0, COL_GA:COL_CQ].set(b_gates)
    return w_all, b_all


def _pack_uq(w_uq):
    w = w_uq.reshape(Q_LORA, B_HEADS, QK_NOPE + QK_ROPE)
    w = jnp.pad(w, ((0, 0), (0, 0), (0, MLA_HEAD_PAD - QK_NOPE - QK_ROPE)))
    return w.reshape(Q_LORA, B_HEADS * MLA_HEAD_PAD).astype(BF16)


def kernel(x, w_in, b_gates, a_w_out, mla_q_norm, mla_w_uq, mla_kv_norm, mla_w_ukv, mla_w_out, w_out,
           ln1_g, ln1_b, peer_w_query, peer_sub_keys_1, peer_sub_keys_2, peer_expert_down,
           peer_expert_up, ln2_g, ln2_b):
    B, S, D = x.shape
    assert D == D_MODEL and w_in.shape[0] == DEPTH
    T = B * S
    cos_t, sin_t = _rope_tables(S)
    h = x.reshape(T, D)
    for l in range(DEPTH):
        w_all, b_all = _pack_input_weights(w_in[l], b_gates[l])
        proj, qkv_s = _in_projection(h, w_all, b_all, B, S)

        (w1, d1), (w4, d4), (w16, d16) = A_PATTERNS
        assert (d1, d4, d16) == (1, 4, STREAMS)
        o1, l1 = _dilated_dense(proj, B, S, w1)
        o4, l4 = _dilated_streams(qkv_s, w4, d4)
        o16, l16 = _dilated_streams(qkv_s, w16, d16)
        ya = _combine_patterns(o1, l1, o4, l4, o16, l16, B, S)

        cqn, ckvn, krope = _latent_prep(proj, mla_q_norm[l][None], mla_kv_norm[l][None], cos_t, sin_t, S)
        q = _q_up(cqn, _pack_uq(mla_w_uq[l]), cos_t, sin_t, S)
        w_ukv = mla_w_ukv[l].reshape(KV_LORA, B_HEADS, QK_NOPE + V_HEAD)
        wk = w_ukv[:, :, :QK_NOPE].reshape(KV_LORA, B_HEADS * QK_NOPE).astype(BF16)
        wvt = w_ukv[:, :, QK_NOPE:].reshape(KV_LORA, B_HEADS * V_HEAD).T.astype(BF16)
        kn, vt = _kv_up(ckvn, wk, wvt)
        yb = _mla_attention(q, kn, krope, vt, B, S)

        u = _branch_mix(ya, yb, a_w_out[l].astype(BF16), mla_w_out[l].astype(BF16), proj)
        h1, h1b = _out_projection_ln(u, w_out[l].astype(BF16), h, ln1_g[l][None], ln1_b[l][None])

        qp = _matmul(h1b, peer_w_query[l].astype(BF16), "peer_query")
        i1, i2, g = _peer_route(qp, peer_sub_keys_1[l].astype(BF16), peer_sub_keys_2[l].astype(BF16))
        w3 = _gate_matrix(i1, i2, g)
        h = _peer_experts(h1b, peer_expert_down[l].T.astype(BF16), peer_expert_up[l].astype(BF16),
                          w3, h1, ln2_g[l][None], ln2_b[l][None])
    return h.reshape(B, S, D)
```

```python
import functools
import math

import jax
import jax.numpy as jnp
from jax import lax
from jax.experimental import pallas as pl
from jax.experimental.pallas import tpu as pltpu

F32 = jnp.float32
BF16 = jnp.bfloat16

D_MODEL = 2048
A_HEADS = 16
A_HEAD_DIM = 128
A_PATTERNS = ((128, 1), (512, 4), (2048, 16))
A_BLOCK = 128
A_WIDTH = A_HEADS * A_HEAD_DIM
B_HEADS = 16
Q_LORA = 512
KV_LORA = 512
QK_NOPE = 128
QK_ROPE = 64
V_HEAD = 128
ROPE_THETA = 10000.0
N_KEYS = 128
PEER_HEADS = 8
PEER_QDIM = 256
PEER_TOPK = 16
N_EXPERTS = N_KEYS * N_KEYS
N_SLOTS = PEER_HEADS * PEER_TOPK
LN_EPS = 1e-5
RMS_EPS = 1e-6
DEPTH = 1
ALPHA = (2.0 * DEPTH) ** 0.25
NEG = -1e30

LANE = 128
MLA_HEAD_PAD = 256
VMEM_LIMIT = 56 * 1024 * 1024
PITCH_PAD = LANE

COL_Q, COL_K, COL_V = 0, A_WIDTH, 2 * A_WIDTH
COL_GA = 3 * A_WIDTH
COL_GB = COL_GA + D_MODEL
COL_CQ = COL_GB + D_MODEL
COL_CKV = COL_CQ + Q_LORA
COL_KR = COL_CKV + KV_LORA
PROJ_WIDTH = COL_CQ + 2048

_NT = (((1,), (1,)), ((), ()))


def _cparams(*sem):
    return pltpu.CompilerParams(dimension_semantics=sem, vmem_limit_bytes=VMEM_LIMIT)


def _layer_norm(z, g, b):
    mu = jnp.mean(z, axis=-1, keepdims=True)
    zc = z - mu
    var = jnp.mean(zc * zc, axis=-1, keepdims=True)
    return zc * lax.rsqrt(var + LN_EPS) * g + b


def _inproj_kernel(x_ref, w_ref, b_ref, o_ref, qs_ref, slab_ref, *, qkv_hi, gate_lo, gate_hi):
    j = pl.program_id(1)
    acc = jnp.dot(x_ref[...].astype(BF16), w_ref[...], preferred_element_type=F32)
    is_gate = jnp.logical_and(j >= gate_lo, j < gate_hi)

    @pl.when(is_gate)
    def _():
        o_ref[...] = jax.nn.sigmoid(acc + b_ref[...]).astype(o_ref.dtype)

    @pl.when(jnp.logical_not(is_gate))
    def _():
        o_ref[...] = acc.astype(o_ref.dtype)

    @pl.when(j < qkv_hi)
    def _():
        steps = qs_ref.shape[1]
        for c in range(slab_ref.shape[0]):
            slab_ref[c] = acc[:, c * LANE:(c + 1) * LANE]
        for r in range(STREAMS):
            for c in range(slab_ref.shape[0]):
                rows = slab_ref[c, pl.ds(r, steps, stride=STREAMS), :]
                qs_ref[_stream_of_residue(r), :, c * LANE:(c + 1) * LANE] = rows.astype(qs_ref.dtype)


def _in_projection(x2, w_all, b_all, B, S):
    T = x2.shape[0]
    tm, tn = min(1024, S), 1024
    nt = S // tm
    qkv_hi = COL_GA // tn
    kern = functools.partial(_inproj_kernel, qkv_hi=qkv_hi, gate_lo=COL_GA // tn, gate_hi=COL_CQ // tn)
    return pl.pallas_call(
        kern,
        out_shape=(
            jax.ShapeDtypeStruct((T, PROJ_WIDTH), BF16),
            jax.ShapeDtypeStruct((B, STREAMS, S // STREAMS, 3 * A_WIDTH), BF16),
        ),
        grid=(T // tm, PROJ_WIDTH // tn),
        in_specs=[
            pl.BlockSpec((tm, D_MODEL), lambda i, j: (i, 0)),
            pl.BlockSpec((D_MODEL, tn), lambda i, j: (0, j)),
            pl.BlockSpec((1, tn), lambda i, j: (0, j)),
        ],
        out_specs=(
            pl.BlockSpec((tm, tn), lambda i, j: (i, j)),
            pl.BlockSpec((None, STREAMS, tm // STREAMS, tn),
                         lambda i, j: (i // nt, 0, i % nt, jnp.minimum(j, qkv_hi - 1))),
        ),
        scratch_shapes=[pltpu.VMEM((tn // LANE, tm, LANE), F32)],
        compiler_params=_cparams("parallel", "arbitrary"),
        name="in_projection",
    )(x2, w_all, b_all)


def _block_pos(idx, groups):
    if groups == 1:
        return idx
    per = A_BLOCK // groups
    return groups * (idx % per) + idx // per


def _dilated_kernel(q_ref, kp_ref, ko_ref, vp_ref, vo_ref, o_ref, lse_ref, *, dilation, steps, groups):
    n = pl.program_id(2)
    blk = A_BLOCK
    i = lax.broadcasted_iota(jnp.int32, (blk, 2 * blk), 0)
    c = lax.broadcasted_iota(jnp.int32, (blk, 2 * blk), 1)
    rel = blk + _block_pos(i, groups) - (blk * (c // blk) + _block_pos(c % blk, groups))
    valid = (rel >= 0) & (rel <= steps) & ((c >= blk) | (n > 0))
    dist = (dilation * rel).astype(F32)
    lane = lax.broadcasted_iota(jnp.int32, (blk, LANE), 1)
    lse_all = jnp.zeros((blk, LANE), F32)

    def rows(ref, sl):
        return ref[..., sl].reshape(blk, A_HEAD_DIM)

    for h in range(A_HEADS):
        sl = slice(h * A_HEAD_DIM, (h + 1) * A_HEAD_DIM)
        q = rows(q_ref, sl)
        k = jnp.concatenate([rows(kp_ref, sl), rows(ko_ref, sl)], axis=0)
        v = jnp.concatenate([rows(vp_ref, sl), rows(vo_ref, sl)], axis=0)
        s = lax.dot_general(q, k, _NT, preferred_element_type=F32) * (A_HEAD_DIM ** -0.5)
        slope = 2.0 ** (-8.0 * (h + 1) / A_HEADS)
        logits = jnp.where(valid, s - slope * dist, NEG)
        m = jnp.max(logits, axis=-1, keepdims=True)
        p = jnp.exp(logits - m)
        z = jnp.sum(p, axis=-1, keepdims=True)
        o = jnp.dot(p.astype(BF16), v, preferred_element_type=F32) / z
        o_ref[..., sl] = o.astype(o_ref.dtype).reshape(o_ref.shape[:-1] + (A_HEAD_DIM,))
        lse_all = jnp.where(lane == h, m + jnp.log(z), lse_all)
    lse_ref[...] = lse_all.reshape(lse_ref.shape)


STREAMS = max(d for _, d in A_PATTERNS)


def _stream_of_residue(r):
    return (r % 4) * 4 + r // 4


def _dilated_dense(proj, B, S, window):
    T = B * S
    nb = S // A_BLOCK
    blk = (A_BLOCK, A_WIDTH)
    kern = functools.partial(_dilated_kernel, dilation=1, steps=window, groups=1)
    return pl.pallas_call(
        kern,
        out_shape=(jax.ShapeDtypeStruct((T, A_WIDTH), BF16), jax.ShapeDtypeStruct((T, LANE), F32)),
        grid=(B, 1, nb),
        in_specs=[
            pl.BlockSpec(blk, lambda b, r, n: (b * nb + n, COL_Q // A_WIDTH)),
            pl.BlockSpec(blk, lambda b, r, n: (b * nb + jnp.maximum(n - 1, 0), COL_K // A_WIDTH)),
            pl.BlockSpec(blk, lambda b, r, n: (b * nb + n, COL_K // A_WIDTH)),
            pl.BlockSpec(blk, lambda b, r, n: (b * nb + jnp.maximum(n - 1, 0), COL_V // A_WIDTH)),
            pl.BlockSpec(blk, lambda b, r, n: (b * nb + n, COL_V // A_WIDTH)),
        ],
        out_specs=(
            pl.BlockSpec(blk, lambda b, r, n: (b * nb + n, 0)),
            pl.BlockSpec((A_BLOCK, LANE), lambda b, r, n: (b * nb + n, 0)),
        ),
        compiler_params=_cparams("parallel", "parallel", "arbitrary"),
        name="dilated_attention_d1",
    )(proj, proj, proj, proj, proj)


def _dilated_streams(qkv_s, window, dilation):
    B, ns, Ls, _ = qkv_s.shape
    groups = STREAMS // dilation
    per = A_BLOCK // groups
    assert ns == STREAMS and Ls % per == 0
    blk = (None, groups, per, A_WIDTH)
    kern = functools.partial(_dilated_kernel, dilation=dilation, steps=window // dilation, groups=groups)
    return pl.pallas_call(
        kern,
        out_shape=(
            jax.ShapeDtypeStruct((B, STREAMS, Ls, A_WIDTH), BF16),
            jax.ShapeDtypeStruct((B, STREAMS, Ls, LANE), F32),
        ),
        grid=(B, STREAMS // groups, Ls // per),
        in_specs=[
            pl.BlockSpec(blk, lambda b, r, n: (b, r, n, 0)),
            pl.BlockSpec(blk, lambda b, r, n: (b, r, jnp.maximum(n - 1, 0), 1)),
            pl.BlockSpec(blk, lambda b, r, n: (b, r, n, 1)),
            pl.BlockSpec(blk, lambda b, r, n: (b, r, jnp.maximum(n - 1, 0), 2)),
            pl.BlockSpec(blk, lambda b, r, n: (b, r, n, 2)),
        ],
        out_specs=(
            pl.BlockSpec(blk, lambda b, r, n: (b, r, n, 0)),
            pl.BlockSpec((None, groups, per, LANE), lambda b, r, n: (b, r, n, 0)),
        ),
        compiler_params=_cparams("parallel", "parallel", "arbitrary"),
        name=f"dilated_attention_d{dilation}",
    )(qkv_s, qkv_s, qkv_s, qkv_s, qkv_s)


COMBINE_STEPS = 16


def _combine_kernel(o1_ref, o2_ref, o3_ref, l1_ref, l2_ref, l3_ref, y_ref, ob_ref, oc_ref, lb_ref, lc_ref):
    for r in range(STREAMS):
        s = _stream_of_residue(r)
        tok = pl.ds(r, COMBINE_STEPS, stride=STREAMS)
        lb_ref[tok, :] = l2_ref[s]
        lc_ref[tok, :] = l3_ref[s]
        for h in range(A_HEADS):
            sl = slice(h * A_HEAD_DIM, (h + 1) * A_HEAD_DIM)
            ob_ref[h, tok, :] = o2_ref[s, :, sl].astype(F32)
            oc_ref[h, tok, :] = o3_ref[s, :, sl].astype(F32)
    a, b, c = l1_ref[...], lb_ref[...], lc_ref[...]
    m = jnp.maximum(jnp.maximum(a, b), c)
    ea, eb, ec = jnp.exp(a - m), jnp.exp(b - m), jnp.exp(c - m)
    inv = 1.0 / (ea + eb + ec)
    wa, wb, wc = ea * inv, eb * inv, ec * inv
    for h in range(A_HEADS):
        sl = slice(h * A_HEAD_DIM, (h + 1) * A_HEAD_DIM)
        y = (wa[:, h:h + 1] * o1_ref[:, sl].astype(F32)
             + wb[:, h:h + 1] * ob_ref[h]
             + wc[:, h:h + 1] * oc_ref[h])
        y_ref[:, sl] = y.astype(y_ref.dtype)


def _combine_patterns(o1, l1, o4, l4, o16, l16, B, S):
    T = B * S
    tm = COMBINE_STEPS * STREAMS
    nt = S // tm
    tok_o = pl.BlockSpec((tm, A_WIDTH), lambda b, i: (b * nt + i, 0))
    tok_l = pl.BlockSpec((tm, LANE), lambda b, i: (b * nt + i, 0))
    str_o = pl.BlockSpec((None, STREAMS, COMBINE_STEPS, A_WIDTH), lambda b, i: (b, 0, i, 0))
    str_l = pl.BlockSpec((None, STREAMS, COMBINE_STEPS, LANE), lambda b, i: (b, 0, i, 0))
    return pl.pallas_call(
        _combine_kernel,
        out_shape=jax.ShapeDtypeStruct((T, A_WIDTH), BF16),
        grid=(B, nt),
        in_specs=[tok_o, str_o, str_o, tok_l, str_l, str_l],
        out_specs=tok_o,
        scratch_shapes=[
            pltpu.VMEM((A_HEADS, tm, A_HEAD_DIM), F32),
            pltpu.VMEM((A_HEADS, tm, A_HEAD_DIM), F32),
            pltpu.VMEM((tm, LANE), F32),
            pltpu.VMEM((tm, LANE), F32),
        ],
        compiler_params=_cparams("parallel", "parallel"),
        name="combine_patterns",
    )(o1, o4, o16, l1, l4, l16)


def _rope_lanes(t, cos, sin):
    lane = lax.broadcasted_iota(jnp.int32, t.shape, 1)
    half = QK_ROPE // 2
    rot = jnp.where(lane < half, pltpu.roll(t, LANE - half, 1), pltpu.roll(t, half, 1))
    return t * cos + rot * sin


def _rms_norm(x, g):
    ms = jnp.mean(x * x, axis=-1, keepdims=True)
    return x * lax.rsqrt(ms + RMS_EPS) * g


def _latent_kernel(cq_ref, ckv_ref, kr_ref, gq_ref, gkv_ref, cos_ref, sin_ref, cqn_ref, ckvn_ref, krope_ref):
    cqn_ref[...] = _rms_norm(cq_ref[...].astype(F32), gq_ref[...]).astype(cqn_ref.dtype)
    ckvn_ref[...] = _rms_norm(ckv_ref[...].astype(F32), gkv_ref[...]).astype(ckvn_ref.dtype)
    krope_ref[...] = _rope_lanes(kr_ref[...].astype(F32), cos_ref[...], sin_ref[...]).astype(krope_ref.dtype)


def _latent_prep(proj, gq, gkv, cos_t, sin_t, S):
    T = proj.shape[0]
    tm = min(512, S)
    ns = S // tm
    return pl.pallas_call(
        _latent_kernel,
        out_shape=(
            jax.ShapeDtypeStruct((T, Q_LORA), BF16),
            jax.ShapeDtypeStruct((T, KV_LORA), BF16),
            jax.ShapeDtypeStruct((T, LANE), BF16),
        ),
        grid=(T // tm,),
        in_specs=[
            pl.BlockSpec((tm, Q_LORA), lambda i: (i, COL_CQ // Q_LORA)),
            pl.BlockSpec((tm, KV_LORA), lambda i: (i, COL_CKV // KV_LORA)),
            pl.BlockSpec((tm, LANE), lambda i: (i, COL_KR // LANE)),
            pl.BlockSpec((1, Q_LORA), lambda i: (0, 0)),
            pl.BlockSpec((1, KV_LORA), lambda i: (0, 0)),
            pl.BlockSpec((tm, LANE), lambda i: (i % ns, 0)),
            pl.BlockSpec((tm, LANE), lambda i: (i % ns, 0)),
        ],
        out_specs=(
            pl.BlockSpec((tm, Q_LORA), lambda i: (i, 0)),
            pl.BlockSpec((tm, KV_LORA), lambda i: (i, 0)),
            pl.BlockSpec((tm, LANE), lambda i: (i, 0)),
        ),
        compiler_params=_cparams("parallel"),
        name="latent_prep",
    )(proj, proj, proj, gq, gkv, cos_t, sin_t)


def _qup_kernel(c_ref, wt_ref, cos_ref, sin_ref, o_ref, *, scale):
    acc = lax.dot_general(wt_ref[...], c_ref[...], _NT, preferred_element_type=F32) * scale
    cos, sin = cos_ref[...], sin_ref[...]
    half = QK_ROPE // 2
    for hb in range(acc.shape[0] // MLA_HEAD_PAD):
        lo = hb * MLA_HEAD_PAD
        r1 = acc[lo + QK_NOPE:lo + QK_NOPE + half]
        r2 = acc[lo + QK_NOPE + half:lo + QK_NOPE + QK_ROPE]
        o_ref[lo:lo + QK_NOPE] = acc[lo:lo + QK_NOPE].astype(o_ref.dtype)
        o_ref[lo + QK_NOPE:lo + QK_NOPE + half] = (r1 * cos - r2 * sin).astype(o_ref.dtype)
        o_ref[lo + QK_NOPE + half:lo + QK_NOPE + QK_ROPE] = (r2 * cos + r1 * sin).astype(o_ref.dtype)
        o_ref[lo + QK_NOPE + QK_ROPE:lo + MLA_HEAD_PAD] = acc[lo + QK_NOPE + QK_ROPE:lo + MLA_HEAD_PAD].astype(o_ref.dtype)


def _q_up(cqn, w_uq_pt, cos_rt, sin_rt, S):
    T = cqn.shape[0]
    N = w_uq_pt.shape[0]
    tm, tn = min(512, S), 1024
    ns = S // tm
    half = QK_ROPE // 2
    kern = functools.partial(_qup_kernel, scale=(QK_NOPE + QK_ROPE) ** -0.5 * math.log2(math.e))
    return pl.pallas_call(
        kern,
        out_shape=jax.ShapeDtypeStruct((N, T), BF16),
        grid=(T // tm, N // tn),
        in_specs=[
            pl.BlockSpec((tm, Q_LORA), lambda i, j: (i, 0)),
            pl.BlockSpec((tn, Q_LORA), lambda i, j: (j, 0)),
            pl.BlockSpec((half, tm), lambda i, j: (0, i % ns)),
            pl.BlockSpec((half, tm), lambda i, j: (0, i % ns)),
        ],
        out_specs=pl.BlockSpec((tn, tm), lambda i, j: (j, i)),
        compiler_params=_cparams("parallel", "arbitrary"),
        name="mla_q_up",
    )(cqn, w_uq_pt, cos_rt, sin_rt)


def _mm_kernel(a_ref, w_ref, o_ref):
    o_ref[...] = jnp.dot(a_ref[...], w_ref[...], preferred_element_type=F32).astype(o_ref.dtype)


def _matmul(a, w, name, tm=512, tn=1024):
    M, K = a.shape
    N = w.shape[1]
    tm, tn = min(tm, M), min(tn, N)
    return pl.pallas_call(
        _mm_kernel,
        out_shape=jax.ShapeDtypeStruct((M, N), BF16),
        grid=(M // tm, N // tn),
        in_specs=[
            pl.BlockSpec((tm, K), lambda i, j: (i, 0)),
            pl.BlockSpec((K, tn), lambda i, j: (0, j)),
        ],
        out_specs=pl.BlockSpec((tm, tn), lambda i, j: (i, j)),
        compiler_params=_cparams("parallel", "arbitrary"),
        name=name,
    )(a, w)


def _kvup_kernel(c_ref, wk_ref, wvt_ref, kn_ref, vt_ref):
    c = c_ref[...]
    kn_ref[...] = jnp.dot(c, wk_ref[...], preferred_element_type=F32).astype(kn_ref.dtype)
    vt_ref[...] = lax.dot_general(wvt_ref[...], c, _NT, preferred_element_type=F32).astype(vt_ref.dtype)


def _kv_up(ckvn, wk, wvt):
    T = ckvn.shape[0]
    tm = min(512, T)
    n = B_HEADS * QK_NOPE
    return pl.pallas_call(
        _kvup_kernel,
        out_shape=(jax.ShapeDtypeStruct((T, n), BF16), jax.ShapeDtypeStruct((B_HEADS * V_HEAD, T), BF16)),
        grid=(T // tm,),
        in_specs=[
            pl.BlockSpec((tm, KV_LORA), lambda i: (i, 0)),
            pl.BlockSpec((KV_LORA, n), lambda i: (0, 0)),
            pl.BlockSpec((B_HEADS * V_HEAD, KV_LORA), lambda i: (0, 0)),
        ],
        out_specs=(pl.BlockSpec((tm, n), lambda i: (i, 0)), pl.BlockSpec((B_HEADS * V_HEAD, tm), lambda i: (0, i))),
        compiler_params=_cparams("parallel"),
        name="mla_kv_up",
    )(ckvn, wk, wvt)


def _mla_kernel(q_ref, kn_ref, kr_ref, vt_ref, o_ref, sa_ref, sb_ref, m_ref, l_ref, acc_ref, *, tq):
    qi = pl.program_id(2)
    tk = tq // 2
    q = q_ref[...]
    sa_ref, sb_ref, acc_ref = (r.at[:, pl.ds(0, tq)] for r in (sa_ref, sb_ref, acc_ref))
    m_ref[...] = jnp.full(m_ref.shape, NEG, F32)
    l_ref[...] = jnp.zeros(l_ref.shape, F32)
    acc_ref[...] = jnp.zeros(acc_ref.shape, F32)

    def scores(c, s_ref):
        start = pl.multiple_of(c * tk, tk)
        k = jnp.concatenate([kn_ref[pl.ds(start, tk), :], kr_ref[pl.ds(start, tk), :]], axis=1)
        s_ref[...] = jnp.dot(k, q, preferred_element_type=F32)

    def update(c, s_ref, masked):
        start = pl.multiple_of(c * tk, tk)
        st = s_ref[...]
        if masked:
            key = lax.broadcasted_iota(jnp.int32, st.shape, 0) + (c * tk - qi * tq)
            qry = lax.broadcasted_iota(jnp.int32, st.shape, 1)
            st = jnp.where(key <= qry, st, NEG)
        m_prev = m_ref[...]
        m_new = jnp.maximum(m_prev, jnp.max(st, axis=0, keepdims=True))
        a = jnp.exp2(m_prev - m_new)
        p = jnp.exp2(st - m_new)
        l_ref[...] = a * l_ref[...] + jnp.sum(p, axis=0, keepdims=True)
        pv = jnp.dot(vt_ref[:, pl.ds(start, tk)], p.astype(BF16), preferred_element_type=F32)
        acc_ref[...] = a * acc_ref[...] + pv
        m_ref[...] = m_new

    scores(0, sa_ref)

    def pair(i, carry):
        c = 2 * i
        scores(c + 1, sb_ref)
        update(c, sa_ref, False)
        scores(c + 2, sa_ref)
        update(c + 1, sb_ref, False)
        return carry

    lax.fori_loop(0, qi, pair, 0)
    c = 2 * qi
    scores(c + 1, sb_ref)
    update(c, sa_ref, True)
    update(c + 1, sb_ref, True)
    o_ref[...] = (acc_ref[...] / l_ref[...]).T.astype(o_ref.dtype)


def _mla_attention(qt, kn, krope, vt, B, S):
    T = qt.shape[1]
    tq = min(1024, S)
    nq = S // tq
    kern = functools.partial(_mla_kernel, tq=tq)
    return pl.pallas_call(
        kern,
        out_shape=jax.ShapeDtypeStruct((T, B_HEADS * V_HEAD), BF16),
        grid=(B, B_HEADS, nq),
        in_specs=[
            pl.BlockSpec((MLA_HEAD_PAD, tq), lambda b, h, i: (h, b * nq + i)),
            pl.BlockSpec((S, QK_NOPE), lambda b, h, i: (b, h)),
            pl.BlockSpec((S, LANE), lambda b, h, i: (b, 0)),
            pl.BlockSpec((V_HEAD, S), lambda b, h, i: (h, b)),
        ],
        out_specs=pl.BlockSpec((tq, V_HEAD), lambda b, h, i: (b * nq + i, h)),
        scratch_shapes=[
            pltpu.VMEM((tq // 2, tq + PITCH_PAD), F32),
            pltpu.VMEM((tq // 2, tq + PITCH_PAD), F32),
            pltpu.VMEM((1, tq), F32),
            pltpu.VMEM((1, tq), F32),
            pltpu.VMEM((V_HEAD, tq + PITCH_PAD), F32),
        ],
        compiler_params=_cparams("parallel", "parallel", "arbitrary"),
        name="mla_attention",
    )(qt, kn, krope, vt)


def _branch_kernel(ya_ref, yb_ref, wa_ref, wb_ref, ga_ref, gb_ref, o_ref):
    pa = jnp.dot(ya_ref[...], wa_ref[...], preferred_element_type=F32)
    pb = jnp.dot(yb_ref[...], wb_ref[...], preferred_element_type=F32)
    u = ga_ref[...].astype(F32) * pa + gb_ref[...].astype(F32) * pb
    o_ref[...] = u.astype(o_ref.dtype)


def _branch_mix(ya, yb, wa, wb, proj):
    T = ya.shape[0]
    tm, tn = min(512, T), 1024
    return pl.pallas_call(
        _branch_kernel,
        out_shape=jax.ShapeDtypeStruct((T, D_MODEL), BF16),
        grid=(T // tm, D_MODEL // tn),
        in_specs=[
            pl.BlockSpec((tm, A_WIDTH), lambda i, j: (i, 0)),
            pl.BlockSpec((tm, B_HEADS * V_HEAD), lambda i, j: (i, 0)),
            pl.BlockSpec((A_WIDTH, tn), lambda i, j: (0, j)),
            pl.BlockSpec((B_HEADS * V_HEAD, tn), lambda i, j: (0, j)),
            pl.BlockSpec((tm, tn), lambda i, j: (i, COL_GA // tn + j)),
            pl.BlockSpec((tm, tn), lambda i, j: (i, COL_GB // tn + j)),
        ],
        out_specs=pl.BlockSpec((tm, tn), lambda i, j: (i, j)),
        compiler_params=_cparams("parallel", "arbitrary"),
        name="branch_mix",
    )(ya, yb, wa, wb, proj, proj)


def _outln_kernel(u_ref, w_ref, x_ref, g_ref, b_ref, h_ref, hb_ref):
    mix = jnp.dot(u_ref[...], w_ref[...], preferred_element_type=F32)
    h = _layer_norm(ALPHA * x_ref[...] + mix, g_ref[...], b_ref[...])
    h_ref[...] = h
    hb_ref[...] = h.astype(hb_ref.dtype)


def _out_projection_ln(u, w_out, x2, g, b):
    T = u.shape[0]
    tm = min(256, T)
    row = pl.BlockSpec((tm, D_MODEL), lambda i: (i, 0))
    vec = pl.BlockSpec((1, D_MODEL), lambda i: (0, 0))
    return pl.pallas_call(
        _outln_kernel,
        out_shape=(
            jax.ShapeDtypeStruct((T, D_MODEL), F32),
            jax.ShapeDtypeStruct((T, D_MODEL), BF16),
        ),
        grid=(T // tm,),
        in_specs=[row, pl.BlockSpec((D_MODEL, D_MODEL), lambda i: (0, 0)), row, vec, vec],
        out_specs=(row, row),
        compiler_params=_cparams("parallel"),
        name="out_projection_ln",
    )(u, w_out, x2, g, b)


def _topk_axis0(s, ids, k):
    big = jnp.int32(2 ** 30)
    vals, idxs = [], []
    for _ in range(k):
        m = jnp.max(s, axis=0, keepdims=True)
        idx = jnp.min(jnp.where(s == m, ids, big), axis=0, keepdims=True)
        vals.append(m)
        idxs.append(idx)
        s = jnp.where(ids == idx, -jnp.inf, s)
    return vals, idxs


def _select_rows(rows, sel):
    out = jnp.zeros(sel.shape, rows[0].dtype)
    for a, r in enumerate(rows):
        out = jnp.where(sel == a, r, out)
    return out


def _route_kernel(q_ref, k1_ref, k2_ref, i1_ref, i2_ref, g_ref):
    half = PEER_QDIM // 2
    k, tm = PEER_TOPK, q_ref.shape[0]
    key_id = lax.broadcasted_iota(jnp.int32, (N_KEYS, tm), 0)
    sub = lax.broadcasted_iota(jnp.int32, (k // 2, tm), 0)
    cand_pos = jnp.concatenate([a * k + sub for a in range(k // 2)] + [k // 2 + sub, (k // 2 + sub) * k], axis=0)
    i1_all, i2_all, g_all = [], [], []
    for h in range(PEER_HEADS):
        q1 = q_ref[:, h * PEER_QDIM:h * PEER_QDIM + half]
        q2 = q_ref[:, h * PEER_QDIM + half:(h + 1) * PEER_QDIM]
        s1 = lax.dot_general(k1_ref[...], q1, _NT, preferred_element_type=F32)
        s2 = lax.dot_general(k2_ref[...], q2, _NT, preferred_element_type=F32)
        v1, i1 = _topk_axis0(s1, key_id, k)
        v2, i2 = _topk_axis0(s2, key_id, k)
        v1m = jnp.concatenate(v1, axis=0)
        v2m = jnp.concatenate(v2, axis=0)
        cand = jnp.concatenate(
            [v1[a] + v2m[:k // 2] for a in range(k // 2)] + [v1[0] + v2m[k // 2:], v1m[k // 2:] + v2[0]], axis=0)
        ts, pos = _topk_axis0(cand, cand_pos, k)
        top = jnp.concatenate(ts, axis=0)
        e = jnp.exp(top - ts[0])
        g_all.append(e / jnp.sum(e, axis=0, keepdims=True))
        posm = jnp.concatenate(pos, axis=0)
        i1_all.append(_select_rows(i1, posm >> int(math.log2(k))))
        i2_all.append(_select_rows(i2, posm & (k - 1)))
    i1_ref[...] = jnp.concatenate(i1_all, axis=0).T
    i2_ref[...] = jnp.concatenate(i2_all, axis=0).T
    g_ref[...] = jnp.concatenate(g_all, axis=0).T


def _peer_route(qp, k1, k2):
    T = qp.shape[0]
    tm = min(256, T)
    slot = pl.BlockSpec((tm, N_SLOTS), lambda i: (i, 0))
    keys = pl.BlockSpec((N_KEYS, PEER_QDIM // 2), lambda i: (0, 0))
    return pl.pallas_call(
        _route_kernel,
        out_shape=(
            jax.ShapeDtypeStruct((T, N_SLOTS), jnp.int32),
            jax.ShapeDtypeStruct((T, N_SLOTS), jnp.int32),
            jax.ShapeDtypeStruct((T, N_SLOTS), F32),
        ),
        grid=(T // tm,),
        in_specs=[pl.BlockSpec((tm, PEER_HEADS * PEER_QDIM), lambda i: (i, 0)), keys, keys],
        out_specs=(slot, slot, slot),
        compiler_params=_cparams("parallel"),
        name="peer_route",
    )(qp, k1, k2)


GATE_GROUP = 16


def _gate_matrix_kernel(i1_ref, i2_ref, g_ref, w_ref):
    key = lax.broadcasted_iota(jnp.int32, (N_KEYS, N_SLOTS), 0)

    def body(tg, carry):
        t0 = pl.multiple_of(tg * GATE_GROUP, GATE_GROUP)
        per_token = []
        for u in range(GATE_GROUP):
            r1 = i1_ref[pl.ds(t0 + u, 1), :]
            r2 = i2_ref[pl.ds(t0 + u, 1), :]
            g = g_ref[pl.ds(t0 + u, 1), :]
            a = jnp.where(key == r1, 1.0, 0.0).astype(BF16)
            b = jnp.where(key == r2, g, 0.0).astype(BF16)
            per_token.append(lax.dot_general(a, b, _NT, preferred_element_type=F32))
        w = pltpu.einshape("tid->itd", jnp.stack(per_token, axis=0))
        w_ref[:, pl.ds(t0, GATE_GROUP), :] = w.astype(w_ref.dtype)
        return carry

    lax.fori_loop(0, w_ref.shape[1] // GATE_GROUP, body, 0)


def _gate_matrix(i1, i2, g):
    T = i1.shape[0]
    tb = min(128, T)
    slot = pl.BlockSpec((tb, N_SLOTS), lambda i: (i, 0))
    return pl.pallas_call(
        _gate_matrix_kernel,
        out_shape=jax.ShapeDtypeStruct((N_KEYS, T, N_KEYS), BF16),
        grid=(T // tb,),
        in_specs=[slot, slot, slot],
        out_specs=pl.BlockSpec((N_KEYS, tb, N_KEYS), lambda i: (0, i, 0)),
        compiler_params=_cparams("parallel"),
        name="peer_gate_matrix",
    )(i1, i2, g)


def _experts_kernel(hb_ref, dn_ref, up_ref, w_ref, h_ref, g_ref, b_ref, o_ref, acc_ref):
    j = pl.program_id(1)
    acc_ref = acc_ref.at[:, pl.ds(0, D_MODEL)]

    @pl.when(j == 0)
    def _():
        acc_ref[...] = jnp.zeros(acc_ref.shape, F32)

    pre = jnp.dot(hb_ref[...], dn_ref[...], preferred_element_type=F32)
    act = 0.5 * pre * (1.0 + lax.erf(pre * (2.0 ** -0.5)))
    gates = jnp.concatenate([w_ref[m] for m in range(w_ref.shape[0])], axis=1)
    act = act * gates.astype(F32)
    acc_ref[...] += jnp.dot(act.astype(BF16), up_ref[...], preferred_element_type=F32)

    @pl.when(j == pl.num_programs(1) - 1)
    def _():
        o_ref[...] = _layer_norm(ALPHA * h_ref[...] + acc_ref[...], g_ref[...], b_ref[...])


def _peer_experts(hb, down_t, up, w3, h, g, b):
    T = hb.shape[0]
    tm, te = min(512, T), 1024
    row = pl.BlockSpec((tm, D_MODEL), lambda i, j: (i, 0))
    vec = pl.BlockSpec((1, D_MODEL), lambda i, j: (0, 0))
    return pl.pallas_call(
        _experts_kernel,
        out_shape=jax.ShapeDtypeStruct((T, D_MODEL), F32),
        grid=(T // tm, N_EXPERTS // te),
        in_specs=[
            row,
            pl.BlockSpec((D_MODEL, te), lambda i, j: (0, j)),
            pl.BlockSpec((te, D_MODEL), lambda i, j: (j, 0)),
            pl.BlockSpec((te // N_KEYS, tm, N_KEYS), lambda i, j: (j, i, 0)),
            row, vec, vec,
        ],
        out_specs=row,
        scratch_shapes=[pltpu.VMEM((tm, D_MODEL + PITCH_PAD), F32)],
        compiler_params=_cparams("parallel", "arbitrary"),
        name="peer_experts",
    )(hb, down_t, up, w3, h, g, b)


def _rope_tables(S):
    half = QK_ROPE // 2
    inv_freq = ROPE_THETA ** (-jnp.arange(half, dtype=F32) / half)
    ang = jnp.arange(S, dtype=jnp.int32).astype(F32)[:, None] * inv_freq[None, :]
    cos, sin = jnp.cos(ang), jnp.sin(ang)
    zeros = jnp.zeros((S, LANE - QK_ROPE), F32)
    lanes = (jnp.concatenate([cos, cos, zeros], axis=1), jnp.concatenate([-sin, sin, zeros], axis=1))
    return lanes, (cos.T, sin.T)


def _pack_input_weights(w_in, b_gates):
    wq, wk, wv, wcq, wckv, wkr, wg = jnp.split(
        w_in, (A_WIDTH, 2 * A_WIDTH, 3 * A_WIDTH, 3 * A_WIDTH + Q_LORA,
               3 * A_WIDTH + Q_LORA + KV_LORA, 3 * A_WIDTH + Q_LORA + KV_LORA + QK_ROPE), axis=1)
    pad = jnp.zeros((D_MODEL, PROJ_WIDTH - COL_KR - QK_ROPE), w_in.dtype)
    w_all = jnp.concatenate([wq, wk, wv, wg, wcq, wckv, wkr, pad], axis=1).astype(BF16)
    b_all = jnp.zeros((1, PROJ_WIDTH), F32).at[0, COL_GA:COL_CQ].set(b_gates)
    return w_all, b_all


def _pack_uq(w_uq):
    w = w_uq.reshape(Q_LORA, B_HEADS, QK_NOPE + QK_ROPE)
    w = jnp.pad(w, ((0, 0), (0, 0), (0, MLA_HEAD_PAD - QK_NOPE - QK_ROPE)))
    return w.reshape(Q_LORA, B_HEADS * MLA_HEAD_PAD).T.astype(BF16)


def kernel(x, w_in, b_gates, a_w_out, mla_q_norm, mla_w_uq, mla_kv_norm, mla_w_ukv, mla_w_out, w_out,
           ln1_g, ln1_b, peer_w_query, peer_sub_keys_1, peer_sub_keys_2, peer_expert_down,
           peer_expert_up, ln2_g, ln2_b):
    B, S, D = x.shape
    assert D == D_MODEL and w_in.shape[0] == DEPTH
    T = B * S
    (cos_t, sin_t), (cos_rt, sin_rt) = _rope_tables(S)
    h = x.reshape(T, D)
    for l in range(DEPTH):
        w_all, b_all = _pack_input_weights(w_in[l], b_gates[l])
        proj, qkv_s = _in_projection(h, w_all, b_all, B, S)

        (w1, d1), (w4, d4), (w16, d16) = A_PATTERNS
        assert (d1, d4, d16) == (1, 4, STREAMS)
        o1, l1 = _dilated_dense(proj, B, S, w1)
        o4, l4 = _dilated_streams(qkv_s, w4, d4)
        o16, l16 = _dilated_streams(qkv_s, w16, d16)
        ya = _combine_patterns(o1, l1, o4, l4, o16, l16, B, S)

        cqn, ckvn, krope = _latent_prep(proj, mla_q_norm[l][None], mla_kv_norm[l][None], cos_t, sin_t, S)
        qt = _q_up(cqn, _pack_uq(mla_w_uq[l]), cos_rt, sin_rt, S)
        w_ukv = mla_w_ukv[l].reshape(KV_LORA, B_HEADS, QK_NOPE + V_HEAD)
        wk = w_ukv[:, :, :QK_NOPE].reshape(KV_LORA, B_HEADS * QK_NOPE).astype(BF16)
        wvt = w_ukv[:, :, QK_NOPE:].reshape(KV_LORA, B_HEADS * V_HEAD).T.astype(BF16)
        kn, vt = _kv_up(ckvn, wk, wvt)
        yb = _mla_attention(qt, kn, krope, vt, B, S)

        u = _branch_mix(ya, yb, a_w_out[l].astype(BF16), mla_w_out[l].astype(BF16), proj)
        h1, h1b = _out_projection_ln(u, w_out[l].astype(BF16), h, ln1_g[l][None], ln1_b[l][None])

        qp = _matmul(h1b, peer_w_query[l].astype(BF16), "peer_query")
        i1, i2, g = _peer_route(qp, peer_sub_keys_1[l].astype(BF16), peer_sub_keys_2[l].astype(BF16))
        w3 = _gate_matrix(i1, i2, g)
        h = _peer_experts(h1b, peer_expert_down[l].T.astype(BF16), peer_expert_up[l].astype(BF16),
                          w3, h1, ln2_g[l][None], ln2_b[l][None])
    return h.reshape(B, S, D)
```

```python
import functools
import math

import jax
import jax.numpy as jnp
from jax import lax
from jax.experimental import pallas as pl
from jax.experimental.pallas import tpu as pltpu

F32 = jnp.float32
BF16 = jnp.bfloat16

D_MODEL = 2048
A_HEADS = 16
A_HEAD_DIM = 128
A_PATTERNS = ((128, 1), (512, 4), (2048, 16))
A_BLOCK = 128
A_WIDTH = A_HEADS * A_HEAD_DIM
B_HEADS = 16
Q_LORA = 512
KV_LORA = 512
QK_NOPE = 128
QK_ROPE = 64
V_HEAD = 128
ROPE_THETA = 10000.0
N_KEYS = 128
PEER_HEADS = 8
PEER_QDIM = 256
PEER_TOPK = 16
N_EXPERTS = N_KEYS * N_KEYS
N_SLOTS = PEER_HEADS * PEER_TOPK
LN_EPS = 1e-5
RMS_EPS = 1e-6
DEPTH = 1
ALPHA = (2.0 * DEPTH) ** 0.25
NEG = -1e30

LANE = 128
MLA_HEAD_PAD = 256
VMEM_LIMIT = 56 * 1024 * 1024
PITCH_PAD = LANE

COL_Q, COL_K, COL_V = 0, A_WIDTH, 2 * A_WIDTH
COL_GA = 3 * A_WIDTH
COL_GB = COL_GA + D_MODEL
COL_CQ = COL_GB + D_MODEL
COL_CKV = COL_CQ + Q_LORA
COL_KR = COL_CKV + KV_LORA
PROJ_WIDTH = COL_CQ + 2048

_NT = (((1,), (1,)), ((), ()))


def _cparams(*sem):
    return pltpu.CompilerParams(dimension_semantics=sem, vmem_limit_bytes=VMEM_LIMIT)


def _layer_norm(z, g, b):
    mu = jnp.mean(z, axis=-1, keepdims=True)
    zc = z - mu
    var = jnp.mean(zc * zc, axis=-1, keepdims=True)
    return zc * lax.rsqrt(var + LN_EPS) * g + b


def _inproj_kernel(x_ref, w_ref, b_ref, o_ref, qs_ref, slab_ref, *, qkv_hi, gate_lo, gate_hi):
    j = pl.program_id(1)
    acc = jnp.dot(x_ref[...].astype(BF16), w_ref[...], preferred_element_type=F32)
    is_gate = jnp.logical_and(j >= gate_lo, j < gate_hi)

    @pl.when(is_gate)
    def _():
        o_ref[...] = jax.nn.sigmoid(acc + b_ref[...]).astype(o_ref.dtype)

    @pl.when(jnp.logical_not(is_gate))
    def _():
        o_ref[...] = acc.astype(o_ref.dtype)

    @pl.when(j < qkv_hi)
    def _():
        steps = qs_ref.shape[1]
        for c in range(slab_ref.shape[0]):
            slab_ref[c] = acc[:, c * LANE:(c + 1) * LANE]
        for r in range(STREAMS):
            for c in range(slab_ref.shape[0]):
                rows = slab_ref[c, pl.ds(r, steps, stride=STREAMS), :]
                qs_ref[_stream_of_residue(r), :, c * LANE:(c + 1) * LANE] = rows.astype(qs_ref.dtype)


def _in_projection(x2, w_all, b_all, B, S):
    T = x2.shape[0]
    tm, tn = min(1024, S), 1024
    nt = S // tm
    qkv_hi = COL_GA // tn
    kern = functools.partial(_inproj_kernel, qkv_hi=qkv_hi, gate_lo=COL_GA // tn, gate_hi=COL_CQ // tn)
    return pl.pallas_call(
        kern,
        out_shape=(
            jax.ShapeDtypeStruct((T, PROJ_WIDTH), BF16),
            jax.ShapeDtypeStruct((B, STREAMS, S // STREAMS, 3 * A_WIDTH), BF16),
        ),
        grid=(T // tm, PROJ_WIDTH // tn),
        in_specs=[
            pl.BlockSpec((tm, D_MODEL), lambda i, j: (i, 0)),
            pl.BlockSpec((D_MODEL, tn), lambda i, j: (0, j)),
            pl.BlockSpec((1, tn), lambda i, j: (0, j)),
        ],
        out_specs=(
            pl.BlockSpec((tm, tn), lambda i, j: (i, j)),
            pl.BlockSpec((None, STREAMS, tm // STREAMS, tn),
                         lambda i, j: (i // nt, 0, i % nt, jnp.minimum(j, qkv_hi - 1))),
        ),
        scratch_shapes=[pltpu.VMEM((tn // LANE, tm, LANE), F32)],
        compiler_params=_cparams("parallel", "arbitrary"),
        name="in_projection",
    )(x2, w_all, b_all)


def _block_pos(idx, groups):
    if groups == 1:
        return idx
    per = A_BLOCK // groups
    return groups * (idx % per) + idx // per


def _dilated_bias(dilation, steps, groups):
    blk = A_BLOCK
    i = jnp.arange(blk, dtype=jnp.int32)[:, None]
    c = jnp.arange(2 * blk, dtype=jnp.int32)[None, :]
    rel = blk + _block_pos(i, groups) - (blk * (c // blk) + _block_pos(c % blk, groups))
    band = (rel >= 0) & (rel <= steps)
    slopes = jnp.asarray([2.0 ** (-8.0 * (h + 1) / A_HEADS) for h in range(A_HEADS)], F32)
    bias = -slopes[:, None, None] * (dilation * rel).astype(F32)[None]
    first = jnp.where(band & (c >= blk), bias, NEG)
    later = jnp.where(band, bias, NEG)
    return jnp.stack([first, later], axis=0)


def _dilated_kernel(q_ref, kp_ref, ko_ref, vp_ref, vo_ref, bias_ref, o_ref, lse_ref):
    blk = A_BLOCK
    lane = lax.broadcasted_iota(jnp.int32, (blk, LANE), 1)
    lse_all = jnp.zeros((blk, LANE), F32)

    def rows(ref, sl):
        return ref[..., sl].reshape(blk, A_HEAD_DIM)

    for h in range(A_HEADS):
        sl = slice(h * A_HEAD_DIM, (h + 1) * A_HEAD_DIM)
        q = rows(q_ref, sl)
        k = jnp.concatenate([rows(kp_ref, sl), rows(ko_ref, sl)], axis=0)
        v = jnp.concatenate([rows(vp_ref, sl), rows(vo_ref, sl)], axis=0)
        logits = lax.dot_general(q, k, _NT, preferred_element_type=F32) + bias_ref[h]
        m = jnp.max(logits, axis=-1, keepdims=True)
        p = jnp.exp(logits - m)
        z = jnp.sum(p, axis=-1, keepdims=True)
        o = jnp.dot(p.astype(BF16), v, preferred_element_type=F32) / z
        o_ref[..., sl] = o.astype(o_ref.dtype).reshape(o_ref.shape[:-1] + (A_HEAD_DIM,))
        lse_all = jnp.where(lane == h, m + jnp.log(z), lse_all)
    lse_ref[...] = lse_all.reshape(lse_ref.shape)


STREAMS = max(d for _, d in A_PATTERNS)


def _stream_of_residue(r):
    return (r % 4) * 4 + r // 4


def _dilated_dense(proj, B, S, window):
    T = B * S
    nb = S // A_BLOCK
    blk = (A_BLOCK, A_WIDTH)
    bias_spec = pl.BlockSpec((None, A_HEADS, A_BLOCK, 2 * A_BLOCK), lambda b, r, n: (jnp.minimum(n, 1), 0, 0, 0))
    return pl.pallas_call(
        _dilated_kernel,
        out_shape=(jax.ShapeDtypeStruct((T, A_WIDTH), BF16), jax.ShapeDtypeStruct((T, LANE), F32)),
        grid=(B, 1, nb),
        in_specs=[
            pl.BlockSpec(blk, lambda b, r, n: (b * nb + n, COL_Q // A_WIDTH)),
            pl.BlockSpec(blk, lambda b, r, n: (b * nb + jnp.maximum(n - 1, 0), COL_K // A_WIDTH)),
            pl.BlockSpec(blk, lambda b, r, n: (b * nb + n, COL_K // A_WIDTH)),
            pl.BlockSpec(blk, lambda b, r, n: (b * nb + jnp.maximum(n - 1, 0), COL_V // A_WIDTH)),
            pl.BlockSpec(blk, lambda b, r, n: (b * nb + n, COL_V // A_WIDTH)),
            bias_spec,
        ],
        out_specs=(
            pl.BlockSpec(blk, lambda b, r, n: (b * nb + n, 0)),
            pl.BlockSpec((A_BLOCK, LANE), lambda b, r, n: (b * nb + n, 0)),
        ),
        compiler_params=_cparams("parallel", "parallel", "arbitrary"),
        name="dilated_attention_d1",
    )(proj, proj, proj, proj, proj, _dilated_bias(1, window, 1))


def _dilated_streams(qkv_s, window, dilation):
    B, ns, Ls, _ = qkv_s.shape
    groups = STREAMS // dilation
    per = A_BLOCK // groups
    assert ns == STREAMS and Ls % per == 0
    blk = (None, groups, per, A_WIDTH)
    bias_spec = pl.BlockSpec((None, A_HEADS, A_BLOCK, 2 * A_BLOCK), lambda b, r, n: (jnp.minimum(n, 1), 0, 0, 0))
    return pl.pallas_call(
        _dilated_kernel,
        out_shape=(
            jax.ShapeDtypeStruct((B, STREAMS, Ls, A_WIDTH), BF16),
            jax.ShapeDtypeStruct((B, STREAMS, Ls, LANE), F32),
        ),
        grid=(B, STREAMS // groups, Ls // per),
        in_specs=[
            pl.BlockSpec(blk, lambda b, r, n: (b, r, n, 0)),
            pl.BlockSpec(blk, lambda b, r, n: (b, r, jnp.maximum(n - 1, 0), 1)),
            pl.BlockSpec(blk, lambda b, r, n: (b, r, n, 1)),
            pl.BlockSpec(blk, lambda b, r, n: (b, r, jnp.maximum(n - 1, 0), 2)),
            pl.BlockSpec(blk, lambda b, r, n: (b, r, n, 2)),
            bias_spec,
        ],
        out_specs=(
            pl.BlockSpec(blk, lambda b, r, n: (b, r, n, 0)),
            pl.BlockSpec((None, groups, per, LANE), lambda b, r, n: (b, r, n, 0)),
        ),
        compiler_params=_cparams("parallel", "parallel", "arbitrary"),
        name=f"dilated_attention_d{dilation}",
    )(qkv_s, qkv_s, qkv_s, qkv_s, qkv_s, _dilated_bias(dilation, window // dilation, groups))


COMBINE_STEPS = 16


def _combine_kernel(o1_ref, o2_ref, o3_ref, l1_ref, l2_ref, l3_ref, y_ref, ob_ref, oc_ref, lb_ref, lc_ref):
    for r in range(STREAMS):
        s = _stream_of_residue(r)
        tok = pl.ds(r, COMBINE_STEPS, stride=STREAMS)
        lb_ref[tok, :] = l2_ref[s]
        lc_ref[tok, :] = l3_ref[s]
        for h in range(A_HEADS):
            sl = slice(h * A_HEAD_DIM, (h + 1) * A_HEAD_DIM)
            ob_ref[h, tok, :] = o2_ref[s, :, sl].astype(F32)
            oc_ref[h, tok, :] = o3_ref[s, :, sl].astype(F32)
    a, b, c = l1_ref[...], lb_ref[...], lc_ref[...]
    m = jnp.maximum(jnp.maximum(a, b), c)
    ea, eb, ec = jnp.exp(a - m), jnp.exp(b - m), jnp.exp(c - m)
    inv = 1.0 / (ea + eb + ec)
    wa, wb, wc = ea * inv, eb * inv, ec * inv
    for h in range(A_HEADS):
        sl = slice(h * A_HEAD_DIM, (h + 1) * A_HEAD_DIM)
        y = (wa[:, h:h + 1] * o1_ref[:, sl].astype(F32)
             + wb[:, h:h + 1] * ob_ref[h]
             + wc[:, h:h + 1] * oc_ref[h])
        y_ref[:, sl] = y.astype(y_ref.dtype)


def _combine_patterns(o1, l1, o4, l4, o16, l16, B, S):
    T = B * S
    tm = COMBINE_STEPS * STREAMS
    nt = S // tm
    tok_o = pl.BlockSpec((tm, A_WIDTH), lambda b, i: (b * nt + i, 0))
    tok_l = pl.BlockSpec((tm, LANE), lambda b, i: (b * nt + i, 0))
    str_o = pl.BlockSpec((None, STREAMS, COMBINE_STEPS, A_WIDTH), lambda b, i: (b, 0, i, 0))
    str_l = pl.BlockSpec((None, STREAMS, COMBINE_STEPS, LANE), lambda b, i: (b, 0, i, 0))
    return pl.pallas_call(
        _combine_kernel,
        out_shape=jax.ShapeDtypeStruct((T, A_WIDTH), BF16),
        grid=(B, nt),
        in_specs=[tok_o, str_o, str_o, tok_l, str_l, str_l],
        out_specs=tok_o,
        scratch_shapes=[
            pltpu.VMEM((A_HEADS, tm, A_HEAD_DIM), F32),
            pltpu.VMEM((A_HEADS, tm, A_HEAD_DIM), F32),
            pltpu.VMEM((tm, LANE), F32),
            pltpu.VMEM((tm, LANE), F32),
        ],
        compiler_params=_cparams("parallel", "parallel"),
        name="combine_patterns",
    )(o1, o4, o16, l1, l4, l16)


def _rope_lanes(t, cos, sin):
    lane = lax.broadcasted_iota(jnp.int32, t.shape, 1)
    half = QK_ROPE // 2
    rot = jnp.where(lane < half, pltpu.roll(t, LANE - half, 1), pltpu.roll(t, half, 1))
    return t * cos + rot * sin


def _rms_norm(x, g):
    ms = jnp.mean(x * x, axis=-1, keepdims=True)
    return x * lax.rsqrt(ms + RMS_EPS) * g


def _latent_kernel(cq_ref, ckv_ref, kr_ref, gq_ref, gkv_ref, cos_ref, sin_ref, cqn_ref, ckvn_ref, krope_ref):
    cqn_ref[...] = _rms_norm(cq_ref[...].astype(F32), gq_ref[...]).astype(cqn_ref.dtype)
    ckvn_ref[...] = _rms_norm(ckv_ref[...].astype(F32), gkv_ref[...]).astype(ckvn_ref.dtype)
    krope_ref[...] = _rope_lanes(kr_ref[...].astype(F32), cos_ref[...], sin_ref[...]).astype(krope_ref.dtype)


def _latent_prep(proj, gq, gkv, cos_t, sin_t, S):
    T = proj.shape[0]
    tm = min(512, S)
    ns = S // tm
    return pl.pallas_call(
        _latent_kernel,
        out_shape=(
            jax.ShapeDtypeStruct((T, Q_LORA), BF16),
            jax.ShapeDtypeStruct((T, KV_LORA), BF16),
            jax.ShapeDtypeStruct((T, LANE), BF16),
        ),
        grid=(T // tm,),
        in_specs=[
            pl.BlockSpec((tm, Q_LORA), lambda i: (i, COL_CQ // Q_LORA)),
            pl.BlockSpec((tm, KV_LORA), lambda i: (i, COL_CKV // KV_LORA)),
            pl.BlockSpec((tm, LANE), lambda i: (i, COL_KR // LANE)),
            pl.BlockSpec((1, Q_LORA), lambda i: (0, 0)),
            pl.BlockSpec((1, KV_LORA), lambda i: (0, 0)),
            pl.BlockSpec((tm, LANE), lambda i: (i % ns, 0)),
            pl.BlockSpec((tm, LANE), lambda i: (i % ns, 0)),
        ],
        out_specs=(
            pl.BlockSpec((tm, Q_LORA), lambda i: (i, 0)),
            pl.BlockSpec((tm, KV_LORA), lambda i: (i, 0)),
            pl.BlockSpec((tm, LANE), lambda i: (i, 0)),
        ),
        compiler_params=_cparams("parallel"),
        name="latent_prep",
    )(proj, proj, proj, gq, gkv, cos_t, sin_t)


def _qup_kernel(c_ref, wt_ref, cos_ref, sin_ref, o_ref, *, scale):
    acc = lax.dot_general(wt_ref[...], c_ref[...], _NT, preferred_element_type=F32) * scale
    cos, sin = cos_ref[...], sin_ref[...]
    half = QK_ROPE // 2
    for hb in range(acc.shape[0] // MLA_HEAD_PAD):
        lo = hb * MLA_HEAD_PAD
        r1 = acc[lo + QK_NOPE:lo + QK_NOPE + half]
        r2 = acc[lo + QK_NOPE + half:lo + QK_NOPE + QK_ROPE]
        o_ref[lo:lo + QK_NOPE] = acc[lo:lo + QK_NOPE].astype(o_ref.dtype)
        o_ref[lo + QK_NOPE:lo + QK_NOPE + half] = (r1 * cos - r2 * sin).astype(o_ref.dtype)
        o_ref[lo + QK_NOPE + half:lo + QK_NOPE + QK_ROPE] = (r2 * cos + r1 * sin).astype(o_ref.dtype)
        o_ref[lo + QK_NOPE + QK_ROPE:lo + MLA_HEAD_PAD] = acc[lo + QK_NOPE + QK_ROPE:lo + MLA_HEAD_PAD].astype(o_ref.dtype)


def _q_up(cqn, w_uq_pt, cos_rt, sin_rt, S):
    T = cqn.shape[0]
    N = w_uq_pt.shape[0]
    tm, tn = min(1024, S), 1024
    ns = S // tm
    half = QK_ROPE // 2
    kern = functools.partial(_qup_kernel, scale=(QK_NOPE + QK_ROPE) ** -0.5 * math.log2(math.e))
    return pl.pallas_call(
        kern,
        out_shape=jax.ShapeDtypeStruct((N, T), BF16),
        grid=(T // tm, N // tn),
        in_specs=[
            pl.BlockSpec((tm, Q_LORA), lambda i, j: (i, 0)),
            pl.BlockSpec((tn, Q_LORA), lambda i, j: (j, 0)),
            pl.BlockSpec((half, tm), lambda i, j: (0, i % ns)),
            pl.BlockSpec((half, tm), lambda i, j: (0, i % ns)),
        ],
        out_specs=pl.BlockSpec((tn, tm), lambda i, j: (j, i)),
        compiler_params=_cparams("parallel", "arbitrary"),
        name="mla_q_up",
    )(cqn, w_uq_pt, cos_rt, sin_rt)


def _mm_kernel(a_ref, w_ref, o_ref):
    o_ref[...] = jnp.dot(a_ref[...], w_ref[...], preferred_element_type=F32).astype(o_ref.dtype)


def _matmul(a, w, name, tm=512, tn=1024):
    M, K = a.shape
    N = w.shape[1]
    tm, tn = min(tm, M), min(tn, N)
    return pl.pallas_call(
        _mm_kernel,
        out_shape=jax.ShapeDtypeStruct((M, N), BF16),
        grid=(M // tm, N // tn),
        in_specs=[
            pl.BlockSpec((tm, K), lambda i, j: (i, 0)),
            pl.BlockSpec((K, tn), lambda i, j: (0, j)),
        ],
        out_specs=pl.BlockSpec((tm, tn), lambda i, j: (i, j)),
        compiler_params=_cparams("parallel", "arbitrary"),
        name=name,
    )(a, w)


def _kvup_kernel(c_ref, wk_ref, wvt_ref, kn_ref, vt_ref):
    c = c_ref[...]
    kn_ref[...] = jnp.dot(c, wk_ref[...], preferred_element_type=F32).astype(kn_ref.dtype)
    vt_ref[...] = lax.dot_general(wvt_ref[...], c, _NT, preferred_element_type=F32).astype(vt_ref.dtype)


def _kv_up(ckvn, wk, wvt):
    T = ckvn.shape[0]
    tm = min(1024, T)
    n = B_HEADS * QK_NOPE
    return pl.pallas_call(
        _kvup_kernel,
        out_shape=(jax.ShapeDtypeStruct((T, n), BF16), jax.ShapeDtypeStruct((B_HEADS * V_HEAD, T), BF16)),
        grid=(T // tm,),
        in_specs=[
            pl.BlockSpec((tm, KV_LORA), lambda i: (i, 0)),
            pl.BlockSpec((KV_LORA, n), lambda i: (0, 0)),
            pl.BlockSpec((B_HEADS * V_HEAD, KV_LORA), lambda i: (0, 0)),
        ],
        out_specs=(pl.BlockSpec((tm, n), lambda i: (i, 0)), pl.BlockSpec((B_HEADS * V_HEAD, tm), lambda i: (0, i))),
        compiler_params=_cparams("parallel"),
        name="mla_kv_up",
    )(ckvn, wk, wvt)


def _mla_kernel(q_ref, kn_ref, kr_ref, vt_ref, o_ref, sa_ref, sb_ref, m_ref, l_ref, acc_ref, *, tq):
    qi = pl.program_id(2)
    tk = tq // 2
    q = q_ref[...]
    sa_ref, sb_ref, acc_ref = (r.at[:, pl.ds(0, tq)] for r in (sa_ref, sb_ref, acc_ref))
    m_ref[...] = jnp.full(m_ref.shape, NEG, F32)
    l_ref[...] = jnp.zeros(l_ref.shape, F32)
    acc_ref[...] = jnp.zeros(acc_ref.shape, F32)

    def scores(c, s_ref):
        start = pl.multiple_of(c * tk, tk)
        k = jnp.concatenate([kn_ref[pl.ds(start, tk), :], kr_ref[pl.ds(start, tk), :]], axis=1)
        s_ref[...] = jnp.dot(k, q, preferred_element_type=F32)

    def update(c, s_ref, masked):
        start = pl.multiple_of(c * tk, tk)
        st = s_ref[...]
        if masked:
            key = lax.broadcasted_iota(jnp.int32, st.shape, 0) + (c * tk - qi * tq)
            qry = lax.broadcasted_iota(jnp.int32, st.shape, 1)
            st = jnp.where(key <= qry, st, NEG)
        m_prev = m_ref[...]
        m_new = jnp.maximum(m_prev, jnp.max(st, axis=0, keepdims=True))
        a = jnp.exp2(m_prev - m_new)
        p = jnp.exp2(st - m_new)
        l_ref[...] = a * l_ref[...] + jnp.sum(p, axis=0, keepdims=True)
        pv = jnp.dot(vt_ref[:, pl.ds(start, tk)], p.astype(BF16), preferred_element_type=F32)
        acc_ref[...] = a * acc_ref[...] + pv
        m_ref[...] = m_new

    scores(0, sa_ref)

    def pair(i, carry):
        c = 2 * i
        scores(c + 1, sb_ref)
        update(c, sa_ref, False)
        scores(c + 2, sa_ref)
        update(c + 1, sb_ref, False)
        return carry

    lax.fori_loop(0, qi, pair, 0)
    c = 2 * qi
    scores(c + 1, sb_ref)
    update(c, sa_ref, True)
    update(c + 1, sb_ref, True)
    o_ref[...] = (acc_ref[...] / l_ref[...]).T.astype(o_ref.dtype)


def _mla_attention(qt, kn, krope, vt, B, S):
    T = qt.shape[1]
    tq = min(1024, S)
    nq = S // tq
    kern = functools.partial(_mla_kernel, tq=tq)
    return pl.pallas_call(
        kern,
        out_shape=jax.ShapeDtypeStruct((T, B_HEADS * V_HEAD), BF16),
        grid=(B, B_HEADS, nq),
        in_specs=[
            pl.BlockSpec((MLA_HEAD_PAD, tq), lambda b, h, i: (h, b * nq + i)),
            pl.BlockSpec((S, QK_NOPE), lambda b, h, i: (b, h)),
            pl.BlockSpec((S, LANE), lambda b, h, i: (b, 0)),
            pl.BlockSpec((V_HEAD, S), lambda b, h, i: (h, b)),
        ],
        out_specs=pl.BlockSpec((tq, V_HEAD), lambda b, h, i: (b * nq + i, h)),
        scratch_shapes=[
            pltpu.VMEM((tq // 2, tq + PITCH_PAD), F32),
            pltpu.VMEM((tq // 2, tq + PITCH_PAD), F32),
            pltpu.VMEM((1, tq), F32),
            pltpu.VMEM((1, tq), F32),
            pltpu.VMEM((V_HEAD, tq + PITCH_PAD), F32),
        ],
        compiler_params=_cparams("parallel", "parallel", "arbitrary"),
        name="mla_attention",
    )(qt, kn, krope, vt)


def _branch_kernel(ya_ref, yb_ref, wa_ref, wb_ref, ga_ref, gb_ref, o_ref):
    pa = jnp.dot(ya_ref[...], wa_ref[...], preferred_element_type=F32)
    pb = jnp.dot(yb_ref[...], wb_ref[...], preferred_element_type=F32)
    u = ga_ref[...].astype(F32) * pa + gb_ref[...].astype(F32) * pb
    o_ref[...] = u.astype(o_ref.dtype)


def _branch_mix(ya, yb, wa, wb, proj):
    T = ya.shape[0]
    tm, tn = min(512, T), 1024
    return pl.pallas_call(
        _branch_kernel,
        out_shape=jax.ShapeDtypeStruct((T, D_MODEL), BF16),
        grid=(T // tm, D_MODEL // tn),
        in_specs=[
            pl.BlockSpec((tm, A_WIDTH), lambda i, j: (i, 0)),
            pl.BlockSpec((tm, B_HEADS * V_HEAD), lambda i, j: (i, 0)),
            pl.BlockSpec((A_WIDTH, tn), lambda i, j: (0, j)),
            pl.BlockSpec((B_HEADS * V_HEAD, tn), lambda i, j: (0, j)),
            pl.BlockSpec((tm, tn), lambda i, j: (i, COL_GA // tn + j)),
            pl.BlockSpec((tm, tn), lambda i, j: (i, COL_GB // tn + j)),
        ],
        out_specs=pl.BlockSpec((tm, tn), lambda i, j: (i, j)),
        compiler_params=_cparams("parallel", "arbitrary"),
        name="branch_mix",
    )(ya, yb, wa, wb, proj, proj)


def _outln_kernel(u_ref, w_ref, x_ref, g_ref, b_ref, h_ref, hb_ref):
    mix = jnp.dot(u_ref[...], w_ref[...], preferred_element_type=F32)
    h = _layer_norm(ALPHA * x_ref[...] + mix, g_ref[...], b_ref[...])
    h_ref[...] = h
    hb_ref[...] = h.astype(hb_ref.dtype)


def _out_projection_ln(u, w_out, x2, g, b):
    T = u.shape[0]
    tm = min(256, T)
    row = pl.BlockSpec((tm, D_MODEL), lambda i: (i, 0))
    vec = pl.BlockSpec((1, D_MODEL), lambda i: (0, 0))
    return pl.pallas_call(
        _outln_kernel,
        out_shape=(
            jax.ShapeDtypeStruct((T, D_MODEL), F32),
            jax.ShapeDtypeStruct((T, D_MODEL), BF16),
        ),
        grid=(T // tm,),
        in_specs=[row, pl.BlockSpec((D_MODEL, D_MODEL), lambda i: (0, 0)), row, vec, vec],
        out_specs=(row, row),
        compiler_params=_cparams("parallel"),
        name="out_projection_ln",
    )(u, w_out, x2, g, b)


def _topk_axis0(s, ids, k):
    big = jnp.int32(2 ** 30)
    vals, idxs = [], []
    for _ in range(k):
        m = jnp.max(s, axis=0, keepdims=True)
        idx = jnp.min(jnp.where(s == m, ids, big), axis=0, keepdims=True)
        vals.append(m)
        idxs.append(idx)
        s = jnp.where(ids == idx, -jnp.inf, s)
    return vals, idxs


def _select_rows(rows, sel):
    out = jnp.zeros(sel.shape, rows[0].dtype)
    for a, r in enumerate(rows):
        out = jnp.where(sel == a, r, out)
    return out


def _route_kernel(q_ref, k1_ref, k2_ref, i1_ref, i2_ref, g_ref):
    half = PEER_QDIM // 2
    k, tm = PEER_TOPK, q_ref.shape[0]
    key_id = lax.broadcasted_iota(jnp.int32, (N_KEYS, tm), 0)
    sub = lax.broadcasted_iota(jnp.int32, (k // 2, tm), 0)
    cand_pos = jnp.concatenate([a * k + sub for a in range(k // 2)] + [k // 2 + sub, (k // 2 + sub) * k], axis=0)
    i1_all, i2_all, g_all = [], [], []
    for h in range(PEER_HEADS):
        q1 = q_ref[:, h * PEER_QDIM:h * PEER_QDIM + half]
        q2 = q_ref[:, h * PEER_QDIM + half:(h + 1) * PEER_QDIM]
        s1 = lax.dot_general(k1_ref[...], q1, _NT, preferred_element_type=F32)
        s2 = lax.dot_general(k2_ref[...], q2, _NT, preferred_element_type=F32)
        v1, i1 = _topk_axis0(s1, key_id, k)
        v2, i2 = _topk_axis0(s2, key_id, k)
        v1m = jnp.concatenate(v1, axis=0)
        v2m = jnp.concatenate(v2, axis=0)
        cand = jnp.concatenate(
            [v1[a] + v2m[:k // 2] for a in range(k // 2)] + [v1[0] + v2m[k // 2:], v1m[k // 2:] + v2[0]], axis=0)
        ts, pos = _topk_axis0(cand, cand_pos, k)
        top = jnp.concatenate(ts, axis=0)
        e = jnp.exp(top - ts[0])
        g_all.append(e / jnp.sum(e, axis=0, keepdims=True))
        posm = jnp.concatenate(pos, axis=0)
        i1_all.append(_select_rows(i1, posm >> int(math.log2(k))))
        i2_all.append(_select_rows(i2, posm & (k - 1)))
    i1_ref[...] = jnp.concatenate(i1_all, axis=0).T
    i2_ref[...] = jnp.concatenate(i2_all, axis=0).T
    g_ref[...] = jnp.concatenate(g_all, axis=0).T


def _peer_route(qp, k1, k2):
    T = qp.shape[0]
    tm = min(256, T)
    slot = pl.BlockSpec((tm, N_SLOTS), lambda i: (i, 0))
    keys = pl.BlockSpec((N_KEYS, PEER_QDIM // 2), lambda i: (0, 0))
    return pl.pallas_call(
        _route_kernel,
        out_shape=(
            jax.ShapeDtypeStruct((T, N_SLOTS), jnp.int32),
            jax.ShapeDtypeStruct((T, N_SLOTS), jnp.int32),
            jax.ShapeDtypeStruct((T, N_SLOTS), F32),
        ),
        grid=(T // tm,),
        in_specs=[pl.BlockSpec((tm, PEER_HEADS * PEER_QDIM), lambda i: (i, 0)), keys, keys],
        out_specs=(slot, slot, slot),
        compiler_params=_cparams("parallel"),
        name="peer_route",
    )(qp, k1, k2)


GATE_GROUP = 16


def _gate_matrix_kernel(i1_ref, i2_ref, g_ref, w_ref):
    key = lax.broadcasted_iota(jnp.int32, (N_KEYS, N_SLOTS), 0)

    def body(tg, carry):
        t0 = pl.multiple_of(tg * GATE_GROUP, GATE_GROUP)
        per_token = []
        for u in range(GATE_GROUP):
            r1 = i1_ref[pl.ds(t0 + u, 1), :]
            r2 = i2_ref[pl.ds(t0 + u, 1), :]
            g = g_ref[pl.ds(t0 + u, 1), :]
            a = jnp.where(key == r1, 1.0, 0.0).astype(BF16)
            b = jnp.where(key == r2, g, 0.0).astype(BF16)
            per_token.append(lax.dot_general(a, b, _NT, preferred_element_type=F32))
        w = pltpu.einshape("tid->itd", jnp.stack(per_token, axis=0))
        w_ref[:, pl.ds(t0, GATE_GROUP), :] = w.astype(w_ref.dtype)
        return carry

    lax.fori_loop(0, w_ref.shape[1] // GATE_GROUP, body, 0)


def _gate_matrix(i1, i2, g):
    T = i1.shape[0]
    tb = min(128, T)
    slot = pl.BlockSpec((tb, N_SLOTS), lambda i: (i, 0))
    return pl.pallas_call(
        _gate_matrix_kernel,
        out_shape=jax.ShapeDtypeStruct((N_KEYS, T, N_KEYS), BF16),
        grid=(T // tb,),
        in_specs=[slot, slot, slot],
        out_specs=pl.BlockSpec((N_KEYS, tb, N_KEYS), lambda i: (0, i, 0)),
        compiler_params=_cparams("parallel"),
        name="peer_gate_matrix",
    )(i1, i2, g)


EXPERT_TILE = 1024


def _experts_kernel(hb_ref, dn_ref, up_ref, w_ref, y_ref, acc_ref):
    j = pl.program_id(1)

    @pl.when(j == 0)
    def _():
        acc_ref[...] = jnp.zeros(acc_ref.shape, F32)

    pre = jnp.dot(hb_ref[...], dn_ref[...], preferred_element_type=F32)
    act = 0.5 * pre * (1.0 + lax.erf(pre * (2.0 ** -0.5)))
    gates = jnp.concatenate([w_ref[m] for m in range(w_ref.shape[0])], axis=1)
    act = act * gates.astype(F32)
    acc_ref[...] += jnp.dot(act.astype(BF16), up_ref[...], preferred_element_type=F32)

    @pl.when(j == pl.num_programs(1) - 1)
    def _():
        y_ref[...] = acc_ref[...].astype(y_ref.dtype)


def _peer_experts(hb, down_tiles, up, w3):
    T = hb.shape[0]
    tm, te = min(1024, T), EXPERT_TILE
    row = pl.BlockSpec((tm, D_MODEL), lambda i, j: (i, 0))
    return pl.pallas_call(
        _experts_kernel,
        out_shape=jax.ShapeDtypeStruct((T, D_MODEL), BF16),
        grid=(T // tm, N_EXPERTS // te),
        in_specs=[
            row,
            pl.BlockSpec((None, D_MODEL, te), lambda i, j: (j, 0, 0)),
            pl.BlockSpec((te, D_MODEL), lambda i, j: (j, 0)),
            pl.BlockSpec((te // N_KEYS, tm, N_KEYS), lambda i, j: (j, i, 0)),
        ],
        out_specs=row,
        scratch_shapes=[pltpu.VMEM((tm, D_MODEL), F32)],
        compiler_params=_cparams("parallel", "arbitrary"),
        name="peer_experts",
    )(hb, down_tiles, up, w3)


def _residual_ln_kernel(y_ref, h_ref, g_ref, b_ref, o_ref):
    o_ref[...] = _layer_norm(ALPHA * h_ref[...] + y_ref[...].astype(F32), g_ref[...], b_ref[...])


def _residual_ln(y, h, g, b):
    T = y.shape[0]
    tm = min(512, T)
    row = pl.BlockSpec((tm, D_MODEL), lambda i: (i, 0))
    vec = pl.BlockSpec((1, D_MODEL), lambda i: (0, 0))
    return pl.pallas_call(
        _residual_ln_kernel,
        out_shape=jax.ShapeDtypeStruct((T, D_MODEL), F32),
        grid=(T // tm,),
        in_specs=[row, row, vec, vec],
        out_specs=row,
        compiler_params=_cparams("parallel"),
        name="peer_residual_ln",
    )(y, h, g, b)


def _rope_tables(S):
    half = QK_ROPE // 2
    inv_freq = ROPE_THETA ** (-jnp.arange(half, dtype=F32) / half)
    ang = jnp.arange(S, dtype=jnp.int32).astype(F32)[:, None] * inv_freq[None, :]
    cos, sin = jnp.cos(ang), jnp.sin(ang)
    zeros = jnp.zeros((S, LANE - QK_ROPE), F32)
    lanes = (jnp.concatenate([cos, cos, zeros], axis=1), jnp.concatenate([-sin, sin, zeros], axis=1))
    return lanes, (cos.T, sin.T)


def _pack_input_weights(w_in, b_gates):
    wq, wk, wv, wcq, wckv, wkr, wg = jnp.split(
        w_in, (A_WIDTH, 2 * A_WIDTH, 3 * A_WIDTH, 3 * A_WIDTH + Q_LORA,
               3 * A_WIDTH + Q_LORA + KV_LORA, 3 * A_WIDTH + Q_LORA + KV_LORA + QK_ROPE), axis=1)
    pad = jnp.zeros((D_MODEL, PROJ_WIDTH - COL_KR - QK_ROPE), w_in.dtype)
    wq = wq * (A_HEAD_DIM ** -0.5)
    w_all = jnp.concatenate([wq, wk, wv, wg, wcq, wckv, wkr, pad], axis=1).astype(BF16)
    b_all = jnp.zeros((1, PROJ_WIDTH), F32).at[0, COL_GA:COL_CQ].set(b_gates)
    return w_all, b_all


def _pack_uq(w_uq):
    w = w_uq.reshape(Q_LORA, B_HEADS, QK_NOPE + QK_ROPE)
    w = jnp.pad(w, ((0, 0), (0, 0), (0, MLA_HEAD_PAD - QK_NOPE - QK_ROPE)))
    return w.reshape(Q_LORA, B_HEADS * MLA_HEAD_PAD).T.astype(BF16)


def kernel(x, w_in, b_gates, a_w_out, mla_q_norm, mla_w_uq, mla_kv_norm, mla_w_ukv, mla_w_out, w_out,
           ln1_g, ln1_b, peer_w_query, peer_sub_keys_1, peer_sub_keys_2, peer_expert_down,
           peer_expert_up, ln2_g, ln2_b):
    B, S, D = x.shape
    assert D == D_MODEL and w_in.shape[0] == DEPTH
    T = B * S
    (cos_t, sin_t), (cos_rt, sin_rt) = _rope_tables(S)
    h = x.reshape(T, D)
    for l in range(DEPTH):
        w_all, b_all = _pack_input_weights(w_in[l], b_gates[l])
        proj, qkv_s = _in_projection(h, w_all, b_all, B, S)

        (w1, d1), (w4, d4), (w16, d16) = A_PATTERNS
        assert (d1, d4, d16) == (1, 4, STREAMS)
        o1, l1 = _dilated_dense(proj, B, S, w1)
        o4, l4 = _dilated_streams(qkv_s, w4, d4)
        o16, l16 = _dilated_streams(qkv_s, w16, d16)
        ya = _combine_patterns(o1, l1, o4, l4, o16, l16, B, S)

        cqn, ckvn, krope = _latent_prep(proj, mla_q_norm[l][None], mla_kv_norm[l][None], cos_t, sin_t, S)
        qt = _q_up(cqn, _pack_uq(mla_w_uq[l]), cos_rt, sin_rt, S)
        w_ukv = mla_w_ukv[l].reshape(KV_LORA, B_HEADS, QK_NOPE + V_HEAD)
        wk = w_ukv[:, :, :QK_NOPE].reshape(KV_LORA, B_HEADS * QK_NOPE).astype(BF16)
        wvt = w_ukv[:, :, QK_NOPE:].reshape(KV_LORA, B_HEADS * V_HEAD).T.astype(BF16)
        kn, vt = _kv_up(ckvn, wk, wvt)
        yb = _mla_attention(qt, kn, krope, vt, B, S)

        u = _branch_mix(ya, yb, a_w_out[l].astype(BF16), mla_w_out[l].astype(BF16), proj)
        h1, h1b = _out_projection_ln(u, w_out[l].astype(BF16), h, ln1_g[l][None], ln1_b[l][None])

        qp = _matmul(h1b, peer_w_query[l].astype(BF16), "peer_query")
        i1, i2, g = _peer_route(qp, peer_sub_keys_1[l].astype(BF16), peer_sub_keys_2[l].astype(BF16))
        w3 = _gate_matrix(i1, i2, g)
        down_tiles = peer_expert_down[l].reshape(N_EXPERTS // EXPERT_TILE, EXPERT_TILE, D).transpose(0, 2, 1)
        yp = _peer_experts(h1b, down_tiles.astype(BF16), peer_expert_up[l].astype(BF16), w3)
        h = _residual_ln(yp, h1, ln2_g[l][None], ln2_b[l][None])
    return h.reshape(B, S, D)
```

```python
import functools
import math

import jax
import jax.numpy as jnp
from jax import lax
from jax.experimental import pallas as pl
from jax.experimental.pallas import tpu as pltpu

F32 = jnp.float32
BF16 = jnp.bfloat16

D_MODEL = 2048
A_HEADS = 16
A_HEAD_DIM = 128
A_PATTERNS = ((128, 1), (512, 4), (2048, 16))
A_BLOCK = 128
A_WIDTH = A_HEADS * A_HEAD_DIM
B_HEADS = 16
Q_LORA = 512
KV_LORA = 512
QK_NOPE = 128
QK_ROPE = 64
V_HEAD = 128
ROPE_THETA = 10000.0
N_KEYS = 128
PEER_HEADS = 8
PEER_QDIM = 256
PEER_TOPK = 16
N_EXPERTS = N_KEYS * N_KEYS
N_SLOTS = PEER_HEADS * PEER_TOPK
LN_EPS = 1e-5
RMS_EPS = 1e-6
DEPTH = 1
ALPHA = (2.0 * DEPTH) ** 0.25
NEG = -1e30

LANE = 128
MLA_HEAD_PAD = 256
VMEM_LIMIT = 56 * 1024 * 1024
PITCH_PAD = LANE

COL_Q, COL_K, COL_V = 0, A_WIDTH, 2 * A_WIDTH
COL_GA = 3 * A_WIDTH
COL_GB = COL_GA + D_MODEL
COL_CQ = COL_GB + D_MODEL
COL_CKV = COL_CQ + Q_LORA
COL_KR = COL_CKV + KV_LORA
PROJ_WIDTH = COL_CQ + 2048

_NT = (((1,), (1,)), ((), ()))


def _cparams(*sem):
    return pltpu.CompilerParams(dimension_semantics=sem, vmem_limit_bytes=VMEM_LIMIT)


def _layer_norm(z, g, b):
    mu = jnp.mean(z, axis=-1, keepdims=True)
    zc = z - mu
    var = jnp.mean(zc * zc, axis=-1, keepdims=True)
    return zc * lax.rsqrt(var + LN_EPS) * g + b


STREAMS = max(d for _, d in A_PATTERNS)
PERM_ROWS = 16 * STREAMS


def _stream_of_residue(r):
    return (r % 4) * 4 + r // 4


def _stream_permutation():
    tok = jnp.arange(PERM_ROWS, dtype=jnp.int32)
    out_row = _stream_of_residue(tok % STREAMS) * (PERM_ROWS // STREAMS) + tok // STREAMS
    return (out_row[None, :] == jnp.arange(PERM_ROWS, dtype=jnp.int32)[:, None]).astype(BF16)


def _inproj_kernel(x_ref, w_ref, b_ref, perm_ref, o_ref, qs_ref, *, qkv_hi, gate_lo, gate_hi):
    j = pl.program_id(1)
    acc = jnp.dot(x_ref[...].astype(BF16), w_ref[...], preferred_element_type=F32)
    is_gate = jnp.logical_and(j >= gate_lo, j < gate_hi)

    @pl.when(is_gate)
    def _():
        o_ref[...] = jax.nn.sigmoid(acc + b_ref[...]).astype(o_ref.dtype)

    @pl.when(jnp.logical_not(is_gate))
    def _():
        o_ref[...] = acc.astype(o_ref.dtype)

    @pl.when(j < qkv_hi)
    def _():
        rows = acc.astype(BF16)
        per = PERM_ROWS // STREAMS
        for g in range(rows.shape[0] // PERM_ROWS):
            grouped = jnp.dot(perm_ref[...], rows[g * PERM_ROWS:(g + 1) * PERM_ROWS],
                              preferred_element_type=F32).astype(qs_ref.dtype)
            for s in range(STREAMS):
                qs_ref[s, g * per:(g + 1) * per, :] = grouped[s * per:(s + 1) * per]


def _in_projection(x2, w_all, b_all, B, S):
    T = x2.shape[0]
    tm, tn = min(1024, S), 1024
    nt = S // tm
    qkv_hi = COL_GA // tn
    kern = functools.partial(_inproj_kernel, qkv_hi=qkv_hi, gate_lo=COL_GA // tn, gate_hi=COL_CQ // tn)
    return pl.pallas_call(
        kern,
        out_shape=(
            jax.ShapeDtypeStruct((T, PROJ_WIDTH), BF16),
            jax.ShapeDtypeStruct((B, STREAMS, S // STREAMS, 3 * A_WIDTH), BF16),
        ),
        grid=(T // tm, PROJ_WIDTH // tn),
        in_specs=[
            pl.BlockSpec((tm, D_MODEL), lambda i, j: (i, 0)),
            pl.BlockSpec((D_MODEL, tn), lambda i, j: (0, j)),
            pl.BlockSpec((1, tn), lambda i, j: (0, j)),
            pl.BlockSpec((PERM_ROWS, PERM_ROWS), lambda i, j: (0, 0)),
        ],
        out_specs=(
            pl.BlockSpec((tm, tn), lambda i, j: (i, j)),
            pl.BlockSpec((None, STREAMS, tm // STREAMS, tn),
                         lambda i, j: (i // nt, 0, i % nt, jnp.minimum(j, qkv_hi - 1))),
        ),
        compiler_params=_cparams("parallel", "arbitrary"),
        name="in_projection",
    )(x2, w_all, b_all, _stream_permutation())


def _block_pos(idx, groups):
    if groups == 1:
        return idx
    per = A_BLOCK // groups
    return groups * (idx % per) + idx // per


def _dilated_bias(dilation, steps, groups):
    blk = A_BLOCK
    i = jnp.arange(blk, dtype=jnp.int32)[:, None]
    c = jnp.arange(2 * blk, dtype=jnp.int32)[None, :]
    rel = blk + _block_pos(i, groups) - (blk * (c // blk) + _block_pos(c % blk, groups))
    band = (rel >= 0) & (rel <= steps)
    slopes = jnp.asarray([2.0 ** (-8.0 * (h + 1) / A_HEADS) for h in range(A_HEADS)], F32)
    bias = -slopes[:, None, None] * (dilation * rel).astype(F32)[None]
    first = jnp.where(band & (c >= blk), bias, NEG)
    later = jnp.where(band, bias, NEG)
    return jnp.stack([first, later], axis=0)


def _dilated_kernel(q_ref, kp_ref, ko_ref, vp_ref, vo_ref, bias_ref, o_ref, lse_ref):
    blk = A_BLOCK
    lane = lax.broadcasted_iota(jnp.int32, (blk, LANE), 1)
    lse_all = jnp.zeros((blk, LANE), F32)

    def rows(ref, sl):
        return ref[..., sl].reshape(blk, A_HEAD_DIM)

    for h in range(A_HEADS):
        sl = slice(h * A_HEAD_DIM, (h + 1) * A_HEAD_DIM)
        q = rows(q_ref, sl)
        k = jnp.concatenate([rows(kp_ref, sl), rows(ko_ref, sl)], axis=0)
        v = jnp.concatenate([rows(vp_ref, sl), rows(vo_ref, sl)], axis=0)
        logits = lax.dot_general(q, k, _NT, preferred_element_type=F32) + bias_ref[h]
        m = jnp.max(logits, axis=-1, keepdims=True)
        p = jnp.exp(logits - m)
        z = jnp.sum(p, axis=-1, keepdims=True)
        o = jnp.dot(p.astype(BF16), v, preferred_element_type=F32) / z
        o_ref[..., sl] = o.astype(o_ref.dtype).reshape(o_ref.shape[:-1] + (A_HEAD_DIM,))
        lse_all = jnp.where(lane == h, m + jnp.log(z), lse_all)
    lse_ref[...] = lse_all.reshape(lse_ref.shape)


def _dilated_dense(proj, B, S, window):
    T = B * S
    nb = S // A_BLOCK
    blk = (A_BLOCK, A_WIDTH)
    bias_spec = pl.BlockSpec((None, A_HEADS, A_BLOCK, 2 * A_BLOCK), lambda b, r, n: (jnp.minimum(n, 1), 0, 0, 0))
    return pl.pallas_call(
        _dilated_kernel,
        out_shape=(jax.ShapeDtypeStruct((T, A_WIDTH), BF16), jax.ShapeDtypeStruct((T, LANE), F32)),
        grid=(B, 1, nb),
        in_specs=[
            pl.BlockSpec(blk, lambda b, r, n: (b * nb + n, COL_Q // A_WIDTH)),
            pl.BlockSpec(blk, lambda b, r, n: (b * nb + jnp.maximum(n - 1, 0), COL_K // A_WIDTH)),
            pl.BlockSpec(blk, lambda b, r, n: (b * nb + n, COL_K // A_WIDTH)),
            pl.BlockSpec(blk, lambda b, r, n: (b * nb + jnp.maximum(n - 1, 0), COL_V // A_WIDTH)),
            pl.BlockSpec(blk, lambda b, r, n: (b * nb + n, COL_V // A_WIDTH)),
            bias_spec,
        ],
        out_specs=(
            pl.BlockSpec(blk, lambda b, r, n: (b * nb + n, 0)),
            pl.BlockSpec((A_BLOCK, LANE), lambda b, r, n: (b * nb + n, 0)),
        ),
        compiler_params=_cparams("parallel", "parallel", "arbitrary"),
        name="dilated_attention_d1",
    )(proj, proj, proj, proj, proj, _dilated_bias(1, window, 1))


def _dilated_streams(qkv_s, window, dilation):
    B, ns, Ls, _ = qkv_s.shape
    groups = STREAMS // dilation
    per = A_BLOCK // groups
    assert ns == STREAMS and Ls % per == 0
    blk = (None, groups, per, A_WIDTH)
    bias_spec = pl.BlockSpec((None, A_HEADS, A_BLOCK, 2 * A_BLOCK), lambda b, r, n: (jnp.minimum(n, 1), 0, 0, 0))
    return pl.pallas_call(
        _dilated_kernel,
        out_shape=(
            jax.ShapeDtypeStruct((B, STREAMS, Ls, A_WIDTH), BF16),
            jax.ShapeDtypeStruct((B, STREAMS, Ls, LANE), F32),
        ),
        grid=(B, STREAMS // groups, Ls // per),
        in_specs=[
            pl.BlockSpec(blk, lambda b, r, n: (b, r, n, 0)),
            pl.BlockSpec(blk, lambda b, r, n: (b, r, jnp.maximum(n - 1, 0), 1)),
            pl.BlockSpec(blk, lambda b, r, n: (b, r, n, 1)),
            pl.BlockSpec(blk, lambda b, r, n: (b, r, jnp.maximum(n - 1, 0), 2)),
            pl.BlockSpec(blk, lambda b, r, n: (b, r, n, 2)),
            bias_spec,
        ],
        out_specs=(
            pl.BlockSpec(blk, lambda b, r, n: (b, r, n, 0)),
            pl.BlockSpec((None, groups, per, LANE), lambda b, r, n: (b, r, n, 0)),
        ),
        compiler_params=_cparams("parallel", "parallel", "arbitrary"),
        name=f"dilated_attention_d{dilation}",
    )(qkv_s, qkv_s, qkv_s, qkv_s, qkv_s, _dilated_bias(dilation, window // dilation, groups))


COMBINE_STEPS = 16


def _combine_kernel(o1_ref, o2_ref, o3_ref, l1_ref, l2_ref, l3_ref, y_ref, ob_ref, oc_ref, lb_ref, lc_ref):
    for r in range(STREAMS):
        s = _stream_of_residue(r)
        tok = pl.ds(r, COMBINE_STEPS, stride=STREAMS)
        lb_ref[tok, :] = l2_ref[s]
        lc_ref[tok, :] = l3_ref[s]
        for h in range(A_HEADS):
            sl = slice(h * A_HEAD_DIM, (h + 1) * A_HEAD_DIM)
            ob_ref[h, tok, :] = o2_ref[s, :, sl].astype(F32)
            oc_ref[h, tok, :] = o3_ref[s, :, sl].astype(F32)
    a, b, c = l1_ref[...], lb_ref[...], lc_ref[...]
    m = jnp.maximum(jnp.maximum(a, b), c)
    ea, eb, ec = jnp.exp(a - m), jnp.exp(b - m), jnp.exp(c - m)
    inv = 1.0 / (ea + eb + ec)
    wa, wb, wc = ea * inv, eb * inv, ec * inv
    for h in range(A_HEADS):
        sl = slice(h * A_HEAD_DIM, (h + 1) * A_HEAD_DIM)
        y = (wa[:, h:h + 1] * o1_ref[:, sl].astype(F32)
             + wb[:, h:h + 1] * ob_ref[h]
             + wc[:, h:h + 1] * oc_ref[h])
        y_ref[:, sl] = y.astype(y_ref.dtype)


def _combine_patterns(o1, l1, o4, l4, o16, l16, B, S):
    T = B * S
    tm = COMBINE_STEPS * STREAMS
    nt = S // tm
    tok_o = pl.BlockSpec((tm, A_WIDTH), lambda b, i: (b * nt + i, 0))
    tok_l = pl.BlockSpec((tm, LANE), lambda b, i: (b * nt + i, 0))
    str_o = pl.BlockSpec((None, STREAMS, COMBINE_STEPS, A_WIDTH), lambda b, i: (b, 0, i, 0))
    str_l = pl.BlockSpec((None, STREAMS, COMBINE_STEPS, LANE), lambda b, i: (b, 0, i, 0))
    return pl.pallas_call(
        _combine_kernel,
        out_shape=jax.ShapeDtypeStruct((T, A_WIDTH), BF16),
        grid=(B, nt),
        in_specs=[tok_o, str_o, str_o, tok_l, str_l, str_l],
        out_specs=tok_o,
        scratch_shapes=[
            pltpu.VMEM((A_HEADS, tm, A_HEAD_DIM), F32),
            pltpu.VMEM((A_HEADS, tm, A_HEAD_DIM), F32),
            pltpu.VMEM((tm, LANE), F32),
            pltpu.VMEM((tm, LANE), F32),
        ],
        compiler_params=_cparams("parallel", "parallel"),
        name="combine_patterns",
    )(o1, o4, o16, l1, l4, l16)


def _rope_lanes(t, cos, sin):
    lane = lax.broadcasted_iota(jnp.int32, t.shape, 1)
    half = QK_ROPE // 2
    rot = jnp.where(lane < half, pltpu.roll(t, LANE - half, 1), pltpu.roll(t, half, 1))
    return t * cos + rot * sin


def _rms_norm(x, g):
    ms = jnp.mean(x * x, axis=-1, keepdims=True)
    return x * lax.rsqrt(ms + RMS_EPS) * g


def _latent_kernel(cq_ref, ckv_ref, kr_ref, gq_ref, gkv_ref, cos_ref, sin_ref, cqn_ref, ckvn_ref, krope_ref):
    cqn_ref[...] = _rms_norm(cq_ref[...].astype(F32), gq_ref[...]).astype(cqn_ref.dtype)
    ckvn_ref[...] = _rms_norm(ckv_ref[...].astype(F32), gkv_ref[...]).astype(ckvn_ref.dtype)
    krope_ref[...] = _rope_lanes(kr_ref[...].astype(F32), cos_ref[...], sin_ref[...]).astype(krope_ref.dtype)


def _latent_prep(proj, gq, gkv, cos_t, sin_t, S):
    T = proj.shape[0]
    tm = min(512, S)
    ns = S // tm
    return pl.pallas_call(
        _latent_kernel,
        out_shape=(
            jax.ShapeDtypeStruct((T, Q_LORA), BF16),
            jax.ShapeDtypeStruct((T, KV_LORA), BF16),
            jax.ShapeDtypeStruct((T, LANE), BF16),
        ),
        grid=(T // tm,),
        in_specs=[
            pl.BlockSpec((tm, Q_LORA), lambda i: (i, COL_CQ // Q_LORA)),
            pl.BlockSpec((tm, KV_LORA), lambda i: (i, COL_CKV // KV_LORA)),
            pl.BlockSpec((tm, LANE), lambda i: (i, COL_KR // LANE)),
            pl.BlockSpec((1, Q_LORA), lambda i: (0, 0)),
            pl.BlockSpec((1, KV_LORA), lambda i: (0, 0)),
            pl.BlockSpec((tm, LANE), lambda i: (i % ns, 0)),
            pl.BlockSpec((tm, LANE), lambda i: (i % ns, 0)),
        ],
        out_specs=(
            pl.BlockSpec((tm, Q_LORA), lambda i: (i, 0)),
            pl.BlockSpec((tm, KV_LORA), lambda i: (i, 0)),
            pl.BlockSpec((tm, LANE), lambda i: (i, 0)),
        ),
        compiler_params=_cparams("parallel"),
        name="latent_prep",
    )(proj, proj, proj, gq, gkv, cos_t, sin_t)


def _qup_kernel(c_ref, wt_ref, cos_ref, sin_ref, o_ref, *, scale):
    acc = lax.dot_general(wt_ref[...], c_ref[...], _NT, preferred_element_type=F32) * scale
    cos, sin = cos_ref[...], sin_ref[...]
    half = QK_ROPE // 2
    for hb in range(acc.shape[0] // MLA_HEAD_PAD):
        lo = hb * MLA_HEAD_PAD
        r1 = acc[lo + QK_NOPE:lo + QK_NOPE + half]
        r2 = acc[lo + QK_NOPE + half:lo + QK_NOPE + QK_ROPE]
        o_ref[lo:lo + QK_NOPE] = acc[lo:lo + QK_NOPE].astype(o_ref.dtype)
        o_ref[lo + QK_NOPE:lo + QK_NOPE + half] = (r1 * cos - r2 * sin).astype(o_ref.dtype)
        o_ref[lo + QK_NOPE + half:lo + QK_NOPE + QK_ROPE] = (r2 * cos + r1 * sin).astype(o_ref.dtype)
        o_ref[lo + QK_NOPE + QK_ROPE:lo + MLA_HEAD_PAD] = acc[lo + QK_NOPE + QK_ROPE:lo + MLA_HEAD_PAD].astype(o_ref.dtype)


def _q_up(cqn, w_uq_pt, cos_rt, sin_rt, S):
    T = cqn.shape[0]
    N = w_uq_pt.shape[0]
    tm, tn = min(1024, S), 1024
    ns = S // tm
    half = QK_ROPE // 2
    kern = functools.partial(_qup_kernel, scale=(QK_NOPE + QK_ROPE) ** -0.5 * math.log2(math.e))
    return pl.pallas_call(
        kern,
        out_shape=jax.ShapeDtypeStruct((N, T), BF16),
        grid=(T // tm, N // tn),
        in_specs=[
            pl.BlockSpec((tm, Q_LORA), lambda i, j: (i, 0)),
            pl.BlockSpec((tn, Q_LORA), lambda i, j: (j, 0)),
            pl.BlockSpec((half, tm), lambda i, j: (0, i % ns)),
            pl.BlockSpec((half, tm), lambda i, j: (0, i % ns)),
        ],
        out_specs=pl.BlockSpec((tn, tm), lambda i, j: (j, i)),
        compiler_params=_cparams("parallel", "arbitrary"),
        name="mla_q_up",
    )(cqn, w_uq_pt, cos_rt, sin_rt)


def _mm_kernel(a_ref, w_ref, o_ref):
    o_ref[...] = jnp.dot(a_ref[...], w_ref[...], preferred_element_type=F32).astype(o_ref.dtype)


def _matmul(a, w, name, tm=512, tn=1024):
    M, K = a.shape
    N = w.shape[1]
    tm, tn = min(tm, M), min(tn, N)
    return pl.pallas_call(
        _mm_kernel,
        out_shape=jax.ShapeDtypeStruct((M, N), BF16),
        grid=(M // tm, N // tn),
        in_specs=[
            pl.BlockSpec((tm, K), lambda i, j: (i, 0)),
            pl.BlockSpec((K, tn), lambda i, j: (0, j)),
        ],
        out_specs=pl.BlockSpec((tm, tn), lambda i, j: (i, j)),
        compiler_params=_cparams("parallel", "arbitrary"),
        name=name,
    )(a, w)


def _kvup_kernel(c_ref, wk_ref, wvt_ref, kn_ref, vt_ref):
    c = c_ref[...]
    kn_ref[...] = jnp.dot(c, wk_ref[...], preferred_element_type=F32).astype(kn_ref.dtype)
    vt_ref[...] = lax.dot_general(wvt_ref[...], c, _NT, preferred_element_type=F32).astype(vt_ref.dtype)


def _kv_up(ckvn, wk, wvt):
    T = ckvn.shape[0]
    tm = min(1024, T)
    n = B_HEADS * QK_NOPE
    return pl.pallas_call(
        _kvup_kernel,
        out_shape=(jax.ShapeDtypeStruct((T, n), BF16), jax.ShapeDtypeStruct((B_HEADS * V_HEAD, T), BF16)),
        grid=(T // tm,),
        in_specs=[
            pl.BlockSpec((tm, KV_LORA), lambda i: (i, 0)),
            pl.BlockSpec((KV_LORA, n), lambda i: (0, 0)),
            pl.BlockSpec((B_HEADS * V_HEAD, KV_LORA), lambda i: (0, 0)),
        ],
        out_specs=(pl.BlockSpec((tm, n), lambda i: (i, 0)), pl.BlockSpec((B_HEADS * V_HEAD, tm), lambda i: (0, i))),
        compiler_params=_cparams("parallel"),
        name="mla_kv_up",
    )(ckvn, wk, wvt)


def _mla_kernel(q_ref, kn_ref, kr_ref, vt_ref, o_ref, sa_ref, sb_ref, xa_ref, xb_ref, m_ref, l_ref, acc_ref, *, tq):
    qi = pl.program_id(2)
    tk = tq // 2
    q = q_ref[...]
    sa_ref, sb_ref, acc_ref = (r.at[:, pl.ds(0, tq)] for r in (sa_ref, sb_ref, acc_ref))
    m_ref[...] = jnp.full(m_ref.shape, NEG, F32)
    l_ref[...] = jnp.zeros(l_ref.shape, F32)
    acc_ref[...] = jnp.zeros(acc_ref.shape, F32)

    def scores(c, s_ref, x_ref):
        start = pl.multiple_of(c * tk, tk)
        k = jnp.concatenate([kn_ref[pl.ds(start, tk), :], kr_ref[pl.ds(start, tk), :]], axis=1)
        st = jnp.dot(k, q, preferred_element_type=F32)
        s_ref[...] = st
        x_ref[...] = jnp.max(st, axis=0, keepdims=True)

    def update(c, s_ref, x_ref, masked):
        start = pl.multiple_of(c * tk, tk)
        st = s_ref[...]
        if masked:
            key = lax.broadcasted_iota(jnp.int32, st.shape, 0) + (c * tk - qi * tq)
            qry = lax.broadcasted_iota(jnp.int32, st.shape, 1)
            st = jnp.where(key <= qry, st, NEG)
            cmax = jnp.max(st, axis=0, keepdims=True)
        else:
            cmax = x_ref[...]
        m_prev = m_ref[...]
        m_new = jnp.maximum(m_prev, cmax)
        a = jnp.exp2(m_prev - m_new)
        p = jnp.exp2(st - m_new)
        l_ref[...] = a * l_ref[...] + jnp.sum(p, axis=0, keepdims=True)
        pv = jnp.dot(vt_ref[:, pl.ds(start, tk)], p.astype(BF16), preferred_element_type=F32)
        acc_ref[...] = a * acc_ref[...] + pv
        m_ref[...] = m_new

    scores(0, sa_ref, xa_ref)

    def pair(i):
        c = 2 * i
        scores(c + 1, sb_ref, xb_ref)
        update(c, sa_ref, xa_ref, False)
        scores(c + 2, sa_ref, xa_ref)
        update(c + 1, sb_ref, xb_ref, False)

    def two_pairs(i, carry):
        pair(2 * i)
        pair(2 * i + 1)
        return carry

    lax.fori_loop(0, qi // 2, two_pairs, 0)

    @pl.when(qi % 2 == 1)
    def _():
        pair(qi - 1)

    c = 2 * qi
    scores(c + 1, sb_ref, xb_ref)
    update(c, sa_ref, xa_ref, True)
    update(c + 1, sb_ref, xb_ref, True)
    o_ref[...] = (acc_ref[...] / l_ref[...]).T.astype(o_ref.dtype)


def _mla_attention(qt, kn, krope, vt, B, S):
    T = qt.shape[1]
    tq = min(1024, S)
    nq = S // tq
    kern = functools.partial(_mla_kernel, tq=tq)
    return pl.pallas_call(
        kern,
        out_shape=jax.ShapeDtypeStruct((T, B_HEADS * V_HEAD), BF16),
        grid=(B, B_HEADS, nq),
        in_specs=[
            pl.BlockSpec((MLA_HEAD_PAD, tq), lambda b, h, i: (h, b * nq + i)),
            pl.BlockSpec((S, QK_NOPE), lambda b, h, i: (b, h)),
            pl.BlockSpec((S, LANE), lambda b, h, i: (b, 0)),
            pl.BlockSpec((V_HEAD, S), lambda b, h, i: (h, b)),
        ],
        out_specs=pl.BlockSpec((tq, V_HEAD), lambda b, h, i: (b * nq + i, h)),
        scratch_shapes=[
            pltpu.VMEM((tq // 2, tq + PITCH_PAD), F32),
            pltpu.VMEM((tq // 2, tq + PITCH_PAD), F32),
            pltpu.VMEM((1, tq), F32),
            pltpu.VMEM((1, tq), F32),
            pltpu.VMEM((1, tq), F32),
            pltpu.VMEM((1, tq), F32),
            pltpu.VMEM((V_HEAD, tq + PITCH_PAD), F32),
        ],
        compiler_params=_cparams("parallel", "parallel", "arbitrary"),
        name="mla_attention",
    )(qt, kn, krope, vt)


def _branch_kernel(ya_ref, yb_ref, wa_ref, wb_ref, ga_ref, gb_ref, o_ref):
    pa = jnp.dot(ya_ref[...], wa_ref[...], preferred_element_type=F32)
    pb = jnp.dot(yb_ref[...], wb_ref[...], preferred_element_type=F32)
    u = ga_ref[...].astype(F32) * pa + gb_ref[...].astype(F32) * pb
    o_ref[...] = u.astype(o_ref.dtype)


def _branch_mix(ya, yb, wa, wb, proj):
    T = ya.shape[0]
    tm, tn = min(512, T), 1024
    return pl.pallas_call(
        _branch_kernel,
        out_shape=jax.ShapeDtypeStruct((T, D_MODEL), BF16),
        grid=(T // tm, D_MODEL // tn),
        in_specs=[
            pl.BlockSpec((tm, A_WIDTH), lambda i, j: (i, 0)),
            pl.BlockSpec((tm, B_HEADS * V_HEAD), lambda i, j: (i, 0)),
            pl.BlockSpec((A_WIDTH, tn), lambda i, j: (0, j)),
            pl.BlockSpec((B_HEADS * V_HEAD, tn), lambda i, j: (0, j)),
            pl.BlockSpec((tm, tn), lambda i, j: (i, COL_GA // tn + j)),
            pl.BlockSpec((tm, tn), lambda i, j: (i, COL_GB // tn + j)),
        ],
        out_specs=pl.BlockSpec((tm, tn), lambda i, j: (i, j)),
        compiler_params=_cparams("parallel", "arbitrary"),
        name="branch_mix",
    )(ya, yb, wa, wb, proj, proj)


def _outln_kernel(u_ref, w_ref, x_ref, g_ref, b_ref, h_ref, hb_ref):
    mix = jnp.dot(u_ref[...], w_ref[...], preferred_element_type=F32)
    h = _layer_norm(ALPHA * x_ref[...] + mix, g_ref[...], b_ref[...])
    h_ref[...] = h
    hb_ref[...] = h.astype(hb_ref.dtype)


def _out_projection_ln(u, w_out, x2, g, b):
    T = u.shape[0]
    tm = min(256, T)
    row = pl.BlockSpec((tm, D_MODEL), lambda i: (i, 0))
    vec = pl.BlockSpec((1, D_MODEL), lambda i: (0, 0))
    return pl.pallas_call(
        _outln_kernel,
        out_shape=(
            jax.ShapeDtypeStruct((T, D_MODEL), F32),
            jax.ShapeDtypeStruct((T, D_MODEL), BF16),
        ),
        grid=(T // tm,),
        in_specs=[row, pl.BlockSpec((D_MODEL, D_MODEL), lambda i: (0, 0)), row, vec, vec],
        out_specs=(row, row),
        compiler_params=_cparams("parallel"),
        name="out_projection_ln",
    )(u, w_out, x2, g, b)


def _topk_axis0(s, ids, k):
    big = jnp.int32(2 ** 30)
    vals, idxs = [], []
    for _ in range(k):
        m = jnp.max(s, axis=0, keepdims=True)
        idx = jnp.min(jnp.where(s == m, ids, big), axis=0, keepdims=True)
        vals.append(m)
        idxs.append(idx)
        s = jnp.where(ids == idx, -jnp.inf, s)
    return vals, idxs


def _select_rows(rows, sel):
    out = jnp.zeros(sel.shape, rows[0].dtype)
    for a, r in enumerate(rows):
        out = jnp.where(sel == a, r, out)
    return out


def _route_kernel(q_ref, k1_ref, k2_ref, i1_ref, i2_ref, g_ref):
    half = PEER_QDIM // 2
    k, tm = PEER_TOPK, q_ref.shape[0]
    key_id = lax.broadcasted_iota(jnp.int32, (N_KEYS, tm), 0)
    sub = lax.broadcasted_iota(jnp.int32, (k // 2, tm), 0)
    cand_pos = jnp.concatenate([a * k + sub for a in range(k // 2)] + [k // 2 + sub, (k // 2 + sub) * k], axis=0)
    i1_all, i2_all, g_all = [], [], []
    for h in range(PEER_HEADS):
        q1 = q_ref[:, h * PEER_QDIM:h * PEER_QDIM + half]
        q2 = q_ref[:, h * PEER_QDIM + half:(h + 1) * PEER_QDIM]
        s1 = lax.dot_general(k1_ref[...], q1, _NT, preferred_element_type=F32)
        s2 = lax.dot_general(k2_ref[...], q2, _NT, preferred_element_type=F32)
        v1, i1 = _topk_axis0(s1, key_id, k)
        v2, i2 = _topk_axis0(s2, key_id, k)
        v1m = jnp.concatenate(v1, axis=0)
        v2m = jnp.concatenate(v2, axis=0)
        cand = jnp.concatenate(
            [v1[a] + v2m[:k // 2] for a in range(k // 2)] + [v1[0] + v2m[k // 2:], v1m[k // 2:] + v2[0]], axis=0)
        ts, pos = _topk_axis0(cand, cand_pos, k)
        top = jnp.concatenate(ts, axis=0)
        e = jnp.exp(top - ts[0])
        g_all.append(e / jnp.sum(e, axis=0, keepdims=True))
        posm = jnp.concatenate(pos, axis=0)
        i1_all.append(_select_rows(i1, posm >> int(math.log2(k))))
        i2_all.append(_select_rows(i2, posm & (k - 1)))
    i1_ref[...] = jnp.concatenate(i1_all, axis=0).T
    i2_ref[...] = jnp.concatenate(i2_all, axis=0).T
    g_ref[...] = jnp.concatenate(g_all, axis=0).T


def _peer_route(qp, k1, k2):
    T = qp.shape[0]
    tm = min(256, T)
    slot = pl.BlockSpec((tm, N_SLOTS), lambda i: (i, 0))
    keys = pl.BlockSpec((N_KEYS, PEER_QDIM // 2), lambda i: (0, 0))
    return pl.pallas_call(
        _route_kernel,
        out_shape=(
            jax.ShapeDtypeStruct((T, N_SLOTS), jnp.int32),
            jax.ShapeDtypeStruct((T, N_SLOTS), jnp.int32),
            jax.ShapeDtypeStruct((T, N_SLOTS), F32),
        ),
        grid=(T // tm,),
        in_specs=[pl.BlockSpec((tm, PEER_HEADS * PEER_QDIM), lambda i: (i, 0)), keys, keys],
        out_specs=(slot, slot, slot),
        compiler_params=_cparams("parallel"),
        name="peer_route",
    )(qp, k1, k2)


GATE_GROUP = 16


def _gate_matrix_kernel(i1_ref, i2_ref, g_ref, w_ref):
    key = lax.broadcasted_iota(jnp.int32, (N_KEYS, N_SLOTS), 0)

    def body(tg, carry):
        t0 = pl.multiple_of(tg * GATE_GROUP, GATE_GROUP)
        per_token = []
        for u in range(GATE_GROUP):
            r1 = i1_ref[pl.ds(t0 + u, 1), :]
            r2 = i2_ref[pl.ds(t0 + u, 1), :]
            g = g_ref[pl.ds(t0 + u, 1), :]
            a = jnp.where(key == r1, 1.0, 0.0).astype(BF16)
            b = jnp.where(key == r2, g, 0.0).astype(BF16)
            per_token.append(lax.dot_general(a, b, _NT, preferred_element_type=F32))
        w = pltpu.einshape("tid->itd", jnp.stack(per_token, axis=0))
        w_ref[:, pl.ds(t0, GATE_GROUP), :] = w.astype(w_ref.dtype)
        return carry

    lax.fori_loop(0, w_ref.shape[1] // GATE_GROUP, body, 0)


def _gate_matrix(i1, i2, g):
    T = i1.shape[0]
    tb = min(128, T)
    slot = pl.BlockSpec((tb, N_SLOTS), lambda i: (i, 0))
    return pl.pallas_call(
        _gate_matrix_kernel,
        out_shape=jax.ShapeDtypeStruct((N_KEYS, T, N_KEYS), BF16),
        grid=(T // tb,),
        in_specs=[slot, slot, slot],
        out_specs=pl.BlockSpec((N_KEYS, tb, N_KEYS), lambda i: (0, i, 0)),
        compiler_params=_cparams("parallel"),
        name="peer_gate_matrix",
    )(i1, i2, g)


EXPERT_TILE = 1024


def _experts_kernel(hb_ref, dn_ref, up_ref, w_ref, y_ref, acc_ref):
    j = pl.program_id(1)

    @pl.when(j == 0)
    def _():
        acc_ref[...] = jnp.zeros(acc_ref.shape, F32)

    pre = jnp.dot(hb_ref[...], dn_ref[...], preferred_element_type=F32)
    act = 0.5 * pre * (1.0 + lax.erf(pre * (2.0 ** -0.5)))
    gates = jnp.concatenate([w_ref[m] for m in range(w_ref.shape[0])], axis=1)
    act = act * gates.astype(F32)
    acc_ref[...] += jnp.dot(act.astype(BF16), up_ref[...], preferred_element_type=F32)

    @pl.when(j == pl.num_programs(1) - 1)
    def _():
        y_ref[...] = acc_ref[...].astype(y_ref.dtype)


def _peer_experts(hb, down_tiles, up, w3):
    T = hb.shape[0]
    tm, te = min(1024, T), EXPERT_TILE
    row = pl.BlockSpec((tm, D_MODEL), lambda i, j: (i, 0))
    return pl.pallas_call(
        _experts_kernel,
        out_shape=jax.ShapeDtypeStruct((T, D_MODEL), BF16),
        grid=(T // tm, N_EXPERTS // te),
        in_specs=[
            row,
            pl.BlockSpec((None, D_MODEL, te), lambda i, j: (j, 0, 0)),
            pl.BlockSpec((te, D_MODEL), lambda i, j: (j, 0)),
            pl.BlockSpec((te // N_KEYS, tm, N_KEYS), lambda i, j: (j, i, 0)),
        ],
        out_specs=row,
        scratch_shapes=[pltpu.VMEM((tm, D_MODEL), F32)],
        compiler_params=_cparams("parallel", "arbitrary"),
        name="peer_experts",
    )(hb, down_tiles, up, w3)


def _residual_ln_kernel(y_ref, h_ref, g_ref, b_ref, o_ref):
    o_ref[...] = _layer_norm(ALPHA * h_ref[...] + y_ref[...].astype(F32), g_ref[...], b_ref[...])


def _residual_ln(y, h, g, b):
    T = y.shape[0]
    tm = min(512, T)
    row = pl.BlockSpec((tm, D_MODEL), lambda i: (i, 0))
    vec = pl.BlockSpec((1, D_MODEL), lambda i: (0, 0))
    return pl.pallas_call(
        _residual_ln_kernel,
        out_shape=jax.ShapeDtypeStruct((T, D_MODEL), F32),
        grid=(T // tm,),
        in_specs=[row, row, vec, vec],
        out_specs=row,
        compiler_params=_cparams("parallel"),
        name="peer_residual_ln",
    )(y, h, g, b)


def _rope_tables(S):
    half = QK_ROPE // 2
    inv_freq = ROPE_THETA ** (-jnp.arange(half, dtype=F32) / half)
    ang = jnp.arange(S, dtype=jnp.int32).astype(F32)[:, None] * inv_freq[None, :]
    cos, sin = jnp.cos(ang), jnp.sin(ang)
    zeros = jnp.zeros((S, LANE - QK_ROPE), F32)
    lanes = (jnp.concatenate([cos, cos, zeros], axis=1), jnp.concatenate([-sin, sin, zeros], axis=1))
    return lanes, (cos.T, sin.T)


def _pack_input_weights(w_in, b_gates):
    wq, wk, wv, wcq, wckv, wkr, wg = jnp.split(
        w_in, (A_WIDTH, 2 * A_WIDTH, 3 * A_WIDTH, 3 * A_WIDTH + Q_LORA,
               3 * A_WIDTH + Q_LORA + KV_LORA, 3 * A_WIDTH + Q_LORA + KV_LORA + QK_ROPE), axis=1)
    pad = jnp.zeros((D_MODEL, PROJ_WIDTH - COL_KR - QK_ROPE), w_in.dtype)
    wq = wq * (A_HEAD_DIM ** -0.5)
    w_all = jnp.concatenate([wq, wk, wv, wg, wcq, wckv, wkr, pad], axis=1).astype(BF16)
    b_all = jnp.zeros((1, PROJ_WIDTH), F32).at[0, COL_GA:COL_CQ].set(b_gates)
    return w_all, b_all


def _pack_uq(w_uq):
    w = w_uq.reshape(Q_LORA, B_HEADS, QK_NOPE + QK_ROPE)
    w = jnp.pad(w, ((0, 0), (0, 0), (0, MLA_HEAD_PAD - QK_NOPE - QK_ROPE)))
    return w.reshape(Q_LORA, B_HEADS * MLA_HEAD_PAD).T.astype(BF16)


def kernel(x, w_in, b_gates, a_w_out, mla_q_norm, mla_w_uq, mla_kv_norm, mla_w_ukv, mla_w_out, w_out,
           ln1_g, ln1_b, peer_w_query, peer_sub_keys_1, peer_sub_keys_2, peer_expert_down,
           peer_expert_up, ln2_g, ln2_b):
    B, S, D = x.shape
    assert D == D_MODEL and w_in.shape[0] == DEPTH
    T = B * S
    (cos_t, sin_t), (cos_rt, sin_rt) = _rope_tables(S)
    h = x.reshape(T, D)
    for l in range(DEPTH):
        w_all, b_all = _pack_input_weights(w_in[l], b_gates[l])
        proj, qkv_s = _in_projection(h, w_all, b_all, B, S)

        (w1, d1), (w4, d4), (w16, d16) = A_PATTERNS
        assert (d1, d4, d16) == (1, 4, STREAMS)
        o1, l1 = _dilated_dense(proj, B, S, w1)
        o4, l4 = _dilated_streams(qkv_s, w4, d4)
        o16, l16 = _dilated_streams(qkv_s, w16, d16)
        ya = _combine_patterns(o1, l1, o4, l4, o16, l16, B, S)

        cqn, ckvn, krope = _latent_prep(proj, mla_q_norm[l][None], mla_kv_norm[l][None], cos_t, sin_t, S)
        qt = _q_up(cqn, _pack_uq(mla_w_uq[l]), cos_rt, sin_rt, S)
        w_ukv = mla_w_ukv[l].reshape(KV_LORA, B_HEADS, QK_NOPE + V_HEAD)
        wk = w_ukv[:, :, :QK_NOPE].reshape(KV_LORA, B_HEADS * QK_NOPE).astype(BF16)
        wvt = w_ukv[:, :, QK_NOPE:].reshape(KV_LORA, B_HEADS * V_HEAD).T.astype(BF16)
        kn, vt = _kv_up(ckvn, wk, wvt)
        yb = _mla_attention(qt, kn, krope, vt, B, S)

        u = _branch_mix(ya, yb, a_w_out[l].astype(BF16), mla_w_out[l].astype(BF16), proj)
        h1, h1b = _out_projection_ln(u, w_out[l].astype(BF16), h, ln1_g[l][None], ln1_b[l][None])

        qp = _matmul(h1b, peer_w_query[l].astype(BF16), "peer_query")
        i1, i2, g = _peer_route(qp, peer_sub_keys_1[l].astype(BF16), peer_sub_keys_2[l].astype(BF16))
        w3 = _gate_matrix(i1, i2, g)
        down_tiles = peer_expert_down[l].reshape(N_EXPERTS // EXPERT_TILE, EXPERT_TILE, D).transpose(0, 2, 1)
        yp = _peer_experts(h1b, down_tiles.astype(BF16), peer_expert_up[l].astype(BF16), w3)
        h = _residual_ln(yp, h1, ln2_g[l][None], ln2_b[l][None])
    return h.reshape(B, S, D)
```

```python
import functools
import math

import jax
import jax.numpy as jnp
from jax import lax
from jax.experimental import pallas as pl
from jax.experimental.pallas import tpu as pltpu

F32 = jnp.float32
BF16 = jnp.bfloat16

D_MODEL = 2048
A_HEADS = 16
A_HEAD_DIM = 128
A_PATTERNS = ((128, 1), (512, 4), (2048, 16))
A_BLOCK = 128
A_WIDTH = A_HEADS * A_HEAD_DIM
B_HEADS = 16
Q_LORA = 512
KV_LORA = 512
QK_NOPE = 128
QK_ROPE = 64
V_HEAD = 128
ROPE_THETA = 10000.0
N_KEYS = 128
PEER_HEADS = 8
PEER_QDIM = 256
PEER_TOPK = 16
N_EXPERTS = N_KEYS * N_KEYS
N_SLOTS = PEER_HEADS * PEER_TOPK
LN_EPS = 1e-5
RMS_EPS = 1e-6
DEPTH = 1
ALPHA = (2.0 * DEPTH) ** 0.25
NEG = -1e30

LANE = 128
MLA_HEAD_PAD = 256
VMEM_LIMIT = 56 * 1024 * 1024
PITCH_PAD = LANE

COL_Q, COL_K, COL_V = 0, A_WIDTH, 2 * A_WIDTH
COL_GA = 3 * A_WIDTH
COL_GB = COL_GA + D_MODEL
COL_CQ = COL_GB + D_MODEL
COL_CKV = COL_CQ + Q_LORA
COL_KR = COL_CKV + KV_LORA
PROJ_WIDTH = COL_CQ + 2048

_NT = (((1,), (1,)), ((), ()))


def _cparams(*sem):
    return pltpu.CompilerParams(dimension_semantics=sem, vmem_limit_bytes=VMEM_LIMIT)


def _layer_norm(z, g, b):
    mu = jnp.mean(z, axis=-1, keepdims=True)
    zc = z - mu
    var = jnp.mean(zc * zc, axis=-1, keepdims=True)
    return zc * lax.rsqrt(var + LN_EPS) * g + b


STREAMS = max(d for _, d in A_PATTERNS)
PERM_ROWS = 16 * STREAMS


def _stream_of_residue(r):
    return (r % 4) * 4 + r // 4


def _stream_permutation():
    tok = jnp.arange(PERM_ROWS, dtype=jnp.int32)
    out_row = _stream_of_residue(tok % STREAMS) * (PERM_ROWS // STREAMS) + tok // STREAMS
    return (out_row[None, :] == jnp.arange(PERM_ROWS, dtype=jnp.int32)[:, None]).astype(BF16)


def _inproj_kernel(x_ref, w_ref, b_ref, perm_ref, o_ref, qs_ref, *, qkv_hi, gate_lo, gate_hi):
    j = pl.program_id(1)
    acc = jnp.dot(x_ref[...].astype(BF16), w_ref[...], preferred_element_type=F32)
    is_gate = jnp.logical_and(j >= gate_lo, j < gate_hi)

    @pl.when(is_gate)
    def _():
        o_ref[...] = jax.nn.sigmoid(acc + b_ref[...]).astype(o_ref.dtype)

    @pl.when(jnp.logical_not(is_gate))
    def _():
        o_ref[...] = acc.astype(o_ref.dtype)

    @pl.when(j < qkv_hi)
    def _():
        rows = acc.astype(BF16)
        per = PERM_ROWS // STREAMS
        for g in range(rows.shape[0] // PERM_ROWS):
            grouped = jnp.dot(perm_ref[...], rows[g * PERM_ROWS:(g + 1) * PERM_ROWS],
                              preferred_element_type=F32).astype(qs_ref.dtype)
            for s in range(STREAMS):
                qs_ref[s, g * per:(g + 1) * per, :] = grouped[s * per:(s + 1) * per]


def _in_projection(x2, w_all, b_all, B, S):
    T = x2.shape[0]
    tm, tn = min(1024, S), 1024
    nt = S // tm
    qkv_hi = COL_GA // tn
    kern = functools.partial(_inproj_kernel, qkv_hi=qkv_hi, gate_lo=COL_GA // tn, gate_hi=COL_CQ // tn)
    return pl.pallas_call(
        kern,
        out_shape=(
            jax.ShapeDtypeStruct((T, PROJ_WIDTH), BF16),
            jax.ShapeDtypeStruct((B, STREAMS, S // STREAMS, 3 * A_WIDTH), BF16),
        ),
        grid=(T // tm, PROJ_WIDTH // tn),
        in_specs=[
            pl.BlockSpec((tm, D_MODEL), lambda i, j: (i, 0)),
            pl.BlockSpec((D_MODEL, tn), lambda i, j: (0, j)),
            pl.BlockSpec((1, tn), lambda i, j: (0, j)),
            pl.BlockSpec((PERM_ROWS, PERM_ROWS), lambda i, j: (0, 0)),
        ],
        out_specs=(
            pl.BlockSpec((tm, tn), lambda i, j: (i, j)),
            pl.BlockSpec((None, STREAMS, tm // STREAMS, tn),
                         lambda i, j: (i // nt, 0, i % nt, jnp.minimum(j, qkv_hi - 1))),
        ),
        compiler_params=_cparams("parallel", "arbitrary"),
        name="in_projection",
    )(x2, w_all, b_all, _stream_permutation())


def _block_pos(idx, groups):
    if groups == 1:
        return idx
    per = A_BLOCK // groups
    return groups * (idx % per) + idx // per


def _dilated_bias(dilation, steps, groups):
    blk = A_BLOCK
    i = jnp.arange(blk, dtype=jnp.int32)[:, None]
    c = jnp.arange(2 * blk, dtype=jnp.int32)[None, :]
    rel = blk + _block_pos(i, groups) - (blk * (c // blk) + _block_pos(c % blk, groups))
    band = (rel >= 0) & (rel <= steps)
    slopes = jnp.asarray([2.0 ** (-8.0 * (h + 1) / A_HEADS) for h in range(A_HEADS)], F32)
    bias = -slopes[:, None, None] * (dilation * rel).astype(F32)[None]
    first = jnp.where(band & (c >= blk), bias, NEG)
    later = jnp.where(band, bias, NEG)
    return jnp.stack([first, later], axis=0)


def _dilated_kernel(q_ref, kp_ref, ko_ref, vp_ref, vo_ref, bias_ref, o_ref, lse_ref):
    blk = A_BLOCK
    lane = lax.broadcasted_iota(jnp.int32, (blk, LANE), 1)
    lse_all = jnp.zeros((blk, LANE), F32)

    def rows(ref, sl):
        return ref[..., sl].reshape(blk, A_HEAD_DIM)

    for h in range(A_HEADS):
        sl = slice(h * A_HEAD_DIM, (h + 1) * A_HEAD_DIM)
        q = rows(q_ref, sl)
        k = jnp.concatenate([rows(kp_ref, sl), rows(ko_ref, sl)], axis=0)
        v = jnp.concatenate([rows(vp_ref, sl), rows(vo_ref, sl)], axis=0)
        logits = lax.dot_general(q, k, _NT, preferred_element_type=F32) + bias_ref[h]
        m = jnp.max(logits, axis=-1, keepdims=True)
        p = jnp.exp(logits - m)
        z = jnp.sum(p, axis=-1, keepdims=True)
        o = jnp.dot(p.astype(BF16), v, preferred_element_type=F32) / z
        o_ref[..., sl] = o.astype(o_ref.dtype).reshape(o_ref.shape[:-1] + (A_HEAD_DIM,))
        lse_all = jnp.where(lane == h, m + jnp.log(z), lse_all)
    lse_ref[...] = lse_all.reshape(lse_ref.shape)


def _dilated_dense(proj, B, S, window):
    T = B * S
    nb = S // A_BLOCK
    blk = (A_BLOCK, A_WIDTH)
    bias_spec = pl.BlockSpec((None, A_HEADS, A_BLOCK, 2 * A_BLOCK), lambda b, r, n: (jnp.minimum(n, 1), 0, 0, 0))
    return pl.pallas_call(
        _dilated_kernel,
        out_shape=(jax.ShapeDtypeStruct((T, A_WIDTH), BF16), jax.ShapeDtypeStruct((T, LANE), F32)),
        grid=(B, 1, nb),
        in_specs=[
            pl.BlockSpec(blk, lambda b, r, n: (b * nb + n, COL_Q // A_WIDTH)),
            pl.BlockSpec(blk, lambda b, r, n: (b * nb + jnp.maximum(n - 1, 0), COL_K // A_WIDTH)),
            pl.BlockSpec(blk, lambda b, r, n: (b * nb + n, COL_K // A_WIDTH)),
            pl.BlockSpec(blk, lambda b, r, n: (b * nb + jnp.maximum(n - 1, 0), COL_V // A_WIDTH)),
            pl.BlockSpec(blk, lambda b, r, n: (b * nb + n, COL_V // A_WIDTH)),
            bias_spec,
        ],
        out_specs=(
            pl.BlockSpec(blk, lambda b, r, n: (b * nb + n, 0)),
            pl.BlockSpec((A_BLOCK, LANE), lambda b, r, n: (b * nb + n, 0)),
        ),
        compiler_params=_cparams("parallel", "parallel", "arbitrary"),
        name="dilated_attention_d1",
    )(proj, proj, proj, proj, proj, _dilated_bias(1, window, 1))


def _dilated_streams(qkv_s, window, dilation):
    B, ns, Ls, _ = qkv_s.shape
    groups = STREAMS // dilation
    per = A_BLOCK // groups
    assert ns == STREAMS and Ls % per == 0
    blk = (None, groups, per, A_WIDTH)
    bias_spec = pl.BlockSpec((None, A_HEADS, A_BLOCK, 2 * A_BLOCK), lambda b, r, n: (jnp.minimum(n, 1), 0, 0, 0))
    return pl.pallas_call(
        _dilated_kernel,
        out_shape=(
            jax.ShapeDtypeStruct((B, STREAMS, Ls, A_WIDTH), BF16),
            jax.ShapeDtypeStruct((B, STREAMS, Ls, LANE), F32),
        ),
        grid=(B, STREAMS // groups, Ls // per),
        in_specs=[
            pl.BlockSpec(blk, lambda b, r, n: (b, r, n, 0)),
            pl.BlockSpec(blk, lambda b, r, n: (b, r, jnp.maximum(n - 1, 0), 1)),
            pl.BlockSpec(blk, lambda b, r, n: (b, r, n, 1)),
            pl.BlockSpec(blk, lambda b, r, n: (b, r, jnp.maximum(n - 1, 0), 2)),
            pl.BlockSpec(blk, lambda b, r, n: (b, r, n, 2)),
            bias_spec,
        ],
        out_specs=(
            pl.BlockSpec(blk, lambda b, r, n: (b, r, n, 0)),
            pl.BlockSpec((None, groups, per, LANE), lambda b, r, n: (b, r, n, 0)),
        ),
        compiler_params=_cparams("parallel", "parallel", "arbitrary"),
        name=f"dilated_attention_d{dilation}",
    )(qkv_s, qkv_s, qkv_s, qkv_s, qkv_s, _dilated_bias(dilation, window // dilation, groups))


COMBINE_STEPS = 16


def _combine_kernel(o1_ref, o2_ref, o3_ref, l1_ref, l2_ref, l3_ref, y_ref, ob_ref, oc_ref, lb_ref, lc_ref):
    for r in range(STREAMS):
        s = _stream_of_residue(r)
        tok = pl.ds(r, COMBINE_STEPS, stride=STREAMS)
        lb_ref[tok, :] = l2_ref[s]
        lc_ref[tok, :] = l3_ref[s]
        for h in range(A_HEADS):
            sl = slice(h * A_HEAD_DIM, (h + 1) * A_HEAD_DIM)
            ob_ref[h, tok, :] = o2_ref[s, :, sl].astype(F32)
            oc_ref[h, tok, :] = o3_ref[s, :, sl].astype(F32)
    a, b, c = l1_ref[...], lb_ref[...], lc_ref[...]
    m = jnp.maximum(jnp.maximum(a, b), c)
    ea, eb, ec = jnp.exp(a - m), jnp.exp(b - m), jnp.exp(c - m)
    inv = 1.0 / (ea + eb + ec)
    wa, wb, wc = ea * inv, eb * inv, ec * inv
    for h in range(A_HEADS):
        sl = slice(h * A_HEAD_DIM, (h + 1) * A_HEAD_DIM)
        y = (wa[:, h:h + 1] * o1_ref[:, sl].astype(F32)
             + wb[:, h:h + 1] * ob_ref[h]
             + wc[:, h:h + 1] * oc_ref[h])
        y_ref[:, sl] = y.astype(y_ref.dtype)


def _combine_patterns(o1, l1, o4, l4, o16, l16, B, S):
    T = B * S
    tm = COMBINE_STEPS * STREAMS
    nt = S // tm
    tok_o = pl.BlockSpec((tm, A_WIDTH), lambda b, i: (b * nt + i, 0))
    tok_l = pl.BlockSpec((tm, LANE), lambda b, i: (b * nt + i, 0))
    str_o = pl.BlockSpec((None, STREAMS, COMBINE_STEPS, A_WIDTH), lambda b, i: (b, 0, i, 0))
    str_l = pl.BlockSpec((None, STREAMS, COMBINE_STEPS, LANE), lambda b, i: (b, 0, i, 0))
    return pl.pallas_call(
        _combine_kernel,
        out_shape=jax.ShapeDtypeStruct((T, A_WIDTH), BF16),
        grid=(B, nt),
        in_specs=[tok_o, str_o, str_o, tok_l, str_l, str_l],
        out_specs=tok_o,
        scratch_shapes=[
            pltpu.VMEM((A_HEADS, tm, A_HEAD_DIM), F32),
            pltpu.VMEM((A_HEADS, tm, A_HEAD_DIM), F32),
            pltpu.VMEM((tm, LANE), F32),
            pltpu.VMEM((tm, LANE), F32),
        ],
        compiler_params=_cparams("parallel", "parallel"),
        name="combine_patterns",
    )(o1, o4, o16, l1, l4, l16)


def _rope_lanes(t, cos, sin):
    lane = lax.broadcasted_iota(jnp.int32, t.shape, 1)
    half = QK_ROPE // 2
    rot = jnp.where(lane < half, pltpu.roll(t, LANE - half, 1), pltpu.roll(t, half, 1))
    return t * cos + rot * sin


def _rms_norm(x, g):
    ms = jnp.mean(x * x, axis=-1, keepdims=True)
    return x * lax.rsqrt(ms + RMS_EPS) * g


def _latent_kernel(cq_ref, ckv_ref, kr_ref, gq_ref, gkv_ref, cos_ref, sin_ref, cqn_ref, ckvn_ref, krope_ref):
    cqn_ref[...] = _rms_norm(cq_ref[...].astype(F32), gq_ref[...]).astype(cqn_ref.dtype)
    ckvn_ref[...] = _rms_norm(ckv_ref[...].astype(F32), gkv_ref[...]).astype(ckvn_ref.dtype)
    krope_ref[...] = _rope_lanes(kr_ref[...].astype(F32), cos_ref[...], sin_ref[...]).astype(krope_ref.dtype)


def _latent_prep(proj, gq, gkv, cos_t, sin_t, S):
    T = proj.shape[0]
    tm = min(512, S)
    ns = S // tm
    return pl.pallas_call(
        _latent_kernel,
        out_shape=(
            jax.ShapeDtypeStruct((T, Q_LORA), BF16),
            jax.ShapeDtypeStruct((T, KV_LORA), BF16),
            jax.ShapeDtypeStruct((T, LANE), BF16),
        ),
        grid=(T // tm,),
        in_specs=[
            pl.BlockSpec((tm, Q_LORA), lambda i: (i, COL_CQ // Q_LORA)),
            pl.BlockSpec((tm, KV_LORA), lambda i: (i, COL_CKV // KV_LORA)),
            pl.BlockSpec((tm, LANE), lambda i: (i, COL_KR // LANE)),
            pl.BlockSpec((1, Q_LORA), lambda i: (0, 0)),
            pl.BlockSpec((1, KV_LORA), lambda i: (0, 0)),
            pl.BlockSpec((tm, LANE), lambda i: (i % ns, 0)),
            pl.BlockSpec((tm, LANE), lambda i: (i % ns, 0)),
        ],
        out_specs=(
            pl.BlockSpec((tm, Q_LORA), lambda i: (i, 0)),
            pl.BlockSpec((tm, KV_LORA), lambda i: (i, 0)),
            pl.BlockSpec((tm, LANE), lambda i: (i, 0)),
        ),
        compiler_params=_cparams("parallel"),
        name="latent_prep",
    )(proj, proj, proj, gq, gkv, cos_t, sin_t)


def _qup_kernel(c_ref, wt_ref, cos_ref, sin_ref, o_ref, *, scale):
    acc = lax.dot_general(wt_ref[...], c_ref[...], _NT, preferred_element_type=F32) * scale
    cos, sin = cos_ref[...], sin_ref[...]
    half = QK_ROPE // 2
    for hb in range(acc.shape[0] // MLA_HEAD_PAD):
        lo = hb * MLA_HEAD_PAD
        r1 = acc[lo + QK_NOPE:lo + QK_NOPE + half]
        r2 = acc[lo + QK_NOPE + half:lo + QK_NOPE + QK_ROPE]
        o_ref[lo:lo + QK_NOPE] = acc[lo:lo + QK_NOPE].astype(o_ref.dtype)
        o_ref[lo + QK_NOPE:lo + QK_NOPE + half] = (r1 * cos - r2 * sin).astype(o_ref.dtype)
        o_ref[lo + QK_NOPE + half:lo + QK_NOPE + QK_ROPE] = (r2 * cos + r1 * sin).astype(o_ref.dtype)
        o_ref[lo + QK_NOPE + QK_ROPE:lo + MLA_HEAD_PAD] = acc[lo + QK_NOPE + QK_ROPE:lo + MLA_HEAD_PAD].astype(o_ref.dtype)


def _q_up(cqn, w_uq_pt, cos_rt, sin_rt, S):
    T = cqn.shape[0]
    N = w_uq_pt.shape[0]
    tm, tn = min(1024, S), 1024
    ns = S // tm
    half = QK_ROPE // 2
    kern = functools.partial(_qup_kernel, scale=(QK_NOPE + QK_ROPE) ** -0.5 * math.log2(math.e))
    return pl.pallas_call(
        kern,
        out_shape=jax.ShapeDtypeStruct((N, T), BF16),
        grid=(T // tm, N // tn),
        in_specs=[
            pl.BlockSpec((tm, Q_LORA), lambda i, j: (i, 0)),
            pl.BlockSpec((tn, Q_LORA), lambda i, j: (j, 0)),
            pl.BlockSpec((half, tm), lambda i, j: (0, i % ns)),
            pl.BlockSpec((half, tm), lambda i, j: (0, i % ns)),
        ],
        out_specs=pl.BlockSpec((tn, tm), lambda i, j: (j, i)),
        compiler_params=_cparams("parallel", "arbitrary"),
        name="mla_q_up",
    )(cqn, w_uq_pt, cos_rt, sin_rt)


def _mm_kernel(a_ref, w_ref, o_ref):
    o_ref[...] = jnp.dot(a_ref[...], w_ref[...], preferred_element_type=F32).astype(o_ref.dtype)


def _matmul(a, w, name, tm=512, tn=1024):
    M, K = a.shape
    N = w.shape[1]
    tm, tn = min(tm, M), min(tn, N)
    return pl.pallas_call(
        _mm_kernel,
        out_shape=jax.ShapeDtypeStruct((M, N), BF16),
        grid=(M // tm, N // tn),
        in_specs=[
            pl.BlockSpec((tm, K), lambda i, j: (i, 0)),
            pl.BlockSpec((K, tn), lambda i, j: (0, j)),
        ],
        out_specs=pl.BlockSpec((tm, tn), lambda i, j: (i, j)),
        compiler_params=_cparams("parallel", "arbitrary"),
        name=name,
    )(a, w)


def _kvup_kernel(c_ref, wk_ref, wvt_ref, kn_ref, vt_ref):
    c = c_ref[...]
    kn_ref[...] = jnp.dot(c, wk_ref[...], preferred_element_type=F32).astype(kn_ref.dtype)
    vt_ref[...] = lax.dot_general(wvt_ref[...], c, _NT, preferred_element_type=F32).astype(vt_ref.dtype)


def _kv_up(ckvn, wk, wvt):
    T = ckvn.shape[0]
    tm = min(1024, T)
    n = B_HEADS * QK_NOPE
    return pl.pallas_call(
        _kvup_kernel,
        out_shape=(jax.ShapeDtypeStruct((T, n), BF16), jax.ShapeDtypeStruct((B_HEADS * V_HEAD, T), BF16)),
        grid=(T // tm,),
        in_specs=[
            pl.BlockSpec((tm, KV_LORA), lambda i: (i, 0)),
            pl.BlockSpec((KV_LORA, n), lambda i: (0, 0)),
            pl.BlockSpec((B_HEADS * V_HEAD, KV_LORA), lambda i: (0, 0)),
        ],
        out_specs=(pl.BlockSpec((tm, n), lambda i: (i, 0)), pl.BlockSpec((B_HEADS * V_HEAD, tm), lambda i: (0, i))),
        compiler_params=_cparams("parallel"),
        name="mla_kv_up",
    )(ckvn, wk, wvt)


def _mla_kernel(q_ref, kn_ref, kr_ref, vt_ref, o_ref, sa_ref, sb_ref, xa_ref, xb_ref, m_ref, l_ref, acc_ref, *, tq):
    qi = pl.program_id(2)
    tk = tq // 2
    q = q_ref[...]
    sa_ref, sb_ref, acc_ref = (r.at[:, pl.ds(0, tq)] for r in (sa_ref, sb_ref, acc_ref))
    m_ref[...] = jnp.full(m_ref.shape, NEG, F32)
    l_ref[...] = jnp.zeros(l_ref.shape, F32)
    acc_ref[...] = jnp.zeros(acc_ref.shape, F32)

    def scores(c, s_ref, x_ref):
        start = pl.multiple_of(c * tk, tk)
        k = jnp.concatenate([kn_ref[pl.ds(start, tk), :], kr_ref[pl.ds(start, tk), :]], axis=1)
        st = jnp.dot(k, q, preferred_element_type=F32)
        s_ref[...] = st
        x_ref[...] = jnp.max(st, axis=0, keepdims=True)

    def update(c, s_ref, x_ref, masked):
        start = pl.multiple_of(c * tk, tk)
        st = s_ref[...]
        if masked:
            key = lax.broadcasted_iota(jnp.int32, st.shape, 0) + (c * tk - qi * tq)
            qry = lax.broadcasted_iota(jnp.int32, st.shape, 1)
            st = jnp.where(key <= qry, st, NEG)
            cmax = jnp.max(st, axis=0, keepdims=True)
        else:
            cmax = x_ref[...]
        m_prev = m_ref[...]
        m_new = jnp.maximum(m_prev, cmax)
        a = jnp.exp2(m_prev - m_new)
        p = jnp.exp2(st - m_new)
        l_ref[...] = a * l_ref[...] + jnp.sum(p, axis=0, keepdims=True)
        pv = jnp.dot(vt_ref[:, pl.ds(start, tk)], p.astype(BF16), preferred_element_type=F32)
        acc_ref[...] = a * acc_ref[...] + pv
        m_ref[...] = m_new

    scores(0, sa_ref, xa_ref)

    def pair(i):
        c = 2 * i
        scores(c + 1, sb_ref, xb_ref)
        update(c, sa_ref, xa_ref, False)
        scores(c + 2, sa_ref, xa_ref)
        update(c + 1, sb_ref, xb_ref, False)

    def two_pairs(i, carry):
        pair(2 * i)
        pair(2 * i + 1)
        return carry

    lax.fori_loop(0, qi // 2, two_pairs, 0)

    @pl.when(qi % 2 == 1)
    def _():
        pair(qi - 1)

    c = 2 * qi
    late = pl.ds(tk, tq - tk)
    start = pl.multiple_of((c + 1) * tk, tk)
    k = jnp.concatenate([kn_ref[pl.ds(start, tk), :], kr_ref[pl.ds(start, tk), :]], axis=1)
    sb_ref[:, late] = jnp.dot(k, q_ref[:, late], preferred_element_type=F32)
    update(c, sa_ref, xa_ref, True)
    st = sb_ref[:, late]
    key = lax.broadcasted_iota(jnp.int32, st.shape, 0)
    qry = lax.broadcasted_iota(jnp.int32, st.shape, 1)
    st = jnp.where(key <= qry, st, NEG)
    m_prev = m_ref[:, late]
    m_new = jnp.maximum(m_prev, jnp.max(st, axis=0, keepdims=True))
    a = jnp.exp2(m_prev - m_new)
    p = jnp.exp2(st - m_new)
    l_ref[:, late] = a * l_ref[:, late] + jnp.sum(p, axis=0, keepdims=True)
    pv = jnp.dot(vt_ref[:, pl.ds(start, tk)], p.astype(BF16), preferred_element_type=F32)
    acc_ref[:, late] = a * acc_ref[:, late] + pv
    o_ref[...] = (acc_ref[...] / l_ref[...]).T.astype(o_ref.dtype)


def _mla_attention(qt, kn, krope, vt, B, S):
    T = qt.shape[1]
    tq = min(1024, S)
    nq = S // tq
    kern = functools.partial(_mla_kernel, tq=tq)
    return pl.pallas_call(
        kern,
        out_shape=jax.ShapeDtypeStruct((T, B_HEADS * V_HEAD), BF16),
        grid=(B, B_HEADS, nq),
        in_specs=[
            pl.BlockSpec((MLA_HEAD_PAD, tq), lambda b, h, i: (h, b * nq + i)),
            pl.BlockSpec((S, QK_NOPE), lambda b, h, i: (b, h)),
            pl.BlockSpec((S, LANE), lambda b, h, i: (b, 0)),
            pl.BlockSpec((V_HEAD, S), lambda b, h, i: (h, b)),
        ],
        out_specs=pl.BlockSpec((tq, V_HEAD), lambda b, h, i: (b * nq + i, h)),
        scratch_shapes=[
            pltpu.VMEM((tq // 2, tq + PITCH_PAD), F32),
            pltpu.VMEM((tq // 2, tq + PITCH_PAD), F32),
            pltpu.VMEM((1, tq), F32),
            pltpu.VMEM((1, tq), F32),
            pltpu.VMEM((1, tq), F32),
            pltpu.VMEM((1, tq), F32),
            pltpu.VMEM((V_HEAD, tq + PITCH_PAD), F32),
        ],
        compiler_params=_cparams("parallel", "parallel", "arbitrary"),
        name="mla_attention",
    )(qt, kn, krope, vt)


def _branch_kernel(ya_ref, yb_ref, wa_ref, wb_ref, ga_ref, gb_ref, o_ref):
    pa = jnp.dot(ya_ref[...], wa_ref[...], preferred_element_type=F32)
    pb = jnp.dot(yb_ref[...], wb_ref[...], preferred_element_type=F32)
    u = ga_ref[...].astype(F32) * pa + gb_ref[...].astype(F32) * pb
    o_ref[...] = u.astype(o_ref.dtype)


def _branch_mix(ya, yb, wa, wb, proj):
    T = ya.shape[0]
    tm, tn = min(512, T), 1024
    return pl.pallas_call(
        _branch_kernel,
        out_shape=jax.ShapeDtypeStruct((T, D_MODEL), BF16),
        grid=(T // tm, D_MODEL // tn),
        in_specs=[
            pl.BlockSpec((tm, A_WIDTH), lambda i, j: (i, 0)),
            pl.BlockSpec((tm, B_HEADS * V_HEAD), lambda i, j: (i, 0)),
            pl.BlockSpec((A_WIDTH, tn), lambda i, j: (0, j)),
            pl.BlockSpec((B_HEADS * V_HEAD, tn), lambda i, j: (0, j)),
            pl.BlockSpec((tm, tn), lambda i, j: (i, COL_GA // tn + j)),
            pl.BlockSpec((tm, tn), lambda i, j: (i, COL_GB // tn + j)),
        ],
        out_specs=pl.BlockSpec((tm, tn), lambda i, j: (i, j)),
        compiler_params=_cparams("parallel", "arbitrary"),
        name="branch_mix",
    )(ya, yb, wa, wb, proj, proj)


def _outln_kernel(u_ref, w_ref, x_ref, g_ref, b_ref, h_ref, hb_ref):
    mix = jnp.dot(u_ref[...], w_ref[...], preferred_element_type=F32)
    h = _layer_norm(ALPHA * x_ref[...] + mix, g_ref[...], b_ref[...])
    h_ref[...] = h
    hb_ref[...] = h.astype(hb_ref.dtype)


def _out_projection_ln(u, w_out, x2, g, b):
    T = u.shape[0]
    tm = min(256, T)
    row = pl.BlockSpec((tm, D_MODEL), lambda i: (i, 0))
    vec = pl.BlockSpec((1, D_MODEL), lambda i: (0, 0))
    return pl.pallas_call(
        _outln_kernel,
        out_shape=(
            jax.ShapeDtypeStruct((T, D_MODEL), F32),
            jax.ShapeDtypeStruct((T, D_MODEL), BF16),
        ),
        grid=(T // tm,),
        in_specs=[row, pl.BlockSpec((D_MODEL, D_MODEL), lambda i: (0, 0)), row, vec, vec],
        out_specs=(row, row),
        compiler_params=_cparams("parallel"),
        name="out_projection_ln",
    )(u, w_out, x2, g, b)


def _topk_axis0(s, ids, k):
    big = jnp.int32(2 ** 30)
    vals, idxs = [], []
    for _ in range(k):
        m = jnp.max(s, axis=0, keepdims=True)
        idx = jnp.min(jnp.where(s == m, ids, big), axis=0, keepdims=True)
        vals.append(m)
        idxs.append(idx)
        s = jnp.where(ids == idx, -jnp.inf, s)
    return vals, idxs


def _select_rows(rows, sel):
    out = jnp.zeros(sel.shape, rows[0].dtype)
    for a, r in enumerate(rows):
        out = jnp.where(sel == a, r, out)
    return out


def _route_kernel(q_ref, k1_ref, k2_ref, i1_ref, i2_ref, g_ref):
    half = PEER_QDIM // 2
    k, tm = PEER_TOPK, q_ref.shape[0]
    key_id = lax.broadcasted_iota(jnp.int32, (N_KEYS, tm), 0)
    sub = lax.broadcasted_iota(jnp.int32, (k // 2, tm), 0)
    cand_pos = jnp.concatenate([a * k + sub for a in range(k // 2)] + [k // 2 + sub, (k // 2 + sub) * k], axis=0)
    i1_all, i2_all, g_all = [], [], []
    for h in range(PEER_HEADS):
        q1 = q_ref[:, h * PEER_QDIM:h * PEER_QDIM + half]
        q2 = q_ref[:, h * PEER_QDIM + half:(h + 1) * PEER_QDIM]
        s1 = lax.dot_general(k1_ref[...], q1, _NT, preferred_element_type=F32)
        s2 = lax.dot_general(k2_ref[...], q2, _NT, preferred_element_type=F32)
        v1, i1 = _topk_axis0(s1, key_id, k)
        v2, i2 = _topk_axis0(s2, key_id, k)
        v1m = jnp.concatenate(v1, axis=0)
        v2m = jnp.concatenate(v2, axis=0)
        cand = jnp.concatenate(
            [v1[a] + v2m[:k // 2] for a in range(k // 2)] + [v1[0] + v2m[k // 2:], v1m[k // 2:] + v2[0]], axis=0)
        ts, pos = _topk_axis0(cand, cand_pos, k)
        top = jnp.concatenate(ts, axis=0)
        e = jnp.exp(top - ts[0])
        g_all.append(e / jnp.sum(e, axis=0, keepdims=True))
        posm = jnp.concatenate(pos, axis=0)
        i1_all.append(_select_rows(i1, posm >> int(math.log2(k))))
        i2_all.append(_select_rows(i2, posm & (k - 1)))
    i1_ref[...] = jnp.concatenate(i1_all, axis=0).T
    i2_ref[...] = jnp.concatenate(i2_all, axis=0).T
    g_ref[...] = jnp.concatenate(g_all, axis=0).T


def _peer_route(qp, k1, k2):
    T = qp.shape[0]
    tm = min(256, T)
    slot = pl.BlockSpec((tm, N_SLOTS), lambda i: (i, 0))
    keys = pl.BlockSpec((N_KEYS, PEER_QDIM // 2), lambda i: (0, 0))
    return pl.pallas_call(
        _route_kernel,
        out_shape=(
            jax.ShapeDtypeStruct((T, N_SLOTS), jnp.int32),
            jax.ShapeDtypeStruct((T, N_SLOTS), jnp.int32),
            jax.ShapeDtypeStruct((T, N_SLOTS), F32),
        ),
        grid=(T // tm,),
        in_specs=[pl.BlockSpec((tm, PEER_HEADS * PEER_QDIM), lambda i: (i, 0)), keys, keys],
        out_specs=(slot, slot, slot),
        compiler_params=_cparams("parallel"),
        name="peer_route",
    )(qp, k1, k2)


GATE_GROUP = 64


def _gate_matrix_kernel(i1_ref, i2_ref, g_ref, w_ref):
    key = lax.broadcasted_iota(jnp.int32, (N_KEYS, N_SLOTS), 0)

    def body(tg, carry):
        t0 = pl.multiple_of(tg * GATE_GROUP, GATE_GROUP)
        per_token = []
        for u in range(GATE_GROUP):
            r1 = i1_ref[pl.ds(t0 + u, 1), :]
            r2 = i2_ref[pl.ds(t0 + u, 1), :]
            g = g_ref[pl.ds(t0 + u, 1), :]
            a = jnp.where(key == r1, 1.0, 0.0).astype(BF16)
            b = jnp.where(key == r2, g, 0.0).astype(BF16)
            per_token.append(lax.dot_general(a, b, _NT, preferred_element_type=F32))
        w = pltpu.einshape("tid->itd", jnp.stack(per_token, axis=0))
        w_ref[:, pl.ds(t0, GATE_GROUP), :] = w.astype(w_ref.dtype)
        return carry

    lax.fori_loop(0, w_ref.shape[1] // GATE_GROUP, body, 0)


def _gate_matrix(i1, i2, g):
    T = i1.shape[0]
    tb = min(128, T)
    slot = pl.BlockSpec((tb, N_SLOTS), lambda i: (i, 0))
    return pl.pallas_call(
        _gate_matrix_kernel,
        out_shape=jax.ShapeDtypeStruct((N_KEYS, T, N_KEYS), BF16),
        grid=(T // tb,),
        in_specs=[slot, slot, slot],
        out_specs=pl.BlockSpec((N_KEYS, tb, N_KEYS), lambda i: (0, i, 0)),
        compiler_params=_cparams("parallel"),
        name="peer_gate_matrix",
    )(i1, i2, g)


EXPERT_TILE = 1024


def _experts_kernel(hb_ref, dn_ref, up_ref, w_ref, y_ref, acc_ref):
    j = pl.program_id(1)

    @pl.when(j == 0)
    def _():
        acc_ref[...] = jnp.zeros(acc_ref.shape, F32)

    pre = jnp.dot(hb_ref[...], dn_ref[...], preferred_element_type=F32)
    act = 0.5 * pre * (1.0 + lax.erf(pre * (2.0 ** -0.5)))
    gates = jnp.concatenate([w_ref[m] for m in range(w_ref.shape[0])], axis=1)
    act = act * gates.astype(F32)
    acc_ref[...] += jnp.dot(act.astype(BF16), up_ref[...], preferred_element_type=F32)

    @pl.when(j == pl.num_programs(1) - 1)
    def _():
        y_ref[...] = acc_ref[...].astype(y_ref.dtype)


def _peer_experts(hb, down_tiles, up, w3):
    T = hb.shape[0]
    tm, te = min(1024, T), EXPERT_TILE
    row = pl.BlockSpec((tm, D_MODEL), lambda i, j: (i, 0))
    return pl.pallas_call(
        _experts_kernel,
        out_shape=jax.ShapeDtypeStruct((T, D_MODEL), BF16),
        grid=(T // tm, N_EXPERTS // te),
        in_specs=[
            row,
            pl.BlockSpec((None, D_MODEL, te), lambda i, j: (j, 0, 0)),
            pl.BlockSpec((te, D_MODEL), lambda i, j: (j, 0)),
            pl.BlockSpec((te // N_KEYS, tm, N_KEYS), lambda i, j: (j, i, 0)),
        ],
        out_specs=row,
        scratch_shapes=[pltpu.VMEM((tm, D_MODEL), F32)],
        compiler_params=_cparams("parallel", "arbitrary"),
        name="peer_experts",
    )(hb, down_tiles, up, w3)


def _residual_ln_kernel(y_ref, h_ref, g_ref, b_ref, o_ref):
    o_ref[...] = _layer_norm(ALPHA * h_ref[...] + y_ref[...].astype(F32), g_ref[...], b_ref[...])


def _residual_ln(y, h, g, b):
    T = y.shape[0]
    tm = min(512, T)
    row = pl.BlockSpec((tm, D_MODEL), lambda i: (i, 0))
    vec = pl.BlockSpec((1, D_MODEL), lambda i: (0, 0))
    return pl.pallas_call(
        _residual_ln_kernel,
        out_shape=jax.ShapeDtypeStruct((T, D_MODEL), F32),
        grid=(T // tm,),
        in_specs=[row, row, vec, vec],
        out_specs=row,
        compiler_params=_cparams("parallel"),
        name="peer_residual_ln",
    )(y, h, g, b)


def _rope_tables(S):
    half = QK_ROPE // 2
    inv_freq = ROPE_THETA ** (-jnp.arange(half, dtype=F32) / half)
    ang = jnp.arange(S, dtype=jnp.int32).astype(F32)[:, None] * inv_freq[None, :]
    cos, sin = jnp.cos(ang), jnp.sin(ang)
    zeros = jnp.zeros((S, LANE - QK_ROPE), F32)
    lanes = (jnp.concatenate([cos, cos, zeros], axis=1), jnp.concatenate([-sin, sin, zeros], axis=1))
    return lanes, (cos.T, sin.T)


def _pack_input_weights(w_in, b_gates):
    wq, wk, wv, wcq, wckv, wkr, wg = jnp.split(
        w_in, (A_WIDTH, 2 * A_WIDTH, 3 * A_WIDTH, 3 * A_WIDTH + Q_LORA,
               3 * A_WIDTH + Q_LORA + KV_LORA, 3 * A_WIDTH + Q_LORA + KV_LORA + QK_ROPE), axis=1)
    pad = jnp.zeros((D_MODEL, PROJ_WIDTH - COL_KR - QK_ROPE), w_in.dtype)
    wq = wq * (A_HEAD_DIM ** -0.5)
    w_all = jnp.concatenate([wq, wk, wv, wg, wcq, wckv, wkr, pad], axis=1).astype(BF16)
    b_all = jnp.zeros((1, PROJ_WIDTH), F32).at[0, COL_GA:COL_CQ].set(b_gates)
    return w_all, b_all


def _pack_uq(w_uq):
    w = w_uq.reshape(Q_LORA, B_HEADS, QK_NOPE + QK_ROPE)
    w = jnp.pad(w, ((0, 0), (0, 0), (0, MLA_HEAD_PAD - QK_NOPE - QK_ROPE)))
    return w.reshape(Q_LORA, B_HEADS * MLA_HEAD_PAD).T.astype(BF16)


def kernel(x, w_in, b_gates, a_w_out, mla_q_norm, mla_w_uq, mla_kv_norm, mla_w_ukv, mla_w_out, w_out,
           ln1_g, ln1_b, peer_w_query, peer_sub_keys_1, peer_sub_keys_2, peer_expert_down,
           peer_expert_up, ln2_g, ln2_b):
    B, S, D = x.shape
    assert D == D_MODEL and w_in.shape[0] == DEPTH
    T = B * S
    (cos_t, sin_t), (cos_rt, sin_rt) = _rope_tables(S)
    h = x.reshape(T, D)
    for l in range(DEPTH):
        w_all, b_all = _pack_input_weights(w_in[l], b_gates[l])
        proj, qkv_s = _in_projection(h, w_all, b_all, B, S)

        (w1, d1), (w4, d4), (w16, d16) = A_PATTERNS
        assert (d1, d4, d16) == (1, 4, STREAMS)
        o1, l1 = _dilated_dense(proj, B, S, w1)
        o4, l4 = _dilated_streams(qkv_s, w4, d4)
        o16, l16 = _dilated_streams(qkv_s, w16, d16)
        ya = _combine_patterns(o1, l1, o4, l4, o16, l16, B, S)

        cqn, ckvn, krope = _latent_prep(proj, mla_q_norm[l][None], mla_kv_norm[l][None], cos_t, sin_t, S)
        qt = _q_up(cqn, _pack_uq(mla_w_uq[l]), cos_rt, sin_rt, S)
        w_ukv = mla_w_ukv[l].reshape(KV_LORA, B_HEADS, QK_NOPE + V_HEAD)
        wk = w_ukv[:, :, :QK_NOPE].reshape(KV_LORA, B_HEADS * QK_NOPE).astype(BF16)
        wvt = w_ukv[:, :, QK_NOPE:].reshape(KV_LORA, B_HEADS * V_HEAD).T.astype(BF16)
        kn, vt = _kv_up(ckvn, wk, wvt)
        yb = _mla_attention(qt, kn, krope, vt, B, S)

        u = _branch_mix(ya, yb, a_w_out[l].astype(BF16), mla_w_out[l].astype(BF16), proj)
        h1, h1b = _out_projection_ln(u, w_out[l].astype(BF16), h, ln1_g[l][None], ln1_b[l][None])

        qp = _matmul(h1b, peer_w_query[l].astype(BF16), "peer_query")
        i1, i2, g = _peer_route(qp, peer_sub_keys_1[l].astype(BF16), peer_sub_keys_2[l].astype(BF16))
        w3 = _gate_matrix(i1, i2, g)
        down_tiles = peer_expert_down[l].reshape(N_EXPERTS // EXPERT_TILE, EXPERT_TILE, D).transpose(0, 2, 1)
        yp = _peer_experts(h1b, down_tiles.astype(BF16), peer_expert_up[l].astype(BF16), w3)
        h = _residual_ln(yp, h1, ln2_g[l][None], ln2_b[l][None])
    return h.reshape(B, S, D)
```

```python
import functools
import math

import jax
import jax.numpy as jnp
from jax import lax
from jax.experimental import pallas as pl
from jax.experimental.pallas import tpu as pltpu

F32 = jnp.float32
BF16 = jnp.bfloat16

D_MODEL = 2048
A_HEADS = 16
A_HEAD_DIM = 128
A_PATTERNS = ((128, 1), (512, 4), (2048, 16))
A_BLOCK = 128
A_WIDTH = A_HEADS * A_HEAD_DIM
B_HEADS = 16
Q_LORA = 512
KV_LORA = 512
QK_NOPE = 128
QK_ROPE = 64
V_HEAD = 128
ROPE_THETA = 10000.0
N_KEYS = 128
PEER_HEADS = 8
PEER_QDIM = 256
PEER_TOPK = 16
N_EXPERTS = N_KEYS * N_KEYS
N_SLOTS = PEER_HEADS * PEER_TOPK
LN_EPS = 1e-5
RMS_EPS = 1e-6
DEPTH = 1
ALPHA = (2.0 * DEPTH) ** 0.25
NEG = -1e30

LANE = 128
MLA_HEAD_PAD = 256
VMEM_LIMIT = 56 * 1024 * 1024

COL_Q, COL_K, COL_V = 0, A_WIDTH, 2 * A_WIDTH
COL_GA = 3 * A_WIDTH
COL_GB = COL_GA + D_MODEL
COL_CQ = COL_GB + D_MODEL
COL_CKV = COL_CQ + Q_LORA
COL_KR = COL_CKV + KV_LORA
PROJ_WIDTH = COL_CQ + 2048

_NT = (((1,), (1,)), ((), ()))


def _cparams(*sem):
    return pltpu.CompilerParams(dimension_semantics=sem, vmem_limit_bytes=VMEM_LIMIT)


def _layer_norm(z, g, b):
    mu = jnp.mean(z, axis=-1, keepdims=True)
    zc = z - mu
    var = jnp.mean(zc * zc, axis=-1, keepdims=True)
    return zc * lax.rsqrt(var + LN_EPS) * g + b


STREAMS = max(d for _, d in A_PATTERNS)
PERM_ROWS = 16 * STREAMS


def _stream_of_residue(r):
    return (r % 4) * 4 + r // 4


def _stream_permutation():
    tok = jnp.arange(PERM_ROWS, dtype=jnp.int32)
    out_row = _stream_of_residue(tok % STREAMS) * (PERM_ROWS // STREAMS) + tok // STREAMS
    return (out_row[None, :] == jnp.arange(PERM_ROWS, dtype=jnp.int32)[:, None]).astype(BF16)


def _inproj_kernel(x_ref, w_ref, b_ref, perm_ref, o_ref, qs_ref, *, qkv_hi, gate_lo, gate_hi):
    j = pl.program_id(1)
    acc = jnp.dot(x_ref[...].astype(BF16), w_ref[...], preferred_element_type=F32)
    is_gate = jnp.logical_and(j >= gate_lo, j < gate_hi)

    @pl.when(is_gate)
    def _():
        o_ref[...] = jax.nn.sigmoid(acc + b_ref[...]).astype(o_ref.dtype)

    @pl.when(jnp.logical_not(is_gate))
    def _():
        o_ref[...] = acc.astype(o_ref.dtype)

    @pl.when(j < qkv_hi)
    def _():
        rows = acc.astype(BF16)
        per = PERM_ROWS // STREAMS
        for g in range(rows.shape[0] // PERM_ROWS):
            grouped = jnp.dot(perm_ref[...], rows[g * PERM_ROWS:(g + 1) * PERM_ROWS],
                              preferred_element_type=F32).astype(qs_ref.dtype)
            for s in range(STREAMS):
                qs_ref[s, g * per:(g + 1) * per, :] = grouped[s * per:(s + 1) * per]


def _in_projection(x2, w_all, b_all, B, S):
    T = x2.shape[0]
    tm, tn = min(1024, S), 1024
    nt = S // tm
    qkv_hi = COL_GA // tn
    kern = functools.partial(_inproj_kernel, qkv_hi=qkv_hi, gate_lo=COL_GA // tn, gate_hi=COL_CQ // tn)
    return pl.pallas_call(
        kern,
        out_shape=(
            jax.ShapeDtypeStruct((T, PROJ_WIDTH), BF16),
            jax.ShapeDtypeStruct((B, STREAMS, S // STREAMS, 3 * A_WIDTH), BF16),
        ),
        grid=(T // tm, PROJ_WIDTH // tn),
        in_specs=[
            pl.BlockSpec((tm, D_MODEL), lambda i, j: (i, 0)),
            pl.BlockSpec((D_MODEL, tn), lambda i, j: (0, j)),
            pl.BlockSpec((1, tn), lambda i, j: (0, j)),
            pl.BlockSpec((PERM_ROWS, PERM_ROWS), lambda i, j: (0, 0)),
        ],
        out_specs=(
            pl.BlockSpec((tm, tn), lambda i, j: (i, j)),
            pl.BlockSpec((None, STREAMS, tm // STREAMS, tn),
                         lambda i, j: (i // nt, 0, i % nt, jnp.minimum(j, qkv_hi - 1))),
        ),
        compiler_params=_cparams("parallel", "arbitrary"),
        name="in_projection",
    )(x2, w_all, b_all, _stream_permutation())


def _block_pos(idx, groups):
    if groups == 1:
        return idx
    per = A_BLOCK // groups
    return groups * (idx % per) + idx // per


def _dilated_bias(dilation, steps, groups):
    blk = A_BLOCK
    i = jnp.arange(blk, dtype=jnp.int32)[:, None]
    c = jnp.arange(2 * blk, dtype=jnp.int32)[None, :]
    rel = blk + _block_pos(i, groups) - (blk * (c // blk) + _block_pos(c % blk, groups))
    band = (rel >= 0) & (rel <= steps)
    slopes = jnp.asarray([2.0 ** (-8.0 * (h + 1) / A_HEADS) for h in range(A_HEADS)], F32)
    bias = -slopes[:, None, None] * (dilation * rel).astype(F32)[None]
    first = jnp.where(band & (c >= blk), bias, NEG)
    later = jnp.where(band, bias, NEG)
    return jnp.stack([first, later], axis=0)


DILATED_SUB = 2


def _dilated_kernel(q_ref, kp_ref, ko_ref, vp_ref, vo_ref, bias_ref, o_ref, lse_ref):
    n = pl.program_id(2)
    blk = A_BLOCK
    per = kp_ref.shape[-2]
    lane = lax.broadcasted_iota(jnp.int32, (blk, LANE), 1)

    def rows(ref, u, sl):
        return ref[..., u * per:(u + 1) * per, sl].reshape(blk, A_HEAD_DIM)

    for u in range(DILATED_SUB):
        table = jnp.minimum(n, 1) if u == 0 else 1
        lse_all = jnp.zeros((blk, LANE), F32)
        for h in range(A_HEADS):
            sl = slice(h * A_HEAD_DIM, (h + 1) * A_HEAD_DIM)
            q = rows(q_ref, u, sl)
            k_prev = rows(kp_ref, 0, sl) if u == 0 else rows(ko_ref, u - 1, sl)
            v_prev = rows(vp_ref, 0, sl) if u == 0 else rows(vo_ref, u - 1, sl)
            k = jnp.concatenate([k_prev, rows(ko_ref, u, sl)], axis=0)
            v = jnp.concatenate([v_prev, rows(vo_ref, u, sl)], axis=0)
            logits = lax.dot_general(q, k, _NT, preferred_element_type=F32) + bias_ref[table, h]
            m = jnp.max(logits, axis=-1, keepdims=True)
            p = jnp.exp(logits - m)
            z = jnp.sum(p, axis=-1, keepdims=True)
            o = jnp.dot(p.astype(BF16), v, preferred_element_type=F32) / z
            o_ref[..., u * per:(u + 1) * per, sl] = o.astype(o_ref.dtype).reshape(kp_ref.shape[:-1] + (A_HEAD_DIM,))
            lse_all = jnp.where(lane == h, m + jnp.log(z), lse_all)
        lse_ref[..., u * per:(u + 1) * per, :] = lse_all.reshape(kp_ref.shape[:-1] + (LANE,))


def _dilated_dense(proj, B, S, window):
    T = B * S
    sub = DILATED_SUB
    nb = S // (A_BLOCK * sub)
    prev, own = (A_BLOCK, A_WIDTH), (A_BLOCK * sub, A_WIDTH)
    before = lambda b, n: b * nb * sub + jnp.maximum(n * sub - 1, 0)
    return pl.pallas_call(
        _dilated_kernel,
        out_shape=(jax.ShapeDtypeStruct((T, A_WIDTH), BF16), jax.ShapeDtypeStruct((T, LANE), F32)),
        grid=(B, 1, nb),
        in_specs=[
            pl.BlockSpec(own, lambda b, r, n: (b * nb + n, COL_Q // A_WIDTH)),
            pl.BlockSpec(prev, lambda b, r, n: (before(b, n), COL_K // A_WIDTH)),
            pl.BlockSpec(own, lambda b, r, n: (b * nb + n, COL_K // A_WIDTH)),
            pl.BlockSpec(prev, lambda b, r, n: (before(b, n), COL_V // A_WIDTH)),
            pl.BlockSpec(own, lambda b, r, n: (b * nb + n, COL_V // A_WIDTH)),
            pl.BlockSpec((2, A_HEADS, A_BLOCK, 2 * A_BLOCK), lambda b, r, n: (0, 0, 0, 0)),
        ],
        out_specs=(
            pl.BlockSpec(own, lambda b, r, n: (b * nb + n, 0)),
            pl.BlockSpec((A_BLOCK * sub, LANE), lambda b, r, n: (b * nb + n, 0)),
        ),
        compiler_params=_cparams("parallel", "parallel", "arbitrary"),
        name="dilated_attention_d1",
    )(proj, proj, proj, proj, proj, _dilated_bias(1, window, 1))


def _dilated_streams(qkv_s, window, dilation):
    B, ns, Ls, _ = qkv_s.shape
    groups = STREAMS // dilation
    per = A_BLOCK // groups
    sub = DILATED_SUB
    assert ns == STREAMS and Ls % (per * sub) == 0
    prev, own = (None, groups, per, A_WIDTH), (None, groups, per * sub, A_WIDTH)
    before = lambda n: jnp.maximum(n * sub - 1, 0)
    return pl.pallas_call(
        _dilated_kernel,
        out_shape=(
            jax.ShapeDtypeStruct((B, STREAMS, Ls, A_WIDTH), BF16),
            jax.ShapeDtypeStruct((B, STREAMS, Ls, LANE), F32),
        ),
        grid=(B, STREAMS // groups, Ls // (per * sub)),
        in_specs=[
            pl.BlockSpec(own, lambda b, r, n: (b, r, n, 0)),
            pl.BlockSpec(prev, lambda b, r, n: (b, r, before(n), 1)),
            pl.BlockSpec(own, lambda b, r, n: (b, r, n, 1)),
            pl.BlockSpec(prev, lambda b, r, n: (b, r, before(n), 2)),
            pl.BlockSpec(own, lambda b, r, n: (b, r, n, 2)),
            pl.BlockSpec((2, A_HEADS, A_BLOCK, 2 * A_BLOCK), lambda b, r, n: (0, 0, 0, 0)),
        ],
        out_specs=(
            pl.BlockSpec(own, lambda b, r, n: (b, r, n, 0)),
            pl.BlockSpec((None, groups, per * sub, LANE), lambda b, r, n: (b, r, n, 0)),
        ),
        compiler_params=_cparams("parallel", "parallel", "arbitrary"),
        name=f"dilated_attention_d{dilation}",
    )(qkv_s, qkv_s, qkv_s, qkv_s, qkv_s, _dilated_bias(dilation, window // dilation, groups))


COMBINE_STEPS = 16


def _combine_kernel(o1_ref, o2_ref, o3_ref, l1_ref, l2_ref, l3_ref, y_ref, ob_ref, oc_ref, lb_ref, lc_ref):
    for r in range(STREAMS):
        s = _stream_of_residue(r)
        tok = pl.ds(r, COMBINE_STEPS, stride=STREAMS)
        lb_ref[tok, :] = l2_ref[s]
        lc_ref[tok, :] = l3_ref[s]
        for h in range(A_HEADS):
            sl = slice(h * A_HEAD_DIM, (h + 1) * A_HEAD_DIM)
            ob_ref[h, tok, :] = o2_ref[s, :, sl].astype(F32)
            oc_ref[h, tok, :] = o3_ref[s, :, sl].astype(F32)
    a, b, c = l1_ref[...], lb_ref[...], lc_ref[...]
    m = jnp.maximum(jnp.maximum(a, b), c)
    ea, eb, ec = jnp.exp(a - m), jnp.exp(b - m), jnp.exp(c - m)
    inv = 1.0 / (ea + eb + ec)
    wa, wb, wc = ea * inv, eb * inv, ec * inv
    for h in range(A_HEADS):
        sl = slice(h * A_HEAD_DIM, (h + 1) * A_HEAD_DIM)
        y = (wa[:, h:h + 1] * o1_ref[:, sl].astype(F32)
             + wb[:, h:h + 1] * ob_ref[h]
             + wc[:, h:h + 1] * oc_ref[h])
        y_ref[:, sl] = y.astype(y_ref.dtype)


def _combine_patterns(o1, l1, o4, l4, o16, l16, B, S):
    T = B * S
    tm = COMBINE_STEPS * STREAMS
    nt = S // tm
    tok_o = pl.BlockSpec((tm, A_WIDTH), lambda b, i: (b * nt + i, 0))
    tok_l = pl.BlockSpec((tm, LANE), lambda b, i: (b * nt + i, 0))
    str_o = pl.BlockSpec((None, STREAMS, COMBINE_STEPS, A_WIDTH), lambda b, i: (b, 0, i, 0))
    str_l = pl.BlockSpec((None, STREAMS, COMBINE_STEPS, LANE), lambda b, i: (b, 0, i, 0))
    return pl.pallas_call(
        _combine_kernel,
        out_shape=jax.ShapeDtypeStruct((T, A_WIDTH), BF16),
        grid=(B, nt),
        in_specs=[tok_o, str_o, str_o, tok_l, str_l, str_l],
        out_specs=tok_o,
        scratch_shapes=[
            pltpu.VMEM((A_HEADS, tm, A_HEAD_DIM), F32),
            pltpu.VMEM((A_HEADS, tm, A_HEAD_DIM), F32),
            pltpu.VMEM((tm, LANE), F32),
            pltpu.VMEM((tm, LANE), F32),
        ],
        compiler_params=_cparams("parallel", "parallel"),
        name="combine_patterns",
    )(o1, o4, o16, l1, l4, l16)


def _rope_lanes(t, cos, sin):
    lane = lax.broadcasted_iota(jnp.int32, t.shape, 1)
    half = QK_ROPE // 2
    rot = jnp.where(lane < half, pltpu.roll(t, LANE - half, 1), pltpu.roll(t, half, 1))
    return t * cos + rot * sin


def _rms_norm(x, g):
    ms = jnp.mean(x * x, axis=-1, keepdims=True)
    return x * lax.rsqrt(ms + RMS_EPS) * g


def _latent_kernel(cq_ref, ckv_ref, kr_ref, gq_ref, gkv_ref, cos_ref, sin_ref, cqn_ref, ckvn_ref, krope_ref):
    cqn_ref[...] = _rms_norm(cq_ref[...].astype(F32), gq_ref[...]).astype(cqn_ref.dtype)
    ckvn_ref[...] = _rms_norm(ckv_ref[...].astype(F32), gkv_ref[...]).astype(ckvn_ref.dtype)
    krope_ref[...] = _rope_lanes(kr_ref[...].astype(F32), cos_ref[...], sin_ref[...]).astype(krope_ref.dtype)


def _latent_prep(proj, gq, gkv, cos_t, sin_t, S):
    T = proj.shape[0]
    tm = min(512, S)
    ns = S // tm
    return pl.pallas_call(
        _latent_kernel,
        out_shape=(
            jax.ShapeDtypeStruct((T, Q_LORA), BF16),
            jax.ShapeDtypeStruct((T, KV_LORA), BF16),
            jax.ShapeDtypeStruct((T, LANE), BF16),
        ),
        grid=(T // tm,),
        in_specs=[
            pl.BlockSpec((tm, Q_LORA), lambda i: (i, COL_CQ // Q_LORA)),
            pl.BlockSpec((tm, KV_LORA), lambda i: (i, COL_CKV // KV_LORA)),
            pl.BlockSpec((tm, LANE), lambda i: (i, COL_KR // LANE)),
            pl.BlockSpec((1, Q_LORA), lambda i: (0, 0)),
            pl.BlockSpec((1, KV_LORA), lambda i: (0, 0)),
            pl.BlockSpec((tm, LANE), lambda i: (i % ns, 0)),
            pl.BlockSpec((tm, LANE), lambda i: (i % ns, 0)),
        ],
        out_specs=(
            pl.BlockSpec((tm, Q_LORA), lambda i: (i, 0)),
            pl.BlockSpec((tm, KV_LORA), lambda i: (i, 0)),
            pl.BlockSpec((tm, LANE), lambda i: (i, 0)),
        ),
        compiler_params=_cparams("parallel"),
        name="latent_prep",
    )(proj, proj, proj, gq, gkv, cos_t, sin_t)


def _qup_kernel(c_ref, wt_ref, cos_ref, sin_ref, o_ref, *, scale):
    acc = lax.dot_general(wt_ref[...], c_ref[...], _NT, preferred_element_type=F32) * scale
    cos, sin = cos_ref[...], sin_ref[...]
    half = QK_ROPE // 2
    for hb in range(acc.shape[0] // MLA_HEAD_PAD):
        lo = hb * MLA_HEAD_PAD
        r1 = acc[lo + QK_NOPE:lo + QK_NOPE + half]
        r2 = acc[lo + QK_NOPE + half:lo + QK_NOPE + QK_ROPE]
        o_ref[lo:lo + QK_NOPE] = acc[lo:lo + QK_NOPE].astype(o_ref.dtype)
        o_ref[lo + QK_NOPE:lo + QK_NOPE + half] = (r1 * cos - r2 * sin).astype(o_ref.dtype)
        o_ref[lo + QK_NOPE + half:lo + QK_NOPE + QK_ROPE] = (r2 * cos + r1 * sin).astype(o_ref.dtype)
        o_ref[lo + QK_NOPE + QK_ROPE:lo + MLA_HEAD_PAD] = acc[lo + QK_NOPE + QK_ROPE:lo + MLA_HEAD_PAD].astype(o_ref.dtype)


def _q_up(cqn, w_uq_pt, cos_rt, sin_rt, S):
    T = cqn.shape[0]
    N = w_uq_pt.shape[0]
    tm, tn = min(1024, S), 1024
    ns = S // tm
    half = QK_ROPE // 2
    kern = functools.partial(_qup_kernel, scale=(QK_NOPE + QK_ROPE) ** -0.5 * math.log2(math.e))
    return pl.pallas_call(
        kern,
        out_shape=jax.ShapeDtypeStruct((N, T), BF16),
        grid=(T // tm, N // tn),
        in_specs=[
            pl.BlockSpec((tm, Q_LORA), lambda i, j: (i, 0)),
            pl.BlockSpec((tn, Q_LORA), lambda i, j: (j, 0)),
            pl.BlockSpec((half, tm), lambda i, j: (0, i % ns)),
            pl.BlockSpec((half, tm), lambda i, j: (0, i % ns)),
        ],
        out_specs=pl.BlockSpec((tn, tm), lambda i, j: (j, i)),
        compiler_params=_cparams("parallel", "arbitrary"),
        name="mla_q_up",
    )(cqn, w_uq_pt, cos_rt, sin_rt)


def _mm_kernel(a_ref, w_ref, o_ref):
    o_ref[...] = jnp.dot(a_ref[...], w_ref[...], preferred_element_type=F32).astype(o_ref.dtype)


def _matmul(a, w, name, tm=512, tn=1024):
    M, K = a.shape
    N = w.shape[1]
    tm, tn = min(tm, M), min(tn, N)
    return pl.pallas_call(
        _mm_kernel,
        out_shape=jax.ShapeDtypeStruct((M, N), BF16),
        grid=(M // tm, N // tn),
        in_specs=[
            pl.BlockSpec((tm, K), lambda i, j: (i, 0)),
            pl.BlockSpec((K, tn), lambda i, j: (0, j)),
        ],
        out_specs=pl.BlockSpec((tm, tn), lambda i, j: (i, j)),
        compiler_params=_cparams("parallel", "arbitrary"),
        name=name,
    )(a, w)


def _kvup_kernel(c_ref, wk_ref, wvt_ref, kn_ref, vt_ref):
    c = c_ref[...]
    kn_ref[...] = jnp.dot(c, wk_ref[...], preferred_element_type=F32).astype(kn_ref.dtype)
    vt_ref[...] = lax.dot_general(wvt_ref[...], c, _NT, preferred_element_type=F32).astype(vt_ref.dtype)


def _kv_up(ckvn, wk, wvt):
    T = ckvn.shape[0]
    tm = min(1024, T)
    n = B_HEADS * QK_NOPE
    return pl.pallas_call(
        _kvup_kernel,
        out_shape=(jax.ShapeDtypeStruct((T, n), BF16), jax.ShapeDtypeStruct((B_HEADS * V_HEAD, T), BF16)),
        grid=(T // tm,),
        in_specs=[
            pl.BlockSpec((tm, KV_LORA), lambda i: (i, 0)),
            pl.BlockSpec((KV_LORA, n), lambda i: (0, 0)),
            pl.BlockSpec((B_HEADS * V_HEAD, KV_LORA), lambda i: (0, 0)),
        ],
        out_specs=(pl.BlockSpec((tm, n), lambda i: (i, 0)), pl.BlockSpec((B_HEADS * V_HEAD, tm), lambda i: (0, i))),
        compiler_params=_cparams("parallel"),
        name="mla_kv_up",
    )(ckvn, wk, wvt)


def _mla_kernel(q_ref, kn_ref, kr_ref, vt_ref, o_ref, sa_ref, sb_ref, xa_ref, xb_ref, m_ref, l_ref, acc_ref, *, tq):
    qi = pl.program_id(2)
    tk = tq // 2
    q = q_ref[...]
    m_ref[...] = jnp.full(m_ref.shape, NEG, F32)
    l_ref[...] = jnp.zeros(l_ref.shape, F32)
    acc_ref[...] = jnp.zeros(acc_ref.shape, F32)

    def scores(c, s_ref, x_ref):
        start = pl.multiple_of(c * tk, tk)
        k = jnp.concatenate([kn_ref[pl.ds(start, tk), :], kr_ref[pl.ds(start, tk), :]], axis=1)
        st = jnp.dot(k, q, preferred_element_type=F32)
        s_ref[...] = st
        x_ref[...] = jnp.max(st, axis=0, keepdims=True)

    def update(c, s_ref, x_ref, masked):
        start = pl.multiple_of(c * tk, tk)
        st = s_ref[...]
        if masked:
            key = lax.broadcasted_iota(jnp.int32, st.shape, 0) + (c * tk - qi * tq)
            qry = lax.broadcasted_iota(jnp.int32, st.shape, 1)
            st = jnp.where(key <= qry, st, NEG)
            cmax = jnp.max(st, axis=0, keepdims=True)
        else:
            cmax = x_ref[...]
        m_prev = m_ref[...]
        m_new = jnp.maximum(m_prev, cmax)
        a = jnp.exp2(m_prev - m_new)
        p = jnp.exp2(st - m_new)
        l_ref[...] = a * l_ref[...] + jnp.sum(p, axis=0, keepdims=True)
        pv = jnp.dot(vt_ref[:, pl.ds(start, tk)], p.astype(BF16), preferred_element_type=F32)
        acc_ref[...] = a * acc_ref[...] + pv
        m_ref[...] = m_new

    scores(0, sa_ref, xa_ref)

    def pair(i):
        c = 2 * i
        scores(c + 1, sb_ref, xb_ref)
        update(c, sa_ref, xa_ref, False)
        scores(c + 2, sa_ref, xa_ref)
        update(c + 1, sb_ref, xb_ref, False)

    def two_pairs(i, carry):
        pair(2 * i)
        pair(2 * i + 1)
        return carry

    lax.fori_loop(0, qi // 2, two_pairs, 0)

    @pl.when(qi % 2 == 1)
    def _():
        pair(qi - 1)

    c = 2 * qi
    late = pl.ds(tk, tq - tk)
    start = pl.multiple_of((c + 1) * tk, tk)
    k = jnp.concatenate([kn_ref[pl.ds(start, tk), :], kr_ref[pl.ds(start, tk), :]], axis=1)
    sb_ref[:, late] = jnp.dot(k, q_ref[:, late], preferred_element_type=F32)
    update(c, sa_ref, xa_ref, True)
    st = sb_ref[:, late]
    key = lax.broadcasted_iota(jnp.int32, st.shape, 0)
    qry = lax.broadcasted_iota(jnp.int32, st.shape, 1)
    st = jnp.where(key <= qry, st, NEG)
    m_prev = m_ref[:, late]
    m_new = jnp.maximum(m_prev, jnp.max(st, axis=0, keepdims=True))
    a = jnp.exp2(m_prev - m_new)
    p = jnp.exp2(st - m_new)
    l_ref[:, late] = a * l_ref[:, late] + jnp.sum(p, axis=0, keepdims=True)
    pv = jnp.dot(vt_ref[:, pl.ds(start, tk)], p.astype(BF16), preferred_element_type=F32)
    acc_ref[:, late] = a * acc_ref[:, late] + pv
    o_ref[...] = (acc_ref[...] / l_ref[...]).T.astype(o_ref.dtype)


def _mla_attention(qt, kn, krope, vt, B, S):
    T = qt.shape[1]
    tq = min(1024, S)
    nq = S // tq
    kern = functools.partial(_mla_kernel, tq=tq)
    return pl.pallas_call(
        kern,
        out_shape=jax.ShapeDtypeStruct((T, B_HEADS * V_HEAD), BF16),
        grid=(B, B_HEADS, nq),
        in_specs=[
            pl.BlockSpec((MLA_HEAD_PAD, tq), lambda b, h, i: (h, b * nq + i)),
            pl.BlockSpec((S, QK_NOPE), lambda b, h, i: (b, h)),
            pl.BlockSpec((S, LANE), lambda b, h, i: (b, 0)),
            pl.BlockSpec((V_HEAD, S), lambda b, h, i: (h, b)),
        ],
        out_specs=pl.BlockSpec((tq, V_HEAD), lambda b, h, i: (b * nq + i, h)),
        scratch_shapes=(
            [pltpu.VMEM((tq // 2, tq), F32)] * 2
            + [pltpu.VMEM((1, tq), F32)] * 4
            + [pltpu.VMEM((V_HEAD, tq), F32)]
        ),
        compiler_params=_cparams("parallel", "parallel", "arbitrary"),
        name="mla_attention",
    )(qt, kn, krope, vt)


def _branch_kernel(ya_ref, yb_ref, wa_ref, wb_ref, ga_ref, gb_ref, o_ref):
    pa = jnp.dot(ya_ref[...], wa_ref[...], preferred_element_type=F32)
    pb = jnp.dot(yb_ref[...], wb_ref[...], preferred_element_type=F32)
    u = ga_ref[...].astype(F32) * pa + gb_ref[...].astype(F32) * pb
    o_ref[...] = u.astype(o_ref.dtype)


def _branch_mix(ya, yb, wa, wb, proj):
    T = ya.shape[0]
    tm, tn = min(512, T), 1024
    return pl.pallas_call(
        _branch_kernel,
        out_shape=jax.ShapeDtypeStruct((T, D_MODEL), BF16),
        grid=(T // tm, D_MODEL // tn),
        in_specs=[
            pl.BlockSpec((tm, A_WIDTH), lambda i, j: (i, 0)),
            pl.BlockSpec((tm, B_HEADS * V_HEAD), lambda i, j: (i, 0)),
            pl.BlockSpec((A_WIDTH, tn), lambda i, j: (0, j)),
            pl.BlockSpec((B_HEADS * V_HEAD, tn), lambda i, j: (0, j)),
            pl.BlockSpec((tm, tn), lambda i, j: (i, COL_GA // tn + j)),
            pl.BlockSpec((tm, tn), lambda i, j: (i, COL_GB // tn + j)),
        ],
        out_specs=pl.BlockSpec((tm, tn), lambda i, j: (i, j)),
        compiler_params=_cparams("parallel", "arbitrary"),
        name="branch_mix",
    )(ya, yb, wa, wb, proj, proj)


def _outln_kernel(u_ref, w_ref, x_ref, g_ref, b_ref, h_ref, hb_ref):
    mix = jnp.dot(u_ref[...], w_ref[...], preferred_element_type=F32)
    h = _layer_norm(ALPHA * x_ref[...] + mix, g_ref[...], b_ref[...])
    h_ref[...] = h
    hb_ref[...] = h.astype(hb_ref.dtype)


def _out_projection_ln(u, w_out, x2, g, b):
    T = u.shape[0]
    tm = min(512, T)
    row = pl.BlockSpec((tm, D_MODEL), lambda i: (i, 0))
    vec = pl.BlockSpec((1, D_MODEL), lambda i: (0, 0))
    return pl.pallas_call(
        _outln_kernel,
        out_shape=(
            jax.ShapeDtypeStruct((T, D_MODEL), F32),
            jax.ShapeDtypeStruct((T, D_MODEL), BF16),
        ),
        grid=(T // tm,),
        in_specs=[row, pl.BlockSpec((D_MODEL, D_MODEL), lambda i: (0, 0)), row, vec, vec],
        out_specs=(row, row),
        compiler_params=_cparams("parallel"),
        name="out_projection_ln",
    )(u, w_out, x2, g, b)


def _topk_axis0(s, ids, k):
    big = jnp.int32(2 ** 30)
    vals, idxs = [], []
    for _ in range(k):
        m = jnp.max(s, axis=0, keepdims=True)
        idx = jnp.min(jnp.where(s == m, ids, big), axis=0, keepdims=True)
        vals.append(m)
        idxs.append(idx)
        s = jnp.where(ids == idx, -jnp.inf, s)
    return vals, idxs


def _select_rows(rows, sel):
    out = jnp.zeros(sel.shape, rows[0].dtype)
    for a, r in enumerate(rows):
        out = jnp.where(sel == a, r, out)
    return out


def _route_kernel(q_ref, k1_ref, k2_ref, i1_ref, i2_ref, g_ref):
    half = PEER_QDIM // 2
    k, tm = PEER_TOPK, q_ref.shape[0]
    key_id = lax.broadcasted_iota(jnp.int32, (N_KEYS, tm), 0)
    sub = lax.broadcasted_iota(jnp.int32, (k // 2, tm), 0)
    cand_pos = jnp.concatenate([a * k + sub for a in range(k // 2)] + [k // 2 + sub, (k // 2 + sub) * k], axis=0)
    i1_all, i2_all, g_all = [], [], []
    for h in range(PEER_HEADS):
        q1 = q_ref[:, h * PEER_QDIM:h * PEER_QDIM + half]
        q2 = q_ref[:, h * PEER_QDIM + half:(h + 1) * PEER_QDIM]
        s1 = lax.dot_general(k1_ref[...], q1, _NT, preferred_element_type=F32)
        s2 = lax.dot_general(k2_ref[...], q2, _NT, preferred_element_type=F32)
        v1, i1 = _topk_axis0(s1, key_id, k)
        v2, i2 = _topk_axis0(s2, key_id, k)
        v1m = jnp.concatenate(v1, axis=0)
        v2m = jnp.concatenate(v2, axis=0)
        cand = jnp.concatenate(
            [v1[a] + v2m[:k // 2] for a in range(k // 2)] + [v1[0] + v2m[k // 2:], v1m[k // 2:] + v2[0]], axis=0)
        ts, pos = _topk_axis0(cand, cand_pos, k)
        top = jnp.concatenate(ts, axis=0)
        e = jnp.exp(top - ts[0])
        g_all.append(e / jnp.sum(e, axis=0, keepdims=True))
        posm = jnp.concatenate(pos, axis=0)
        i1_all.append(_select_rows(i1, posm >> int(math.log2(k))))
        i2_all.append(_select_rows(i2, posm & (k - 1)))
    i1_ref[...] = jnp.concatenate(i1_all, axis=0).T
    i2_ref[...] = jnp.concatenate(i2_all, axis=0).T
    g_ref[...] = jnp.concatenate(g_all, axis=0).T


def _peer_route(qp, k1, k2):
    T = qp.shape[0]
    tm = min(256, T)
    slot = pl.BlockSpec((tm, N_SLOTS), lambda i: (i, 0))
    keys = pl.BlockSpec((N_KEYS, PEER_QDIM // 2), lambda i: (0, 0))
    return pl.pallas_call(
        _route_kernel,
        out_shape=(
            jax.ShapeDtypeStruct((T, N_SLOTS), jnp.int32),
            jax.ShapeDtypeStruct((T, N_SLOTS), jnp.int32),
            jax.ShapeDtypeStruct((T, N_SLOTS), F32),
        ),
        grid=(T // tm,),
        in_specs=[pl.BlockSpec((tm, PEER_HEADS * PEER_QDIM), lambda i: (i, 0)), keys, keys],
        out_specs=(slot, slot, slot),
        compiler_params=_cparams("parallel"),
        name="peer_route",
    )(qp, k1, k2)


GATE_GROUP = 64


def _gate_matrix_kernel(i1_ref, i2_ref, g_ref, w_ref):
    key = lax.broadcasted_iota(jnp.int32, (N_KEYS, N_SLOTS), 0)

    def body(tg, carry):
        t0 = pl.multiple_of(tg * GATE_GROUP, GATE_GROUP)
        per_token = []
        for u in range(GATE_GROUP):
            r1 = i1_ref[pl.ds(t0 + u, 1), :]
            r2 = i2_ref[pl.ds(t0 + u, 1), :]
            g = g_ref[pl.ds(t0 + u, 1), :]
            a = jnp.where(key == r1, 1.0, 0.0).astype(BF16)
            b = jnp.where(key == r2, g, 0.0).astype(BF16)
            per_token.append(lax.dot_general(a, b, _NT, preferred_element_type=F32))
        w = pltpu.einshape("tid->itd", jnp.stack(per_token, axis=0))
        w_ref[:, pl.ds(t0, GATE_GROUP), :] = w.astype(w_ref.dtype)
        return carry

    lax.fori_loop(0, w_ref.shape[1] // GATE_GROUP, body, 0)


def _gate_matrix(i1, i2, g):
    T = i1.shape[0]
    tb = min(128, T)
    slot = pl.BlockSpec((tb, N_SLOTS), lambda i: (i, 0))
    return pl.pallas_call(
        _gate_matrix_kernel,
        out_shape=jax.ShapeDtypeStruct((N_KEYS, T, N_KEYS), BF16),
        grid=(T // tb,),
        in_specs=[slot, slot, slot],
        out_specs=pl.BlockSpec((N_KEYS, tb, N_KEYS), lambda i: (0, i, 0)),
        compiler_params=_cparams("parallel"),
        name="peer_gate_matrix",
    )(i1, i2, g)


EXPERT_TILE = 1024


def _experts_kernel(hb_ref, dn_ref, up_ref, w_ref, y_ref, acc_ref):
    j = pl.program_id(1)

    @pl.when(j == 0)
    def _():
        acc_ref[...] = jnp.zeros(acc_ref.shape, F32)

    pre = jnp.dot(hb_ref[...], dn_ref[...], preferred_element_type=F32)
    act = 0.5 * pre * (1.0 + lax.erf(pre * (2.0 ** -0.5)))
    gates = jnp.concatenate([w_ref[m] for m in range(w_ref.shape[0])], axis=1)
    act = act * gates.astype(F32)
    acc_ref[...] += jnp.dot(act.astype(BF16), up_ref[...], preferred_element_type=F32)

    @pl.when(j == pl.num_programs(1) - 1)
    def _():
        y_ref[...] = acc_ref[...].astype(y_ref.dtype)


def _peer_experts(hb, down_tiles, up, w3):
    T = hb.shape[0]
    tm, te = min(1024, T), EXPERT_TILE
    row = pl.BlockSpec((tm, D_MODEL), lambda i, j: (i, 0))
    return pl.pallas_call(
        _experts_kernel,
        out_shape=jax.ShapeDtypeStruct((T, D_MODEL), BF16),
        grid=(T // tm, N_EXPERTS // te),
        in_specs=[
            row,
            pl.BlockSpec((None, D_MODEL, te), lambda i, j: (j, 0, 0)),
            pl.BlockSpec((te, D_MODEL), lambda i, j: (j, 0)),
            pl.BlockSpec((te // N_KEYS, tm, N_KEYS), lambda i, j: (j, i, 0)),
        ],
        out_specs=row,
        scratch_shapes=[pltpu.VMEM((tm, D_MODEL), F32)],
        compiler_params=_cparams("parallel", "arbitrary"),
        name="peer_experts",
    )(hb, down_tiles, up, w3)


def _residual_ln_kernel(y_ref, h_ref, g_ref, b_ref, o_ref):
    o_ref[...] = _layer_norm(ALPHA * h_ref[...] + y_ref[...].astype(F32), g_ref[...], b_ref[...])


def _residual_ln(y, h, g, b):
    T = y.shape[0]
    tm = min(512, T)
    row = pl.BlockSpec((tm, D_MODEL), lambda i: (i, 0))
    vec = pl.BlockSpec((1, D_MODEL), lambda i: (0, 0))
    return pl.pallas_call(
        _residual_ln_kernel,
        out_shape=jax.ShapeDtypeStruct((T, D_MODEL), F32),
        grid=(T // tm,),
        in_specs=[row, row, vec, vec],
        out_specs=row,
        compiler_params=_cparams("parallel"),
        name="peer_residual_ln",
    )(y, h, g, b)


def _rope_tables(S):
    half = QK_ROPE // 2
    inv_freq = ROPE_THETA ** (-jnp.arange(half, dtype=F32) / half)
    ang = jnp.arange(S, dtype=jnp.int32).astype(F32)[:, None] * inv_freq[None, :]
    cos, sin = jnp.cos(ang), jnp.sin(ang)
    zeros = jnp.zeros((S, LANE - QK_ROPE), F32)
    lanes = (jnp.concatenate([cos, cos, zeros], axis=1), jnp.concatenate([-sin, sin, zeros], axis=1))
    return lanes, (cos.T, sin.T)


def _pack_input_weights(w_in, b_gates):
    wq, wk, wv, wcq, wckv, wkr, wg = jnp.split(
        w_in, (A_WIDTH, 2 * A_WIDTH, 3 * A_WIDTH, 3 * A_WIDTH + Q_LORA,
               3 * A_WIDTH + Q_LORA + KV_LORA, 3 * A_WIDTH + Q_LORA + KV_LORA + QK_ROPE), axis=1)
    pad = jnp.zeros((D_MODEL, PROJ_WIDTH - COL_KR - QK_ROPE), w_in.dtype)
    wq = wq * (A_HEAD_DIM ** -0.5)
    w_all = jnp.concatenate([wq, wk, wv, wg, wcq, wckv, wkr, pad], axis=1).astype(BF16)
    b_all = jnp.zeros((1, PROJ_WIDTH), F32).at[0, COL_GA:COL_CQ].set(b_gates)
    return w_all, b_all


def _pack_uq(w_uq):
    w = w_uq.reshape(Q_LORA, B_HEADS, QK_NOPE + QK_ROPE)
    w = jnp.pad(w, ((0, 0), (0, 0), (0, MLA_HEAD_PAD - QK_NOPE - QK_ROPE)))
    return w.reshape(Q_LORA, B_HEADS * MLA_HEAD_PAD).T.astype(BF16)


def kernel(x, w_in, b_gates, a_w_out, mla_q_norm, mla_w_uq, mla_kv_norm, mla_w_ukv, mla_w_out, w_out,
           ln1_g, ln1_b, peer_w_query, peer_sub_keys_1, peer_sub_keys_2, peer_expert_down,
           peer_expert_up, ln2_g, ln2_b):
    B, S, D = x.shape
    assert D == D_MODEL and w_in.shape[0] == DEPTH
    T = B * S
    (cos_t, sin_t), (cos_rt, sin_rt) = _rope_tables(S)
    h = x.reshape(T, D)
    for l in range(DEPTH):
        w_all, b_all = _pack_input_weights(w_in[l], b_gates[l])
        proj, qkv_s = _in_projection(h, w_all, b_all, B, S)

        (w1, d1), (w4, d4), (w16, d16) = A_PATTERNS
        assert (d1, d4, d16) == (1, 4, STREAMS)
        o1, l1 = _dilated_dense(proj, B, S, w1)
        o4, l4 = _dilated_streams(qkv_s, w4, d4)
        o16, l16 = _dilated_streams(qkv_s, w16, d16)
        ya = _combine_patterns(o1, l1, o4, l4, o16, l16, B, S)

        cqn, ckvn, krope = _latent_prep(proj, mla_q_norm[l][None], mla_kv_norm[l][None], cos_t, sin_t, S)
        qt = _q_up(cqn, _pack_uq(mla_w_uq[l]), cos_rt, sin_rt, S)
        w_ukv = mla_w_ukv[l].reshape(KV_LORA, B_HEADS, QK_NOPE + V_HEAD)
        wk = w_ukv[:, :, :QK_NOPE].reshape(KV_LORA, B_HEADS * QK_NOPE).astype(BF16)
        wvt = w_ukv[:, :, QK_NOPE:].reshape(KV_LORA, B_HEADS * V_HEAD).T.astype(BF16)
        kn, vt = _kv_up(ckvn, wk, wvt)
        yb = _mla_attention(qt, kn, krope, vt, B, S)

        u = _branch_mix(ya, yb, a_w_out[l].astype(BF16), mla_w_out[l].astype(BF16), proj)
        h1, h1b = _out_projection_ln(u, w_out[l].astype(BF16), h, ln1_g[l][None], ln1_b[l][None])

        qp = _matmul(h1b, peer_w_query[l].astype(BF16), "peer_query")
        i1, i2, g = _peer_route(qp, peer_sub_keys_1[l].astype(BF16), peer_sub_keys_2[l].astype(BF16))
        w3 = _gate_matrix(i1, i2, g)
        down_tiles = peer_expert_down[l].reshape(N_EXPERTS // EXPERT_TILE, EXPERT_TILE, D).transpose(0, 2, 1)
        yp = _peer_experts(h1b, down_tiles.astype(BF16), peer_expert_up[l].astype(BF16), w3)
        h = _residual_ln(yp, h1, ln2_g[l][None], ln2_b[l][None])
    return h.reshape(B, S, D)
```

```python
import functools
import math

import jax
import jax.numpy as jnp
from jax import lax
from jax.experimental import pallas as pl
from jax.experimental.pallas import tpu as pltpu

F32 = jnp.float32
BF16 = jnp.bfloat16

D_MODEL = 2048
A_HEADS = 16
A_HEAD_DIM = 128
A_PATTERNS = ((128, 1), (512, 4), (2048, 16))
A_BLOCK = 128
A_WIDTH = A_HEADS * A_HEAD_DIM
B_HEADS = 16
Q_LORA = 512
KV_LORA = 512
QK_NOPE = 128
QK_ROPE = 64
V_HEAD = 128
ROPE_THETA = 10000.0
N_KEYS = 128
PEER_HEADS = 8
PEER_QDIM = 256
PEER_TOPK = 16
N_EXPERTS = N_KEYS * N_KEYS
N_SLOTS = PEER_HEADS * PEER_TOPK
LN_EPS = 1e-5
RMS_EPS = 1e-6
DEPTH = 1
ALPHA = (2.0 * DEPTH) ** 0.25
NEG = -1e30

LANE = 128
MLA_HEAD_PAD = 256
VMEM_LIMIT = 56 * 1024 * 1024

COL_Q, COL_K, COL_V = 0, A_WIDTH, 2 * A_WIDTH
COL_GA = 3 * A_WIDTH
COL_GB = COL_GA + D_MODEL
COL_CQ = COL_GB + D_MODEL
COL_CKV = COL_CQ + Q_LORA
COL_KR = COL_CKV + KV_LORA
PROJ_WIDTH = COL_CQ + 2048

_NT = (((1,), (1,)), ((), ()))


def _cparams(*sem):
    return pltpu.CompilerParams(dimension_semantics=sem, vmem_limit_bytes=VMEM_LIMIT)


def _layer_norm(z, g, b):
    mu = jnp.mean(z, axis=-1, keepdims=True)
    zc = z - mu
    var = jnp.mean(zc * zc, axis=-1, keepdims=True)
    return zc * lax.rsqrt(var + LN_EPS) * g + b


STREAMS = max(d for _, d in A_PATTERNS)
PERM_ROWS = 16 * STREAMS


def _stream_of_residue(r):
    return (r % 4) * 4 + r // 4


def _stream_permutation():
    tok = jnp.arange(PERM_ROWS, dtype=jnp.int32)
    out_row = _stream_of_residue(tok % STREAMS) * (PERM_ROWS // STREAMS) + tok // STREAMS
    return (out_row[None, :] == jnp.arange(PERM_ROWS, dtype=jnp.int32)[:, None]).astype(BF16)


def _inproj_kernel(x_ref, w_ref, b_ref, perm_ref, o_ref, qs_ref, xb_ref, *, qkv_hi, gate_lo, gate_hi):
    j = pl.program_id(1)

    @pl.when(j == 0)
    def _():
        xb_ref[...] = x_ref[...].astype(BF16)

    acc = jnp.dot(xb_ref[...], w_ref[...], preferred_element_type=F32)
    is_gate = jnp.logical_and(j >= gate_lo, j < gate_hi)

    @pl.when(is_gate)
    def _():
        o_ref[...] = jax.nn.sigmoid(acc + b_ref[...]).astype(o_ref.dtype)

    @pl.when(jnp.logical_not(is_gate))
    def _():
        o_ref[...] = acc.astype(o_ref.dtype)

    @pl.when(j < qkv_hi)
    def _():
        rows = acc.astype(BF16)
        per = PERM_ROWS // STREAMS
        for g in range(rows.shape[0] // PERM_ROWS):
            grouped = jnp.dot(perm_ref[...], rows[g * PERM_ROWS:(g + 1) * PERM_ROWS],
                              preferred_element_type=F32).astype(qs_ref.dtype)
            for s in range(STREAMS):
                qs_ref[s, g * per:(g + 1) * per, :] = grouped[s * per:(s + 1) * per]


def _in_projection(x2, w_all, b_all, B, S):
    T = x2.shape[0]
    tm, tn = min(1024, S), 1024
    nt = S // tm
    qkv_hi = COL_GA // tn
    kern = functools.partial(_inproj_kernel, qkv_hi=qkv_hi, gate_lo=COL_GA // tn, gate_hi=COL_CQ // tn)
    return pl.pallas_call(
        kern,
        out_shape=(
            jax.ShapeDtypeStruct((T, PROJ_WIDTH), BF16),
            jax.ShapeDtypeStruct((B, STREAMS, S // STREAMS, 3 * A_WIDTH), BF16),
        ),
        grid=(T // tm, PROJ_WIDTH // tn),
        in_specs=[
            pl.BlockSpec((tm, D_MODEL), lambda i, j: (i, 0)),
            pl.BlockSpec((D_MODEL, tn), lambda i, j: (0, j)),
            pl.BlockSpec((1, tn), lambda i, j: (0, j)),
            pl.BlockSpec((PERM_ROWS, PERM_ROWS), lambda i, j: (0, 0)),
        ],
        out_specs=(
            pl.BlockSpec((tm, tn), lambda i, j: (i, j)),
            pl.BlockSpec((None, STREAMS, tm // STREAMS, tn),
                         lambda i, j: (i // nt, 0, i % nt, jnp.minimum(j, qkv_hi - 1))),
        ),
        scratch_shapes=[pltpu.VMEM((tm, D_MODEL), BF16)],
        compiler_params=_cparams("parallel", "arbitrary"),
        name="in_projection",
    )(x2, w_all, b_all, _stream_permutation())


def _block_pos(idx, groups):
    if groups == 1:
        return idx
    per = A_BLOCK // groups
    return groups * (idx % per) + idx // per


def _dilated_bias(dilation, steps, groups):
    blk = A_BLOCK
    i = jnp.arange(blk, dtype=jnp.int32)[:, None]
    c = jnp.arange(2 * blk, dtype=jnp.int32)[None, :]
    rel = blk + _block_pos(i, groups) - (blk * (c // blk) + _block_pos(c % blk, groups))
    band = (rel >= 0) & (rel <= steps)
    slopes = jnp.asarray([2.0 ** (-8.0 * (h + 1) / A_HEADS) for h in range(A_HEADS)], F32)
    bias = -slopes[:, None, None] * (dilation * rel).astype(F32)[None]
    first = jnp.where(band & (c >= blk), bias, NEG)
    later = jnp.where(band, bias, NEG)
    return jnp.stack([first, later], axis=0)


DILATED_SUB = 2


def _dilated_kernel(q_ref, kp_ref, ko_ref, vp_ref, vo_ref, bias_ref, o_ref, lse_ref):
    n = pl.program_id(2)
    blk = A_BLOCK
    per = kp_ref.shape[-2]
    lane = lax.broadcasted_iota(jnp.int32, (blk, LANE), 1)

    def rows(ref, u, sl):
        return ref[..., u * per:(u + 1) * per, sl].reshape(blk, A_HEAD_DIM)

    for u in range(DILATED_SUB):
        table = jnp.minimum(n, 1) if u == 0 else 1
        lse_all = jnp.zeros((blk, LANE), F32)
        for h in range(A_HEADS):
            sl = slice(h * A_HEAD_DIM, (h + 1) * A_HEAD_DIM)
            q = rows(q_ref, u, sl)
            k_prev = rows(kp_ref, 0, sl) if u == 0 else rows(ko_ref, u - 1, sl)
            v_prev = rows(vp_ref, 0, sl) if u == 0 else rows(vo_ref, u - 1, sl)
            k = jnp.concatenate([k_prev, rows(ko_ref, u, sl)], axis=0)
            v = jnp.concatenate([v_prev, rows(vo_ref, u, sl)], axis=0)
            logits = lax.dot_general(q, k, _NT, preferred_element_type=F32) + bias_ref[table, h]
            m = jnp.max(logits, axis=-1, keepdims=True)
            p = jnp.exp(logits - m)
            z = jnp.sum(p, axis=-1, keepdims=True)
            o = jnp.dot(p.astype(BF16), v, preferred_element_type=F32) / z
            o_ref[..., u * per:(u + 1) * per, sl] = o.astype(o_ref.dtype).reshape(kp_ref.shape[:-1] + (A_HEAD_DIM,))
            lse_all = jnp.where(lane == h, m + jnp.log(z), lse_all)
        lse_ref[..., u * per:(u + 1) * per, :] = lse_all.reshape(kp_ref.shape[:-1] + (LANE,))


def _dilated_dense(proj, B, S, window):
    T = B * S
    sub = DILATED_SUB
    nb = S // (A_BLOCK * sub)
    prev, own = (A_BLOCK, A_WIDTH), (A_BLOCK * sub, A_WIDTH)
    before = lambda b, n: b * nb * sub + jnp.maximum(n * sub - 1, 0)
    return pl.pallas_call(
        _dilated_kernel,
        out_shape=(jax.ShapeDtypeStruct((T, A_WIDTH), BF16), jax.ShapeDtypeStruct((T, LANE), F32)),
        grid=(B, 1, nb),
        in_specs=[
            pl.BlockSpec(own, lambda b, r, n: (b * nb + n, COL_Q // A_WIDTH)),
            pl.BlockSpec(prev, lambda b, r, n: (before(b, n), COL_K // A_WIDTH)),
            pl.BlockSpec(own, lambda b, r, n: (b * nb + n, COL_K // A_WIDTH)),
            pl.BlockSpec(prev, lambda b, r, n: (before(b, n), COL_V // A_WIDTH)),
            pl.BlockSpec(own, lambda b, r, n: (b * nb + n, COL_V // A_WIDTH)),
            pl.BlockSpec((2, A_HEADS, A_BLOCK, 2 * A_BLOCK), lambda b, r, n: (0, 0, 0, 0)),
        ],
        out_specs=(
            pl.BlockSpec(own, lambda b, r, n: (b * nb + n, 0)),
            pl.BlockSpec((A_BLOCK * sub, LANE), lambda b, r, n: (b * nb + n, 0)),
        ),
        compiler_params=_cparams("parallel", "parallel", "arbitrary"),
        name="dilated_attention_d1",
    )(proj, proj, proj, proj, proj, _dilated_bias(1, window, 1))


def _dilated_streams(qkv_s, window, dilation):
    B, ns, Ls, _ = qkv_s.shape
    groups = STREAMS // dilation
    per = A_BLOCK // groups
    sub = DILATED_SUB
    assert ns == STREAMS and Ls % (per * sub) == 0
    prev, own = (None, groups, per, A_WIDTH), (None, groups, per * sub, A_WIDTH)
    before = lambda n: jnp.maximum(n * sub - 1, 0)
    return pl.pallas_call(
        _dilated_kernel,
        out_shape=(
            jax.ShapeDtypeStruct((B, STREAMS, Ls, A_WIDTH), BF16),
            jax.ShapeDtypeStruct((B, STREAMS, Ls, LANE), F32),
        ),
        grid=(B, STREAMS // groups, Ls // (per * sub)),
        in_specs=[
            pl.BlockSpec(own, lambda b, r, n: (b, r, n, 0)),
            pl.BlockSpec(prev, lambda b, r, n: (b, r, before(n), 1)),
            pl.BlockSpec(own, lambda b, r, n: (b, r, n, 1)),
            pl.BlockSpec(prev, lambda b, r, n: (b, r, before(n), 2)),
            pl.BlockSpec(own, lambda b, r, n: (b, r, n, 2)),
            pl.BlockSpec((2, A_HEADS, A_BLOCK, 2 * A_BLOCK), lambda b, r, n: (0, 0, 0, 0)),
        ],
        out_specs=(
            pl.BlockSpec(own, lambda b, r, n: (b, r, n, 0)),
            pl.BlockSpec((None, groups, per * sub, LANE), lambda b, r, n: (b, r, n, 0)),
        ),
        compiler_params=_cparams("parallel", "parallel", "arbitrary"),
        name=f"dilated_attention_d{dilation}",
    )(qkv_s, qkv_s, qkv_s, qkv_s, qkv_s, _dilated_bias(dilation, window // dilation, groups))


COMBINE_STEPS = 16


def _combine_kernel(o1_ref, o2_ref, o3_ref, l1_ref, l2_ref, l3_ref, y_ref, ob_ref, oc_ref, lb_ref, lc_ref):
    for r in range(STREAMS):
        s = _stream_of_residue(r)
        tok = pl.ds(r, COMBINE_STEPS, stride=STREAMS)
        lb_ref[tok, :] = l2_ref[s]
        lc_ref[tok, :] = l3_ref[s]
        for h in range(A_HEADS):
            sl = slice(h * A_HEAD_DIM, (h + 1) * A_HEAD_DIM)
            ob_ref[h, tok, :] = o2_ref[s, :, sl].astype(F32)
            oc_ref[h, tok, :] = o3_ref[s, :, sl].astype(F32)
    a, b, c = l1_ref[...], lb_ref[...], lc_ref[...]
    m = jnp.maximum(jnp.maximum(a, b), c)
    ea, eb, ec = jnp.exp(a - m), jnp.exp(b - m), jnp.exp(c - m)
    inv = 1.0 / (ea + eb + ec)
    wa, wb, wc = ea * inv, eb * inv, ec * inv
    for h in range(A_HEADS):
        sl = slice(h * A_HEAD_DIM, (h + 1) * A_HEAD_DIM)
        y = (wa[:, h:h + 1] * o1_ref[:, sl].astype(F32)
             + wb[:, h:h + 1] * ob_ref[h]
             + wc[:, h:h + 1] * oc_ref[h])
        y_ref[:, sl] = y.astype(y_ref.dtype)


def _combine_patterns(o1, l1, o4, l4, o16, l16, B, S):
    T = B * S
    tm = COMBINE_STEPS * STREAMS
    nt = S // tm
    tok_o = pl.BlockSpec((tm, A_WIDTH), lambda b, i: (b * nt + i, 0))
    tok_l = pl.BlockSpec((tm, LANE), lambda b, i: (b * nt + i, 0))
    str_o = pl.BlockSpec((None, STREAMS, COMBINE_STEPS, A_WIDTH), lambda b, i: (b, 0, i, 0))
    str_l = pl.BlockSpec((None, STREAMS, COMBINE_STEPS, LANE), lambda b, i: (b, 0, i, 0))
    return pl.pallas_call(
        _combine_kernel,
        out_shape=jax.ShapeDtypeStruct((T, A_WIDTH), BF16),
        grid=(B, nt),
        in_specs=[tok_o, str_o, str_o, tok_l, str_l, str_l],
        out_specs=tok_o,
        scratch_shapes=[
            pltpu.VMEM((A_HEADS, tm, A_HEAD_DIM), F32),
            pltpu.VMEM((A_HEADS, tm, A_HEAD_DIM), F32),
            pltpu.VMEM((tm, LANE), F32),
            pltpu.VMEM((tm, LANE), F32),
        ],
        compiler_params=_cparams("parallel", "parallel"),
        name="combine_patterns",
    )(o1, o4, o16, l1, l4, l16)


def _rope_lanes(t, cos, sin):
    lane = lax.broadcasted_iota(jnp.int32, t.shape, 1)
    half = QK_ROPE // 2
    rot = jnp.where(lane < half, pltpu.roll(t, LANE - half, 1), pltpu.roll(t, half, 1))
    return t * cos + rot * sin


def _rms_norm(x, g):
    ms = jnp.mean(x * x, axis=-1, keepdims=True)
    return x * lax.rsqrt(ms + RMS_EPS) * g


def _latent_kernel(cq_ref, ckv_ref, kr_ref, gq_ref, gkv_ref, cos_ref, sin_ref, cqn_ref, ckvn_ref, krope_ref):
    cqn_ref[...] = _rms_norm(cq_ref[...].astype(F32), gq_ref[...]).astype(cqn_ref.dtype)
    ckvn_ref[...] = _rms_norm(ckv_ref[...].astype(F32), gkv_ref[...]).astype(ckvn_ref.dtype)
    krope_ref[...] = _rope_lanes(kr_ref[...].astype(F32), cos_ref[...], sin_ref[...]).astype(krope_ref.dtype)


def _latent_prep(proj, gq, gkv, cos_t, sin_t, S):
    T = proj.shape[0]
    tm = min(512, S)
    ns = S // tm
    return pl.pallas_call(
        _latent_kernel,
        out_shape=(
            jax.ShapeDtypeStruct((T, Q_LORA), BF16),
            jax.ShapeDtypeStruct((T, KV_LORA), BF16),
            jax.ShapeDtypeStruct((T, LANE), BF16),
        ),
        grid=(T // tm,),
        in_specs=[
            pl.BlockSpec((tm, Q_LORA), lambda i: (i, COL_CQ // Q_LORA)),
            pl.BlockSpec((tm, KV_LORA), lambda i: (i, COL_CKV // KV_LORA)),
            pl.BlockSpec((tm, LANE), lambda i: (i, COL_KR // LANE)),
            pl.BlockSpec((1, Q_LORA), lambda i: (0, 0)),
            pl.BlockSpec((1, KV_LORA), lambda i: (0, 0)),
            pl.BlockSpec((tm, LANE), lambda i: (i % ns, 0)),
            pl.BlockSpec((tm, LANE), lambda i: (i % ns, 0)),
        ],
        out_specs=(
            pl.BlockSpec((tm, Q_LORA), lambda i: (i, 0)),
            pl.BlockSpec((tm, KV_LORA), lambda i: (i, 0)),
            pl.BlockSpec((tm, LANE), lambda i: (i, 0)),
        ),
        compiler_params=_cparams("parallel"),
        name="latent_prep",
    )(proj, proj, proj, gq, gkv, cos_t, sin_t)


def _qup_kernel(c_ref, wt_ref, cos_ref, sin_ref, o_ref, *, scale):
    acc = lax.dot_general(wt_ref[...], c_ref[...], _NT, preferred_element_type=F32) * scale
    cos, sin = cos_ref[...], sin_ref[...]
    half = QK_ROPE // 2
    for hb in range(acc.shape[0] // MLA_HEAD_PAD):
        lo = hb * MLA_HEAD_PAD
        r1 = acc[lo + QK_NOPE:lo + QK_NOPE + half]
        r2 = acc[lo + QK_NOPE + half:lo + QK_NOPE + QK_ROPE]
        o_ref[lo:lo + QK_NOPE] = acc[lo:lo + QK_NOPE].astype(o_ref.dtype)
        o_ref[lo + QK_NOPE:lo + QK_NOPE + half] = (r1 * cos - r2 * sin).astype(o_ref.dtype)
        o_ref[lo + QK_NOPE + half:lo + QK_NOPE + QK_ROPE] = (r2 * cos + r1 * sin).astype(o_ref.dtype)
        o_ref[lo + QK_NOPE + QK_ROPE:lo + MLA_HEAD_PAD] = acc[lo + QK_NOPE + QK_ROPE:lo + MLA_HEAD_PAD].astype(o_ref.dtype)


def _q_up(cqn, w_uq_pt, cos_rt, sin_rt, S):
    T = cqn.shape[0]
    N = w_uq_pt.shape[0]
    tm, tn = min(1024, S), 1024
    ns = S // tm
    half = QK_ROPE // 2
    kern = functools.partial(_qup_kernel, scale=(QK_NOPE + QK_ROPE) ** -0.5 * math.log2(math.e))
    return pl.pallas_call(
        kern,
        out_shape=jax.ShapeDtypeStruct((N, T), BF16),
        grid=(T // tm, N // tn),
        in_specs=[
            pl.BlockSpec((tm, Q_LORA), lambda i, j: (i, 0)),
            pl.BlockSpec((tn, Q_LORA), lambda i, j: (j, 0)),
            pl.BlockSpec((half, tm), lambda i, j: (0, i % ns)),
            pl.BlockSpec((half, tm), lambda i, j: (0, i % ns)),
        ],
        out_specs=pl.BlockSpec((tn, tm), lambda i, j: (j, i)),
        compiler_params=_cparams("parallel", "arbitrary"),
        name="mla_q_up",
    )(cqn, w_uq_pt, cos_rt, sin_rt)


def _mm_kernel(a_ref, w_ref, o_ref):
    o_ref[...] = jnp.dot(a_ref[...], w_ref[...], preferred_element_type=F32).astype(o_ref.dtype)


def _matmul(a, w, name, tm=512, tn=1024):
    M, K = a.shape
    N = w.shape[1]
    tm, tn = min(tm, M), min(tn, N)
    return pl.pallas_call(
        _mm_kernel,
        out_shape=jax.ShapeDtypeStruct((M, N), BF16),
        grid=(M // tm, N // tn),
        in_specs=[
            pl.BlockSpec((tm, K), lambda i, j: (i, 0)),
            pl.BlockSpec((K, tn), lambda i, j: (0, j)),
        ],
        out_specs=pl.BlockSpec((tm, tn), lambda i, j: (i, j)),
        compiler_params=_cparams("parallel", "arbitrary"),
        name=name,
    )(a, w)


def _kvup_kernel(c_ref, wk_ref, wvt_ref, kn_ref, vt_ref):
    c = c_ref[...]
    kn_ref[...] = jnp.dot(c, wk_ref[...], preferred_element_type=F32).astype(kn_ref.dtype)
    vt_ref[...] = lax.dot_general(wvt_ref[...], c, _NT, preferred_element_type=F32).astype(vt_ref.dtype)


def _kv_up(ckvn, wk, wvt):
    T = ckvn.shape[0]
    tm = min(1024, T)
    n = B_HEADS * QK_NOPE
    return pl.pallas_call(
        _kvup_kernel,
        out_shape=(jax.ShapeDtypeStruct((T, n), BF16), jax.ShapeDtypeStruct((B_HEADS * V_HEAD, T), BF16)),
        grid=(T // tm,),
        in_specs=[
            pl.BlockSpec((tm, KV_LORA), lambda i: (i, 0)),
            pl.BlockSpec((KV_LORA, n), lambda i: (0, 0)),
            pl.BlockSpec((B_HEADS * V_HEAD, KV_LORA), lambda i: (0, 0)),
        ],
        out_specs=(pl.BlockSpec((tm, n), lambda i: (i, 0)), pl.BlockSpec((B_HEADS * V_HEAD, tm), lambda i: (0, i))),
        compiler_params=_cparams("parallel"),
        name="mla_kv_up",
    )(ckvn, wk, wvt)


def _mla_kernel(q_ref, kn_ref, kr_ref, vt_ref, o_ref, sa_ref, sb_ref, xa_ref, xb_ref, m_ref, l_ref, acc_ref, *, tq):
    qi = pl.program_id(2)
    tk = tq // 2
    q = q_ref[...]
    m_ref[...] = jnp.full(m_ref.shape, NEG, F32)
    l_ref[...] = jnp.zeros(l_ref.shape, F32)
    acc_ref[...] = jnp.zeros(acc_ref.shape, F32)

    def scores(c, s_ref, x_ref):
        start = pl.multiple_of(c * tk, tk)
        k = jnp.concatenate([kn_ref[pl.ds(start, tk), :], kr_ref[pl.ds(start, tk), :]], axis=1)
        st = jnp.dot(k, q, preferred_element_type=F32)
        s_ref[...] = st
        x_ref[...] = jnp.max(st, axis=0, keepdims=True)

    def update(c, s_ref, x_ref, masked):
        start = pl.multiple_of(c * tk, tk)
        st = s_ref[...]
        if masked:
            key = lax.broadcasted_iota(jnp.int32, st.shape, 0) + (c * tk - qi * tq)
            qry = lax.broadcasted_iota(jnp.int32, st.shape, 1)
            st = jnp.where(key <= qry, st, NEG)
            cmax = jnp.max(st, axis=0, keepdims=True)
        else:
            cmax = x_ref[...]
        m_prev = m_ref[...]
        m_new = jnp.maximum(m_prev, cmax)
        a = jnp.exp2(m_prev - m_new)
        p = jnp.exp2(st - m_new)
        l_ref[...] = a * l_ref[...] + jnp.sum(p, axis=0, keepdims=True)
        pv = jnp.dot(vt_ref[:, pl.ds(start, tk)], p.astype(BF16), preferred_element_type=F32)
        acc_ref[...] = a * acc_ref[...] + pv
        m_ref[...] = m_new

    scores(0, sa_ref, xa_ref)

    def pair(i):
        c = 2 * i
        scores(c + 1, sb_ref, xb_ref)
        update(c, sa_ref, xa_ref, False)
        scores(c + 2, sa_ref, xa_ref)
        update(c + 1, sb_ref, xb_ref, False)

    def two_pairs(i, carry):
        pair(2 * i)
        pair(2 * i + 1)
        return carry

    lax.fori_loop(0, qi // 2, two_pairs, 0)

    @pl.when(qi % 2 == 1)
    def _():
        pair(qi - 1)

    c = 2 * qi
    late = pl.ds(tk, tq - tk)
    start = pl.multiple_of((c + 1) * tk, tk)
    k = jnp.concatenate([kn_ref[pl.ds(start, tk), :], kr_ref[pl.ds(start, tk), :]], axis=1)
    sb_ref[:, late] = jnp.dot(k, q_ref[:, late], preferred_element_type=F32)
    update(c, sa_ref, xa_ref, True)
    st = sb_ref[:, late]
    key = lax.broadcasted_iota(jnp.int32, st.shape, 0)
    qry = lax.broadcasted_iota(jnp.int32, st.shape, 1)
    st = jnp.where(key <= qry, st, NEG)
    m_prev = m_ref[:, late]
    m_new = jnp.maximum(m_prev, jnp.max(st, axis=0, keepdims=True))
    a = jnp.exp2(m_prev - m_new)
    p = jnp.exp2(st - m_new)
    l_ref[:, late] = a * l_ref[:, late] + jnp.sum(p, axis=0, keepdims=True)
    pv = jnp.dot(vt_ref[:, pl.ds(start, tk)], p.astype(BF16), preferred_element_type=F32)
    acc_ref[:, late] = a * acc_ref[:, late] + pv
    o_ref[...] = (acc_ref[...] / l_ref[...]).T.astype(o_ref.dtype)


def _mla_attention(qt, kn, krope, vt, B, S):
    T = qt.shape[1]
    tq = min(1024, S)
    nq = S // tq
    kern = functools.partial(_mla_kernel, tq=tq)
    return pl.pallas_call(
        kern,
        out_shape=jax.ShapeDtypeStruct((T, B_HEADS * V_HEAD), BF16),
        grid=(B, B_HEADS, nq),
        in_specs=[
            pl.BlockSpec((MLA_HEAD_PAD, tq), lambda b, h, i: (h, b * nq + i)),
            pl.BlockSpec((S, QK_NOPE), lambda b, h, i: (b, h)),
            pl.BlockSpec((S, LANE), lambda b, h, i: (b, 0)),
            pl.BlockSpec((V_HEAD, S), lambda b, h, i: (h, b)),
        ],
        out_specs=pl.BlockSpec((tq, V_HEAD), lambda b, h, i: (b * nq + i, h)),
        scratch_shapes=(
            [pltpu.VMEM((tq // 2, tq), F32)] * 2
            + [pltpu.VMEM((1, tq), F32)] * 4
            + [pltpu.VMEM((V_HEAD, tq), F32)]
        ),
        compiler_params=_cparams("parallel", "parallel", "arbitrary"),
        name="mla_attention",
    )(qt, kn, krope, vt)


def _branch_kernel(ya_ref, yb_ref, wa_ref, wb_ref, ga_ref, gb_ref, o_ref):
    pa = jnp.dot(ya_ref[...], wa_ref[...], preferred_element_type=F32)
    pb = jnp.dot(yb_ref[...], wb_ref[...], preferred_element_type=F32)
    u = ga_ref[...].astype(F32) * pa + gb_ref[...].astype(F32) * pb
    o_ref[...] = u.astype(o_ref.dtype)


def _branch_mix(ya, yb, wa, wb, proj):
    T = ya.shape[0]
    tm, tn = min(512, T), 1024
    return pl.pallas_call(
        _branch_kernel,
        out_shape=jax.ShapeDtypeStruct((T, D_MODEL), BF16),
        grid=(T // tm, D_MODEL // tn),
        in_specs=[
            pl.BlockSpec((tm, A_WIDTH), lambda i, j: (i, 0)),
            pl.BlockSpec((tm, B_HEADS * V_HEAD), lambda i, j: (i, 0)),
            pl.BlockSpec((A_WIDTH, tn), lambda i, j: (0, j)),
            pl.BlockSpec((B_HEADS * V_HEAD, tn), lambda i, j: (0, j)),
            pl.BlockSpec((tm, tn), lambda i, j: (i, COL_GA // tn + j)),
            pl.BlockSpec((tm, tn), lambda i, j: (i, COL_GB // tn + j)),
        ],
        out_specs=pl.BlockSpec((tm, tn), lambda i, j: (i, j)),
        compiler_params=_cparams("parallel", "arbitrary"),
        name="branch_mix",
    )(ya, yb, wa, wb, proj, proj)


def _outln_kernel(u_ref, w_ref, x_ref, g_ref, b_ref, h_ref, hb_ref):
    mix = jnp.dot(u_ref[...], w_ref[...], preferred_element_type=F32)
    h = _layer_norm(ALPHA * x_ref[...] + mix, g_ref[...], b_ref[...])
    h_ref[...] = h
    hb_ref[...] = h.astype(hb_ref.dtype)


def _out_projection_ln(u, w_out, x2, g, b):
    T = u.shape[0]
    tm = min(512, T)
    row = pl.BlockSpec((tm, D_MODEL), lambda i: (i, 0))
    vec = pl.BlockSpec((1, D_MODEL), lambda i: (0, 0))
    return pl.pallas_call(
        _outln_kernel,
        out_shape=(
            jax.ShapeDtypeStruct((T, D_MODEL), F32),
            jax.ShapeDtypeStruct((T, D_MODEL), BF16),
        ),
        grid=(T // tm,),
        in_specs=[row, pl.BlockSpec((D_MODEL, D_MODEL), lambda i: (0, 0)), row, vec, vec],
        out_specs=(row, row),
        compiler_params=_cparams("parallel"),
        name="out_projection_ln",
    )(u, w_out, x2, g, b)


def _topk_axis0(s, ids, k):
    big = jnp.int32(2 ** 30)
    vals, idxs = [], []
    for _ in range(k):
        m = jnp.max(s, axis=0, keepdims=True)
        idx = jnp.min(jnp.where(s == m, ids, big), axis=0, keepdims=True)
        vals.append(m)
        idxs.append(idx)
        s = jnp.where(ids == idx, -jnp.inf, s)
    return vals, idxs


def _select_rows(rows, sel):
    out = jnp.zeros(sel.shape, rows[0].dtype)
    for a, r in enumerate(rows):
        out = jnp.where(sel == a, r, out)
    return out


def _route_kernel(q_ref, k1_ref, k2_ref, i1_ref, i2_ref, g_ref):
    half = PEER_QDIM // 2
    k, tm = PEER_TOPK, q_ref.shape[0]
    key_id = lax.broadcasted_iota(jnp.int32, (N_KEYS, tm), 0)
    sub = lax.broadcasted_iota(jnp.int32, (k // 2, tm), 0)
    cand_pos = jnp.concatenate([a * k + sub for a in range(k // 2)] + [k // 2 + sub, (k // 2 + sub) * k], axis=0)
    i1_all, i2_all, g_all = [], [], []
    for h in range(PEER_HEADS):
        q1 = q_ref[:, h * PEER_QDIM:h * PEER_QDIM + half]
        q2 = q_ref[:, h * PEER_QDIM + half:(h + 1) * PEER_QDIM]
        s1 = lax.dot_general(k1_ref[...], q1, _NT, preferred_element_type=F32)
        s2 = lax.dot_general(k2_ref[...], q2, _NT, preferred_element_type=F32)
        v1, i1 = _topk_axis0(s1, key_id, k)
        v2, i2 = _topk_axis0(s2, key_id, k)
        v1m = jnp.concatenate(v1, axis=0)
        v2m = jnp.concatenate(v2, axis=0)
        cand = jnp.concatenate(
            [v1[a] + v2m[:k // 2] for a in range(k // 2)] + [v1[0] + v2m[k // 2:], v1m[k // 2:] + v2[0]], axis=0)
        ts, pos = _topk_axis0(cand, cand_pos, k)
        top = jnp.concatenate(ts, axis=0)
        e = jnp.exp(top - ts[0])
        g_all.append(e / jnp.sum(e, axis=0, keepdims=True))
        posm = jnp.concatenate(pos, axis=0)
        i1_all.append(_select_rows(i1, posm >> int(math.log2(k))))
        i2_all.append(_select_rows(i2, posm & (k - 1)))
    i1_ref[...] = jnp.concatenate(i1_all, axis=0).T
    i2_ref[...] = jnp.concatenate(i2_all, axis=0).T
    g_ref[...] = jnp.concatenate(g_all, axis=0).T


def _peer_route(qp, k1, k2):
    T = qp.shape[0]
    tm = min(256, T)
    slot = pl.BlockSpec((tm, N_SLOTS), lambda i: (i, 0))
    keys = pl.BlockSpec((N_KEYS, PEER_QDIM // 2), lambda i: (0, 0))
    return pl.pallas_call(
        _route_kernel,
        out_shape=(
            jax.ShapeDtypeStruct((T, N_SLOTS), jnp.int32),
            jax.ShapeDtypeStruct((T, N_SLOTS), jnp.int32),
            jax.ShapeDtypeStruct((T, N_SLOTS), F32),
        ),
        grid=(T // tm,),
        in_specs=[pl.BlockSpec((tm, PEER_HEADS * PEER_QDIM), lambda i: (i, 0)), keys, keys],
        out_specs=(slot, slot, slot),
        compiler_params=_cparams("parallel"),
        name="peer_route",
    )(qp, k1, k2)


GATE_GROUP = 64


def _gate_matrix_kernel(i1_ref, i2_ref, g_ref, w_ref):
    key = lax.broadcasted_iota(jnp.int32, (N_KEYS, N_SLOTS), 0)

    def body(tg, carry):
        t0 = pl.multiple_of(tg * GATE_GROUP, GATE_GROUP)
        per_token = []
        for u in range(GATE_GROUP):
            r1 = i1_ref[pl.ds(t0 + u, 1), :]
            r2 = i2_ref[pl.ds(t0 + u, 1), :]
            g = g_ref[pl.ds(t0 + u, 1), :]
            a = jnp.where(key == r1, 1.0, 0.0).astype(BF16)
            b = jnp.where(key == r2, g, 0.0).astype(BF16)
            per_token.append(lax.dot_general(a, b, _NT, preferred_element_type=F32))
        w = pltpu.einshape("tid->itd", jnp.stack(per_token, axis=0))
        w_ref[:, pl.ds(t0, GATE_GROUP), :] = w.astype(w_ref.dtype)
        return carry

    lax.fori_loop(0, w_ref.shape[1] // GATE_GROUP, body, 0)


def _gate_matrix(i1, i2, g):
    T = i1.shape[0]
    tb = min(128, T)
    slot = pl.BlockSpec((tb, N_SLOTS), lambda i: (i, 0))
    return pl.pallas_call(
        _gate_matrix_kernel,
        out_shape=jax.ShapeDtypeStruct((N_KEYS, T, N_KEYS), BF16),
        grid=(T // tb,),
        in_specs=[slot, slot, slot],
        out_specs=pl.BlockSpec((N_KEYS, tb, N_KEYS), lambda i: (0, i, 0)),
        compiler_params=_cparams("parallel"),
        name="peer_gate_matrix",
    )(i1, i2, g)


EXPERT_TILE = 1024


def _experts_kernel(hb_ref, dn_ref, up_ref, w_ref, y_ref, acc_ref):
    j = pl.program_id(1)

    @pl.when(j == 0)
    def _():
        acc_ref[...] = jnp.zeros(acc_ref.shape, F32)

    pre = jnp.dot(hb_ref[...], dn_ref[...], preferred_element_type=F32)
    act = 0.5 * pre * (1.0 + lax.erf(pre * (2.0 ** -0.5)))
    gates = jnp.concatenate([w_ref[m] for m in range(w_ref.shape[0])], axis=1)
    act = act * gates.astype(F32)
    acc_ref[...] += jnp.dot(act.astype(BF16), up_ref[...], preferred_element_type=F32)

    @pl.when(j == pl.num_programs(1) - 1)
    def _():
        y_ref[...] = acc_ref[...].astype(y_ref.dtype)


def _peer_experts(hb, down_tiles, up, w3):
    T = hb.shape[0]
    tm, te = min(1024, T), EXPERT_TILE
    row = pl.BlockSpec((tm, D_MODEL), lambda i, j: (i, 0))
    return pl.pallas_call(
        _experts_kernel,
        out_shape=jax.ShapeDtypeStruct((T, D_MODEL), BF16),
        grid=(T // tm, N_EXPERTS // te),
        in_specs=[
            row,
            pl.BlockSpec((None, D_MODEL, te), lambda i, j: (j, 0, 0)),
            pl.BlockSpec((te, D_MODEL), lambda i, j: (j, 0)),
            pl.BlockSpec((te // N_KEYS, tm, N_KEYS), lambda i, j: (j, i, 0)),
        ],
        out_specs=row,
        scratch_shapes=[pltpu.VMEM((tm, D_MODEL), F32)],
        compiler_params=_cparams("parallel", "arbitrary"),
        name="peer_experts",
    )(hb, down_tiles, up, w3)


def _residual_ln_kernel(y_ref, h_ref, g_ref, b_ref, o_ref):
    o_ref[...] = _layer_norm(ALPHA * h_ref[...] + y_ref[...].astype(F32), g_ref[...], b_ref[...])


def _residual_ln(y, h, g, b):
    T = y.shape[0]
    tm = min(512, T)
    row = pl.BlockSpec((tm, D_MODEL), lambda i: (i, 0))
    vec = pl.BlockSpec((1, D_MODEL), lambda i: (0, 0))
    return pl.pallas_call(
        _residual_ln_kernel,
        out_shape=jax.ShapeDtypeStruct((T, D_MODEL), F32),
        grid=(T // tm,),
        in_specs=[row, row, vec, vec],
        out_specs=row,
        compiler_params=_cparams("parallel"),
        name="peer_residual_ln",
    )(y, h, g, b)


def _rope_tables(S):
    half = QK_ROPE // 2
    inv_freq = ROPE_THETA ** (-jnp.arange(half, dtype=F32) / half)
    ang = jnp.arange(S, dtype=jnp.int32).astype(F32)[:, None] * inv_freq[None, :]
    cos, sin = jnp.cos(ang), jnp.sin(ang)
    zeros = jnp.zeros((S, LANE - QK_ROPE), F32)
    lanes = (jnp.concatenate([cos, cos, zeros], axis=1), jnp.concatenate([-sin, sin, zeros], axis=1))
    return lanes, (cos.T, sin.T)


def _pack_input_weights(w_in, b_gates):
    wq, wk, wv, wcq, wckv, wkr, wg = jnp.split(
        w_in, (A_WIDTH, 2 * A_WIDTH, 3 * A_WIDTH, 3 * A_WIDTH + Q_LORA,
               3 * A_WIDTH + Q_LORA + KV_LORA, 3 * A_WIDTH + Q_LORA + KV_LORA + QK_ROPE), axis=1)
    pad = jnp.zeros((D_MODEL, PROJ_WIDTH - COL_KR - QK_ROPE), w_in.dtype)
    wq = wq * (A_HEAD_DIM ** -0.5)
    w_all = jnp.concatenate([wq, wk, wv, wg, wcq, wckv, wkr, pad], axis=1).astype(BF16)
    b_all = jnp.zeros((1, PROJ_WIDTH), F32).at[0, COL_GA:COL_CQ].set(b_gates)
    return w_all, b_all


def _pack_uq(w_uq):
    w = w_uq.reshape(Q_LORA, B_HEADS, QK_NOPE + QK_ROPE)
    w = jnp.pad(w, ((0, 0), (0, 0), (0, MLA_HEAD_PAD - QK_NOPE - QK_ROPE)))
    return w.reshape(Q_LORA, B_HEADS * MLA_HEAD_PAD).T.astype(BF16)


def kernel(x, w_in, b_gates, a_w_out, mla_q_norm, mla_w_uq, mla_kv_norm, mla_w_ukv, mla_w_out, w_out,
           ln1_g, ln1_b, peer_w_query, peer_sub_keys_1, peer_sub_keys_2, peer_expert_down,
           peer_expert_up, ln2_g, ln2_b):
    B, S, D = x.shape
    assert D == D_MODEL and w_in.shape[0] == DEPTH
    T = B * S
    (cos_t, sin_t), (cos_rt, sin_rt) = _rope_tables(S)
    h = x.reshape(T, D)
    for l in range(DEPTH):
        w_all, b_all = _pack_input_weights(w_in[l], b_gates[l])
        proj, qkv_s = _in_projection(h, w_all, b_all, B, S)

        (w1, d1), (w4, d4), (w16, d16) = A_PATTERNS
        assert (d1, d4, d16) == (1, 4, STREAMS)
        o1, l1 = _dilated_dense(proj, B, S, w1)
        o4, l4 = _dilated_streams(qkv_s, w4, d4)
        o16, l16 = _dilated_streams(qkv_s, w16, d16)
        ya = _combine_patterns(o1, l1, o4, l4, o16, l16, B, S)

        cqn, ckvn, krope = _latent_prep(proj, mla_q_norm[l][None], mla_kv_norm[l][None], cos_t, sin_t, S)
        qt = _q_up(cqn, _pack_uq(mla_w_uq[l]), cos_rt, sin_rt, S)
        w_ukv = mla_w_ukv[l].reshape(KV_LORA, B_HEADS, QK_NOPE + V_HEAD)
        wk = w_ukv[:, :, :QK_NOPE].reshape(KV_LORA, B_HEADS * QK_NOPE).astype(BF16)
        wvt = w_ukv[:, :, QK_NOPE:].reshape(KV_LORA, B_HEADS * V_HEAD).T.astype(BF16)
        kn, vt = _kv_up(ckvn, wk, wvt)
        yb = _mla_attention(qt, kn, krope, vt, B, S)

        u = _branch_mix(ya, yb, a_w_out[l].astype(BF16), mla_w_out[l].astype(BF16), proj)
        h1, h1b = _out_projection_ln(u, w_out[l].astype(BF16), h, ln1_g[l][None], ln1_b[l][None])

        qp = _matmul(h1b, peer_w_query[l].astype(BF16), "peer_query", tm=1024)
        i1, i2, g = _peer_route(qp, peer_sub_keys_1[l].astype(BF16), peer_sub_keys_2[l].astype(BF16))
        w3 = _gate_matrix(i1, i2, g)
        down_tiles = peer_expert_down[l].reshape(N_EXPERTS // EXPERT_TILE, EXPERT_TILE, D).transpose(0, 2, 1)
        yp = _peer_experts(h1b, down_tiles.astype(BF16), peer_expert_up[l].astype(BF16), w3)
        h = _residual_ln(yp, h1, ln2_g[l][None], ln2_b[l][None])
    return h.reshape(B, S, D)
```

```python
import functools
import math

import jax
import jax.numpy as jnp
from jax import lax
from jax.experimental import pallas as pl
from jax.experimental.pallas import tpu as pltpu

F32 = jnp.float32
BF16 = jnp.bfloat16

D_MODEL = 2048
A_HEADS = 16
A_HEAD_DIM = 128
A_PATTERNS = ((128, 1), (512, 4), (2048, 16))
A_BLOCK = 128
A_WIDTH = A_HEADS * A_HEAD_DIM
B_HEADS = 16
Q_LORA = 512
KV_LORA = 512
QK_NOPE = 128
QK_ROPE = 64
V_HEAD = 128
ROPE_THETA = 10000.0
N_KEYS = 128
PEER_HEADS = 8
PEER_QDIM = 256
PEER_TOPK = 16
N_EXPERTS = N_KEYS * N_KEYS
N_SLOTS = PEER_HEADS * PEER_TOPK
LN_EPS = 1e-5
RMS_EPS = 1e-6
DEPTH = 1
ALPHA = (2.0 * DEPTH) ** 0.25
NEG = -1e30

LANE = 128
MLA_HEAD_PAD = 256
VMEM_LIMIT = 56 * 1024 * 1024

COL_Q, COL_K, COL_V = 0, A_WIDTH, 2 * A_WIDTH
COL_GA = 3 * A_WIDTH
COL_GB = COL_GA + D_MODEL
COL_CQ = COL_GB + D_MODEL
COL_CKV = COL_CQ + Q_LORA
COL_KR = COL_CKV + KV_LORA
PROJ_WIDTH = COL_CQ + 2048

_NT = (((1,), (1,)), ((), ()))


def _cparams(*sem):
    return pltpu.CompilerParams(dimension_semantics=sem, vmem_limit_bytes=VMEM_LIMIT)


def _layer_norm(z, g, b):
    mu = jnp.mean(z, axis=-1, keepdims=True)
    zc = z - mu
    var = jnp.mean(zc * zc, axis=-1, keepdims=True)
    return zc * lax.rsqrt(var + LN_EPS) * g + b


STREAMS = max(d for _, d in A_PATTERNS)
PERM_ROWS = 16 * STREAMS


def _stream_of_residue(r):
    return (r % 4) * 4 + r // 4


def _stream_permutation():
    tok = jnp.arange(PERM_ROWS, dtype=jnp.int32)
    out_row = _stream_of_residue(tok % STREAMS) * (PERM_ROWS // STREAMS) + tok // STREAMS
    return (out_row[None, :] == jnp.arange(PERM_ROWS, dtype=jnp.int32)[:, None]).astype(BF16)


def _inproj_kernel(x_ref, w_ref, b_ref, perm_ref, o_ref, qs_ref, *, qkv_hi, gate_lo, gate_hi):
    j = pl.program_id(1)
    acc = jnp.dot(x_ref[...].astype(BF16), w_ref[...], preferred_element_type=F32)
    is_gate = jnp.logical_and(j >= gate_lo, j < gate_hi)

    @pl.when(is_gate)
    def _():
        o_ref[...] = jax.nn.sigmoid(acc + b_ref[...]).astype(o_ref.dtype)

    @pl.when(jnp.logical_not(is_gate))
    def _():
        o_ref[...] = acc.astype(o_ref.dtype)

    @pl.when(j < qkv_hi)
    def _():
        rows = acc.astype(BF16)
        per = PERM_ROWS // STREAMS
        for g in range(rows.shape[0] // PERM_ROWS):
            grouped = jnp.dot(perm_ref[...], rows[g * PERM_ROWS:(g + 1) * PERM_ROWS],
                              preferred_element_type=F32).astype(qs_ref.dtype)
            for s in range(STREAMS):
                qs_ref[s, g * per:(g + 1) * per, :] = grouped[s * per:(s + 1) * per]


def _in_projection(x2, w_all, b_all, B, S):
    T = x2.shape[0]
    tm, tn = min(1024, S), 1024
    nt = S // tm
    qkv_hi = COL_GA // tn
    kern = functools.partial(_inproj_kernel, qkv_hi=qkv_hi, gate_lo=COL_GA // tn, gate_hi=COL_CQ // tn)
    return pl.pallas_call(
        kern,
        out_shape=(
            jax.ShapeDtypeStruct((T, PROJ_WIDTH), BF16),
            jax.ShapeDtypeStruct((B, STREAMS, S // STREAMS, 3 * A_WIDTH), BF16),
        ),
        grid=(T // tm, PROJ_WIDTH // tn),
        in_specs=[
            pl.BlockSpec((tm, D_MODEL), lambda i, j: (i, 0)),
            pl.BlockSpec((D_MODEL, tn), lambda i, j: (0, j)),
            pl.BlockSpec((1, tn), lambda i, j: (0, j)),
            pl.BlockSpec((PERM_ROWS, PERM_ROWS), lambda i, j: (0, 0)),
        ],
        out_specs=(
            pl.BlockSpec((tm, tn), lambda i, j: (i, j)),
            pl.BlockSpec((None, STREAMS, tm // STREAMS, tn),
                         lambda i, j: (i // nt, 0, i % nt, jnp.minimum(j, qkv_hi - 1))),
        ),
        compiler_params=_cparams("parallel", "arbitrary"),
        name="in_projection",
    )(x2, w_all, b_all, _stream_permutation())


def _block_pos(idx, groups):
    if groups == 1:
        return idx
    per = A_BLOCK // groups
    return groups * (idx % per) + idx // per


def _dilated_bias(dilation, steps, groups):
    blk = A_BLOCK
    i = jnp.arange(blk, dtype=jnp.int32)[:, None]
    c = jnp.arange(2 * blk, dtype=jnp.int32)[None, :]
    rel = blk + _block_pos(i, groups) - (blk * (c // blk) + _block_pos(c % blk, groups))
    band = (rel >= 0) & (rel <= steps)
    slopes = jnp.asarray([2.0 ** (-8.0 * (h + 1) / A_HEADS) for h in range(A_HEADS)], F32)
    bias = -slopes[:, None, None] * (dilation * rel).astype(F32)[None]
    first = jnp.where(band & (c >= blk), bias, NEG)
    later = jnp.where(band, bias, NEG)
    return jnp.stack([first, later], axis=0)


DILATED_SUB = 2


def _dilated_kernel(q_ref, kp_ref, ko_ref, vp_ref, vo_ref, bias_ref, o_ref, lse_ref):
    n = pl.program_id(2)
    blk = A_BLOCK
    per = kp_ref.shape[-2]
    lane = lax.broadcasted_iota(jnp.int32, (blk, LANE), 1)

    def rows(ref, u, sl):
        return ref[..., u * per:(u + 1) * per, sl].reshape(blk, A_HEAD_DIM)

    for u in range(DILATED_SUB):
        table = jnp.minimum(n, 1) if u == 0 else 1
        lse_all = jnp.zeros((blk, LANE), F32)
        for h in range(A_HEADS):
            sl = slice(h * A_HEAD_DIM, (h + 1) * A_HEAD_DIM)
            q = rows(q_ref, u, sl)
            k_prev = rows(kp_ref, 0, sl) if u == 0 else rows(ko_ref, u - 1, sl)
            v_prev = rows(vp_ref, 0, sl) if u == 0 else rows(vo_ref, u - 1, sl)
            k = jnp.concatenate([k_prev, rows(ko_ref, u, sl)], axis=0)
            v = jnp.concatenate([v_prev, rows(vo_ref, u, sl)], axis=0)
            logits = lax.dot_general(q, k, _NT, preferred_element_type=F32) + bias_ref[table, h]
            m = jnp.max(logits, axis=-1, keepdims=True)
            p = jnp.exp(logits - m)
            z = jnp.sum(p, axis=-1, keepdims=True)
            o = jnp.dot(p.astype(BF16), v, preferred_element_type=F32) / z
            o_ref[..., u * per:(u + 1) * per, sl] = o.astype(o_ref.dtype).reshape(kp_ref.shape[:-1] + (A_HEAD_DIM,))
            lse_all = jnp.where(lane == h, m + jnp.log(z), lse_all)
        lse_ref[..., u * per:(u + 1) * per, :] = lse_all.reshape(kp_ref.shape[:-1] + (LANE,))


def _dilated_dense(proj, B, S, window):
    T = B * S
    sub = DILATED_SUB
    nb = S // (A_BLOCK * sub)
    prev, own = (A_BLOCK, A_WIDTH), (A_BLOCK * sub, A_WIDTH)
    before = lambda b, n: b * nb * sub + jnp.maximum(n * sub - 1, 0)
    return pl.pallas_call(
        _dilated_kernel,
        out_shape=(jax.ShapeDtypeStruct((T, A_WIDTH), BF16), jax.ShapeDtypeStruct((T, LANE), F32)),
        grid=(B, 1, nb),
        in_specs=[
            pl.BlockSpec(own, lambda b, r, n: (b * nb + n, COL_Q // A_WIDTH)),
            pl.BlockSpec(prev, lambda b, r, n: (before(b, n), COL_K // A_WIDTH)),
            pl.BlockSpec(own, lambda b, r, n: (b * nb + n, COL_K // A_WIDTH)),
            pl.BlockSpec(prev, lambda b, r, n: (before(b, n), COL_V // A_WIDTH)),
            pl.BlockSpec(own, lambda b, r, n: (b * nb + n, COL_V // A_WIDTH)),
            pl.BlockSpec((2, A_HEADS, A_BLOCK, 2 * A_BLOCK), lambda b, r, n: (0, 0, 0, 0)),
        ],
        out_specs=(
            pl.BlockSpec(own, lambda b, r, n: (b * nb + n, 0)),
            pl.BlockSpec((A_BLOCK * sub, LANE), lambda b, r, n: (b * nb + n, 0)),
        ),
        compiler_params=_cparams("parallel", "parallel", "arbitrary"),
        name="dilated_attention_d1",
    )(proj, proj, proj, proj, proj, _dilated_bias(1, window, 1))


def _dilated_streams(qkv_s, window, dilation):
    B, ns, Ls, _ = qkv_s.shape
    groups = STREAMS // dilation
    per = A_BLOCK // groups
    sub = DILATED_SUB
    assert ns == STREAMS and Ls % (per * sub) == 0
    prev, own = (None, groups, per, A_WIDTH), (None, groups, per * sub, A_WIDTH)
    before = lambda n: jnp.maximum(n * sub - 1, 0)
    return pl.pallas_call(
        _dilated_kernel,
        out_shape=(
            jax.ShapeDtypeStruct((B, STREAMS, Ls, A_WIDTH), BF16),
            jax.ShapeDtypeStruct((B, STREAMS, Ls, LANE), F32),
        ),
        grid=(B, STREAMS // groups, Ls // (per * sub)),
        in_specs=[
            pl.BlockSpec(own, lambda b, r, n: (b, r, n, 0)),
            pl.BlockSpec(prev, lambda b, r, n: (b, r, before(n), 1)),
            pl.BlockSpec(own, lambda b, r, n: (b, r, n, 1)),
            pl.BlockSpec(prev, lambda b, r, n: (b, r, before(n), 2)),
            pl.BlockSpec(own, lambda b, r, n: (b, r, n, 2)),
            pl.BlockSpec((2, A_HEADS, A_BLOCK, 2 * A_BLOCK), lambda b, r, n: (0, 0, 0, 0)),
        ],
        out_specs=(
            pl.BlockSpec(own, lambda b, r, n: (b, r, n, 0)),
            pl.BlockSpec((None, groups, per * sub, LANE), lambda b, r, n: (b, r, n, 0)),
        ),
        compiler_params=_cparams("parallel", "parallel", "arbitrary"),
        name=f"dilated_attention_d{dilation}",
    )(qkv_s, qkv_s, qkv_s, qkv_s, qkv_s, _dilated_bias(dilation, window // dilation, groups))


COMBINE_STEPS = 16


def _combine_kernel(o1_ref, o2_ref, o3_ref, l1_ref, l2_ref, l3_ref, y_ref, ob_ref, oc_ref, lb_ref, lc_ref):
    for r in range(STREAMS):
        s = _stream_of_residue(r)
        tok = pl.ds(r, COMBINE_STEPS, stride=STREAMS)
        lb_ref[tok, :] = l2_ref[s]
        lc_ref[tok, :] = l3_ref[s]
        for h in range(A_HEADS):
            sl = slice(h * A_HEAD_DIM, (h + 1) * A_HEAD_DIM)
            ob_ref[h, tok, :] = o2_ref[s, :, sl].astype(F32)
            oc_ref[h, tok, :] = o3_ref[s, :, sl].astype(F32)
    a, b, c = l1_ref[...], lb_ref[...], lc_ref[...]
    m = jnp.maximum(jnp.maximum(a, b), c)
    ea, eb, ec = jnp.exp(a - m), jnp.exp(b - m), jnp.exp(c - m)
    inv = 1.0 / (ea + eb + ec)
    wa, wb, wc = ea * inv, eb * inv, ec * inv
    for h in range(A_HEADS):
        sl = slice(h * A_HEAD_DIM, (h + 1) * A_HEAD_DIM)
        y = (wa[:, h:h + 1] * o1_ref[:, sl].astype(F32)
             + wb[:, h:h + 1] * ob_ref[h]
             + wc[:, h:h + 1] * oc_ref[h])
        y_ref[:, sl] = y.astype(y_ref.dtype)


def _combine_patterns(o1, l1, o4, l4, o16, l16, B, S):
    T = B * S
    tm = COMBINE_STEPS * STREAMS
    nt = S // tm
    tok_o = pl.BlockSpec((tm, A_WIDTH), lambda b, i: (b * nt + i, 0))
    tok_l = pl.BlockSpec((tm, LANE), lambda b, i: (b * nt + i, 0))
    str_o = pl.BlockSpec((None, STREAMS, COMBINE_STEPS, A_WIDTH), lambda b, i: (b, 0, i, 0))
    str_l = pl.BlockSpec((None, STREAMS, COMBINE_STEPS, LANE), lambda b, i: (b, 0, i, 0))
    return pl.pallas_call(
        _combine_kernel,
        out_shape=jax.ShapeDtypeStruct((T, A_WIDTH), BF16),
        grid=(B, nt),
        in_specs=[tok_o, str_o, str_o, tok_l, str_l, str_l],
        out_specs=tok_o,
        scratch_shapes=[
            pltpu.VMEM((A_HEADS, tm, A_HEAD_DIM), F32),
            pltpu.VMEM((A_HEADS, tm, A_HEAD_DIM), F32),
            pltpu.VMEM((tm, LANE), F32),
            pltpu.VMEM((tm, LANE), F32),
        ],
        compiler_params=_cparams("parallel", "parallel"),
        name="combine_patterns",
    )(o1, o4, o16, l1, l4, l16)


def _rope_lanes(t, cos, sin):
    lane = lax.broadcasted_iota(jnp.int32, t.shape, 1)
    half = QK_ROPE // 2
    rot = jnp.where(lane < half, pltpu.roll(t, LANE - half, 1), pltpu.roll(t, half, 1))
    return t * cos + rot * sin


def _rms_norm(x, g):
    ms = jnp.mean(x * x, axis=-1, keepdims=True)
    return x * lax.rsqrt(ms + RMS_EPS) * g


def _latent_kernel(cq_ref, ckv_ref, kr_ref, gq_ref, gkv_ref, cos_ref, sin_ref, cqn_ref, ckvn_ref, krope_ref):
    cqn_ref[...] = _rms_norm(cq_ref[...].astype(F32), gq_ref[...]).astype(cqn_ref.dtype)
    ckvn_ref[...] = _rms_norm(ckv_ref[...].astype(F32), gkv_ref[...]).astype(ckvn_ref.dtype)
    krope_ref[...] = _rope_lanes(kr_ref[...].astype(F32), cos_ref[...], sin_ref[...]).astype(krope_ref.dtype)


def _latent_prep(proj, gq, gkv, cos_t, sin_t, S):
    T = proj.shape[0]
    tm = min(512, S)
    ns = S // tm
    return pl.pallas_call(
        _latent_kernel,
        out_shape=(
            jax.ShapeDtypeStruct((T, Q_LORA), BF16),
            jax.ShapeDtypeStruct((T, KV_LORA), BF16),
            jax.ShapeDtypeStruct((T, LANE), BF16),
        ),
        grid=(T // tm,),
        in_specs=[
            pl.BlockSpec((tm, Q_LORA), lambda i: (i, COL_CQ // Q_LORA)),
            pl.BlockSpec((tm, KV_LORA), lambda i: (i, COL_CKV // KV_LORA)),
            pl.BlockSpec((tm, LANE), lambda i: (i, COL_KR // LANE)),
            pl.BlockSpec((1, Q_LORA), lambda i: (0, 0)),
            pl.BlockSpec((1, KV_LORA), lambda i: (0, 0)),
            pl.BlockSpec((tm, LANE), lambda i: (i % ns, 0)),
            pl.BlockSpec((tm, LANE), lambda i: (i % ns, 0)),
        ],
        out_specs=(
            pl.BlockSpec((tm, Q_LORA), lambda i: (i, 0)),
            pl.BlockSpec((tm, KV_LORA), lambda i: (i, 0)),
            pl.BlockSpec((tm, LANE), lambda i: (i, 0)),
        ),
        compiler_params=_cparams("parallel"),
        name="latent_prep",
    )(proj, proj, proj, gq, gkv, cos_t, sin_t)


def _qup_kernel(c_ref, wt_ref, cos_ref, sin_ref, o_ref, *, scale):
    acc = lax.dot_general(wt_ref[...], c_ref[...], _NT, preferred_element_type=F32) * scale
    cos, sin = cos_ref[...], sin_ref[...]
    half = QK_ROPE // 2
    for hb in range(acc.shape[0] // MLA_HEAD_PAD):
        lo = hb * MLA_HEAD_PAD
        r1 = acc[lo + QK_NOPE:lo + QK_NOPE + half]
        r2 = acc[lo + QK_NOPE + half:lo + QK_NOPE + QK_ROPE]
        o_ref[lo:lo + QK_NOPE] = acc[lo:lo + QK_NOPE].astype(o_ref.dtype)
        o_ref[lo + QK_NOPE:lo + QK_NOPE + half] = (r1 * cos - r2 * sin).astype(o_ref.dtype)
        o_ref[lo + QK_NOPE + half:lo + QK_NOPE + QK_ROPE] = (r2 * cos + r1 * sin).astype(o_ref.dtype)
        o_ref[lo + QK_NOPE + QK_ROPE:lo + MLA_HEAD_PAD] = acc[lo + QK_NOPE + QK_ROPE:lo + MLA_HEAD_PAD].astype(o_ref.dtype)


def _q_up(cqn, w_uq_pt, cos_rt, sin_rt, S):
    T = cqn.shape[0]
    N = w_uq_pt.shape[0]
    tm, tn = min(1024, S), 1024
    ns = S // tm
    half = QK_ROPE // 2
    kern = functools.partial(_qup_kernel, scale=(QK_NOPE + QK_ROPE) ** -0.5 * math.log2(math.e))
    return pl.pallas_call(
        kern,
        out_shape=jax.ShapeDtypeStruct((N, T), BF16),
        grid=(T // tm, N // tn),
        in_specs=[
            pl.BlockSpec((tm, Q_LORA), lambda i, j: (i, 0)),
            pl.BlockSpec((tn, Q_LORA), lambda i, j: (j, 0)),
            pl.BlockSpec((half, tm), lambda i, j: (0, i % ns)),
            pl.BlockSpec((half, tm), lambda i, j: (0, i % ns)),
        ],
        out_specs=pl.BlockSpec((tn, tm), lambda i, j: (j, i)),
        compiler_params=_cparams("parallel", "arbitrary"),
        name="mla_q_up",
    )(cqn, w_uq_pt, cos_rt, sin_rt)


def _mm_kernel(a_ref, w_ref, o_ref):
    o_ref[...] = jnp.dot(a_ref[...], w_ref[...], preferred_element_type=F32).astype(o_ref.dtype)


def _matmul(a, w, name, tm=512, tn=1024):
    M, K = a.shape
    N = w.shape[1]
    tm, tn = min(tm, M), min(tn, N)
    return pl.pallas_call(
        _mm_kernel,
        out_shape=jax.ShapeDtypeStruct((M, N), BF16),
        grid=(M // tm, N // tn),
        in_specs=[
            pl.BlockSpec((tm, K), lambda i, j: (i, 0)),
            pl.BlockSpec((K, tn), lambda i, j: (0, j)),
        ],
        out_specs=pl.BlockSpec((tm, tn), lambda i, j: (i, j)),
        compiler_params=_cparams("parallel", "arbitrary"),
        name=name,
    )(a, w)


def _kvup_kernel(c_ref, wk_ref, wvt_ref, kn_ref, vt_ref):
    c = c_ref[...]
    kn_ref[...] = jnp.dot(c, wk_ref[...], preferred_element_type=F32).astype(kn_ref.dtype)
    vt_ref[...] = lax.dot_general(wvt_ref[...], c, _NT, preferred_element_type=F32).astype(vt_ref.dtype)


def _kv_up(ckvn, wk, wvt):
    T = ckvn.shape[0]
    tm = min(1024, T)
    n = B_HEADS * QK_NOPE
    return pl.pallas_call(
        _kvup_kernel,
        out_shape=(jax.ShapeDtypeStruct((T, n), BF16), jax.ShapeDtypeStruct((B_HEADS * V_HEAD, T), BF16)),
        grid=(T // tm,),
        in_specs=[
            pl.BlockSpec((tm, KV_LORA), lambda i: (i, 0)),
            pl.BlockSpec((KV_LORA, n), lambda i: (0, 0)),
            pl.BlockSpec((B_HEADS * V_HEAD, KV_LORA), lambda i: (0, 0)),
        ],
        out_specs=(pl.BlockSpec((tm, n), lambda i: (i, 0)), pl.BlockSpec((B_HEADS * V_HEAD, tm), lambda i: (0, i))),
        compiler_params=_cparams("parallel"),
        name="mla_kv_up",
    )(ckvn, wk, wvt)


def _mla_kernel(q_ref, kn_ref, kr_ref, vt_ref, o_ref, sa_ref, sb_ref, xa_ref, xb_ref, m_ref, l_ref, acc_ref, *, tq):
    qi = pl.program_id(2)
    tk = tq // 2
    q = q_ref[...]
    m_ref[...] = jnp.full(m_ref.shape, NEG, F32)
    l_ref[...] = jnp.zeros(l_ref.shape, F32)
    acc_ref[...] = jnp.zeros(acc_ref.shape, F32)

    def scores(c, s_ref, x_ref):
        start = pl.multiple_of(c * tk, tk)
        k = jnp.concatenate([kn_ref[pl.ds(start, tk), :], kr_ref[pl.ds(start, tk), :]], axis=1)
        st = jnp.dot(k, q, preferred_element_type=F32)
        s_ref[...] = st
        x_ref[...] = jnp.max(st, axis=0, keepdims=True)

    def update(c, s_ref, x_ref, masked):
        start = pl.multiple_of(c * tk, tk)
        st = s_ref[...]
        if masked:
            key = lax.broadcasted_iota(jnp.int32, st.shape, 0) + (c * tk - qi * tq)
            qry = lax.broadcasted_iota(jnp.int32, st.shape, 1)
            st = jnp.where(key <= qry, st, NEG)
            cmax = jnp.max(st, axis=0, keepdims=True)
        else:
            cmax = x_ref[...]
        m_prev = m_ref[...]
        m_new = jnp.maximum(m_prev, cmax)
        a = jnp.exp2(m_prev - m_new)
        p = jnp.exp2(st - m_new)
        l_ref[...] = a * l_ref[...] + jnp.sum(p, axis=0, keepdims=True)
        pv = jnp.dot(vt_ref[:, pl.ds(start, tk)], p.astype(BF16), preferred_element_type=F32)
        acc_ref[...] = a * acc_ref[...] + pv
        m_ref[...] = m_new

    scores(0, sa_ref, xa_ref)

    def pair(i):
        c = 2 * i
        scores(c + 1, sb_ref, xb_ref)
        update(c, sa_ref, xa_ref, False)
        scores(c + 2, sa_ref, xa_ref)
        update(c + 1, sb_ref, xb_ref, False)

    def two_pairs(i, carry):
        pair(2 * i)
        pair(2 * i + 1)
        return carry

    lax.fori_loop(0, qi // 2, two_pairs, 0)

    @pl.when(qi % 2 == 1)
    def _():
        pair(qi - 1)

    c = 2 * qi
    late = pl.ds(tk, tq - tk)
    start = pl.multiple_of((c + 1) * tk, tk)
    k = jnp.concatenate([kn_ref[pl.ds(start, tk), :], kr_ref[pl.ds(start, tk), :]], axis=1)
    sb_ref[:, late] = jnp.dot(k, q_ref[:, late], preferred_element_type=F32)
    update(c, sa_ref, xa_ref, True)
    st = sb_ref[:, late]
    key = lax.broadcasted_iota(jnp.int32, st.shape, 0)
    qry = lax.broadcasted_iota(jnp.int32, st.shape, 1)
    st = jnp.where(key <= qry, st, NEG)
    m_prev = m_ref[:, late]
    m_new = jnp.maximum(m_prev, jnp.max(st, axis=0, keepdims=True))
    a = jnp.exp2(m_prev - m_new)
    p = jnp.exp2(st - m_new)
    l_ref[:, late] = a * l_ref[:, late] + jnp.sum(p, axis=0, keepdims=True)
    pv = jnp.dot(vt_ref[:, pl.ds(start, tk)], p.astype(BF16), preferred_element_type=F32)
    acc_ref[:, late] = a * acc_ref[:, late] + pv
    o_ref[...] = (acc_ref[...] / l_ref[...]).T.astype(o_ref.dtype)


def _mla_attention(qt, kn, krope, vt, B, S):
    T = qt.shape[1]
    tq = min(1024, S)
    nq = S // tq
    kern = functools.partial(_mla_kernel, tq=tq)
    return pl.pallas_call(
        kern,
        out_shape=jax.ShapeDtypeStruct((T, B_HEADS * V_HEAD), BF16),
        grid=(B, B_HEADS, nq),
        in_specs=[
            pl.BlockSpec((MLA_HEAD_PAD, tq), lambda b, h, i: (h, b * nq + i)),
            pl.BlockSpec((S, QK_NOPE), lambda b, h, i: (b, h)),
            pl.BlockSpec((S, LANE), lambda b, h, i: (b, 0)),
            pl.BlockSpec((V_HEAD, S), lambda b, h, i: (h, b)),
        ],
        out_specs=pl.BlockSpec((tq, V_HEAD), lambda b, h, i: (b * nq + i, h)),
        scratch_shapes=(
            [pltpu.VMEM((tq // 2, tq), F32)] * 2
            + [pltpu.VMEM((1, tq), F32)] * 4
            + [pltpu.VMEM((V_HEAD, tq), F32)]
        ),
        compiler_params=_cparams("parallel", "parallel", "arbitrary"),
        name="mla_attention",
    )(qt, kn, krope, vt)


def _branch_kernel(ya_ref, yb_ref, wa_ref, wb_ref, ga_ref, gb_ref, o_ref):
    pa = jnp.dot(ya_ref[...], wa_ref[...], preferred_element_type=F32)
    pb = jnp.dot(yb_ref[...], wb_ref[...], preferred_element_type=F32)
    u = ga_ref[...].astype(F32) * pa + gb_ref[...].astype(F32) * pb
    o_ref[...] = u.astype(o_ref.dtype)


def _branch_mix(ya, yb, wa, wb, proj):
    T = ya.shape[0]
    tm, tn = min(512, T), 1024
    return pl.pallas_call(
        _branch_kernel,
        out_shape=jax.ShapeDtypeStruct((T, D_MODEL), BF16),
        grid=(T // tm, D_MODEL // tn),
        in_specs=[
            pl.BlockSpec((tm, A_WIDTH), lambda i, j: (i, 0)),
            pl.BlockSpec((tm, B_HEADS * V_HEAD), lambda i, j: (i, 0)),
            pl.BlockSpec((A_WIDTH, tn), lambda i, j: (0, j)),
            pl.BlockSpec((B_HEADS * V_HEAD, tn), lambda i, j: (0, j)),
            pl.BlockSpec((tm, tn), lambda i, j: (i, COL_GA // tn + j)),
            pl.BlockSpec((tm, tn), lambda i, j: (i, COL_GB // tn + j)),
        ],
        out_specs=pl.BlockSpec((tm, tn), lambda i, j: (i, j)),
        compiler_params=_cparams("parallel", "arbitrary"),
        name="branch_mix",
    )(ya, yb, wa, wb, proj, proj)


def _outln_kernel(u_ref, w_ref, x_ref, g_ref, b_ref, h_ref, hb_ref):
    mix = jnp.dot(u_ref[...], w_ref[...], preferred_element_type=F32)
    h = _layer_norm(ALPHA * x_ref[...] + mix, g_ref[...], b_ref[...])
    h_ref[...] = h
    hb_ref[...] = h.astype(hb_ref.dtype)


def _out_projection_ln(u, w_out, x2, g, b):
    T = u.shape[0]
    tm = min(512, T)
    row = pl.BlockSpec((tm, D_MODEL), lambda i: (i, 0))
    vec = pl.BlockSpec((1, D_MODEL), lambda i: (0, 0))
    return pl.pallas_call(
        _outln_kernel,
        out_shape=(
            jax.ShapeDtypeStruct((T, D_MODEL), F32),
            jax.ShapeDtypeStruct((T, D_MODEL), BF16),
        ),
        grid=(T // tm,),
        in_specs=[row, pl.BlockSpec((D_MODEL, D_MODEL), lambda i: (0, 0)), row, vec, vec],
        out_specs=(row, row),
        compiler_params=_cparams("parallel"),
        name="out_projection_ln",
    )(u, w_out, x2, g, b)


def _topk_axis0(s, ids, k):
    big = jnp.int32(2 ** 30)
    vals, idxs = [], []
    for _ in range(k):
        m = jnp.max(s, axis=0, keepdims=True)
        idx = jnp.min(jnp.where(s == m, ids, big), axis=0, keepdims=True)
        vals.append(m)
        idxs.append(idx)
        s = jnp.where(ids == idx, -jnp.inf, s)
    return vals, idxs


def _select_rows(rows, sel):
    out = jnp.zeros(sel.shape, rows[0].dtype)
    for a, r in enumerate(rows):
        out = jnp.where(sel == a, r, out)
    return out


def _route_kernel(q_ref, k1_ref, k2_ref, i1_ref, i2_ref, g_ref):
    half = PEER_QDIM // 2
    k, tm = PEER_TOPK, q_ref.shape[0]
    key_id = lax.broadcasted_iota(jnp.int32, (N_KEYS, tm), 0)
    sub = lax.broadcasted_iota(jnp.int32, (k // 2, tm), 0)
    cand_pos = jnp.concatenate([a * k + sub for a in range(k // 2)] + [k // 2 + sub, (k // 2 + sub) * k], axis=0)
    i1_all, i2_all, g_all = [], [], []
    for h in range(PEER_HEADS):
        q1 = q_ref[:, h * PEER_QDIM:h * PEER_QDIM + half]
        q2 = q_ref[:, h * PEER_QDIM + half:(h + 1) * PEER_QDIM]
        s1 = lax.dot_general(k1_ref[...], q1, _NT, preferred_element_type=F32)
        s2 = lax.dot_general(k2_ref[...], q2, _NT, preferred_element_type=F32)
        v1, i1 = _topk_axis0(s1, key_id, k)
        v2, i2 = _topk_axis0(s2, key_id, k)
        v1m = jnp.concatenate(v1, axis=0)
        v2m = jnp.concatenate(v2, axis=0)
        cand = jnp.concatenate(
            [v1[a] + v2m[:k // 2] for a in range(k // 2)] + [v1[0] + v2m[k // 2:], v1m[k // 2:] + v2[0]], axis=0)
        ts, pos = _topk_axis0(cand, cand_pos, k)
        top = jnp.concatenate(ts, axis=0)
        e = jnp.exp(top - ts[0])
        g_all.append(e / jnp.sum(e, axis=0, keepdims=True))
        posm = jnp.concatenate(pos, axis=0)
        i1_all.append(_select_rows(i1, posm >> int(math.log2(k))))
        i2_all.append(_select_rows(i2, posm & (k - 1)))
    i1_ref[...] = jnp.concatenate(i1_all, axis=0).T
    i2_ref[...] = jnp.concatenate(i2_all, axis=0).T
    g_ref[...] = jnp.concatenate(g_all, axis=0).T


def _peer_route(qp, k1, k2):
    T = qp.shape[0]
    tm = min(256, T)
    slot = pl.BlockSpec((tm, N_SLOTS), lambda i: (i, 0))
    keys = pl.BlockSpec((N_KEYS, PEER_QDIM // 2), lambda i: (0, 0))
    return pl.pallas_call(
        _route_kernel,
        out_shape=(
            jax.ShapeDtypeStruct((T, N_SLOTS), jnp.int32),
            jax.ShapeDtypeStruct((T, N_SLOTS), jnp.int32),
            jax.ShapeDtypeStruct((T, N_SLOTS), F32),
        ),
        grid=(T // tm,),
        in_specs=[pl.BlockSpec((tm, PEER_HEADS * PEER_QDIM), lambda i: (i, 0)), keys, keys],
        out_specs=(slot, slot, slot),
        compiler_params=_cparams("parallel"),
        name="peer_route",
    )(qp, k1, k2)


GATE_GROUP = 64


def _gate_matrix_kernel(i1_ref, i2_ref, g_ref, w_ref):
    key = lax.broadcasted_iota(jnp.int32, (N_KEYS, N_SLOTS), 0)

    def body(tg, carry):
        t0 = pl.multiple_of(tg * GATE_GROUP, GATE_GROUP)
        per_token = []
        for u in range(GATE_GROUP):
            r1 = i1_ref[pl.ds(t0 + u, 1), :]
            r2 = i2_ref[pl.ds(t0 + u, 1), :]
            g = g_ref[pl.ds(t0 + u, 1), :]
            a = jnp.where(key == r1, 1.0, 0.0).astype(BF16)
            b = jnp.where(key == r2, g, 0.0).astype(BF16)
            per_token.append(lax.dot_general(a, b, _NT, preferred_element_type=F32))
        w = pltpu.einshape("tid->itd", jnp.stack(per_token, axis=0))
        w_ref[:, pl.ds(t0, GATE_GROUP), :] = w.astype(w_ref.dtype)
        return carry

    lax.fori_loop(0, w_ref.shape[1] // GATE_GROUP, body, 0)


def _gate_matrix(i1, i2, g):
    T = i1.shape[0]
    tb = min(256, T)
    slot = pl.BlockSpec((tb, N_SLOTS), lambda i: (i, 0))
    return pl.pallas_call(
        _gate_matrix_kernel,
        out_shape=jax.ShapeDtypeStruct((N_KEYS, T, N_KEYS), BF16),
        grid=(T // tb,),
        in_specs=[slot, slot, slot],
        out_specs=pl.BlockSpec((N_KEYS, tb, N_KEYS), lambda i: (0, i, 0)),
        compiler_params=_cparams("parallel"),
        name="peer_gate_matrix",
    )(i1, i2, g)


EXPERT_TILE = 1024


def _experts_kernel(hb_ref, dn_ref, up_ref, w_ref, y_ref, acc_ref):
    j = pl.program_id(1)

    @pl.when(j == 0)
    def _():
        acc_ref[...] = jnp.zeros(acc_ref.shape, F32)

    pre = jnp.dot(hb_ref[...], dn_ref[...], preferred_element_type=F32)
    act = 0.5 * pre * (1.0 + lax.erf(pre * (2.0 ** -0.5)))
    gates = jnp.concatenate([w_ref[m] for m in range(w_ref.shape[0])], axis=1)
    act = act * gates.astype(F32)
    acc_ref[...] += jnp.dot(act.astype(BF16), up_ref[...], preferred_element_type=F32)

    @pl.when(j == pl.num_programs(1) - 1)
    def _():
        y_ref[...] = acc_ref[...].astype(y_ref.dtype)


def _peer_experts(hb, down_tiles, up, w3):
    T = hb.shape[0]
    tm, te = min(1024, T), EXPERT_TILE
    row = pl.BlockSpec((tm, D_MODEL), lambda i, j: (i, 0))
    return pl.pallas_call(
        _experts_kernel,
        out_shape=jax.ShapeDtypeStruct((T, D_MODEL), BF16),
        grid=(T // tm, N_EXPERTS // te),
        in_specs=[
            row,
            pl.BlockSpec((None, D_MODEL, te), lambda i, j: (j, 0, 0)),
            pl.BlockSpec((te, D_MODEL), lambda i, j: (j, 0)),
            pl.BlockSpec((te // N_KEYS, tm, N_KEYS), lambda i, j: (j, i, 0)),
        ],
        out_specs=row,
        scratch_shapes=[pltpu.VMEM((tm, D_MODEL), F32)],
        compiler_params=_cparams("parallel", "arbitrary"),
        name="peer_experts",
    )(hb, down_tiles, up, w3)


def _residual_ln_kernel(y_ref, h_ref, g_ref, b_ref, o_ref):
    o_ref[...] = _layer_norm(ALPHA * h_ref[...] + y_ref[...].astype(F32), g_ref[...], b_ref[...])


def _residual_ln(y, h, g, b):
    T = y.shape[0]
    tm = min(512, T)
    row = pl.BlockSpec((tm, D_MODEL), lambda i: (i, 0))
    vec = pl.BlockSpec((1, D_MODEL), lambda i: (0, 0))
    return pl.pallas_call(
        _residual_ln_kernel,
        out_shape=jax.ShapeDtypeStruct((T, D_MODEL), F32),
        grid=(T // tm,),
        in_specs=[row, row, vec, vec],
        out_specs=row,
        compiler_params=_cparams("parallel"),
        name="peer_residual_ln",
    )(y, h, g, b)


def _rope_tables(S):
    half = QK_ROPE // 2
    inv_freq = ROPE_THETA ** (-jnp.arange(half, dtype=F32) / half)
    ang = jnp.arange(S, dtype=jnp.int32).astype(F32)[:, None] * inv_freq[None, :]
    cos, sin = jnp.cos(ang), jnp.sin(ang)
    zeros = jnp.zeros((S, LANE - QK_ROPE), F32)
    lanes = (jnp.concatenate([cos, cos, zeros], axis=1), jnp.concatenate([-sin, sin, zeros], axis=1))
    return lanes, (cos.T, sin.T)


def _pack_input_weights(w_in, b_gates):
    wq, wk, wv, wcq, wckv, wkr, wg = jnp.split(
        w_in, (A_WIDTH, 2 * A_WIDTH, 3 * A_WIDTH, 3 * A_WIDTH + Q_LORA,
               3 * A_WIDTH + Q_LORA + KV_LORA, 3 * A_WIDTH + Q_LORA + KV_LORA + QK_ROPE), axis=1)
    pad = jnp.zeros((D_MODEL, PROJ_WIDTH - COL_KR - QK_ROPE), w_in.dtype)
    wq = wq * (A_HEAD_DIM ** -0.5)
    w_all = jnp.concatenate([wq, wk, wv, wg, wcq, wckv, wkr, pad], axis=1).astype(BF16)
    b_all = jnp.zeros((1, PROJ_WIDTH), F32).at[0, COL_GA:COL_CQ].set(b_gates)
    return w_all, b_all


def _pack_uq(w_uq):
    w = w_uq.reshape(Q_LORA, B_HEADS, QK_NOPE + QK_ROPE)
    w = jnp.pad(w, ((0, 0), (0, 0), (0, MLA_HEAD_PAD - QK_NOPE - QK_ROPE)))
    return w.reshape(Q_LORA, B_HEADS * MLA_HEAD_PAD).T.astype(BF16)


def kernel(x, w_in, b_gates, a_w_out, mla_q_norm, mla_w_uq, mla_kv_norm, mla_w_ukv, mla_w_out, w_out,
           ln1_g, ln1_b, peer_w_query, peer_sub_keys_1, peer_sub_keys_2, peer_expert_down,
           peer_expert_up, ln2_g, ln2_b):
    B, S, D = x.shape
    assert D == D_MODEL and w_in.shape[0] == DEPTH
    T = B * S
    (cos_t, sin_t), (cos_rt, sin_rt) = _rope_tables(S)
    h = x.reshape(T, D)
    for l in range(DEPTH):
        w_all, b_all = _pack_input_weights(w_in[l], b_gates[l])
        proj, qkv_s = _in_projection(h, w_all, b_all, B, S)

        (w1, d1), (w4, d4), (w16, d16) = A_PATTERNS
        assert (d1, d4, d16) == (1, 4, STREAMS)
        o1, l1 = _dilated_dense(proj, B, S, w1)
        o4, l4 = _dilated_streams(qkv_s, w4, d4)
        o16, l16 = _dilated_streams(qkv_s, w16, d16)
        ya = _combine_patterns(o1, l1, o4, l4, o16, l16, B, S)

        cqn, ckvn, krope = _latent_prep(proj, mla_q_norm[l][None], mla_kv_norm[l][None], cos_t, sin_t, S)
        qt = _q_up(cqn, _pack_uq(mla_w_uq[l]), cos_rt, sin_rt, S)
        w_ukv = mla_w_ukv[l].reshape(KV_LORA, B_HEADS, QK_NOPE + V_HEAD)
        wk = w_ukv[:, :, :QK_NOPE].reshape(KV_LORA, B_HEADS * QK_NOPE).astype(BF16)
        wvt = w_ukv[:, :, QK_NOPE:].reshape(KV_LORA, B_HEADS * V_HEAD).T.astype(BF16)
        kn, vt = _kv_up(ckvn, wk, wvt)
        yb = _mla_attention(qt, kn, krope, vt, B, S)

        u = _branch_mix(ya, yb, a_w_out[l].astype(BF16), mla_w_out[l].astype(BF16), proj)
        h1, h1b = _out_projection_ln(u, w_out[l].astype(BF16), h, ln1_g[l][None], ln1_b[l][None])

        qp = _matmul(h1b, peer_w_query[l].astype(BF16), "peer_query", tm=1024)
        i1, i2, g = _peer_route(qp, peer_sub_keys_1[l].astype(BF16), peer_sub_keys_2[l].astype(BF16))
        w3 = _gate_matrix(i1, i2, g)
        down_tiles = peer_expert_down[l].reshape(N_EXPERTS // EXPERT_TILE, EXPERT_TILE, D).transpose(0, 2, 1)
        yp = _peer_experts(h1b, down_tiles.astype(BF16), peer_expert_up[l].astype(BF16), w3)
        h = _residual_ln(yp, h1, ln2_g[l][None], ln2_b[l][None])
    return h.reshape(B, S, D)
```

```python
import functools
import math

import jax
import jax.numpy as jnp
from jax import lax
from jax.experimental import pallas as pl
from jax.experimental.pallas import tpu as pltpu

F32 = jnp.float32
BF16 = jnp.bfloat16

D_MODEL = 2048
A_HEADS = 16
A_HEAD_DIM = 128
A_PATTERNS = ((128, 1), (512, 4), (2048, 16))
A_BLOCK = 128
A_WIDTH = A_HEADS * A_HEAD_DIM
B_HEADS = 16
Q_LORA = 512
KV_LORA = 512
QK_NOPE = 128
QK_ROPE = 64
V_HEAD = 128
ROPE_THETA = 10000.0
N_KEYS = 128
PEER_HEADS = 8
PEER_QDIM = 256
PEER_TOPK = 16
N_EXPERTS = N_KEYS * N_KEYS
N_SLOTS = PEER_HEADS * PEER_TOPK
LN_EPS = 1e-5
RMS_EPS = 1e-6
DEPTH = 1
ALPHA = (2.0 * DEPTH) ** 0.25
NEG = -1e30

LANE = 128
MLA_HEAD_PAD = 256
VMEM_LIMIT = 56 * 1024 * 1024
ROWS_WIDE, ROWS_MEDIUM, ROWS_NARROW = 1024, 512, 256
COL_TILE = 1024

COL_Q, COL_K, COL_V = 0, A_WIDTH, 2 * A_WIDTH
COL_GA = 3 * A_WIDTH
COL_GB = COL_GA + D_MODEL
COL_CQ = COL_GB + D_MODEL
COL_CKV = COL_CQ + Q_LORA
COL_KR = COL_CKV + KV_LORA
PROJ_WIDTH = COL_CQ + 2048

_NT = (((1,), (1,)), ((), ()))


def _cparams(*sem):
    return pltpu.CompilerParams(dimension_semantics=sem, vmem_limit_bytes=VMEM_LIMIT)


def _layer_norm(z, g, b):
    mu = jnp.mean(z, axis=-1, keepdims=True)
    zc = z - mu
    var = jnp.mean(zc * zc, axis=-1, keepdims=True)
    return zc * lax.rsqrt(var + LN_EPS) * g + b


STREAMS = max(d for _, d in A_PATTERNS)
PERM_ROWS = 16 * STREAMS


MID_DILATION = A_PATTERNS[1][1]


def _stream_of_residue(r):
    return (r % MID_DILATION) * (STREAMS // MID_DILATION) + r // MID_DILATION


def _stream_permutation():
    tok = jnp.arange(PERM_ROWS, dtype=jnp.int32)
    out_row = _stream_of_residue(tok % STREAMS) * (PERM_ROWS // STREAMS) + tok // STREAMS
    return (out_row[None, :] == jnp.arange(PERM_ROWS, dtype=jnp.int32)[:, None]).astype(BF16)


def _inproj_kernel(x_ref, w_ref, b_ref, perm_ref, o_ref, qs_ref, *, qkv_hi, gate_lo, gate_hi):
    j = pl.program_id(1)
    acc = jnp.dot(x_ref[...].astype(BF16), w_ref[...], preferred_element_type=F32)
    is_gate = jnp.logical_and(j >= gate_lo, j < gate_hi)

    @pl.when(is_gate)
    def _():
        o_ref[...] = jax.nn.sigmoid(acc + b_ref[...]).astype(o_ref.dtype)

    @pl.when(jnp.logical_not(is_gate))
    def _():
        o_ref[...] = acc.astype(o_ref.dtype)

    @pl.when(j < qkv_hi)
    def _():
        rows = acc.astype(BF16)
        per = PERM_ROWS // STREAMS
        for g in range(rows.shape[0] // PERM_ROWS):
            grouped = jnp.dot(perm_ref[...], rows[g * PERM_ROWS:(g + 1) * PERM_ROWS],
                              preferred_element_type=F32).astype(qs_ref.dtype)
            for s in range(STREAMS):
                qs_ref[s, g * per:(g + 1) * per, :] = grouped[s * per:(s + 1) * per]


def _in_projection(x2, w_all, b_all, B, S):
    T = x2.shape[0]
    tm, tn = min(ROWS_WIDE, S), COL_TILE
    nt = S // tm
    qkv_hi = COL_GA // tn
    kern = functools.partial(_inproj_kernel, qkv_hi=qkv_hi, gate_lo=COL_GA // tn, gate_hi=COL_CQ // tn)
    return pl.pallas_call(
        kern,
        out_shape=(
            jax.ShapeDtypeStruct((T, PROJ_WIDTH), BF16),
            jax.ShapeDtypeStruct((B, STREAMS, S // STREAMS, 3 * A_WIDTH), BF16),
        ),
        grid=(T // tm, PROJ_WIDTH // tn),
        in_specs=[
            pl.BlockSpec((tm, D_MODEL), lambda i, j: (i, 0)),
            pl.BlockSpec((D_MODEL, tn), lambda i, j: (0, j)),
            pl.BlockSpec((1, tn), lambda i, j: (0, j)),
            pl.BlockSpec((PERM_ROWS, PERM_ROWS), lambda i, j: (0, 0)),
        ],
        out_specs=(
            pl.BlockSpec((tm, tn), lambda i, j: (i, j)),
            pl.BlockSpec((None, STREAMS, tm // STREAMS, tn),
                         lambda i, j: (i // nt, 0, i % nt, jnp.minimum(j, qkv_hi - 1))),
        ),
        compiler_params=_cparams("parallel", "arbitrary"),
        name="in_projection",
    )(x2, w_all, b_all, _stream_permutation())


def _block_pos(idx, groups):
    if groups == 1:
        return idx
    per = A_BLOCK // groups
    return groups * (idx % per) + idx // per


def _dilated_bias(dilation, steps, groups):
    blk = A_BLOCK
    i = jnp.arange(blk, dtype=jnp.int32)[:, None]
    c = jnp.arange(2 * blk, dtype=jnp.int32)[None, :]
    rel = blk + _block_pos(i, groups) - (blk * (c // blk) + _block_pos(c % blk, groups))
    band = (rel >= 0) & (rel <= steps)
    slopes = jnp.asarray([2.0 ** (-8.0 * (h + 1) / A_HEADS) for h in range(A_HEADS)], F32)
    bias = -slopes[:, None, None] * (dilation * rel).astype(F32)[None]
    first = jnp.where(band & (c >= blk), bias, NEG)
    later = jnp.where(band, bias, NEG)
    return jnp.stack([first, later], axis=0)


DILATED_SUB = 2


def _dilated_kernel(q_ref, kp_ref, ko_ref, vp_ref, vo_ref, bias_ref, o_ref, lse_ref):
    n = pl.program_id(2)
    blk = A_BLOCK
    per = kp_ref.shape[-2]
    lane = lax.broadcasted_iota(jnp.int32, (blk, LANE), 1)

    def rows(ref, u, sl):
        return ref[..., u * per:(u + 1) * per, sl].reshape(blk, A_HEAD_DIM)

    for u in range(DILATED_SUB):
        table = jnp.minimum(n, 1) if u == 0 else 1
        lse_all = jnp.zeros((blk, LANE), F32)
        for h in range(A_HEADS):
            sl = slice(h * A_HEAD_DIM, (h + 1) * A_HEAD_DIM)
            q = rows(q_ref, u, sl)
            k_prev = rows(kp_ref, 0, sl) if u == 0 else rows(ko_ref, u - 1, sl)
            v_prev = rows(vp_ref, 0, sl) if u == 0 else rows(vo_ref, u - 1, sl)
            k = jnp.concatenate([k_prev, rows(ko_ref, u, sl)], axis=0)
            v = jnp.concatenate([v_prev, rows(vo_ref, u, sl)], axis=0)
            logits = lax.dot_general(q, k, _NT, preferred_element_type=F32) + bias_ref[table, h]
            m = jnp.max(logits, axis=-1, keepdims=True)
            p = jnp.exp(logits - m)
            z = jnp.sum(p, axis=-1, keepdims=True)
            o = jnp.dot(p.astype(BF16), v, preferred_element_type=F32) / z
            o_ref[..., u * per:(u + 1) * per, sl] = o.astype(o_ref.dtype).reshape(kp_ref.shape[:-1] + (A_HEAD_DIM,))
            lse_all = jnp.where(lane == h, m + jnp.log(z), lse_all)
        lse_ref[..., u * per:(u + 1) * per, :] = lse_all.reshape(kp_ref.shape[:-1] + (LANE,))


def _dilated_dense(proj, B, S, window):
    T = B * S
    sub = DILATED_SUB
    nb = S // (A_BLOCK * sub)
    prev, own = (A_BLOCK, A_WIDTH), (A_BLOCK * sub, A_WIDTH)
    before = lambda b, n: b * nb * sub + jnp.maximum(n * sub - 1, 0)
    return pl.pallas_call(
        _dilated_kernel,
        out_shape=(jax.ShapeDtypeStruct((T, A_WIDTH), BF16), jax.ShapeDtypeStruct((T, LANE), F32)),
        grid=(B, 1, nb),
        in_specs=[
            pl.BlockSpec(own, lambda b, r, n: (b * nb + n, COL_Q // A_WIDTH)),
            pl.BlockSpec(prev, lambda b, r, n: (before(b, n), COL_K // A_WIDTH)),
            pl.BlockSpec(own, lambda b, r, n: (b * nb + n, COL_K // A_WIDTH)),
            pl.BlockSpec(prev, lambda b, r, n: (before(b, n), COL_V // A_WIDTH)),
            pl.BlockSpec(own, lambda b, r, n: (b * nb + n, COL_V // A_WIDTH)),
            pl.BlockSpec((2, A_HEADS, A_BLOCK, 2 * A_BLOCK), lambda b, r, n: (0, 0, 0, 0)),
        ],
        out_specs=(
            pl.BlockSpec(own, lambda b, r, n: (b * nb + n, 0)),
            pl.BlockSpec((A_BLOCK * sub, LANE), lambda b, r, n: (b * nb + n, 0)),
        ),
        compiler_params=_cparams("parallel", "parallel", "arbitrary"),
        name="dilated_attention_d1",
    )(proj, proj, proj, proj, proj, _dilated_bias(1, window, 1))


def _dilated_streams(qkv_s, window, dilation):
    B, ns, Ls, _ = qkv_s.shape
    groups = STREAMS // dilation
    per = A_BLOCK // groups
    sub = DILATED_SUB
    assert ns == STREAMS and Ls % (per * sub) == 0
    prev, own = (None, groups, per, A_WIDTH), (None, groups, per * sub, A_WIDTH)
    before = lambda n: jnp.maximum(n * sub - 1, 0)
    return pl.pallas_call(
        _dilated_kernel,
        out_shape=(
            jax.ShapeDtypeStruct((B, STREAMS, Ls, A_WIDTH), BF16),
            jax.ShapeDtypeStruct((B, STREAMS, Ls, LANE), F32),
        ),
        grid=(B, STREAMS // groups, Ls // (per * sub)),
        in_specs=[
            pl.BlockSpec(own, lambda b, r, n: (b, r, n, 0)),
            pl.BlockSpec(prev, lambda b, r, n: (b, r, before(n), 1)),
            pl.BlockSpec(own, lambda b, r, n: (b, r, n, 1)),
            pl.BlockSpec(prev, lambda b, r, n: (b, r, before(n), 2)),
            pl.BlockSpec(own, lambda b, r, n: (b, r, n, 2)),
            pl.BlockSpec((2, A_HEADS, A_BLOCK, 2 * A_BLOCK), lambda b, r, n: (0, 0, 0, 0)),
        ],
        out_specs=(
            pl.BlockSpec(own, lambda b, r, n: (b, r, n, 0)),
            pl.BlockSpec((None, groups, per * sub, LANE), lambda b, r, n: (b, r, n, 0)),
        ),
        compiler_params=_cparams("parallel", "parallel", "arbitrary"),
        name=f"dilated_attention_d{dilation}",
    )(qkv_s, qkv_s, qkv_s, qkv_s, qkv_s, _dilated_bias(dilation, window // dilation, groups))


COMBINE_STEPS = 16


def _combine_kernel(o1_ref, o2_ref, o3_ref, l1_ref, l2_ref, l3_ref, y_ref, ob_ref, oc_ref, lb_ref, lc_ref):
    for r in range(STREAMS):
        s = _stream_of_residue(r)
        tok = pl.ds(r, COMBINE_STEPS, stride=STREAMS)
        lb_ref[tok, :] = l2_ref[s]
        lc_ref[tok, :] = l3_ref[s]
        for h in range(A_HEADS):
            sl = slice(h * A_HEAD_DIM, (h + 1) * A_HEAD_DIM)
            ob_ref[h, tok, :] = o2_ref[s, :, sl].astype(F32)
            oc_ref[h, tok, :] = o3_ref[s, :, sl].astype(F32)
    a, b, c = l1_ref[...], lb_ref[...], lc_ref[...]
    m = jnp.maximum(jnp.maximum(a, b), c)
    ea, eb, ec = jnp.exp(a - m), jnp.exp(b - m), jnp.exp(c - m)
    inv = 1.0 / (ea + eb + ec)
    wa, wb, wc = ea * inv, eb * inv, ec * inv
    for h in range(A_HEADS):
        sl = slice(h * A_HEAD_DIM, (h + 1) * A_HEAD_DIM)
        y = (wa[:, h:h + 1] * o1_ref[:, sl].astype(F32)
             + wb[:, h:h + 1] * ob_ref[h]
             + wc[:, h:h + 1] * oc_ref[h])
        y_ref[:, sl] = y.astype(y_ref.dtype)


def _combine_patterns(o1, l1, o4, l4, o16, l16, B, S):
    T = B * S
    tm = COMBINE_STEPS * STREAMS
    nt = S // tm
    tok_o = pl.BlockSpec((tm, A_WIDTH), lambda b, i: (b * nt + i, 0))
    tok_l = pl.BlockSpec((tm, LANE), lambda b, i: (b * nt + i, 0))
    str_o = pl.BlockSpec((None, STREAMS, COMBINE_STEPS, A_WIDTH), lambda b, i: (b, 0, i, 0))
    str_l = pl.BlockSpec((None, STREAMS, COMBINE_STEPS, LANE), lambda b, i: (b, 0, i, 0))
    return pl.pallas_call(
        _combine_kernel,
        out_shape=jax.ShapeDtypeStruct((T, A_WIDTH), BF16),
        grid=(B, nt),
        in_specs=[tok_o, str_o, str_o, tok_l, str_l, str_l],
        out_specs=tok_o,
        scratch_shapes=[
            pltpu.VMEM((A_HEADS, tm, A_HEAD_DIM), F32),
            pltpu.VMEM((A_HEADS, tm, A_HEAD_DIM), F32),
            pltpu.VMEM((tm, LANE), F32),
            pltpu.VMEM((tm, LANE), F32),
        ],
        compiler_params=_cparams("parallel", "parallel"),
        name="combine_patterns",
    )(o1, o4, o16, l1, l4, l16)


def _rope_lanes(t, cos, sin):
    lane = lax.broadcasted_iota(jnp.int32, t.shape, 1)
    half = QK_ROPE // 2
    rot = jnp.where(lane < half, pltpu.roll(t, LANE - half, 1), pltpu.roll(t, half, 1))
    return t * cos + rot * sin


def _rms_norm(x, g):
    ms = jnp.mean(x * x, axis=-1, keepdims=True)
    return x * lax.rsqrt(ms + RMS_EPS) * g


def _latent_kernel(cq_ref, ckv_ref, kr_ref, gq_ref, gkv_ref, cos_ref, sin_ref, cqn_ref, ckvn_ref, krope_ref):
    cqn_ref[...] = _rms_norm(cq_ref[...].astype(F32), gq_ref[...]).astype(cqn_ref.dtype)
    ckvn_ref[...] = _rms_norm(ckv_ref[...].astype(F32), gkv_ref[...]).astype(ckvn_ref.dtype)
    krope_ref[...] = _rope_lanes(kr_ref[...].astype(F32), cos_ref[...], sin_ref[...]).astype(krope_ref.dtype)


def _latent_prep(proj, gq, gkv, cos_t, sin_t, S):
    T = proj.shape[0]
    tm = min(ROWS_MEDIUM, S)
    ns = S // tm
    return pl.pallas_call(
        _latent_kernel,
        out_shape=(
            jax.ShapeDtypeStruct((T, Q_LORA), BF16),
            jax.ShapeDtypeStruct((T, KV_LORA), BF16),
            jax.ShapeDtypeStruct((T, LANE), BF16),
        ),
        grid=(T // tm,),
        in_specs=[
            pl.BlockSpec((tm, Q_LORA), lambda i: (i, COL_CQ // Q_LORA)),
            pl.BlockSpec((tm, KV_LORA), lambda i: (i, COL_CKV // KV_LORA)),
            pl.BlockSpec((tm, LANE), lambda i: (i, COL_KR // LANE)),
            pl.BlockSpec((1, Q_LORA), lambda i: (0, 0)),
            pl.BlockSpec((1, KV_LORA), lambda i: (0, 0)),
            pl.BlockSpec((tm, LANE), lambda i: (i % ns, 0)),
            pl.BlockSpec((tm, LANE), lambda i: (i % ns, 0)),
        ],
        out_specs=(
            pl.BlockSpec((tm, Q_LORA), lambda i: (i, 0)),
            pl.BlockSpec((tm, KV_LORA), lambda i: (i, 0)),
            pl.BlockSpec((tm, LANE), lambda i: (i, 0)),
        ),
        compiler_params=_cparams("parallel"),
        name="latent_prep",
    )(proj, proj, proj, gq, gkv, cos_t, sin_t)


def _qup_kernel(c_ref, wt_ref, cos_ref, sin_ref, o_ref, *, scale):
    acc = lax.dot_general(wt_ref[...], c_ref[...], _NT, preferred_element_type=F32) * scale
    cos, sin = cos_ref[...], sin_ref[...]
    half = QK_ROPE // 2
    for hb in range(acc.shape[0] // MLA_HEAD_PAD):
        lo = hb * MLA_HEAD_PAD
        r1 = acc[lo + QK_NOPE:lo + QK_NOPE + half]
        r2 = acc[lo + QK_NOPE + half:lo + QK_NOPE + QK_ROPE]
        o_ref[lo:lo + QK_NOPE] = acc[lo:lo + QK_NOPE].astype(o_ref.dtype)
        o_ref[lo + QK_NOPE:lo + QK_NOPE + half] = (r1 * cos - r2 * sin).astype(o_ref.dtype)
        o_ref[lo + QK_NOPE + half:lo + QK_NOPE + QK_ROPE] = (r2 * cos + r1 * sin).astype(o_ref.dtype)
        o_ref[lo + QK_NOPE + QK_ROPE:lo + MLA_HEAD_PAD] = acc[lo + QK_NOPE + QK_ROPE:lo + MLA_HEAD_PAD].astype(o_ref.dtype)


def _q_up(cqn, w_uq_pt, cos_rt, sin_rt, S):
    T = cqn.shape[0]
    N = w_uq_pt.shape[0]
    tm, tn = min(ROWS_WIDE, S), COL_TILE
    ns = S // tm
    half = QK_ROPE // 2
    kern = functools.partial(_qup_kernel, scale=(QK_NOPE + QK_ROPE) ** -0.5 * math.log2(math.e))
    return pl.pallas_call(
        kern,
        out_shape=jax.ShapeDtypeStruct((N, T), BF16),
        grid=(T // tm, N // tn),
        in_specs=[
            pl.BlockSpec((tm, Q_LORA), lambda i, j: (i, 0)),
            pl.BlockSpec((tn, Q_LORA), lambda i, j: (j, 0)),
            pl.BlockSpec((half, tm), lambda i, j: (0, i % ns)),
            pl.BlockSpec((half, tm), lambda i, j: (0, i % ns)),
        ],
        out_specs=pl.BlockSpec((tn, tm), lambda i, j: (j, i)),
        compiler_params=_cparams("parallel", "arbitrary"),
        name="mla_q_up",
    )(cqn, w_uq_pt, cos_rt, sin_rt)


def _mm_kernel(a_ref, w_ref, o_ref):
    o_ref[...] = jnp.dot(a_ref[...], w_ref[...], preferred_element_type=F32).astype(o_ref.dtype)


def _matmul(a, w, name, tm=ROWS_WIDE, tn=COL_TILE):
    M, K = a.shape
    N = w.shape[1]
    tm, tn = min(tm, M), min(tn, N)
    return pl.pallas_call(
        _mm_kernel,
        out_shape=jax.ShapeDtypeStruct((M, N), BF16),
        grid=(M // tm, N // tn),
        in_specs=[
            pl.BlockSpec((tm, K), lambda i, j: (i, 0)),
            pl.BlockSpec((K, tn), lambda i, j: (0, j)),
        ],
        out_specs=pl.BlockSpec((tm, tn), lambda i, j: (i, j)),
        compiler_params=_cparams("parallel", "arbitrary"),
        name=name,
    )(a, w)


def _kvup_kernel(c_ref, wk_ref, wvt_ref, kn_ref, vt_ref):
    c = c_ref[...]
    kn_ref[...] = jnp.dot(c, wk_ref[...], preferred_element_type=F32).astype(kn_ref.dtype)
    vt_ref[...] = lax.dot_general(wvt_ref[...], c, _NT, preferred_element_type=F32).astype(vt_ref.dtype)


def _kv_up(ckvn, wk, wvt):
    T = ckvn.shape[0]
    tm = min(ROWS_WIDE, T)
    n = B_HEADS * QK_NOPE
    return pl.pallas_call(
        _kvup_kernel,
        out_shape=(jax.ShapeDtypeStruct((T, n), BF16), jax.ShapeDtypeStruct((B_HEADS * V_HEAD, T), BF16)),
        grid=(T // tm,),
        in_specs=[
            pl.BlockSpec((tm, KV_LORA), lambda i: (i, 0)),
            pl.BlockSpec((KV_LORA, n), lambda i: (0, 0)),
            pl.BlockSpec((B_HEADS * V_HEAD, KV_LORA), lambda i: (0, 0)),
        ],
        out_specs=(pl.BlockSpec((tm, n), lambda i: (i, 0)), pl.BlockSpec((B_HEADS * V_HEAD, tm), lambda i: (0, i))),
        compiler_params=_cparams("parallel"),
        name="mla_kv_up",
    )(ckvn, wk, wvt)


def _mla_kernel(q_ref, kn_ref, kr_ref, vt_ref, o_ref, sa_ref, sb_ref, xa_ref, xb_ref, m_ref, l_ref, acc_ref, *, tq):
    qi = pl.program_id(2)
    tk = tq // 2
    q = q_ref[...]
    m_ref[...] = jnp.full(m_ref.shape, NEG, F32)
    l_ref[...] = jnp.zeros(l_ref.shape, F32)
    acc_ref[...] = jnp.zeros(acc_ref.shape, F32)

    def scores(c, s_ref, x_ref):
        start = pl.multiple_of(c * tk, tk)
        k = jnp.concatenate([kn_ref[pl.ds(start, tk), :], kr_ref[pl.ds(start, tk), :]], axis=1)
        st = jnp.dot(k, q, preferred_element_type=F32)
        s_ref[...] = st
        x_ref[...] = jnp.max(st, axis=0, keepdims=True)

    def update(c, s_ref, x_ref, masked):
        start = pl.multiple_of(c * tk, tk)
        st = s_ref[...]
        if masked:
            key = lax.broadcasted_iota(jnp.int32, st.shape, 0) + (c * tk - qi * tq)
            qry = lax.broadcasted_iota(jnp.int32, st.shape, 1)
            st = jnp.where(key <= qry, st, NEG)
            cmax = jnp.max(st, axis=0, keepdims=True)
        else:
            cmax = x_ref[...]
        m_prev = m_ref[...]
        m_new = jnp.maximum(m_prev, cmax)
        a = jnp.exp2(m_prev - m_new)
        p = jnp.exp2(st - m_new)
        l_ref[...] = a * l_ref[...] + jnp.sum(p, axis=0, keepdims=True)
        pv = jnp.dot(vt_ref[:, pl.ds(start, tk)], p.astype(BF16), preferred_element_type=F32)
        acc_ref[...] = a * acc_ref[...] + pv
        m_ref[...] = m_new

    scores(0, sa_ref, xa_ref)

    def pair(i):
        c = 2 * i
        scores(c + 1, sb_ref, xb_ref)
        update(c, sa_ref, xa_ref, False)
        scores(c + 2, sa_ref, xa_ref)
        update(c + 1, sb_ref, xb_ref, False)

    def two_pairs(i, carry):
        pair(2 * i)
        pair(2 * i + 1)
        return carry

    lax.fori_loop(0, qi // 2, two_pairs, 0)

    @pl.when(qi % 2 == 1)
    def _():
        pair(qi - 1)

    c = 2 * qi
    late = pl.ds(tk, tq - tk)
    start = pl.multiple_of((c + 1) * tk, tk)
    k = jnp.concatenate([kn_ref[pl.ds(start, tk), :], kr_ref[pl.ds(start, tk), :]], axis=1)
    sb_ref[:, late] = jnp.dot(k, q_ref[:, late], preferred_element_type=F32)
    update(c, sa_ref, xa_ref, True)
    st = sb_ref[:, late]
    key = lax.broadcasted_iota(jnp.int32, st.shape, 0)
    qry = lax.broadcasted_iota(jnp.int32, st.shape, 1)
    st = jnp.where(key <= qry, st, NEG)
    m_prev = m_ref[:, late]
    m_new = jnp.maximum(m_prev, jnp.max(st, axis=0, keepdims=True))
    a = jnp.exp2(m_prev - m_new)
    p = jnp.exp2(st - m_new)
    l_ref[:, late] = a * l_ref[:, late] + jnp.sum(p, axis=0, keepdims=True)
    pv = jnp.dot(vt_ref[:, pl.ds(start, tk)], p.astype(BF16), preferred_element_type=F32)
    acc_ref[:, late] = a * acc_ref[:, late] + pv
    o_ref[...] = (acc_ref[...] / l_ref[...]).T.astype(o_ref.dtype)


def _mla_attention(qt, kn, krope, vt, B, S):
    T = qt.shape[1]
    tq = min(ROWS_WIDE, S)
    nq = S // tq
    kern = functools.partial(_mla_kernel, tq=tq)
    return pl.pallas_call(
        kern,
        out_shape=jax.ShapeDtypeStruct((T, B_HEADS * V_HEAD), BF16),
        grid=(B, B_HEADS, nq),
        in_specs=[
            pl.BlockSpec((MLA_HEAD_PAD, tq), lambda b, h, i: (h, b * nq + i)),
            pl.BlockSpec((S, QK_NOPE), lambda b, h, i: (b, h)),
            pl.BlockSpec((S, LANE), lambda b, h, i: (b, 0)),
            pl.BlockSpec((V_HEAD, S), lambda b, h, i: (h, b)),
        ],
        out_specs=pl.BlockSpec((tq, V_HEAD), lambda b, h, i: (b * nq + i, h)),
        scratch_shapes=(
            [pltpu.VMEM((tq // 2, tq), F32)] * 2
            + [pltpu.VMEM((1, tq), F32)] * 4
            + [pltpu.VMEM((V_HEAD, tq), F32)]
        ),
        compiler_params=_cparams("parallel", "parallel", "arbitrary"),
        name="mla_attention",
    )(qt, kn, krope, vt)


def _branch_kernel(ya_ref, yb_ref, wa_ref, wb_ref, ga_ref, gb_ref, o_ref):
    pa = jnp.dot(ya_ref[...], wa_ref[...], preferred_element_type=F32)
    pb = jnp.dot(yb_ref[...], wb_ref[...], preferred_element_type=F32)
    u = ga_ref[...].astype(F32) * pa + gb_ref[...].astype(F32) * pb
    o_ref[...] = u.astype(o_ref.dtype)


def _branch_mix(ya, yb, wa, wb, proj):
    T = ya.shape[0]
    tm, tn = min(ROWS_MEDIUM, T), COL_TILE
    return pl.pallas_call(
        _branch_kernel,
        out_shape=jax.ShapeDtypeStruct((T, D_MODEL), BF16),
        grid=(T // tm, D_MODEL // tn),
        in_specs=[
            pl.BlockSpec((tm, A_WIDTH), lambda i, j: (i, 0)),
            pl.BlockSpec((tm, B_HEADS * V_HEAD), lambda i, j: (i, 0)),
            pl.BlockSpec((A_WIDTH, tn), lambda i, j: (0, j)),
            pl.BlockSpec((B_HEADS * V_HEAD, tn), lambda i, j: (0, j)),
            pl.BlockSpec((tm, tn), lambda i, j: (i, COL_GA // tn + j)),
            pl.BlockSpec((tm, tn), lambda i, j: (i, COL_GB // tn + j)),
        ],
        out_specs=pl.BlockSpec((tm, tn), lambda i, j: (i, j)),
        compiler_params=_cparams("parallel", "arbitrary"),
        name="branch_mix",
    )(ya, yb, wa, wb, proj, proj)


def _outln_kernel(u_ref, w_ref, x_ref, g_ref, b_ref, h_ref, hb_ref):
    mix = jnp.dot(u_ref[...], w_ref[...], preferred_element_type=F32)
    h = _layer_norm(ALPHA * x_ref[...] + mix, g_ref[...], b_ref[...])
    h_ref[...] = h
    hb_ref[...] = h.astype(hb_ref.dtype)


def _out_projection_ln(u, w_out, x2, g, b):
    T = u.shape[0]
    tm = min(ROWS_MEDIUM, T)
    row = pl.BlockSpec((tm, D_MODEL), lambda i: (i, 0))
    vec = pl.BlockSpec((1, D_MODEL), lambda i: (0, 0))
    return pl.pallas_call(
        _outln_kernel,
        out_shape=(
            jax.ShapeDtypeStruct((T, D_MODEL), F32),
            jax.ShapeDtypeStruct((T, D_MODEL), BF16),
        ),
        grid=(T // tm,),
        in_specs=[row, pl.BlockSpec((D_MODEL, D_MODEL), lambda i: (0, 0)), row, vec, vec],
        out_specs=(row, row),
        compiler_params=_cparams("parallel"),
        name="out_projection_ln",
    )(u, w_out, x2, g, b)


def _topk_axis0(s, ids, k):
    big = jnp.int32(2 ** 30)
    vals, idxs = [], []
    for _ in range(k):
        m = jnp.max(s, axis=0, keepdims=True)
        idx = jnp.min(jnp.where(s == m, ids, big), axis=0, keepdims=True)
        vals.append(m)
        idxs.append(idx)
        s = jnp.where(ids == idx, -jnp.inf, s)
    return vals, idxs


def _select_rows(rows, sel):
    out = jnp.zeros(sel.shape, rows[0].dtype)
    for a, r in enumerate(rows):
        out = jnp.where(sel == a, r, out)
    return out


def _route_kernel(q_ref, k1_ref, k2_ref, i1_ref, i2_ref, g_ref):
    half = PEER_QDIM // 2
    k, tm = PEER_TOPK, q_ref.shape[0]
    key_id = lax.broadcasted_iota(jnp.int32, (N_KEYS, tm), 0)
    sub = lax.broadcasted_iota(jnp.int32, (k // 2, tm), 0)
    cand_pos = jnp.concatenate([a * k + sub for a in range(k // 2)] + [k // 2 + sub, (k // 2 + sub) * k], axis=0)
    i1_all, i2_all, g_all = [], [], []
    for h in range(PEER_HEADS):
        q1 = q_ref[:, h * PEER_QDIM:h * PEER_QDIM + half]
        q2 = q_ref[:, h * PEER_QDIM + half:(h + 1) * PEER_QDIM]
        s1 = lax.dot_general(k1_ref[...], q1, _NT, preferred_element_type=F32)
        s2 = lax.dot_general(k2_ref[...], q2, _NT, preferred_element_type=F32)
        v1, i1 = _topk_axis0(s1, key_id, k)
        v2, i2 = _topk_axis0(s2, key_id, k)
        v1m = jnp.concatenate(v1, axis=0)
        v2m = jnp.concatenate(v2, axis=0)
        cand = jnp.concatenate(
            [v1[a] + v2m[:k // 2] for a in range(k // 2)] + [v1[0] + v2m[k // 2:], v1m[k // 2:] + v2[0]], axis=0)
        ts, pos = _topk_axis0(cand, cand_pos, k)
        top = jnp.concatenate(ts, axis=0)
        e = jnp.exp(top - ts[0])
        g_all.append(e / jnp.sum(e, axis=0, keepdims=True))
        posm = jnp.concatenate(pos, axis=0)
        i1_all.append(_select_rows(i1, posm >> int(math.log2(k))))
        i2_all.append(_select_rows(i2, posm & (k - 1)))
    i1_ref[...] = jnp.concatenate(i1_all, axis=0).T
    i2_ref[...] = jnp.concatenate(i2_all, axis=0).T
    g_ref[...] = jnp.concatenate(g_all, axis=0).T


def _peer_route(qp, k1, k2):
    T = qp.shape[0]
    tm = min(ROWS_NARROW, T)
    slot = pl.BlockSpec((tm, N_SLOTS), lambda i: (i, 0))
    keys = pl.BlockSpec((N_KEYS, PEER_QDIM // 2), lambda i: (0, 0))
    return pl.pallas_call(
        _route_kernel,
        out_shape=(
            jax.ShapeDtypeStruct((T, N_SLOTS), jnp.int32),
            jax.ShapeDtypeStruct((T, N_SLOTS), jnp.int32),
            jax.ShapeDtypeStruct((T, N_SLOTS), F32),
        ),
        grid=(T // tm,),
        in_specs=[pl.BlockSpec((tm, PEER_HEADS * PEER_QDIM), lambda i: (i, 0)), keys, keys],
        out_specs=(slot, slot, slot),
        compiler_params=_cparams("parallel"),
        name="peer_route",
    )(qp, k1, k2)


GATE_GROUP = 64


def _gate_matrix_kernel(i1_ref, i2_ref, g_ref, w_ref):
    key = lax.broadcasted_iota(jnp.int32, (N_KEYS, N_SLOTS), 0)

    def body(tg, carry):
        t0 = pl.multiple_of(tg * GATE_GROUP, GATE_GROUP)
        per_token = []
        for u in range(GATE_GROUP):
            r1 = i1_ref[pl.ds(t0 + u, 1), :]
            r2 = i2_ref[pl.ds(t0 + u, 1), :]
            g = g_ref[pl.ds(t0 + u, 1), :]
            a = jnp.where(key == r1, 1.0, 0.0).astype(BF16)
            b = jnp.where(key == r2, g, 0.0).astype(BF16)
            per_token.append(lax.dot_general(a, b, _NT, preferred_element_type=F32))
        w = pltpu.einshape("tid->itd", jnp.stack(per_token, axis=0))
        w_ref[:, pl.ds(t0, GATE_GROUP), :] = w.astype(w_ref.dtype)
        return carry

    lax.fori_loop(0, w_ref.shape[1] // GATE_GROUP, body, 0)


def _gate_matrix(i1, i2, g):
    T = i1.shape[0]
    tb = min(ROWS_NARROW, T)
    slot = pl.BlockSpec((tb, N_SLOTS), lambda i: (i, 0))
    return pl.pallas_call(
        _gate_matrix_kernel,
        out_shape=jax.ShapeDtypeStruct((N_KEYS, T, N_KEYS), BF16),
        grid=(T // tb,),
        in_specs=[slot, slot, slot],
        out_specs=pl.BlockSpec((N_KEYS, tb, N_KEYS), lambda i: (0, i, 0)),
        compiler_params=_cparams("parallel"),
        name="peer_gate_matrix",
    )(i1, i2, g)


EXPERT_TILE = COL_TILE


def _experts_kernel(hb_ref, dn_ref, up_ref, w_ref, y_ref, acc_ref):
    j = pl.program_id(1)

    @pl.when(j == 0)
    def _():
        acc_ref[...] = jnp.zeros(acc_ref.shape, F32)

    pre = jnp.dot(hb_ref[...], dn_ref[...], preferred_element_type=F32)
    act = 0.5 * pre * (1.0 + lax.erf(pre * (2.0 ** -0.5)))
    gates = jnp.concatenate([w_ref[m] for m in range(w_ref.shape[0])], axis=1)
    act = act * gates.astype(F32)
    acc_ref[...] += jnp.dot(act.astype(BF16), up_ref[...], preferred_element_type=F32)

    @pl.when(j == pl.num_programs(1) - 1)
    def _():
        y_ref[...] = acc_ref[...].astype(y_ref.dtype)


def _peer_experts(hb, down_tiles, up, w3):
    T = hb.shape[0]
    tm, te = min(ROWS_WIDE, T), EXPERT_TILE
    row = pl.BlockSpec((tm, D_MODEL), lambda i, j: (i, 0))
    return pl.pallas_call(
        _experts_kernel,
        out_shape=jax.ShapeDtypeStruct((T, D_MODEL), BF16),
        grid=(T // tm, N_EXPERTS // te),
        in_specs=[
            row,
            pl.BlockSpec((None, D_MODEL, te), lambda i, j: (j, 0, 0)),
            pl.BlockSpec((te, D_MODEL), lambda i, j: (j, 0)),
            pl.BlockSpec((te // N_KEYS, tm, N_KEYS), lambda i, j: (j, i, 0)),
        ],
        out_specs=row,
        scratch_shapes=[pltpu.VMEM((tm, D_MODEL), F32)],
        compiler_params=_cparams("parallel", "arbitrary"),
        name="peer_experts",
    )(hb, down_tiles, up, w3)


def _residual_ln_kernel(y_ref, h_ref, g_ref, b_ref, o_ref):
    o_ref[...] = _layer_norm(ALPHA * h_ref[...] + y_ref[...].astype(F32), g_ref[...], b_ref[...])


def _residual_ln(y, h, g, b):
    T = y.shape[0]
    tm = min(ROWS_MEDIUM, T)
    row = pl.BlockSpec((tm, D_MODEL), lambda i: (i, 0))
    vec = pl.BlockSpec((1, D_MODEL), lambda i: (0, 0))
    return pl.pallas_call(
        _residual_ln_kernel,
        out_shape=jax.ShapeDtypeStruct((T, D_MODEL), F32),
        grid=(T // tm,),
        in_specs=[row, row, vec, vec],
        out_specs=row,
        compiler_params=_cparams("parallel"),
        name="peer_residual_ln",
    )(y, h, g, b)


def _rope_tables(S):
    half = QK_ROPE // 2
    inv_freq = ROPE_THETA ** (-jnp.arange(half, dtype=F32) / half)
    ang = jnp.arange(S, dtype=jnp.int32).astype(F32)[:, None] * inv_freq[None, :]
    cos, sin = jnp.cos(ang), jnp.sin(ang)
    zeros = jnp.zeros((S, LANE - QK_ROPE), F32)
    lanes = (jnp.concatenate([cos, cos, zeros], axis=1), jnp.concatenate([-sin, sin, zeros], axis=1))
    return lanes, (cos.T, sin.T)


def _pack_input_weights(w_in, b_gates):
    wq, wk, wv, wcq, wckv, wkr, wg = jnp.split(
        w_in, (A_WIDTH, 2 * A_WIDTH, 3 * A_WIDTH, 3 * A_WIDTH + Q_LORA,
               3 * A_WIDTH + Q_LORA + KV_LORA, 3 * A_WIDTH + Q_LORA + KV_LORA + QK_ROPE), axis=1)
    pad = jnp.zeros((D_MODEL, PROJ_WIDTH - COL_KR - QK_ROPE), w_in.dtype)
    wq = wq * (A_HEAD_DIM ** -0.5)
    w_all = jnp.concatenate([wq, wk, wv, wg, wcq, wckv, wkr, pad], axis=1).astype(BF16)
    b_all = jnp.zeros((1, PROJ_WIDTH), F32).at[0, COL_GA:COL_CQ].set(b_gates)
    return w_all, b_all


def _pack_uq(w_uq):
    w = w_uq.reshape(Q_LORA, B_HEADS, QK_NOPE + QK_ROPE)
    w = jnp.pad(w, ((0, 0), (0, 0), (0, MLA_HEAD_PAD - QK_NOPE - QK_ROPE)))
    return w.reshape(Q_LORA, B_HEADS * MLA_HEAD_PAD).T.astype(BF16)


def kernel(x, w_in, b_gates, a_w_out, mla_q_norm, mla_w_uq, mla_kv_norm, mla_w_ukv, mla_w_out, w_out,
           ln1_g, ln1_b, peer_w_query, peer_sub_keys_1, peer_sub_keys_2, peer_expert_down,
           peer_expert_up, ln2_g, ln2_b):
    B, S, D = x.shape
    assert D == D_MODEL and w_in.shape[0] == DEPTH
    T = B * S
    (cos_t, sin_t), (cos_rt, sin_rt) = _rope_tables(S)
    h = x.reshape(T, D)
    for l in range(DEPTH):
        w_all, b_all = _pack_input_weights(w_in[l], b_gates[l])
        proj, qkv_s = _in_projection(h, w_all, b_all, B, S)

        (w1, d1), (w4, d4), (w16, d16) = A_PATTERNS
        assert (d1, d4, d16) == (1, 4, STREAMS)
        o1, l1 = _dilated_dense(proj, B, S, w1)
        o4, l4 = _dilated_streams(qkv_s, w4, d4)
        o16, l16 = _dilated_streams(qkv_s, w16, d16)
        ya = _combine_patterns(o1, l1, o4, l4, o16, l16, B, S)

        cqn, ckvn, krope = _latent_prep(proj, mla_q_norm[l][None], mla_kv_norm[l][None], cos_t, sin_t, S)
        qt = _q_up(cqn, _pack_uq(mla_w_uq[l]), cos_rt, sin_rt, S)
        w_ukv = mla_w_ukv[l].reshape(KV_LORA, B_HEADS, QK_NOPE + V_HEAD)
        wk = w_ukv[:, :, :QK_NOPE].reshape(KV_LORA, B_HEADS * QK_NOPE).astype(BF16)
        wvt = w_ukv[:, :, QK_NOPE:].reshape(KV_LORA, B_HEADS * V_HEAD).T.astype(BF16)
        kn, vt = _kv_up(ckvn, wk, wvt)
        yb = _mla_attention(qt, kn, krope, vt, B, S)

        u = _branch_mix(ya, yb, a_w_out[l].astype(BF16), mla_w_out[l].astype(BF16), proj)
        h1, h1b = _out_projection_ln(u, w_out[l].astype(BF16), h, ln1_g[l][None], ln1_b[l][None])

        qp = _matmul(h1b, peer_w_query[l].astype(BF16), "peer_query")
        i1, i2, g = _peer_route(qp, peer_sub_keys_1[l].astype(BF16), peer_sub_keys_2[l].astype(BF16))
        w3 = _gate_matrix(i1, i2, g)
        down_tiles = peer_expert_down[l].reshape(N_EXPERTS // EXPERT_TILE, EXPERT_TILE, D).transpose(0, 2, 1)
        yp = _peer_experts(h1b, down_tiles.astype(BF16), peer_expert_up[l].astype(BF16), w3)
        h = _residual_ln(yp, h1, ln2_g[l][None], ln2_b[l][None])
    return h.reshape(B, S, D)
```

```python
import functools
import math

import jax
import jax.numpy as jnp
from jax import lax
from jax.experimental import pallas as pl
from jax.experimental.pallas import tpu as pltpu

F32 = jnp.float32
BF16 = jnp.bfloat16

D_MODEL = 2048
A_HEADS = 16
A_HEAD_DIM = 128
A_PATTERNS = ((128, 1), (512, 4), (2048, 16))
A_BLOCK = 128
A_WIDTH = A_HEADS * A_HEAD_DIM
B_HEADS = 16
Q_LORA = 512
KV_LORA = 512
QK_NOPE = 128
QK_ROPE = 64
V_HEAD = 128
ROPE_THETA = 10000.0
N_KEYS = 128
PEER_HEADS = 8
PEER_QDIM = 256
PEER_TOPK = 16
N_EXPERTS = N_KEYS * N_KEYS
N_SLOTS = PEER_HEADS * PEER_TOPK
LN_EPS = 1e-5
RMS_EPS = 1e-6
DEPTH = 1
ALPHA = (2.0 * DEPTH) ** 0.25
NEG = -1e30

LANE = 128
MLA_HEAD_PAD = 256
VMEM_LIMIT = 56 * 1024 * 1024
ROWS_WIDE, ROWS_MEDIUM, ROWS_NARROW = 1024, 512, 256
COL_TILE = 1024

COL_Q, COL_K, COL_V = 0, A_WIDTH, 2 * A_WIDTH
COL_GA = 3 * A_WIDTH
COL_GB = COL_GA + D_MODEL
COL_CQ = COL_GB + D_MODEL
COL_CKV = COL_CQ + Q_LORA
COL_KR = COL_CKV + KV_LORA
PROJ_WIDTH = COL_CQ + 2048

_NT = (((1,), (1,)), ((), ()))


def _cparams(*sem):
    return pltpu.CompilerParams(dimension_semantics=sem, vmem_limit_bytes=VMEM_LIMIT)


def _layer_norm(z, g, b):
    mu = jnp.mean(z, axis=-1, keepdims=True)
    zc = z - mu
    var = jnp.mean(zc * zc, axis=-1, keepdims=True)
    return zc * lax.rsqrt(var + LN_EPS) * g + b


STREAMS = max(d for _, d in A_PATTERNS)
PERM_ROWS = 16 * STREAMS


MID_DILATION = A_PATTERNS[1][1]


def _stream_of_residue(r):
    return (r % MID_DILATION) * (STREAMS // MID_DILATION) + r // MID_DILATION


def _stream_permutation():
    tok = jnp.arange(PERM_ROWS, dtype=jnp.int32)
    out_row = _stream_of_residue(tok % STREAMS) * (PERM_ROWS // STREAMS) + tok // STREAMS
    return (out_row[None, :] == jnp.arange(PERM_ROWS, dtype=jnp.int32)[:, None]).astype(BF16)


def _inproj_kernel(x_ref, w_ref, b_ref, perm_ref, o_ref, qs_ref, *, qkv_hi, gate_lo, gate_hi):
    j = pl.program_id(1)
    acc = jnp.dot(x_ref[...].astype(BF16), w_ref[...], preferred_element_type=F32)
    is_gate = jnp.logical_and(j >= gate_lo, j < gate_hi)

    @pl.when(is_gate)
    def _():
        o_ref[...] = jax.nn.sigmoid(acc + b_ref[...]).astype(o_ref.dtype)

    @pl.when(jnp.logical_not(is_gate))
    def _():
        o_ref[...] = acc.astype(o_ref.dtype)

    @pl.when(j < qkv_hi)
    def _():
        rows = acc.astype(BF16)
        per = PERM_ROWS // STREAMS
        for g in range(rows.shape[0] // PERM_ROWS):
            grouped = jnp.dot(perm_ref[...], rows[g * PERM_ROWS:(g + 1) * PERM_ROWS],
                              preferred_element_type=F32).astype(qs_ref.dtype)
            for s in range(STREAMS):
                qs_ref[s, g * per:(g + 1) * per, :] = grouped[s * per:(s + 1) * per]


def _in_projection(x2, w_all, b_all, B, S):
    T = x2.shape[0]
    tm, tn = min(ROWS_WIDE, S), COL_TILE
    nt = S // tm
    qkv_hi = COL_GA // tn
    kern = functools.partial(_inproj_kernel, qkv_hi=qkv_hi, gate_lo=COL_GA // tn, gate_hi=COL_CQ // tn)
    return pl.pallas_call(
        kern,
        out_shape=(
            jax.ShapeDtypeStruct((T, PROJ_WIDTH), BF16),
            jax.ShapeDtypeStruct((B, STREAMS, S // STREAMS, 3 * A_WIDTH), BF16),
        ),
        grid=(T // tm, PROJ_WIDTH // tn),
        in_specs=[
            pl.BlockSpec((tm, D_MODEL), lambda i, j: (i, 0)),
            pl.BlockSpec((D_MODEL, tn), lambda i, j: (0, j)),
            pl.BlockSpec((1, tn), lambda i, j: (0, j)),
            pl.BlockSpec((PERM_ROWS, PERM_ROWS), lambda i, j: (0, 0)),
        ],
        out_specs=(
            pl.BlockSpec((tm, tn), lambda i, j: (i, j)),
            pl.BlockSpec((None, STREAMS, tm // STREAMS, tn),
                         lambda i, j: (i // nt, 0, i % nt, jnp.minimum(j, qkv_hi - 1))),
        ),
        compiler_params=_cparams("parallel", "arbitrary"),
        name="in_projection",
    )(x2, w_all, b_all, _stream_permutation())


def _block_pos(idx, groups):
    if groups == 1:
        return idx
    per = A_BLOCK // groups
    return groups * (idx % per) + idx // per


def _dilated_bias(dilation, steps, groups):
    blk = A_BLOCK
    i = jnp.arange(blk, dtype=jnp.int32)[:, None]
    c = jnp.arange(2 * blk, dtype=jnp.int32)[None, :]
    rel = blk + _block_pos(i, groups) - (blk * (c // blk) + _block_pos(c % blk, groups))
    band = (rel >= 0) & (rel <= steps)
    slopes = jnp.asarray([2.0 ** (-8.0 * (h + 1) / A_HEADS) for h in range(A_HEADS)], F32)
    bias = -slopes[:, None, None] * (dilation * rel).astype(F32)[None]
    first = jnp.where(band & (c >= blk), bias, NEG)
    later = jnp.where(band, bias, NEG)
    return jnp.stack([first, later], axis=0)


DILATED_SUB = 2


def _dilated_kernel(q_ref, kp_ref, ko_ref, vp_ref, vo_ref, bias_ref, o_ref, lse_ref):
    n = pl.program_id(2)
    blk = A_BLOCK
    per = kp_ref.shape[-2]
    lane = lax.broadcasted_iota(jnp.int32, (blk, LANE), 1)

    def rows(ref, u, sl):
        return ref[..., u * per:(u + 1) * per, sl].reshape(blk, A_HEAD_DIM)

    for u in range(DILATED_SUB):
        table = jnp.minimum(n, 1) if u == 0 else 1
        lse_all = jnp.zeros((blk, LANE), F32)
        for h in range(A_HEADS):
            sl = slice(h * A_HEAD_DIM, (h + 1) * A_HEAD_DIM)
            q = rows(q_ref, u, sl)
            k_prev = rows(kp_ref, 0, sl) if u == 0 else rows(ko_ref, u - 1, sl)
            v_prev = rows(vp_ref, 0, sl) if u == 0 else rows(vo_ref, u - 1, sl)
            k = jnp.concatenate([k_prev, rows(ko_ref, u, sl)], axis=0)
            v = jnp.concatenate([v_prev, rows(vo_ref, u, sl)], axis=0)
            logits = lax.dot_general(q, k, _NT, preferred_element_type=F32) + bias_ref[table, h]
            m = jnp.max(logits, axis=-1, keepdims=True)
            p = jnp.exp(logits - m)
            z = jnp.sum(p, axis=-1, keepdims=True)
            o = jnp.dot(p.astype(BF16), v, preferred_element_type=F32) / z
            o_ref[..., u * per:(u + 1) * per, sl] = o.astype(o_ref.dtype).reshape(kp_ref.shape[:-1] + (A_HEAD_DIM,))
            lse_all = jnp.where(lane == h, m + jnp.log(z), lse_all)
        lse_ref[..., u * per:(u + 1) * per, :] = lse_all.reshape(kp_ref.shape[:-1] + (LANE,))


def _dilated_dense(proj, B, S, window):
    T = B * S
    sub = DILATED_SUB
    nb = S // (A_BLOCK * sub)
    prev, own = (A_BLOCK, A_WIDTH), (A_BLOCK * sub, A_WIDTH)
    before = lambda b, n: b * nb * sub + jnp.maximum(n * sub - 1, 0)
    return pl.pallas_call(
        _dilated_kernel,
        out_shape=(jax.ShapeDtypeStruct((T, A_WIDTH), BF16), jax.ShapeDtypeStruct((T, LANE), F32)),
        grid=(B, 1, nb),
        in_specs=[
            pl.BlockSpec(own, lambda b, r, n: (b * nb + n, COL_Q // A_WIDTH)),
            pl.BlockSpec(prev, lambda b, r, n: (before(b, n), COL_K // A_WIDTH)),
            pl.BlockSpec(own, lambda b, r, n: (b * nb + n, COL_K // A_WIDTH)),
            pl.BlockSpec(prev, lambda b, r, n: (before(b, n), COL_V // A_WIDTH)),
            pl.BlockSpec(own, lambda b, r, n: (b * nb + n, COL_V // A_WIDTH)),
            pl.BlockSpec((2, A_HEADS, A_BLOCK, 2 * A_BLOCK), lambda b, r, n: (0, 0, 0, 0)),
        ],
        out_specs=(
            pl.BlockSpec(own, lambda b, r, n: (b * nb + n, 0)),
            pl.BlockSpec((A_BLOCK * sub, LANE), lambda b, r, n: (b * nb + n, 0)),
        ),
        compiler_params=_cparams("parallel", "parallel", "arbitrary"),
        name="dilated_attention_d1",
    )(proj, proj, proj, proj, proj, _dilated_bias(1, window, 1))


def _dilated_streams(qkv_s, window, dilation):
    B, ns, Ls, _ = qkv_s.shape
    groups = STREAMS // dilation
    per = A_BLOCK // groups
    sub = DILATED_SUB
    assert ns == STREAMS and Ls % (per * sub) == 0
    prev, own = (None, groups, per, A_WIDTH), (None, groups, per * sub, A_WIDTH)
    before = lambda n: jnp.maximum(n * sub - 1, 0)
    return pl.pallas_call(
        _dilated_kernel,
        out_shape=(
            jax.ShapeDtypeStruct((B, STREAMS, Ls, A_WIDTH), BF16),
            jax.ShapeDtypeStruct((B, STREAMS, Ls, LANE), F32),
        ),
        grid=(B, STREAMS // groups, Ls // (per * sub)),
        in_specs=[
            pl.BlockSpec(own, lambda b, r, n: (b, r, n, 0)),
            pl.BlockSpec(prev, lambda b, r, n: (b, r, before(n), 1)),
            pl.BlockSpec(own, lambda b, r, n: (b, r, n, 1)),
            pl.BlockSpec(prev, lambda b, r, n: (b, r, before(n), 2)),
            pl.BlockSpec(own, lambda b, r, n: (b, r, n, 2)),
            pl.BlockSpec((2, A_HEADS, A_BLOCK, 2 * A_BLOCK), lambda b, r, n: (0, 0, 0, 0)),
        ],
        out_specs=(
            pl.BlockSpec(own, lambda b, r, n: (b, r, n, 0)),
            pl.BlockSpec((None, groups, per * sub, LANE), lambda b, r, n: (b, r, n, 0)),
        ),
        compiler_params=_cparams("parallel", "parallel", "arbitrary"),
        name=f"dilated_attention_d{dilation}",
    )(qkv_s, qkv_s, qkv_s, qkv_s, qkv_s, _dilated_bias(dilation, window // dilation, groups))


COMBINE_STEPS = 16


def _combine_kernel(o1_ref, o2_ref, o3_ref, l1_ref, l2_ref, l3_ref, y_ref, ob_ref, oc_ref, lb_ref, lc_ref):
    for r in range(STREAMS):
        s = _stream_of_residue(r)
        tok = pl.ds(r, COMBINE_STEPS, stride=STREAMS)
        lb_ref[tok, :] = l2_ref[s]
        lc_ref[tok, :] = l3_ref[s]
        for h in range(A_HEADS):
            sl = slice(h * A_HEAD_DIM, (h + 1) * A_HEAD_DIM)
            ob_ref[h, tok, :] = o2_ref[s, :, sl].astype(F32)
            oc_ref[h, tok, :] = o3_ref[s, :, sl].astype(F32)
    a, b, c = l1_ref[...], lb_ref[...], lc_ref[...]
    m = jnp.maximum(jnp.maximum(a, b), c)
    ea, eb, ec = jnp.exp(a - m), jnp.exp(b - m), jnp.exp(c - m)
    inv = 1.0 / (ea + eb + ec)
    wa, wb, wc = ea * inv, eb * inv, ec * inv
    for h in range(A_HEADS):
        sl = slice(h * A_HEAD_DIM, (h + 1) * A_HEAD_DIM)
        y = (wa[:, h:h + 1] * o1_ref[:, sl].astype(F32)
             + wb[:, h:h + 1] * ob_ref[h]
             + wc[:, h:h + 1] * oc_ref[h])
        y_ref[:, sl] = y.astype(y_ref.dtype)


def _combine_patterns(o1, l1, o4, l4, o16, l16, B, S):
    T = B * S
    tm = COMBINE_STEPS * STREAMS
    nt = S // tm
    tok_o = pl.BlockSpec((tm, A_WIDTH), lambda b, i: (b * nt + i, 0))
    tok_l = pl.BlockSpec((tm, LANE), lambda b, i: (b * nt + i, 0))
    str_o = pl.BlockSpec((None, STREAMS, COMBINE_STEPS, A_WIDTH), lambda b, i: (b, 0, i, 0))
    str_l = pl.BlockSpec((None, STREAMS, COMBINE_STEPS, LANE), lambda b, i: (b, 0, i, 0))
    return pl.pallas_call(
        _combine_kernel,
        out_shape=jax.ShapeDtypeStruct((T, A_WIDTH), BF16),
        grid=(B, nt),
        in_specs=[tok_o, str_o, str_o, tok_l, str_l, str_l],
        out_specs=tok_o,
        scratch_shapes=[
            pltpu.VMEM((A_HEADS, tm, A_HEAD_DIM), F32),
            pltpu.VMEM((A_HEADS, tm, A_HEAD_DIM), F32),
            pltpu.VMEM((tm, LANE), F32),
            pltpu.VMEM((tm, LANE), F32),
        ],
        compiler_params=_cparams("parallel", "parallel"),
        name="combine_patterns",
    )(o1, o4, o16, l1, l4, l16)


def _rope_lanes(t, cos, sin):
    lane = lax.broadcasted_iota(jnp.int32, t.shape, 1)
    half = QK_ROPE // 2
    rot = jnp.where(lane < half, pltpu.roll(t, LANE - half, 1), pltpu.roll(t, half, 1))
    return t * cos + rot * sin


def _rms_norm(x, g):
    ms = jnp.mean(x * x, axis=-1, keepdims=True)
    return x * lax.rsqrt(ms + RMS_EPS) * g


def _latent_kernel(cq_ref, ckv_ref, kr_ref, gq_ref, gkv_ref, cos_ref, sin_ref, cqn_ref, ckvn_ref, krope_ref):
    cqn_ref[...] = _rms_norm(cq_ref[...].astype(F32), gq_ref[...]).astype(cqn_ref.dtype)
    ckvn_ref[...] = _rms_norm(ckv_ref[...].astype(F32), gkv_ref[...]).astype(ckvn_ref.dtype)
    krope_ref[...] = _rope_lanes(kr_ref[...].astype(F32), cos_ref[...], sin_ref[...]).astype(krope_ref.dtype)


def _latent_prep(proj, gq, gkv, cos_t, sin_t, S):
    T = proj.shape[0]
    tm = min(ROWS_MEDIUM, S)
    ns = S // tm
    return pl.pallas_call(
        _latent_kernel,
        out_shape=(
            jax.ShapeDtypeStruct((T, Q_LORA), BF16),
            jax.ShapeDtypeStruct((T, KV_LORA), BF16),
            jax.ShapeDtypeStruct((T, LANE), BF16),
        ),
        grid=(T // tm,),
        in_specs=[
            pl.BlockSpec((tm, Q_LORA), lambda i: (i, COL_CQ // Q_LORA)),
            pl.BlockSpec((tm, KV_LORA), lambda i: (i, COL_CKV // KV_LORA)),
            pl.BlockSpec((tm, LANE), lambda i: (i, COL_KR // LANE)),
            pl.BlockSpec((1, Q_LORA), lambda i: (0, 0)),
            pl.BlockSpec((1, KV_LORA), lambda i: (0, 0)),
            pl.BlockSpec((tm, LANE), lambda i: (i % ns, 0)),
            pl.BlockSpec((tm, LANE), lambda i: (i % ns, 0)),
        ],
        out_specs=(
            pl.BlockSpec((tm, Q_LORA), lambda i: (i, 0)),
            pl.BlockSpec((tm, KV_LORA), lambda i: (i, 0)),
            pl.BlockSpec((tm, LANE), lambda i: (i, 0)),
        ),
        compiler_params=_cparams("parallel"),
        name="latent_prep",
    )(proj, proj, proj, gq, gkv, cos_t, sin_t)


def _qup_kernel(c_ref, wt_ref, cos_ref, sin_ref, o_ref, *, scale):
    acc = lax.dot_general(wt_ref[...], c_ref[...], _NT, preferred_element_type=F32) * scale
    cos, sin = cos_ref[...], sin_ref[...]
    half = QK_ROPE // 2
    for hb in range(acc.shape[0] // MLA_HEAD_PAD):
        lo = hb * MLA_HEAD_PAD
        r1 = acc[lo + QK_NOPE:lo + QK_NOPE + half]
        r2 = acc[lo + QK_NOPE + half:lo + QK_NOPE + QK_ROPE]
        o_ref[lo:lo + QK_NOPE] = acc[lo:lo + QK_NOPE].astype(o_ref.dtype)
        o_ref[lo + QK_NOPE:lo + QK_NOPE + half] = (r1 * cos - r2 * sin).astype(o_ref.dtype)
        o_ref[lo + QK_NOPE + half:lo + QK_NOPE + QK_ROPE] = (r2 * cos + r1 * sin).astype(o_ref.dtype)
        o_ref[lo + QK_NOPE + QK_ROPE:lo + MLA_HEAD_PAD] = acc[lo + QK_NOPE + QK_ROPE:lo + MLA_HEAD_PAD].astype(o_ref.dtype)


def _q_up(cqn, w_uq_pt, cos_rt, sin_rt, S):
    T = cqn.shape[0]
    N = w_uq_pt.shape[0]
    tm, tn = min(ROWS_WIDE, S), COL_TILE
    ns = S // tm
    half = QK_ROPE // 2
    kern = functools.partial(_qup_kernel, scale=(QK_NOPE + QK_ROPE) ** -0.5 * math.log2(math.e))
    return pl.pallas_call(
        kern,
        out_shape=jax.ShapeDtypeStruct((N, T), BF16),
        grid=(T // tm, N // tn),
        in_specs=[
            pl.BlockSpec((tm, Q_LORA), lambda i, j: (i, 0)),
            pl.BlockSpec((tn, Q_LORA), lambda i, j: (j, 0)),
            pl.BlockSpec((half, tm), lambda i, j: (0, i % ns)),
            pl.BlockSpec((half, tm), lambda i, j: (0, i % ns)),
        ],
        out_specs=pl.BlockSpec((tn, tm), lambda i, j: (j, i)),
        compiler_params=_cparams("parallel", "arbitrary"),
        name="mla_q_up",
    )(cqn, w_uq_pt, cos_rt, sin_rt)


def _mm_kernel(a_ref, w_ref, o_ref):
    o_ref[...] = jnp.dot(a_ref[...], w_ref[...], preferred_element_type=F32).astype(o_ref.dtype)


def _matmul(a, w, name, tm=ROWS_WIDE, tn=COL_TILE):
    M, K = a.shape
    N = w.shape[1]
    tm, tn = min(tm, M), min(tn, N)
    return pl.pallas_call(
        _mm_kernel,
        out_shape=jax.ShapeDtypeStruct((M, N), BF16),
        grid=(M // tm, N // tn),
        in_specs=[
            pl.BlockSpec((tm, K), lambda i, j: (i, 0)),
            pl.BlockSpec((K, tn), lambda i, j: (0, j)),
        ],
        out_specs=pl.BlockSpec((tm, tn), lambda i, j: (i, j)),
        compiler_params=_cparams("parallel", "arbitrary"),
        name=name,
    )(a, w)


VT_ROWS = V_HEAD + 16


def _kvup_kernel(c_ref, wk_ref, wvt_ref, ones_ref, kn_ref, vt_ref):
    c = c_ref[...]
    kn_ref[...] = jnp.dot(c, wk_ref[...], preferred_element_type=F32).astype(kn_ref.dtype)
    vt = lax.dot_general(wvt_ref[...], c, _NT, preferred_element_type=F32) + ones_ref[...]
    vt_ref[...] = vt.astype(vt_ref.dtype)


def _kv_up(ckvn, wk, wvt, ones_col):
    T = ckvn.shape[0]
    tm = min(ROWS_WIDE, T)
    n = B_HEADS * QK_NOPE
    nv = B_HEADS * VT_ROWS
    return pl.pallas_call(
        _kvup_kernel,
        out_shape=(jax.ShapeDtypeStruct((T, n), BF16), jax.ShapeDtypeStruct((nv, T), BF16)),
        grid=(T // tm,),
        in_specs=[
            pl.BlockSpec((tm, KV_LORA), lambda i: (i, 0)),
            pl.BlockSpec((KV_LORA, n), lambda i: (0, 0)),
            pl.BlockSpec((nv, KV_LORA), lambda i: (0, 0)),
            pl.BlockSpec((nv, 1), lambda i: (0, 0)),
        ],
        out_specs=(pl.BlockSpec((tm, n), lambda i: (i, 0)), pl.BlockSpec((nv, tm), lambda i: (0, i))),
        compiler_params=_cparams("parallel"),
        name="mla_kv_up",
    )(ckvn, wk, wvt, ones_col)


def _mla_kernel(q_ref, kn_ref, kr_ref, vt_ref, o_ref, sa_ref, sb_ref, xa_ref, xb_ref, m_ref, acc_ref, *, tq):
    qi = pl.program_id(2)
    tk = tq // 2
    q = q_ref[...]
    m_ref[...] = jnp.full(m_ref.shape, NEG, F32)
    acc_ref[...] = jnp.zeros(acc_ref.shape, F32)

    def scores(c, s_ref, x_ref):
        start = pl.multiple_of(c * tk, tk)
        k = jnp.concatenate([kn_ref[pl.ds(start, tk), :], kr_ref[pl.ds(start, tk), :]], axis=1)
        st = jnp.dot(k, q, preferred_element_type=F32)
        s_ref[...] = st
        x_ref[...] = jnp.max(st, axis=0, keepdims=True)

    def update(c, s_ref, x_ref, masked):
        start = pl.multiple_of(c * tk, tk)
        st = s_ref[...]
        if masked:
            key = lax.broadcasted_iota(jnp.int32, st.shape, 0) + (c * tk - qi * tq)
            qry = lax.broadcasted_iota(jnp.int32, st.shape, 1)
            st = jnp.where(key <= qry, st, NEG)
            cmax = jnp.max(st, axis=0, keepdims=True)
        else:
            cmax = x_ref[...]
        m_prev = m_ref[...]
        m_new = jnp.maximum(m_prev, cmax)
        a = jnp.exp2(m_prev - m_new)
        p = jnp.exp2(st - m_new)
        pv = jnp.dot(vt_ref[:, pl.ds(start, tk)], p.astype(BF16), preferred_element_type=F32)
        acc_ref[...] = a * acc_ref[...] + pv
        m_ref[...] = m_new

    scores(0, sa_ref, xa_ref)

    def pair(i):
        c = 2 * i
        scores(c + 1, sb_ref, xb_ref)
        update(c, sa_ref, xa_ref, False)
        scores(c + 2, sa_ref, xa_ref)
        update(c + 1, sb_ref, xb_ref, False)

    def two_pairs(i, carry):
        pair(2 * i)
        pair(2 * i + 1)
        return carry

    lax.fori_loop(0, qi // 2, two_pairs, 0)

    @pl.when(qi % 2 == 1)
    def _():
        pair(qi - 1)

    c = 2 * qi
    late = pl.ds(tk, tq - tk)
    start = pl.multiple_of((c + 1) * tk, tk)
    k = jnp.concatenate([kn_ref[pl.ds(start, tk), :], kr_ref[pl.ds(start, tk), :]], axis=1)
    sb_ref[:, late] = jnp.dot(k, q_ref[:, late], preferred_element_type=F32)
    update(c, sa_ref, xa_ref, True)
    st = sb_ref[:, late]
    key = lax.broadcasted_iota(jnp.int32, st.shape, 0)
    qry = lax.broadcasted_iota(jnp.int32, st.shape, 1)
    st = jnp.where(key <= qry, st, NEG)
    m_prev = m_ref[:, late]
    m_new = jnp.maximum(m_prev, jnp.max(st, axis=0, keepdims=True))
    a = jnp.exp2(m_prev - m_new)
    p = jnp.exp2(st - m_new)
    pv = jnp.dot(vt_ref[:, pl.ds(start, tk)], p.astype(BF16), preferred_element_type=F32)
    acc_ref[:, late] = a * acc_ref[:, late] + pv
    o_ref[...] = (acc_ref[:V_HEAD, :] / acc_ref[V_HEAD:V_HEAD + 1, :]).T.astype(o_ref.dtype)


def _mla_attention(qt, kn, krope, vt, B, S):
    T = qt.shape[1]
    tq = min(ROWS_WIDE, S)
    nq = S // tq
    kern = functools.partial(_mla_kernel, tq=tq)
    return pl.pallas_call(
        kern,
        out_shape=jax.ShapeDtypeStruct((T, B_HEADS * V_HEAD), BF16),
        grid=(B, B_HEADS, nq),
        in_specs=[
            pl.BlockSpec((MLA_HEAD_PAD, tq), lambda b, h, i: (h, b * nq + i)),
            pl.BlockSpec((S, QK_NOPE), lambda b, h, i: (b, h)),
            pl.BlockSpec((S, LANE), lambda b, h, i: (b, 0)),
            pl.BlockSpec((VT_ROWS, S), lambda b, h, i: (h, b)),
        ],
        out_specs=pl.BlockSpec((tq, V_HEAD), lambda b, h, i: (b * nq + i, h)),
        scratch_shapes=(
            [pltpu.VMEM((tq // 2, tq), F32)] * 2
            + [pltpu.VMEM((1, tq), F32)] * 3
            + [pltpu.VMEM((VT_ROWS, tq), F32)]
        ),
        compiler_params=_cparams("parallel", "parallel", "arbitrary"),
        name="mla_attention",
    )(qt, kn, krope, vt)


def _branch_kernel(ya_ref, yb_ref, wa_ref, wb_ref, ga_ref, gb_ref, o_ref):
    pa = jnp.dot(ya_ref[...], wa_ref[...], preferred_element_type=F32)
    pb = jnp.dot(yb_ref[...], wb_ref[...], preferred_element_type=F32)
    u = ga_ref[...].astype(F32) * pa + gb_ref[...].astype(F32) * pb
    o_ref[...] = u.astype(o_ref.dtype)


def _branch_mix(ya, yb, wa, wb, proj):
    T = ya.shape[0]
    tm, tn = min(ROWS_MEDIUM, T), COL_TILE
    return pl.pallas_call(
        _branch_kernel,
        out_shape=jax.ShapeDtypeStruct((T, D_MODEL), BF16),
        grid=(T // tm, D_MODEL // tn),
        in_specs=[
            pl.BlockSpec((tm, A_WIDTH), lambda i, j: (i, 0)),
            pl.BlockSpec((tm, B_HEADS * V_HEAD), lambda i, j: (i, 0)),
            pl.BlockSpec((A_WIDTH, tn), lambda i, j: (0, j)),
            pl.BlockSpec((B_HEADS * V_HEAD, tn), lambda i, j: (0, j)),
            pl.BlockSpec((tm, tn), lambda i, j: (i, COL_GA // tn + j)),
            pl.BlockSpec((tm, tn), lambda i, j: (i, COL_GB // tn + j)),
        ],
        out_specs=pl.BlockSpec((tm, tn), lambda i, j: (i, j)),
        compiler_params=_cparams("parallel", "arbitrary"),
        name="branch_mix",
    )(ya, yb, wa, wb, proj, proj)


def _outln_kernel(u_ref, w_ref, x_ref, g_ref, b_ref, h_ref, hb_ref):
    mix = jnp.dot(u_ref[...], w_ref[...], preferred_element_type=F32)
    h = _layer_norm(ALPHA * x_ref[...] + mix, g_ref[...], b_ref[...])
    h_ref[...] = h
    hb_ref[...] = h.astype(hb_ref.dtype)


def _out_projection_ln(u, w_out, x2, g, b):
    T = u.shape[0]
    tm = min(ROWS_MEDIUM, T)
    row = pl.BlockSpec((tm, D_MODEL), lambda i: (i, 0))
    vec = pl.BlockSpec((1, D_MODEL), lambda i: (0, 0))
    return pl.pallas_call(
        _outln_kernel,
        out_shape=(
            jax.ShapeDtypeStruct((T, D_MODEL), F32),
            jax.ShapeDtypeStruct((T, D_MODEL), BF16),
        ),
        grid=(T // tm,),
        in_specs=[row, pl.BlockSpec((D_MODEL, D_MODEL), lambda i: (0, 0)), row, vec, vec],
        out_specs=(row, row),
        compiler_params=_cparams("parallel"),
        name="out_projection_ln",
    )(u, w_out, x2, g, b)


def _topk_axis0(s, ids, k):
    big = jnp.int32(2 ** 30)
    vals, idxs = [], []
    for _ in range(k):
        m = jnp.max(s, axis=0, keepdims=True)
        idx = jnp.min(jnp.where(s == m, ids, big), axis=0, keepdims=True)
        vals.append(m)
        idxs.append(idx)
        s = jnp.where(ids == idx, -jnp.inf, s)
    return vals, idxs


def _select_rows(rows, sel):
    out = jnp.zeros(sel.shape, rows[0].dtype)
    for a, r in enumerate(rows):
        out = jnp.where(sel == a, r, out)
    return out


def _route_kernel(q_ref, k1_ref, k2_ref, i1_ref, i2_ref, g_ref):
    half = PEER_QDIM // 2
    k, tm = PEER_TOPK, q_ref.shape[0]
    key_id = lax.broadcasted_iota(jnp.int32, (N_KEYS, tm), 0)
    sub = lax.broadcasted_iota(jnp.int32, (k // 2, tm), 0)
    cand_pos = jnp.concatenate([a * k + sub for a in range(k // 2)] + [k // 2 + sub, (k // 2 + sub) * k], axis=0)
    i1_all, i2_all, g_all = [], [], []
    for h in range(PEER_HEADS):
        q1 = q_ref[:, h * PEER_QDIM:h * PEER_QDIM + half]
        q2 = q_ref[:, h * PEER_QDIM + half:(h + 1) * PEER_QDIM]
        s1 = lax.dot_general(k1_ref[...], q1, _NT, preferred_element_type=F32)
        s2 = lax.dot_general(k2_ref[...], q2, _NT, preferred_element_type=F32)
        v1, i1 = _topk_axis0(s1, key_id, k)
        v2, i2 = _topk_axis0(s2, key_id, k)
        v1m = jnp.concatenate(v1, axis=0)
        v2m = jnp.concatenate(v2, axis=0)
        cand = jnp.concatenate(
            [v1[a] + v2m[:k // 2] for a in range(k // 2)] + [v1[0] + v2m[k // 2:], v1m[k // 2:] + v2[0]], axis=0)
        ts, pos = _topk_axis0(cand, cand_pos, k)
        top = jnp.concatenate(ts, axis=0)
        e = jnp.exp(top - ts[0])
        g_all.append(e / jnp.sum(e, axis=0, keepdims=True))
        posm = jnp.concatenate(pos, axis=0)
        i1_all.append(_select_rows(i1, posm >> int(math.log2(k))))
        i2_all.append(_select_rows(i2, posm & (k - 1)))
    i1_ref[...] = jnp.concatenate(i1_all, axis=0).T
    i2_ref[...] = jnp.concatenate(i2_all, axis=0).T
    g_ref[...] = jnp.concatenate(g_all, axis=0).T


def _peer_route(qp, k1, k2):
    T = qp.shape[0]
    tm = min(ROWS_NARROW, T)
    slot = pl.BlockSpec((tm, N_SLOTS), lambda i: (i, 0))
    keys = pl.BlockSpec((N_KEYS, PEER_QDIM // 2), lambda i: (0, 0))
    return pl.pallas_call(
        _route_kernel,
        out_shape=(
            jax.ShapeDtypeStruct((T, N_SLOTS), jnp.int32),
            jax.ShapeDtypeStruct((T, N_SLOTS), jnp.int32),
            jax.ShapeDtypeStruct((T, N_SLOTS), F32),
        ),
        grid=(T // tm,),
        in_specs=[pl.BlockSpec((tm, PEER_HEADS * PEER_QDIM), lambda i: (i, 0)), keys, keys],
        out_specs=(slot, slot, slot),
        compiler_params=_cparams("parallel"),
        name="peer_route",
    )(qp, k1, k2)


GATE_GROUP = 64


def _gate_matrix_kernel(i1_ref, i2_ref, g_ref, w_ref):
    key = lax.broadcasted_iota(jnp.int32, (N_KEYS, N_SLOTS), 0)

    def body(tg, carry):
        t0 = pl.multiple_of(tg * GATE_GROUP, GATE_GROUP)
        per_token = []
        for u in range(GATE_GROUP):
            r1 = i1_ref[pl.ds(t0 + u, 1), :]
            r2 = i2_ref[pl.ds(t0 + u, 1), :]
            g = g_ref[pl.ds(t0 + u, 1), :]
            a = jnp.where(key == r1, 1.0, 0.0).astype(BF16)
            b = jnp.where(key == r2, g, 0.0).astype(BF16)
            per_token.append(lax.dot_general(a, b, _NT, preferred_element_type=F32))
        w = pltpu.einshape("tid->itd", jnp.stack(per_token, axis=0))
        w_ref[:, pl.ds(t0, GATE_GROUP), :] = w.astype(w_ref.dtype)
        return carry

    lax.fori_loop(0, w_ref.shape[1] // GATE_GROUP, body, 0)


def _gate_matrix(i1, i2, g):
    T = i1.shape[0]
    tb = min(ROWS_NARROW, T)
    slot = pl.BlockSpec((tb, N_SLOTS), lambda i: (i, 0))
    return pl.pallas_call(
        _gate_matrix_kernel,
        out_shape=jax.ShapeDtypeStruct((N_KEYS, T, N_KEYS), BF16),
        grid=(T // tb,),
        in_specs=[slot, slot, slot],
        out_specs=pl.BlockSpec((N_KEYS, tb, N_KEYS), lambda i: (0, i, 0)),
        compiler_params=_cparams("parallel"),
        name="peer_gate_matrix",
    )(i1, i2, g)


EXPERT_TILE = COL_TILE


def _experts_kernel(hb_ref, dn_ref, up_ref, w_ref, y_ref, acc_ref):
    j = pl.program_id(1)

    @pl.when(j == 0)
    def _():
        acc_ref[...] = jnp.zeros(acc_ref.shape, F32)

    pre = jnp.dot(hb_ref[...], dn_ref[...], preferred_element_type=F32)
    act = 0.5 * pre * (1.0 + lax.erf(pre * (2.0 ** -0.5)))
    gates = jnp.concatenate([w_ref[m] for m in range(w_ref.shape[0])], axis=1)
    act = act * gates.astype(F32)
    acc_ref[...] += jnp.dot(act.astype(BF16), up_ref[...], preferred_element_type=F32)

    @pl.when(j == pl.num_programs(1) - 1)
    def _():
        y_ref[...] = acc_ref[...].astype(y_ref.dtype)


def _peer_experts(hb, down_tiles, up, w3):
    T = hb.shape[0]
    tm, te = min(ROWS_WIDE, T), EXPERT_TILE
    row = pl.BlockSpec((tm, D_MODEL), lambda i, j: (i, 0))
    return pl.pallas_call(
        _experts_kernel,
        out_shape=jax.ShapeDtypeStruct((T, D_MODEL), BF16),
        grid=(T // tm, N_EXPERTS // te),
        in_specs=[
            row,
            pl.BlockSpec((None, D_MODEL, te), lambda i, j: (j, 0, 0)),
            pl.BlockSpec((te, D_MODEL), lambda i, j: (j, 0)),
            pl.BlockSpec((te // N_KEYS, tm, N_KEYS), lambda i, j: (j, i, 0)),
        ],
        out_specs=row,
        scratch_shapes=[pltpu.VMEM((tm, D_MODEL), F32)],
        compiler_params=_cparams("parallel", "arbitrary"),
        name="peer_experts",
    )(hb, down_tiles, up, w3)


def _residual_ln_kernel(y_ref, h_ref, g_ref, b_ref, o_ref):
    o_ref[...] = _layer_norm(ALPHA * h_ref[...] + y_ref[...].astype(F32), g_ref[...], b_ref[...])


def _residual_ln(y, h, g, b):
    T = y.shape[0]
    tm = min(ROWS_MEDIUM, T)
    row = pl.BlockSpec((tm, D_MODEL), lambda i: (i, 0))
    vec = pl.BlockSpec((1, D_MODEL), lambda i: (0, 0))
    return pl.pallas_call(
        _residual_ln_kernel,
        out_shape=jax.ShapeDtypeStruct((T, D_MODEL), F32),
        grid=(T // tm,),
        in_specs=[row, row, vec, vec],
        out_specs=row,
        compiler_params=_cparams("parallel"),
        name="peer_residual_ln",
    )(y, h, g, b)


def _rope_tables(S):
    half = QK_ROPE // 2
    inv_freq = ROPE_THETA ** (-jnp.arange(half, dtype=F32) / half)
    ang = jnp.arange(S, dtype=jnp.int32).astype(F32)[:, None] * inv_freq[None, :]
    cos, sin = jnp.cos(ang), jnp.sin(ang)
    zeros = jnp.zeros((S, LANE - QK_ROPE), F32)
    lanes = (jnp.concatenate([cos, cos, zeros], axis=1), jnp.concatenate([-sin, sin, zeros], axis=1))
    return lanes, (cos.T, sin.T)


def _pack_input_weights(w_in, b_gates):
    wq, wk, wv, wcq, wckv, wkr, wg = jnp.split(
        w_in, (A_WIDTH, 2 * A_WIDTH, 3 * A_WIDTH, 3 * A_WIDTH + Q_LORA,
               3 * A_WIDTH + Q_LORA + KV_LORA, 3 * A_WIDTH + Q_LORA + KV_LORA + QK_ROPE), axis=1)
    pad = jnp.zeros((D_MODEL, PROJ_WIDTH - COL_KR - QK_ROPE), w_in.dtype)
    wq = wq * (A_HEAD_DIM ** -0.5)
    w_all = jnp.concatenate([wq, wk, wv, wg, wcq, wckv, wkr, pad], axis=1).astype(BF16)
    b_all = jnp.zeros((1, PROJ_WIDTH), F32).at[0, COL_GA:COL_CQ].set(b_gates)
    return w_all, b_all


def _pack_uq(w_uq):
    w = w_uq.reshape(Q_LORA, B_HEADS, QK_NOPE + QK_ROPE)
    w = jnp.pad(w, ((0, 0), (0, 0), (0, MLA_HEAD_PAD - QK_NOPE - QK_ROPE)))
    return w.reshape(Q_LORA, B_HEADS * MLA_HEAD_PAD).T.astype(BF16)


def kernel(x, w_in, b_gates, a_w_out, mla_q_norm, mla_w_uq, mla_kv_norm, mla_w_ukv, mla_w_out, w_out,
           ln1_g, ln1_b, peer_w_query, peer_sub_keys_1, peer_sub_keys_2, peer_expert_down,
           peer_expert_up, ln2_g, ln2_b):
    B, S, D = x.shape
    assert D == D_MODEL and w_in.shape[0] == DEPTH
    T = B * S
    (cos_t, sin_t), (cos_rt, sin_rt) = _rope_tables(S)
    h = x.reshape(T, D)
    for l in range(DEPTH):
        w_all, b_all = _pack_input_weights(w_in[l], b_gates[l])
        proj, qkv_s = _in_projection(h, w_all, b_all, B, S)

        (w1, d1), (w4, d4), (w16, d16) = A_PATTERNS
        assert (d1, d4, d16) == (1, 4, STREAMS)
        o1, l1 = _dilated_dense(proj, B, S, w1)
        o4, l4 = _dilated_streams(qkv_s, w4, d4)
        o16, l16 = _dilated_streams(qkv_s, w16, d16)
        ya = _combine_patterns(o1, l1, o4, l4, o16, l16, B, S)

        cqn, ckvn, krope = _latent_prep(proj, mla_q_norm[l][None], mla_kv_norm[l][None], cos_t, sin_t, S)
        qt = _q_up(cqn, _pack_uq(mla_w_uq[l]), cos_rt, sin_rt, S)
        w_ukv = mla_w_ukv[l].reshape(KV_LORA, B_HEADS, QK_NOPE + V_HEAD)
        wk = w_ukv[:, :, :QK_NOPE].reshape(KV_LORA, B_HEADS * QK_NOPE).astype(BF16)
        wv = jnp.pad(w_ukv[:, :, QK_NOPE:], ((0, 0), (0, 0), (0, VT_ROWS - V_HEAD)))
        wvt = wv.reshape(KV_LORA, B_HEADS * VT_ROWS).T.astype(BF16)
        ones_col = (jnp.arange(B_HEADS * VT_ROWS) % VT_ROWS == V_HEAD).astype(F32)[:, None]
        kn, vt = _kv_up(ckvn, wk, wvt, ones_col)
        yb = _mla_attention(qt, kn, krope, vt, B, S)

        u = _branch_mix(ya, yb, a_w_out[l].astype(BF16), mla_w_out[l].astype(BF16), proj)
        h1, h1b = _out_projection_ln(u, w_out[l].astype(BF16), h, ln1_g[l][None], ln1_b[l][None])

        qp = _matmul(h1b, peer_w_query[l].astype(BF16), "peer_query")
        i1, i2, g = _peer_route(qp, peer_sub_keys_1[l].astype(BF16), peer_sub_keys_2[l].astype(BF16))
        w3 = _gate_matrix(i1, i2, g)
        down_tiles = peer_expert_down[l].reshape(N_EXPERTS // EXPERT_TILE, EXPERT_TILE, D).transpose(0, 2, 1)
        yp = _peer_experts(h1b, down_tiles.astype(BF16), peer_expert_up[l].astype(BF16), w3)
        h = _residual_ln(yp, h1, ln2_g[l][None], ln2_b[l][None])
    return h.reshape(B, S, D)
```

```python
import functools
import math

import jax
import jax.numpy as jnp
from jax import lax
from jax.experimental import pallas as pl
from jax.experimental.pallas import tpu as pltpu

F32 = jnp.float32
BF16 = jnp.bfloat16

D_MODEL = 2048
A_HEADS = 16
A_HEAD_DIM = 128
A_PATTERNS = ((128, 1), (512, 4), (2048, 16))
A_BLOCK = 128
A_WIDTH = A_HEADS * A_HEAD_DIM
B_HEADS = 16
Q_LORA = 512
KV_LORA = 512
QK_NOPE = 128
QK_ROPE = 64
V_HEAD = 128
ROPE_THETA = 10000.0
N_KEYS = 128
PEER_HEADS = 8
PEER_QDIM = 256
PEER_TOPK = 16
N_EXPERTS = N_KEYS * N_KEYS
N_SLOTS = PEER_HEADS * PEER_TOPK
LN_EPS = 1e-5
RMS_EPS = 1e-6
DEPTH = 1
ALPHA = (2.0 * DEPTH) ** 0.25
NEG = -1e30

LANE = 128
MLA_HEAD_PAD = 256
VMEM_LIMIT = 56 * 1024 * 1024
ROWS_WIDE, ROWS_MEDIUM, ROWS_NARROW = 1024, 512, 256
COL_TILE = 1024

COL_Q, COL_K, COL_V = 0, A_WIDTH, 2 * A_WIDTH
COL_GA = 3 * A_WIDTH
COL_GB = COL_GA + D_MODEL
COL_CQ = COL_GB + D_MODEL
COL_CKV = COL_CQ + Q_LORA
COL_KR = COL_CKV + KV_LORA
PROJ_WIDTH = COL_CQ + 2048

_NT = (((1,), (1,)), ((), ()))


def _cparams(*sem):
    return pltpu.CompilerParams(dimension_semantics=sem, vmem_limit_bytes=VMEM_LIMIT)


def _layer_norm(z, g, b):
    mu = jnp.mean(z, axis=-1, keepdims=True)
    zc = z - mu
    var = jnp.mean(zc * zc, axis=-1, keepdims=True)
    return zc * lax.rsqrt(var + LN_EPS) * g + b


STREAMS = max(d for _, d in A_PATTERNS)
PERM_ROWS = 16 * STREAMS


MID_DILATION = A_PATTERNS[1][1]


def _stream_of_residue(r):
    return (r % MID_DILATION) * (STREAMS // MID_DILATION) + r // MID_DILATION


def _stream_permutation():
    tok = jnp.arange(PERM_ROWS, dtype=jnp.int32)
    out_row = _stream_of_residue(tok % STREAMS) * (PERM_ROWS // STREAMS) + tok // STREAMS
    return (out_row[None, :] == jnp.arange(PERM_ROWS, dtype=jnp.int32)[:, None]).astype(BF16)


def _inproj_kernel(x_ref, w_ref, b_ref, perm_ref, o_ref, qs_ref, *, qkv_hi, gate_lo, gate_hi):
    j = pl.program_id(1)
    acc = jnp.dot(x_ref[...].astype(BF16), w_ref[...], preferred_element_type=F32)
    is_gate = jnp.logical_and(j >= gate_lo, j < gate_hi)

    @pl.when(is_gate)
    def _():
        o_ref[...] = jax.nn.sigmoid(acc + b_ref[...]).astype(o_ref.dtype)

    @pl.when(jnp.logical_not(is_gate))
    def _():
        o_ref[...] = acc.astype(o_ref.dtype)

    @pl.when(j < qkv_hi)
    def _():
        rows = acc.astype(BF16)
        per = PERM_ROWS // STREAMS
        for g in range(rows.shape[0] // PERM_ROWS):
            grouped = jnp.dot(perm_ref[...], rows[g * PERM_ROWS:(g + 1) * PERM_ROWS],
                              preferred_element_type=F32).astype(qs_ref.dtype)
            for s in range(STREAMS):
                qs_ref[s, g * per:(g + 1) * per, :] = grouped[s * per:(s + 1) * per]


def _in_projection(x2, w_all, b_all, B, S):
    T = x2.shape[0]
    tm, tn = min(ROWS_WIDE, S), COL_TILE
    nt = S // tm
    qkv_hi = COL_GA // tn
    kern = functools.partial(_inproj_kernel, qkv_hi=qkv_hi, gate_lo=COL_GA // tn, gate_hi=COL_CQ // tn)
    return pl.pallas_call(
        kern,
        out_shape=(
            jax.ShapeDtypeStruct((T, PROJ_WIDTH), BF16),
            jax.ShapeDtypeStruct((B, STREAMS, S // STREAMS, 3 * A_WIDTH), BF16),
        ),
        grid=(T // tm, PROJ_WIDTH // tn),
        in_specs=[
            pl.BlockSpec((tm, D_MODEL), lambda i, j: (i, 0)),
            pl.BlockSpec((D_MODEL, tn), lambda i, j: (0, j)),
            pl.BlockSpec((1, tn), lambda i, j: (0, j)),
            pl.BlockSpec((PERM_ROWS, PERM_ROWS), lambda i, j: (0, 0)),
        ],
        out_specs=(
            pl.BlockSpec((tm, tn), lambda i, j: (i, j)),
            pl.BlockSpec((None, STREAMS, tm // STREAMS, tn),
                         lambda i, j: (i // nt, 0, i % nt, jnp.minimum(j, qkv_hi - 1))),
        ),
        compiler_params=_cparams("parallel", "arbitrary"),
        name="in_projection",
    )(x2, w_all, b_all, _stream_permutation())


def _block_pos(idx, groups):
    if groups == 1:
        return idx
    per = A_BLOCK // groups
    return groups * (idx % per) + idx // per


def _dilated_bias(dilation, steps, groups):
    blk = A_BLOCK
    i = jnp.arange(blk, dtype=jnp.int32)[:, None]
    c = jnp.arange(2 * blk, dtype=jnp.int32)[None, :]
    rel = blk + _block_pos(i, groups) - (blk * (c // blk) + _block_pos(c % blk, groups))
    band = (rel >= 0) & (rel <= steps)
    slopes = jnp.asarray([2.0 ** (-8.0 * (h + 1) / A_HEADS) for h in range(A_HEADS)], F32)
    bias = -slopes[:, None, None] * (dilation * rel).astype(F32)[None]
    first = jnp.where(band & (c >= blk), bias, NEG)
    later = jnp.where(band, bias, NEG)
    return jnp.stack([first, later], axis=0)


DILATED_SUB = 2


def _dilated_kernel(q_ref, kp_ref, ko_ref, vp_ref, vo_ref, bias_ref, o_ref, lse_ref):
    n = pl.program_id(2)
    blk = A_BLOCK
    per = kp_ref.shape[-2]
    lane = lax.broadcasted_iota(jnp.int32, (blk, LANE), 1)

    def rows(ref, u, sl):
        return ref[..., u * per:(u + 1) * per, sl].reshape(blk, A_HEAD_DIM)

    for u in range(DILATED_SUB):
        table = jnp.minimum(n, 1) if u == 0 else 1
        lse_all = jnp.zeros((blk, LANE), F32)
        for h in range(A_HEADS):
            sl = slice(h * A_HEAD_DIM, (h + 1) * A_HEAD_DIM)
            q = rows(q_ref, u, sl)
            k_prev = rows(kp_ref, 0, sl) if u == 0 else rows(ko_ref, u - 1, sl)
            v_prev = rows(vp_ref, 0, sl) if u == 0 else rows(vo_ref, u - 1, sl)
            k = jnp.concatenate([k_prev, rows(ko_ref, u, sl)], axis=0)
            v = jnp.concatenate([v_prev, rows(vo_ref, u, sl)], axis=0)
            logits = lax.dot_general(q, k, _NT, preferred_element_type=F32) + bias_ref[table, h]
            m = jnp.max(logits, axis=-1, keepdims=True)
            p = jnp.exp(logits - m)
            z = jnp.sum(p, axis=-1, keepdims=True)
            o = jnp.dot(p.astype(BF16), v, preferred_element_type=F32) / z
            o_ref[..., u * per:(u + 1) * per, sl] = o.astype(o_ref.dtype).reshape(kp_ref.shape[:-1] + (A_HEAD_DIM,))
            lse_all = jnp.where(lane == h, m + jnp.log(z), lse_all)
        lse_ref[..., u * per:(u + 1) * per, :] = lse_all.reshape(kp_ref.shape[:-1] + (LANE,))


def _dilated_dense(proj, B, S, window):
    T = B * S
    sub = DILATED_SUB
    nb = S // (A_BLOCK * sub)
    prev, own = (A_BLOCK, A_WIDTH), (A_BLOCK * sub, A_WIDTH)
    before = lambda b, n: b * nb * sub + jnp.maximum(n * sub - 1, 0)
    return pl.pallas_call(
        _dilated_kernel,
        out_shape=(jax.ShapeDtypeStruct((T, A_WIDTH), BF16), jax.ShapeDtypeStruct((T, LANE), F32)),
        grid=(B, 1, nb),
        in_specs=[
            pl.BlockSpec(own, lambda b, r, n: (b * nb + n, COL_Q // A_WIDTH)),
            pl.BlockSpec(prev, lambda b, r, n: (before(b, n), COL_K // A_WIDTH)),
            pl.BlockSpec(own, lambda b, r, n: (b * nb + n, COL_K // A_WIDTH)),
            pl.BlockSpec(prev, lambda b, r, n: (before(b, n), COL_V // A_WIDTH)),
            pl.BlockSpec(own, lambda b, r, n: (b * nb + n, COL_V // A_WIDTH)),
            pl.BlockSpec((2, A_HEADS, A_BLOCK, 2 * A_BLOCK), lambda b, r, n: (0, 0, 0, 0)),
        ],
        out_specs=(
            pl.BlockSpec(own, lambda b, r, n: (b * nb + n, 0)),
            pl.BlockSpec((A_BLOCK * sub, LANE), lambda b, r, n: (b * nb + n, 0)),
        ),
        compiler_params=_cparams("parallel", "parallel", "arbitrary"),
        name="dilated_attention_d1",
    )(proj, proj, proj, proj, proj, _dilated_bias(1, window, 1))


def _dilated_streams(qkv_s, window, dilation):
    B, ns, Ls, _ = qkv_s.shape
    groups = STREAMS // dilation
    per = A_BLOCK // groups
    sub = DILATED_SUB
    assert ns == STREAMS and Ls % (per * sub) == 0
    prev, own = (None, groups, per, A_WIDTH), (None, groups, per * sub, A_WIDTH)
    before = lambda n: jnp.maximum(n * sub - 1, 0)
    return pl.pallas_call(
        _dilated_kernel,
        out_shape=(
            jax.ShapeDtypeStruct((B, STREAMS, Ls, A_WIDTH), BF16),
            jax.ShapeDtypeStruct((B, STREAMS, Ls, LANE), F32),
        ),
        grid=(B, STREAMS // groups, Ls // (per * sub)),
        in_specs=[
            pl.BlockSpec(own, lambda b, r, n: (b, r, n, 0)),
            pl.BlockSpec(prev, lambda b, r, n: (b, r, before(n), 1)),
            pl.BlockSpec(own, lambda b, r, n: (b, r, n, 1)),
            pl.BlockSpec(prev, lambda b, r, n: (b, r, before(n), 2)),
            pl.BlockSpec(own, lambda b, r, n: (b, r, n, 2)),
            pl.BlockSpec((2, A_HEADS, A_BLOCK, 2 * A_BLOCK), lambda b, r, n: (0, 0, 0, 0)),
        ],
        out_specs=(
            pl.BlockSpec(own, lambda b, r, n: (b, r, n, 0)),
            pl.BlockSpec((None, groups, per * sub, LANE), lambda b, r, n: (b, r, n, 0)),
        ),
        compiler_params=_cparams("parallel", "parallel", "arbitrary"),
        name=f"dilated_attention_d{dilation}",
    )(qkv_s, qkv_s, qkv_s, qkv_s, qkv_s, _dilated_bias(dilation, window // dilation, groups))


COMBINE_STEPS = 16


def _combine_kernel(o1_ref, o2_ref, o3_ref, l1_ref, l2_ref, l3_ref, y_ref, ob_ref, oc_ref, lb_ref, lc_ref):
    for r in range(STREAMS):
        s = _stream_of_residue(r)
        tok = pl.ds(r, COMBINE_STEPS, stride=STREAMS)
        lb_ref[tok, :] = l2_ref[s]
        lc_ref[tok, :] = l3_ref[s]
        for h in range(A_HEADS):
            sl = slice(h * A_HEAD_DIM, (h + 1) * A_HEAD_DIM)
            ob_ref[h, tok, :] = o2_ref[s, :, sl].astype(F32)
            oc_ref[h, tok, :] = o3_ref[s, :, sl].astype(F32)
    a, b, c = l1_ref[...], lb_ref[...], lc_ref[...]
    m = jnp.maximum(jnp.maximum(a, b), c)
    ea, eb, ec = jnp.exp(a - m), jnp.exp(b - m), jnp.exp(c - m)
    inv = 1.0 / (ea + eb + ec)
    wa, wb, wc = ea * inv, eb * inv, ec * inv
    for h in range(A_HEADS):
        sl = slice(h * A_HEAD_DIM, (h + 1) * A_HEAD_DIM)
        y = (wa[:, h:h + 1] * o1_ref[:, sl].astype(F32)
             + wb[:, h:h + 1] * ob_ref[h]
             + wc[:, h:h + 1] * oc_ref[h])
        y_ref[:, sl] = y.astype(y_ref.dtype)


def _combine_patterns(o1, l1, o4, l4, o16, l16, B, S):
    T = B * S
    tm = COMBINE_STEPS * STREAMS
    nt = S // tm
    tok_o = pl.BlockSpec((tm, A_WIDTH), lambda b, i: (b * nt + i, 0))
    tok_l = pl.BlockSpec((tm, LANE), lambda b, i: (b * nt + i, 0))
    str_o = pl.BlockSpec((None, STREAMS, COMBINE_STEPS, A_WIDTH), lambda b, i: (b, 0, i, 0))
    str_l = pl.BlockSpec((None, STREAMS, COMBINE_STEPS, LANE), lambda b, i: (b, 0, i, 0))
    return pl.pallas_call(
        _combine_kernel,
        out_shape=jax.ShapeDtypeStruct((T, A_WIDTH), BF16),
        grid=(B, nt),
        in_specs=[tok_o, str_o, str_o, tok_l, str_l, str_l],
        out_specs=tok_o,
        scratch_shapes=[
            pltpu.VMEM((A_HEADS, tm, A_HEAD_DIM), F32),
            pltpu.VMEM((A_HEADS, tm, A_HEAD_DIM), F32),
            pltpu.VMEM((tm, LANE), F32),
            pltpu.VMEM((tm, LANE), F32),
        ],
        compiler_params=_cparams("parallel", "parallel"),
        name="combine_patterns",
    )(o1, o4, o16, l1, l4, l16)


def _rope_lanes(t, cos, sin):
    lane = lax.broadcasted_iota(jnp.int32, t.shape, 1)
    half = QK_ROPE // 2
    rot = jnp.where(lane < half, pltpu.roll(t, LANE - half, 1), pltpu.roll(t, half, 1))
    return t * cos + rot * sin


def _rms_norm(x, g):
    ms = jnp.mean(x * x, axis=-1, keepdims=True)
    return x * lax.rsqrt(ms + RMS_EPS) * g


def _latent_kernel(cq_ref, ckv_ref, kr_ref, gq_ref, gkv_ref, cos_ref, sin_ref, cqn_ref, ckvn_ref, krope_ref):
    cqn_ref[...] = _rms_norm(cq_ref[...].astype(F32), gq_ref[...]).astype(cqn_ref.dtype)
    ckvn_ref[...] = _rms_norm(ckv_ref[...].astype(F32), gkv_ref[...]).astype(ckvn_ref.dtype)
    krope_ref[...] = _rope_lanes(kr_ref[...].astype(F32), cos_ref[...], sin_ref[...]).astype(krope_ref.dtype)


def _latent_prep(proj, gq, gkv, cos_t, sin_t, S):
    T = proj.shape[0]
    tm = min(ROWS_MEDIUM, S)
    ns = S // tm
    return pl.pallas_call(
        _latent_kernel,
        out_shape=(
            jax.ShapeDtypeStruct((T, Q_LORA), BF16),
            jax.ShapeDtypeStruct((T, KV_LORA), BF16),
            jax.ShapeDtypeStruct((T, LANE), BF16),
        ),
        grid=(T // tm,),
        in_specs=[
            pl.BlockSpec((tm, Q_LORA), lambda i: (i, COL_CQ // Q_LORA)),
            pl.BlockSpec((tm, KV_LORA), lambda i: (i, COL_CKV // KV_LORA)),
            pl.BlockSpec((tm, LANE), lambda i: (i, COL_KR // LANE)),
            pl.BlockSpec((1, Q_LORA), lambda i: (0, 0)),
            pl.BlockSpec((1, KV_LORA), lambda i: (0, 0)),
            pl.BlockSpec((tm, LANE), lambda i: (i % ns, 0)),
            pl.BlockSpec((tm, LANE), lambda i: (i % ns, 0)),
        ],
        out_specs=(
            pl.BlockSpec((tm, Q_LORA), lambda i: (i, 0)),
            pl.BlockSpec((tm, KV_LORA), lambda i: (i, 0)),
            pl.BlockSpec((tm, LANE), lambda i: (i, 0)),
        ),
        compiler_params=_cparams("parallel"),
        name="latent_prep",
    )(proj, proj, proj, gq, gkv, cos_t, sin_t)


def _qup_kernel(c_ref, wt_ref, cos_ref, sin_ref, o_ref, *, scale):
    acc = lax.dot_general(wt_ref[...], c_ref[...], _NT, preferred_element_type=F32) * scale
    cos, sin = cos_ref[...], sin_ref[...]
    half = QK_ROPE // 2
    for hb in range(acc.shape[0] // MLA_HEAD_PAD):
        lo = hb * MLA_HEAD_PAD
        r1 = acc[lo + QK_NOPE:lo + QK_NOPE + half]
        r2 = acc[lo + QK_NOPE + half:lo + QK_NOPE + QK_ROPE]
        o_ref[lo:lo + QK_NOPE] = acc[lo:lo + QK_NOPE].astype(o_ref.dtype)
        o_ref[lo + QK_NOPE:lo + QK_NOPE + half] = (r1 * cos - r2 * sin).astype(o_ref.dtype)
        o_ref[lo + QK_NOPE + half:lo + QK_NOPE + QK_ROPE] = (r2 * cos + r1 * sin).astype(o_ref.dtype)
        o_ref[lo + QK_NOPE + QK_ROPE:lo + MLA_HEAD_PAD] = acc[lo + QK_NOPE + QK_ROPE:lo + MLA_HEAD_PAD].astype(o_ref.dtype)


def _q_up(cqn, w_uq_pt, cos_rt, sin_rt, S):
    T = cqn.shape[0]
    N = w_uq_pt.shape[0]
    tm, tn = min(ROWS_WIDE, S), COL_TILE
    ns = S // tm
    half = QK_ROPE // 2
    kern = functools.partial(_qup_kernel, scale=(QK_NOPE + QK_ROPE) ** -0.5 * math.log2(math.e))
    return pl.pallas_call(
        kern,
        out_shape=jax.ShapeDtypeStruct((N, T), BF16),
        grid=(T // tm, N // tn),
        in_specs=[
            pl.BlockSpec((tm, Q_LORA), lambda i, j: (i, 0)),
            pl.BlockSpec((tn, Q_LORA), lambda i, j: (j, 0)),
            pl.BlockSpec((half, tm), lambda i, j: (0, i % ns)),
            pl.BlockSpec((half, tm), lambda i, j: (0, i % ns)),
        ],
        out_specs=pl.BlockSpec((tn, tm), lambda i, j: (j, i)),
        compiler_params=_cparams("parallel", "arbitrary"),
        name="mla_q_up",
    )(cqn, w_uq_pt, cos_rt, sin_rt)


def _mm_kernel(a_ref, w_ref, o_ref):
    o_ref[...] = jnp.dot(a_ref[...], w_ref[...], preferred_element_type=F32).astype(o_ref.dtype)


def _matmul(a, w, name, tm=ROWS_WIDE, tn=COL_TILE):
    M, K = a.shape
    N = w.shape[1]
    tm, tn = min(tm, M), min(tn, N)
    return pl.pallas_call(
        _mm_kernel,
        out_shape=jax.ShapeDtypeStruct((M, N), BF16),
        grid=(M // tm, N // tn),
        in_specs=[
            pl.BlockSpec((tm, K), lambda i, j: (i, 0)),
            pl.BlockSpec((K, tn), lambda i, j: (0, j)),
        ],
        out_specs=pl.BlockSpec((tm, tn), lambda i, j: (i, j)),
        compiler_params=_cparams("parallel", "arbitrary"),
        name=name,
    )(a, w)


VT_ROWS = V_HEAD + 16


def _kvup_kernel(c_ref, wk_ref, wvt_ref, ones_ref, kn_ref, vt_ref):
    c = c_ref[...]
    kn_ref[...] = jnp.dot(c, wk_ref[...], preferred_element_type=F32).astype(kn_ref.dtype)
    vt = lax.dot_general(wvt_ref[...], c, _NT, preferred_element_type=F32) + ones_ref[...]
    vt_ref[...] = vt.astype(vt_ref.dtype)


def _kv_up(ckvn, wk, wvt, ones_col):
    T = ckvn.shape[0]
    tm = min(ROWS_WIDE, T)
    n = B_HEADS * QK_NOPE
    nv = B_HEADS * VT_ROWS
    return pl.pallas_call(
        _kvup_kernel,
        out_shape=(jax.ShapeDtypeStruct((T, n), BF16), jax.ShapeDtypeStruct((nv, T), BF16)),
        grid=(T // tm,),
        in_specs=[
            pl.BlockSpec((tm, KV_LORA), lambda i: (i, 0)),
            pl.BlockSpec((KV_LORA, n), lambda i: (0, 0)),
            pl.BlockSpec((nv, KV_LORA), lambda i: (0, 0)),
            pl.BlockSpec((nv, 1), lambda i: (0, 0)),
        ],
        out_specs=(pl.BlockSpec((tm, n), lambda i: (i, 0)), pl.BlockSpec((nv, tm), lambda i: (0, i))),
        compiler_params=_cparams("parallel"),
        name="mla_kv_up",
    )(ckvn, wk, wvt, ones_col)


def _mla_kernel(q_ref, kn_ref, kr_ref, vt_ref, o_ref, sa_ref, sb_ref, xa_ref, xb_ref, m_ref, acc_ref, *, tq):
    qi = pl.program_id(2)
    tk = tq // 2
    q = q_ref[...]
    m_ref[...] = jnp.full(m_ref.shape, NEG, F32)
    acc_ref[...] = jnp.zeros(acc_ref.shape, F32)

    def scores(c, s_ref, x_ref):
        start = pl.multiple_of(c * tk, tk)
        k = jnp.concatenate([kn_ref[pl.ds(start, tk), :], kr_ref[pl.ds(start, tk), :]], axis=1)
        st = jnp.dot(k, q, preferred_element_type=F32)
        s_ref[...] = st
        x_ref[...] = jnp.max(st, axis=0, keepdims=True)

    def update(c, s_ref, x_ref, masked):
        start = pl.multiple_of(c * tk, tk)
        st = s_ref[...]
        if masked:
            key = lax.broadcasted_iota(jnp.int32, st.shape, 0) + (c * tk - qi * tq)
            qry = lax.broadcasted_iota(jnp.int32, st.shape, 1)
            st = jnp.where(key <= qry, st, NEG)
            cmax = jnp.max(st, axis=0, keepdims=True)
        else:
            cmax = x_ref[...]
        m_prev = m_ref[...]
        m_new = jnp.maximum(m_prev, cmax)
        a = jnp.exp2(m_prev - m_new)
        p = jnp.exp2(st - m_new)
        pv = jnp.dot(vt_ref[:, pl.ds(start, tk)], p.astype(BF16), preferred_element_type=F32)
        acc_ref[...] = a * acc_ref[...] + pv
        m_ref[...] = m_new

    scores(0, sa_ref, xa_ref)

    def pair(i):
        c = 2 * i
        scores(c + 1, sb_ref, xb_ref)
        update(c, sa_ref, xa_ref, False)
        scores(c + 2, sa_ref, xa_ref)
        update(c + 1, sb_ref, xb_ref, False)

    def two_pairs(i, carry):
        pair(2 * i)
        pair(2 * i + 1)
        return carry

    lax.fori_loop(0, qi // 2, two_pairs, 0)

    @pl.when(qi % 2 == 1)
    def _():
        pair(qi - 1)

    c = 2 * qi
    late = pl.ds(tk, tq - tk)
    start = pl.multiple_of((c + 1) * tk, tk)
    k = jnp.concatenate([kn_ref[pl.ds(start, tk), :], kr_ref[pl.ds(start, tk), :]], axis=1)
    sb_ref[:, late] = jnp.dot(k, q_ref[:, late], preferred_element_type=F32)
    update(c, sa_ref, xa_ref, True)
    st = sb_ref[:, late]
    key = lax.broadcasted_iota(jnp.int32, st.shape, 0)
    qry = lax.broadcasted_iota(jnp.int32, st.shape, 1)
    st = jnp.where(key <= qry, st, NEG)
    m_prev = m_ref[:, late]
    m_new = jnp.maximum(m_prev, jnp.max(st, axis=0, keepdims=True))
    a = jnp.exp2(m_prev - m_new)
    p = jnp.exp2(st - m_new)
    pv = jnp.dot(vt_ref[:, pl.ds(start, tk)], p.astype(BF16), preferred_element_type=F32)
    acc_ref[:, late] = a * acc_ref[:, late] + pv
    o_ref[...] = (acc_ref[:V_HEAD, :] / acc_ref[V_HEAD:V_HEAD + 1, :]).T.astype(o_ref.dtype)


def _mla_attention(qt, kn, krope, vt, B, S):
    T = qt.shape[1]
    tq = min(ROWS_WIDE, S)
    nq = S // tq
    kern = functools.partial(_mla_kernel, tq=tq)
    return pl.pallas_call(
        kern,
        out_shape=jax.ShapeDtypeStruct((T, B_HEADS * V_HEAD), BF16),
        grid=(B, B_HEADS, nq),
        in_specs=[
            pl.BlockSpec((MLA_HEAD_PAD, tq), lambda b, h, i: (h, b * nq + i)),
            pl.BlockSpec((S, QK_NOPE), lambda b, h, i: (b, h)),
            pl.BlockSpec((S, LANE), lambda b, h, i: (b, 0)),
            pl.BlockSpec((VT_ROWS, S), lambda b, h, i: (h, b)),
        ],
        out_specs=pl.BlockSpec((tq, V_HEAD), lambda b, h, i: (b * nq + i, h)),
        scratch_shapes=(
            [pltpu.VMEM((tq // 2, tq), F32)] * 2
            + [pltpu.VMEM((1, tq), F32)] * 3
            + [pltpu.VMEM((VT_ROWS, tq), F32)]
        ),
        compiler_params=_cparams("parallel", "parallel", "arbitrary"),
        name="mla_attention",
    )(qt, kn, krope, vt)


def _branch_kernel(ya_ref, yb_ref, wa_ref, wb_ref, ga_ref, gb_ref, o_ref):
    pa = jnp.dot(ya_ref[...], wa_ref[...], preferred_element_type=F32)
    pb = jnp.dot(yb_ref[...], wb_ref[...], preferred_element_type=F32)
    u = ga_ref[...].astype(F32) * pa + gb_ref[...].astype(F32) * pb
    o_ref[...] = u.astype(o_ref.dtype)


def _branch_mix(ya, yb, wa, wb, proj):
    T = ya.shape[0]
    tm, tn = min(ROWS_MEDIUM, T), COL_TILE
    return pl.pallas_call(
        _branch_kernel,
        out_shape=jax.ShapeDtypeStruct((T, D_MODEL), BF16),
        grid=(T // tm, D_MODEL // tn),
        in_specs=[
            pl.BlockSpec((tm, A_WIDTH), lambda i, j: (i, 0)),
            pl.BlockSpec((tm, B_HEADS * V_HEAD), lambda i, j: (i, 0)),
            pl.BlockSpec((A_WIDTH, tn), lambda i, j: (0, j)),
            pl.BlockSpec((B_HEADS * V_HEAD, tn), lambda i, j: (0, j)),
            pl.BlockSpec((tm, tn), lambda i, j: (i, COL_GA // tn + j)),
            pl.BlockSpec((tm, tn), lambda i, j: (i, COL_GB // tn + j)),
        ],
        out_specs=pl.BlockSpec((tm, tn), lambda i, j: (i, j)),
        compiler_params=_cparams("parallel", "arbitrary"),
        name="branch_mix",
    )(ya, yb, wa, wb, proj, proj)


def _outln_kernel(u_ref, w_ref, x_ref, g_ref, b_ref, h_ref, hb_ref):
    mix = jnp.dot(u_ref[...], w_ref[...], preferred_element_type=F32)
    h = _layer_norm(ALPHA * x_ref[...] + mix, g_ref[...], b_ref[...])
    h_ref[...] = h
    hb_ref[...] = h.astype(hb_ref.dtype)


def _out_projection_ln(u, w_out, x2, g, b):
    T = u.shape[0]
    tm = min(ROWS_MEDIUM, T)
    row = pl.BlockSpec((tm, D_MODEL), lambda i: (i, 0))
    vec = pl.BlockSpec((1, D_MODEL), lambda i: (0, 0))
    return pl.pallas_call(
        _outln_kernel,
        out_shape=(
            jax.ShapeDtypeStruct((T, D_MODEL), F32),
            jax.ShapeDtypeStruct((T, D_MODEL), BF16),
        ),
        grid=(T // tm,),
        in_specs=[row, pl.BlockSpec((D_MODEL, D_MODEL), lambda i: (0, 0)), row, vec, vec],
        out_specs=(row, row),
        compiler_params=_cparams("parallel"),
        name="out_projection_ln",
    )(u, w_out, x2, g, b)


def _topk_axis0(s, ids, k):
    big = jnp.int32(2 ** 30)
    vals, idxs = [], []
    for _ in range(k):
        m = jnp.max(s, axis=0, keepdims=True)
        idx = jnp.min(jnp.where(s == m, ids, big), axis=0, keepdims=True)
        vals.append(m)
        idxs.append(idx)
        s = jnp.where(ids == idx, -jnp.inf, s)
    return vals, idxs


def _sorting_network(n):
    def merge(lo, hi, r):
        step = r * 2
        if step < hi - lo:
            yield from merge(lo, hi, step)
            yield from merge(lo + r, hi, step)
            yield from ((i, i + r) for i in range(lo + r, hi - r, step))
        else:
            yield (lo, lo + r)

    def sort(lo, hi):
        if hi - lo >= 1:
            mid = lo + (hi - lo) // 2
            yield from sort(lo, mid)
            yield from sort(mid + 1, hi)
            yield from merge(lo, hi, 1)

    return tuple(sort(0, n - 1))


def _topk_keys(s, k):
    sub, nrow = 8, s.shape[0] // 8
    assert nrow == k
    lane_id = lax.broadcasted_iota(jnp.int32, (sub, s.shape[1]), 0)
    val = [s[sub * v:sub * (v + 1)] for v in range(nrow)]
    idx = [lane_id + sub * v for v in range(nrow)]
    for i, j in _sorting_network(nrow):
        swap = (val[j] > val[i]) | ((val[j] == val[i]) & (idx[j] < idx[i]))
        val[i], val[j] = jnp.where(swap, val[j], val[i]), jnp.where(swap, val[i], val[j])
        idx[i], idx[j] = jnp.where(swap, idx[j], idx[i]), jnp.where(swap, idx[i], idx[j])
    big = jnp.int32(2 ** 30)
    vals, idxs = [], []
    for t in range(k):
        m = jnp.max(val[0], axis=0, keepdims=True)
        win = jnp.min(jnp.where(val[0] == m, idx[0], big), axis=0, keepdims=True)
        vals.append(m)
        idxs.append(win)
        won = idx[0] == win
        for r in range(k - 1 - t):
            val[r] = jnp.where(won, val[r + 1], val[r])
            idx[r] = jnp.where(won, idx[r + 1], idx[r])
    return vals, idxs


def _select_rows(rows, sel):
    out = jnp.zeros(sel.shape, rows[0].dtype)
    for a, r in enumerate(rows):
        out = jnp.where(sel == a, r, out)
    return out


def _route_kernel(q_ref, k1_ref, k2_ref, i1_ref, i2_ref, g_ref):
    half = PEER_QDIM // 2
    k, tm = PEER_TOPK, q_ref.shape[0]
    sub = lax.broadcasted_iota(jnp.int32, (k // 2, tm), 0)
    cand_pos = jnp.concatenate([a * k + sub for a in range(k // 2)] + [k // 2 + sub, (k // 2 + sub) * k], axis=0)
    i1_all, i2_all, g_all = [], [], []
    for h in range(PEER_HEADS):
        q1 = q_ref[:, h * PEER_QDIM:h * PEER_QDIM + half]
        q2 = q_ref[:, h * PEER_QDIM + half:(h + 1) * PEER_QDIM]
        s1 = lax.dot_general(k1_ref[...], q1, _NT, preferred_element_type=F32)
        s2 = lax.dot_general(k2_ref[...], q2, _NT, preferred_element_type=F32)
        v1, i1 = _topk_keys(s1, k)
        v2, i2 = _topk_keys(s2, k)
        v1m = jnp.concatenate(v1, axis=0)
        v2m = jnp.concatenate(v2, axis=0)
        cand = jnp.concatenate(
            [v1[a] + v2m[:k // 2] for a in range(k // 2)] + [v1[0] + v2m[k // 2:], v1m[k // 2:] + v2[0]], axis=0)
        ts, pos = _topk_axis0(cand, cand_pos, k)
        top = jnp.concatenate(ts, axis=0)
        e = jnp.exp(top - ts[0])
        g_all.append(e / jnp.sum(e, axis=0, keepdims=True))
        posm = jnp.concatenate(pos, axis=0)
        i1_all.append(_select_rows(i1, posm >> int(math.log2(k))))
        i2_all.append(_select_rows(i2, posm & (k - 1)))
    i1_ref[...] = jnp.concatenate(i1_all, axis=0).T
    i2_ref[...] = jnp.concatenate(i2_all, axis=0).T
    g_ref[...] = jnp.concatenate(g_all, axis=0).T


def _peer_route(qp, k1, k2):
    T = qp.shape[0]
    tm = min(ROWS_NARROW, T)
    slot = pl.BlockSpec((tm, N_SLOTS), lambda i: (i, 0))
    keys = pl.BlockSpec((N_KEYS, PEER_QDIM // 2), lambda i: (0, 0))
    return pl.pallas_call(
        _route_kernel,
        out_shape=(
            jax.ShapeDtypeStruct((T, N_SLOTS), jnp.int32),
            jax.ShapeDtypeStruct((T, N_SLOTS), jnp.int32),
            jax.ShapeDtypeStruct((T, N_SLOTS), F32),
        ),
        grid=(T // tm,),
        in_specs=[pl.BlockSpec((tm, PEER_HEADS * PEER_QDIM), lambda i: (i, 0)), keys, keys],
        out_specs=(slot, slot, slot),
        compiler_params=_cparams("parallel"),
        name="peer_route",
    )(qp, k1, k2)


GATE_GROUP = 64


def _gate_matrix_kernel(i1_ref, i2_ref, g_ref, w_ref):
    key = lax.broadcasted_iota(jnp.int32, (N_KEYS, N_SLOTS), 0)

    def body(tg, carry):
        t0 = pl.multiple_of(tg * GATE_GROUP, GATE_GROUP)
        per_token = []
        for u in range(GATE_GROUP):
            r1 = i1_ref[pl.ds(t0 + u, 1), :]
            r2 = i2_ref[pl.ds(t0 + u, 1), :]
            g = g_ref[pl.ds(t0 + u, 1), :]
            a = jnp.where(key == r1, 1.0, 0.0).astype(BF16)
            b = jnp.where(key == r2, g, 0.0).astype(BF16)
            per_token.append(lax.dot_general(a, b, _NT, preferred_element_type=F32))
        w = pltpu.einshape("tid->itd", jnp.stack(per_token, axis=0))
        w_ref[:, pl.ds(t0, GATE_GROUP), :] = w.astype(w_ref.dtype)
        return carry

    lax.fori_loop(0, w_ref.shape[1] // GATE_GROUP, body, 0)


def _gate_matrix(i1, i2, g):
    T = i1.shape[0]
    tb = min(ROWS_NARROW, T)
    slot = pl.BlockSpec((tb, N_SLOTS), lambda i: (i, 0))
    return pl.pallas_call(
        _gate_matrix_kernel,
        out_shape=jax.ShapeDtypeStruct((N_KEYS, T, N_KEYS), BF16),
        grid=(T // tb,),
        in_specs=[slot, slot, slot],
        out_specs=pl.BlockSpec((N_KEYS, tb, N_KEYS), lambda i: (0, i, 0)),
        compiler_params=_cparams("parallel"),
        name="peer_gate_matrix",
    )(i1, i2, g)


EXPERT_TILE = COL_TILE


def _experts_kernel(hb_ref, dn_ref, up_ref, w_ref, y_ref, acc_ref):
    j = pl.program_id(1)

    @pl.when(j == 0)
    def _():
        acc_ref[...] = jnp.zeros(acc_ref.shape, F32)

    pre = jnp.dot(hb_ref[...], dn_ref[...], preferred_element_type=F32)
    act = 0.5 * pre * (1.0 + lax.erf(pre * (2.0 ** -0.5)))
    gates = jnp.concatenate([w_ref[m] for m in range(w_ref.shape[0])], axis=1)
    act = act * gates.astype(F32)
    acc_ref[...] += jnp.dot(act.astype(BF16), up_ref[...], preferred_element_type=F32)

    @pl.when(j == pl.num_programs(1) - 1)
    def _():
        y_ref[...] = acc_ref[...].astype(y_ref.dtype)


def _peer_experts(hb, down_tiles, up, w3):
    T = hb.shape[0]
    tm, te = min(ROWS_WIDE, T), EXPERT_TILE
    row = pl.BlockSpec((tm, D_MODEL), lambda i, j: (i, 0))
    return pl.pallas_call(
        _experts_kernel,
        out_shape=jax.ShapeDtypeStruct((T, D_MODEL), BF16),
        grid=(T // tm, N_EXPERTS // te),
        in_specs=[
            row,
            pl.BlockSpec((None, D_MODEL, te), lambda i, j: (j, 0, 0)),
            pl.BlockSpec((te, D_MODEL), lambda i, j: (j, 0)),
            pl.BlockSpec((te // N_KEYS, tm, N_KEYS), lambda i, j: (j, i, 0)),
        ],
        out_specs=row,
        scratch_shapes=[pltpu.VMEM((tm, D_MODEL), F32)],
        compiler_params=_cparams("parallel", "arbitrary"),
        name="peer_experts",
    )(hb, down_tiles, up, w3)


def _residual_ln_kernel(y_ref, h_ref, g_ref, b_ref, o_ref):
    o_ref[...] = _layer_norm(ALPHA * h_ref[...] + y_ref[...].astype(F32), g_ref[...], b_ref[...])


def _residual_ln(y, h, g, b):
    T = y.shape[0]
    tm = min(ROWS_MEDIUM, T)
    row = pl.BlockSpec((tm, D_MODEL), lambda i: (i, 0))
    vec = pl.BlockSpec((1, D_MODEL), lambda i: (0, 0))
    return pl.pallas_call(
        _residual_ln_kernel,
        out_shape=jax.ShapeDtypeStruct((T, D_MODEL), F32),
        grid=(T // tm,),
        in_specs=[row, row, vec, vec],
        out_specs=row,
        compiler_params=_cparams("parallel"),
        name="peer_residual_ln",
    )(y, h, g, b)


def _rope_tables(S):
    half = QK_ROPE // 2
    inv_freq = ROPE_THETA ** (-jnp.arange(half, dtype=F32) / half)
    ang = jnp.arange(S, dtype=jnp.int32).astype(F32)[:, None] * inv_freq[None, :]
    cos, sin = jnp.cos(ang), jnp.sin(ang)
    zeros = jnp.zeros((S, LANE - QK_ROPE), F32)
    lanes = (jnp.concatenate([cos, cos, zeros], axis=1), jnp.concatenate([-sin, sin, zeros], axis=1))
    return lanes, (cos.T, sin.T)


def _pack_input_weights(w_in, b_gates):
    wq, wk, wv, wcq, wckv, wkr, wg = jnp.split(
        w_in, (A_WIDTH, 2 * A_WIDTH, 3 * A_WIDTH, 3 * A_WIDTH + Q_LORA,
               3 * A_WIDTH + Q_LORA + KV_LORA, 3 * A_WIDTH + Q_LORA + KV_LORA + QK_ROPE), axis=1)
    pad = jnp.zeros((D_MODEL, PROJ_WIDTH - COL_KR - QK_ROPE), w_in.dtype)
    wq = wq * (A_HEAD_DIM ** -0.5)
    w_all = jnp.concatenate([wq, wk, wv, wg, wcq, wckv, wkr, pad], axis=1).astype(BF16)
    b_all = jnp.zeros((1, PROJ_WIDTH), F32).at[0, COL_GA:COL_CQ].set(b_gates)
    return w_all, b_all


def _pack_uq(w_uq):
    w = w_uq.reshape(Q_LORA, B_HEADS, QK_NOPE + QK_ROPE)
    w = jnp.pad(w, ((0, 0), (0, 0), (0, MLA_HEAD_PAD - QK_NOPE - QK_ROPE)))
    return w.reshape(Q_LORA, B_HEADS * MLA_HEAD_PAD).T.astype(BF16)


def kernel(x, w_in, b_gates, a_w_out, mla_q_norm, mla_w_uq, mla_kv_norm, mla_w_ukv, mla_w_out, w_out,
           ln1_g, ln1_b, peer_w_query, peer_sub_keys_1, peer_sub_keys_2, peer_expert_down,
           peer_expert_up, ln2_g, ln2_b):
    B, S, D = x.shape
    assert D == D_MODEL and w_in.shape[0] == DEPTH
    T = B * S
    (cos_t, sin_t), (cos_rt, sin_rt) = _rope_tables(S)
    h = x.reshape(T, D)
    for l in range(DEPTH):
        w_all, b_all = _pack_input_weights(w_in[l], b_gates[l])
        proj, qkv_s = _in_projection(h, w_all, b_all, B, S)

        (w1, d1), (w4, d4), (w16, d16) = A_PATTERNS
        assert (d1, d4, d16) == (1, 4, STREAMS)
        o1, l1 = _dilated_dense(proj, B, S, w1)
        o4, l4 = _dilated_streams(qkv_s, w4, d4)
        o16, l16 = _dilated_streams(qkv_s, w16, d16)
        ya = _combine_patterns(o1, l1, o4, l4, o16, l16, B, S)

        cqn, ckvn, krope = _latent_prep(proj, mla_q_norm[l][None], mla_kv_norm[l][None], cos_t, sin_t, S)
        qt = _q_up(cqn, _pack_uq(mla_w_uq[l]), cos_rt, sin_rt, S)
        w_ukv = mla_w_ukv[l].reshape(KV_LORA, B_HEADS, QK_NOPE + V_HEAD)
        wk = w_ukv[:, :, :QK_NOPE].reshape(KV_LORA, B_HEADS * QK_NOPE).astype(BF16)
        wv = jnp.pad(w_ukv[:, :, QK_NOPE:], ((0, 0), (0, 0), (0, VT_ROWS - V_HEAD)))
        wvt = wv.reshape(KV_LORA, B_HEADS * VT_ROWS).T.astype(BF16)
        ones_col = (jnp.arange(B_HEADS * VT_ROWS) % VT_ROWS == V_HEAD).astype(F32)[:, None]
        kn, vt = _kv_up(ckvn, wk, wvt, ones_col)
        yb = _mla_attention(qt, kn, krope, vt, B, S)

        u = _branch_mix(ya, yb, a_w_out[l].astype(BF16), mla_w_out[l].astype(BF16), proj)
        h1, h1b = _out_projection_ln(u, w_out[l].astype(BF16), h, ln1_g[l][None], ln1_b[l][None])

        qp = _matmul(h1b, peer_w_query[l].astype(BF16), "peer_query")
        i1, i2, g = _peer_route(qp, peer_sub_keys_1[l].astype(BF16), peer_sub_keys_2[l].astype(BF16))
        w3 = _gate_matrix(i1, i2, g)
        down_tiles = peer_expert_down[l].reshape(N_EXPERTS // EXPERT_TILE, EXPERT_TILE, D).transpose(0, 2, 1)
        yp = _peer_experts(h1b, down_tiles.astype(BF16), peer_expert_up[l].astype(BF16), w3)
        h = _residual_ln(yp, h1, ln2_g[l][None], ln2_b[l][None])
    return h.reshape(B, S, D)
```

```python
import functools
import math

import jax
import jax.numpy as jnp
from jax import lax
from jax.experimental import pallas as pl
from jax.experimental.pallas import tpu as pltpu

F32 = jnp.float32
BF16 = jnp.bfloat16

D_MODEL = 2048
A_HEADS = 16
A_HEAD_DIM = 128
A_PATTERNS = ((128, 1), (512, 4), (2048, 16))
A_BLOCK = 128
A_WIDTH = A_HEADS * A_HEAD_DIM
B_HEADS = 16
Q_LORA = 512
KV_LORA = 512
QK_NOPE = 128
QK_ROPE = 64
V_HEAD = 128
ROPE_THETA = 10000.0
N_KEYS = 128
PEER_HEADS = 8
PEER_QDIM = 256
PEER_TOPK = 16
N_EXPERTS = N_KEYS * N_KEYS
N_SLOTS = PEER_HEADS * PEER_TOPK
LN_EPS = 1e-5
RMS_EPS = 1e-6
DEPTH = 1
ALPHA = (2.0 * DEPTH) ** 0.25
NEG = -1e30

LANE = 128
MLA_HEAD_PAD = 256
VMEM_LIMIT = 56 * 1024 * 1024
ROWS_WIDE, ROWS_MEDIUM, ROWS_NARROW = 1024, 512, 256
COL_TILE = 1024

COL_Q, COL_K, COL_V = 0, A_WIDTH, 2 * A_WIDTH
COL_GA = 3 * A_WIDTH
COL_GB = COL_GA + D_MODEL
COL_CQ = COL_GB + D_MODEL
COL_CKV = COL_CQ + Q_LORA
COL_KR = COL_CKV + KV_LORA
PROJ_WIDTH = COL_CQ + 2048

_NT = (((1,), (1,)), ((), ()))


def _cparams(*sem):
    return pltpu.CompilerParams(dimension_semantics=sem, vmem_limit_bytes=VMEM_LIMIT)


def _layer_norm(z, g, b):
    mu = jnp.mean(z, axis=-1, keepdims=True)
    zc = z - mu
    var = jnp.mean(zc * zc, axis=-1, keepdims=True)
    return zc * lax.rsqrt(var + LN_EPS) * g + b


STREAMS = max(d for _, d in A_PATTERNS)
PERM_ROWS = 16 * STREAMS


MID_DILATION = A_PATTERNS[1][1]


def _stream_of_residue(r):
    return (r % MID_DILATION) * (STREAMS // MID_DILATION) + r // MID_DILATION


def _stream_permutation():
    tok = jnp.arange(PERM_ROWS, dtype=jnp.int32)
    out_row = _stream_of_residue(tok % STREAMS) * (PERM_ROWS // STREAMS) + tok // STREAMS
    return (out_row[None, :] == jnp.arange(PERM_ROWS, dtype=jnp.int32)[:, None]).astype(BF16)


def _inproj_kernel(x_ref, w_ref, b_ref, perm_ref, o_ref, qs_ref, *, qkv_hi, gate_lo, gate_hi):
    j = pl.program_id(1)
    acc = jnp.dot(x_ref[...].astype(BF16), w_ref[...], preferred_element_type=F32)
    is_gate = jnp.logical_and(j >= gate_lo, j < gate_hi)

    @pl.when(is_gate)
    def _():
        o_ref[...] = jax.nn.sigmoid(acc + b_ref[...]).astype(o_ref.dtype)

    @pl.when(jnp.logical_not(is_gate))
    def _():
        o_ref[...] = acc.astype(o_ref.dtype)

    @pl.when(j < qkv_hi)
    def _():
        rows = acc.astype(BF16)
        per = PERM_ROWS // STREAMS
        for g in range(rows.shape[0] // PERM_ROWS):
            grouped = jnp.dot(perm_ref[...], rows[g * PERM_ROWS:(g + 1) * PERM_ROWS],
                              preferred_element_type=F32).astype(qs_ref.dtype)
            for s in range(STREAMS):
                qs_ref[s, g * per:(g + 1) * per, :] = grouped[s * per:(s + 1) * per]


def _in_projection(x2, w_all, b_all, B, S):
    T = x2.shape[0]
    tm, tn = min(ROWS_WIDE, S), COL_TILE
    nt = S // tm
    qkv_hi = COL_GA // tn
    kern = functools.partial(_inproj_kernel, qkv_hi=qkv_hi, gate_lo=COL_GA // tn, gate_hi=COL_CQ // tn)
    return pl.pallas_call(
        kern,
        out_shape=(
            jax.ShapeDtypeStruct((T, PROJ_WIDTH), BF16),
            jax.ShapeDtypeStruct((B, STREAMS, S // STREAMS, 3 * A_WIDTH), BF16),
        ),
        grid=(T // tm, PROJ_WIDTH // tn),
        in_specs=[
            pl.BlockSpec((tm, D_MODEL), lambda i, j: (i, 0)),
            pl.BlockSpec((D_MODEL, tn), lambda i, j: (0, j)),
            pl.BlockSpec((1, tn), lambda i, j: (0, j)),
            pl.BlockSpec((PERM_ROWS, PERM_ROWS), lambda i, j: (0, 0)),
        ],
        out_specs=(
            pl.BlockSpec((tm, tn), lambda i, j: (i, j)),
            pl.BlockSpec((None, STREAMS, tm // STREAMS, tn),
                         lambda i, j: (i // nt, 0, i % nt, jnp.minimum(j, qkv_hi - 1))),
        ),
        compiler_params=_cparams("parallel", "arbitrary"),
        name="in_projection",
    )(x2, w_all, b_all, _stream_permutation())


def _block_pos(idx, groups):
    if groups == 1:
        return idx
    per = A_BLOCK // groups
    return groups * (idx % per) + idx // per


def _dilated_bias(dilation, steps, groups):
    blk = A_BLOCK
    i = jnp.arange(blk, dtype=jnp.int32)[:, None]
    c = jnp.arange(2 * blk, dtype=jnp.int32)[None, :]
    rel = blk + _block_pos(i, groups) - (blk * (c // blk) + _block_pos(c % blk, groups))
    band = (rel >= 0) & (rel <= steps)
    slopes = jnp.asarray([2.0 ** (-8.0 * (h + 1) / A_HEADS) for h in range(A_HEADS)], F32)
    bias = -slopes[:, None, None] * (dilation * rel).astype(F32)[None]
    first = jnp.where(band & (c >= blk), bias, NEG)
    later = jnp.where(band, bias, NEG)
    return jnp.stack([first, later], axis=0)


DILATED_SUB = 2


def _dilated_kernel(q_ref, kp_ref, ko_ref, vp_ref, vo_ref, bias_ref, o_ref, lse_ref):
    n = pl.program_id(2)
    blk = A_BLOCK
    per = kp_ref.shape[-2]
    lane = lax.broadcasted_iota(jnp.int32, (blk, LANE), 1)

    def rows(ref, u, sl):
        return ref[..., u * per:(u + 1) * per, sl].reshape(blk, A_HEAD_DIM)

    for u in range(DILATED_SUB):
        table = jnp.minimum(n, 1) if u == 0 else 1
        lse_all = jnp.zeros((blk, LANE), F32)
        for h in range(A_HEADS):
            sl = slice(h * A_HEAD_DIM, (h + 1) * A_HEAD_DIM)
            q = rows(q_ref, u, sl)
            k_prev = rows(kp_ref, 0, sl) if u == 0 else rows(ko_ref, u - 1, sl)
            v_prev = rows(vp_ref, 0, sl) if u == 0 else rows(vo_ref, u - 1, sl)
            k = jnp.concatenate([k_prev, rows(ko_ref, u, sl)], axis=0)
            v = jnp.concatenate([v_prev, rows(vo_ref, u, sl)], axis=0)
            logits = lax.dot_general(q, k, _NT, preferred_element_type=F32) + bias_ref[table, h]
            m = jnp.max(logits, axis=-1, keepdims=True)
            p = jnp.exp(logits - m)
            z = jnp.sum(p, axis=-1, keepdims=True)
            o = jnp.dot(p.astype(BF16), v, preferred_element_type=F32) / z
            o_ref[..., u * per:(u + 1) * per, sl] = o.astype(o_ref.dtype).reshape(kp_ref.shape[:-1] + (A_HEAD_DIM,))
            lse_all = jnp.where(lane == h, m + jnp.log(z), lse_all)
        lse_ref[..., u * per:(u + 1) * per, :] = lse_all.reshape(kp_ref.shape[:-1] + (LANE,))


def _dilated_dense(proj, B, S, window):
    T = B * S
    sub = DILATED_SUB
    nb = S // (A_BLOCK * sub)
    prev, own = (A_BLOCK, A_WIDTH), (A_BLOCK * sub, A_WIDTH)
    before = lambda b, n: b * nb * sub + jnp.maximum(n * sub - 1, 0)
    return pl.pallas_call(
        _dilated_kernel,
        out_shape=(jax.ShapeDtypeStruct((T, A_WIDTH), BF16), jax.ShapeDtypeStruct((T, LANE), F32)),
        grid=(B, 1, nb),
        in_specs=[
            pl.BlockSpec(own, lambda b, r, n: (b * nb + n, COL_Q // A_WIDTH)),
            pl.BlockSpec(prev, lambda b, r, n: (before(b, n), COL_K // A_WIDTH)),
            pl.BlockSpec(own, lambda b, r, n: (b * nb + n, COL_K // A_WIDTH)),
            pl.BlockSpec(prev, lambda b, r, n: (before(b, n), COL_V // A_WIDTH)),
            pl.BlockSpec(own, lambda b, r, n: (b * nb + n, COL_V // A_WIDTH)),
            pl.BlockSpec((2, A_HEADS, A_BLOCK, 2 * A_BLOCK), lambda b, r, n: (0, 0, 0, 0)),
        ],
        out_specs=(
            pl.BlockSpec(own, lambda b, r, n: (b * nb + n, 0)),
            pl.BlockSpec((A_BLOCK * sub, LANE), lambda b, r, n: (b * nb + n, 0)),
        ),
        compiler_params=_cparams("parallel", "parallel", "arbitrary"),
        name="dilated_attention_d1",
    )(proj, proj, proj, proj, proj, _dilated_bias(1, window, 1))


def _dilated_streams(qkv_s, window, dilation):
    B, ns, Ls, _ = qkv_s.shape
    groups = STREAMS // dilation
    per = A_BLOCK // groups
    sub = DILATED_SUB
    assert ns == STREAMS and Ls % (per * sub) == 0
    prev, own = (None, groups, per, A_WIDTH), (None, groups, per * sub, A_WIDTH)
    before = lambda n: jnp.maximum(n * sub - 1, 0)
    return pl.pallas_call(
        _dilated_kernel,
        out_shape=(
            jax.ShapeDtypeStruct((B, STREAMS, Ls, A_WIDTH), BF16),
            jax.ShapeDtypeStruct((B, STREAMS, Ls, LANE), F32),
        ),
        grid=(B, STREAMS // groups, Ls // (per * sub)),
        in_specs=[
            pl.BlockSpec(own, lambda b, r, n: (b, r, n, 0)),
            pl.BlockSpec(prev, lambda b, r, n: (b, r, before(n), 1)),
            pl.BlockSpec(own, lambda b, r, n: (b, r, n, 1)),
            pl.BlockSpec(prev, lambda b, r, n: (b, r, before(n), 2)),
            pl.BlockSpec(own, lambda b, r, n: (b, r, n, 2)),
            pl.BlockSpec((2, A_HEADS, A_BLOCK, 2 * A_BLOCK), lambda b, r, n: (0, 0, 0, 0)),
        ],
        out_specs=(
            pl.BlockSpec(own, lambda b, r, n: (b, r, n, 0)),
            pl.BlockSpec((None, groups, per * sub, LANE), lambda b, r, n: (b, r, n, 0)),
        ),
        compiler_params=_cparams("parallel", "parallel", "arbitrary"),
        name=f"dilated_attention_d{dilation}",
    )(qkv_s, qkv_s, qkv_s, qkv_s, qkv_s, _dilated_bias(dilation, window // dilation, groups))


COMBINE_STEPS = 16


def _combine_kernel(o1_ref, o2_ref, o3_ref, l1_ref, l2_ref, l3_ref, y_ref, ob_ref, oc_ref, lb_ref, lc_ref):
    for r in range(STREAMS):
        s = _stream_of_residue(r)
        tok = pl.ds(r, COMBINE_STEPS, stride=STREAMS)
        lb_ref[tok, :] = l2_ref[s]
        lc_ref[tok, :] = l3_ref[s]
        for h in range(A_HEADS):
            sl = slice(h * A_HEAD_DIM, (h + 1) * A_HEAD_DIM)
            ob_ref[h, tok, :] = o2_ref[s, :, sl].astype(F32)
            oc_ref[h, tok, :] = o3_ref[s, :, sl].astype(F32)
    a, b, c = l1_ref[...], lb_ref[...], lc_ref[...]
    m = jnp.maximum(jnp.maximum(a, b), c)
    ea, eb, ec = jnp.exp(a - m), jnp.exp(b - m), jnp.exp(c - m)
    inv = 1.0 / (ea + eb + ec)
    wa, wb, wc = ea * inv, eb * inv, ec * inv
    for h in range(A_HEADS):
        sl = slice(h * A_HEAD_DIM, (h + 1) * A_HEAD_DIM)
        y = (wa[:, h:h + 1] * o1_ref[:, sl].astype(F32)
             + wb[:, h:h + 1] * ob_ref[h]
             + wc[:, h:h + 1] * oc_ref[h])
        y_ref[:, sl] = y.astype(y_ref.dtype)


def _combine_patterns(o1, l1, o4, l4, o16, l16, B, S):
    T = B * S
    tm = COMBINE_STEPS * STREAMS
    nt = S // tm
    tok_o = pl.BlockSpec((tm, A_WIDTH), lambda b, i: (b * nt + i, 0))
    tok_l = pl.BlockSpec((tm, LANE), lambda b, i: (b * nt + i, 0))
    str_o = pl.BlockSpec((None, STREAMS, COMBINE_STEPS, A_WIDTH), lambda b, i: (b, 0, i, 0))
    str_l = pl.BlockSpec((None, STREAMS, COMBINE_STEPS, LANE), lambda b, i: (b, 0, i, 0))
    return pl.pallas_call(
        _combine_kernel,
        out_shape=jax.ShapeDtypeStruct((T, A_WIDTH), BF16),
        grid=(B, nt),
        in_specs=[tok_o, str_o, str_o, tok_l, str_l, str_l],
        out_specs=tok_o,
        scratch_shapes=[
            pltpu.VMEM((A_HEADS, tm, A_HEAD_DIM), F32),
            pltpu.VMEM((A_HEADS, tm, A_HEAD_DIM), F32),
            pltpu.VMEM((tm, LANE), F32),
            pltpu.VMEM((tm, LANE), F32),
        ],
        compiler_params=_cparams("parallel", "parallel"),
        name="combine_patterns",
    )(o1, o4, o16, l1, l4, l16)


def _rope_lanes(t, cos, sin):
    lane = lax.broadcasted_iota(jnp.int32, t.shape, 1)
    half = QK_ROPE // 2
    rot = jnp.where(lane < half, pltpu.roll(t, LANE - half, 1), pltpu.roll(t, half, 1))
    return t * cos + rot * sin


def _rms_norm(x, g):
    ms = jnp.mean(x * x, axis=-1, keepdims=True)
    return x * lax.rsqrt(ms + RMS_EPS) * g


def _latent_kernel(cq_ref, ckv_ref, kr_ref, gq_ref, gkv_ref, cos_ref, sin_ref, cqn_ref, ckvn_ref, krope_ref):
    cqn_ref[...] = _rms_norm(cq_ref[...].astype(F32), gq_ref[...]).astype(cqn_ref.dtype)
    ckvn_ref[...] = _rms_norm(ckv_ref[...].astype(F32), gkv_ref[...]).astype(ckvn_ref.dtype)
    krope_ref[...] = _rope_lanes(kr_ref[...].astype(F32), cos_ref[...], sin_ref[...]).astype(krope_ref.dtype)


def _latent_prep(proj, gq, gkv, cos_t, sin_t, S):
    T = proj.shape[0]
    tm = min(ROWS_MEDIUM, S)
    ns = S // tm
    return pl.pallas_call(
        _latent_kernel,
        out_shape=(
            jax.ShapeDtypeStruct((T, Q_LORA), BF16),
            jax.ShapeDtypeStruct((T, KV_LORA), BF16),
            jax.ShapeDtypeStruct((T, LANE), BF16),
        ),
        grid=(T // tm,),
        in_specs=[
            pl.BlockSpec((tm, Q_LORA), lambda i: (i, COL_CQ // Q_LORA)),
            pl.BlockSpec((tm, KV_LORA), lambda i: (i, COL_CKV // KV_LORA)),
            pl.BlockSpec((tm, LANE), lambda i: (i, COL_KR // LANE)),
            pl.BlockSpec((1, Q_LORA), lambda i: (0, 0)),
            pl.BlockSpec((1, KV_LORA), lambda i: (0, 0)),
            pl.BlockSpec((tm, LANE), lambda i: (i % ns, 0)),
            pl.BlockSpec((tm, LANE), lambda i: (i % ns, 0)),
        ],
        out_specs=(
            pl.BlockSpec((tm, Q_LORA), lambda i: (i, 0)),
            pl.BlockSpec((tm, KV_LORA), lambda i: (i, 0)),
            pl.BlockSpec((tm, LANE), lambda i: (i, 0)),
        ),
        compiler_params=_cparams("parallel"),
        name="latent_prep",
    )(proj, proj, proj, gq, gkv, cos_t, sin_t)


def _qup_kernel(c_ref, wt_ref, cos_ref, sin_ref, o_ref, *, scale):
    acc = lax.dot_general(wt_ref[...], c_ref[...], _NT, preferred_element_type=F32) * scale
    cos, sin = cos_ref[...], sin_ref[...]
    half = QK_ROPE // 2
    for hb in range(acc.shape[0] // MLA_HEAD_PAD):
        lo = hb * MLA_HEAD_PAD
        r1 = acc[lo + QK_NOPE:lo + QK_NOPE + half]
        r2 = acc[lo + QK_NOPE + half:lo + QK_NOPE + QK_ROPE]
        o_ref[lo:lo + QK_NOPE] = acc[lo:lo + QK_NOPE].astype(o_ref.dtype)
        o_ref[lo + QK_NOPE:lo + QK_NOPE + half] = (r1 * cos - r2 * sin).astype(o_ref.dtype)
        o_ref[lo + QK_NOPE + half:lo + QK_NOPE + QK_ROPE] = (r2 * cos + r1 * sin).astype(o_ref.dtype)
        o_ref[lo + QK_NOPE + QK_ROPE:lo + MLA_HEAD_PAD] = acc[lo + QK_NOPE + QK_ROPE:lo + MLA_HEAD_PAD].astype(o_ref.dtype)


def _q_up(cqn, w_uq_pt, cos_rt, sin_rt, S):
    T = cqn.shape[0]
    N = w_uq_pt.shape[0]
    tm, tn = min(ROWS_WIDE, S), COL_TILE
    ns = S // tm
    half = QK_ROPE // 2
    kern = functools.partial(_qup_kernel, scale=(QK_NOPE + QK_ROPE) ** -0.5 * math.log2(math.e))
    return pl.pallas_call(
        kern,
        out_shape=jax.ShapeDtypeStruct((N, T), BF16),
        grid=(T // tm, N // tn),
        in_specs=[
            pl.BlockSpec((tm, Q_LORA), lambda i, j: (i, 0)),
            pl.BlockSpec((tn, Q_LORA), lambda i, j: (j, 0)),
            pl.BlockSpec((half, tm), lambda i, j: (0, i % ns)),
            pl.BlockSpec((half, tm), lambda i, j: (0, i % ns)),
        ],
        out_specs=pl.BlockSpec((tn, tm), lambda i, j: (j, i)),
        compiler_params=_cparams("parallel", "arbitrary"),
        name="mla_q_up",
    )(cqn, w_uq_pt, cos_rt, sin_rt)


def _mm_kernel(a_ref, w_ref, o_ref):
    o_ref[...] = jnp.dot(a_ref[...], w_ref[...], preferred_element_type=F32).astype(o_ref.dtype)


def _matmul(a, w, name, tm=ROWS_WIDE, tn=COL_TILE):
    M, K = a.shape
    N = w.shape[1]
    tm, tn = min(tm, M), min(tn, N)
    return pl.pallas_call(
        _mm_kernel,
        out_shape=jax.ShapeDtypeStruct((M, N), BF16),
        grid=(M // tm, N // tn),
        in_specs=[
            pl.BlockSpec((tm, K), lambda i, j: (i, 0)),
            pl.BlockSpec((K, tn), lambda i, j: (0, j)),
        ],
        out_specs=pl.BlockSpec((tm, tn), lambda i, j: (i, j)),
        compiler_params=_cparams("parallel", "arbitrary"),
        name=name,
    )(a, w)


VT_ROWS = V_HEAD + 16


def _kvup_kernel(c_ref, wk_ref, wvt_ref, ones_ref, kn_ref, vt_ref):
    c = c_ref[...]
    kn_ref[...] = jnp.dot(c, wk_ref[...], preferred_element_type=F32).astype(kn_ref.dtype)
    vt = lax.dot_general(wvt_ref[...], c, _NT, preferred_element_type=F32) + ones_ref[...]
    vt_ref[...] = vt.astype(vt_ref.dtype)


def _kv_up(ckvn, wk, wvt, ones_col):
    T = ckvn.shape[0]
    tm = min(ROWS_WIDE, T)
    n = B_HEADS * QK_NOPE
    nv = B_HEADS * VT_ROWS
    return pl.pallas_call(
        _kvup_kernel,
        out_shape=(jax.ShapeDtypeStruct((T, n), BF16), jax.ShapeDtypeStruct((nv, T), BF16)),
        grid=(T // tm,),
        in_specs=[
            pl.BlockSpec((tm, KV_LORA), lambda i: (i, 0)),
            pl.BlockSpec((KV_LORA, n), lambda i: (0, 0)),
            pl.BlockSpec((nv, KV_LORA), lambda i: (0, 0)),
            pl.BlockSpec((nv, 1), lambda i: (0, 0)),
        ],
        out_specs=(pl.BlockSpec((tm, n), lambda i: (i, 0)), pl.BlockSpec((nv, tm), lambda i: (0, i))),
        compiler_params=_cparams("parallel"),
        name="mla_kv_up",
    )(ckvn, wk, wvt, ones_col)


def _mla_kernel(q_ref, kn_ref, kr_ref, vt_in_ref, o_ref, sa_ref, sb_ref, xa_ref, xb_ref, m_ref, acc_ref,
                k_ref, vt_ref, *, tq):
    qi = pl.program_id(2)
    tk = tq // 2

    @pl.when(qi == 0)
    def _():
        def stage(r, carry):
            rows = pl.ds(pl.multiple_of(r * tk, tk), tk)
            k_ref[rows, :QK_NOPE] = kn_ref[rows, :]
            k_ref[rows, QK_NOPE:] = kr_ref[rows, :]
            vt_ref[:, rows] = vt_in_ref[:, rows]
            return carry

        lax.fori_loop(0, kn_ref.shape[0] // tk, stage, 0)

    q = q_ref[...]
    m_ref[...] = jnp.full(m_ref.shape, NEG, F32)
    acc_ref[...] = jnp.zeros(acc_ref.shape, F32)

    def scores(c, s_ref, x_ref):
        start = pl.multiple_of(c * tk, tk)
        k = k_ref[pl.ds(start, tk), :]
        st = jnp.dot(k, q, preferred_element_type=F32)
        s_ref[...] = st
        x_ref[...] = jnp.max(st, axis=0, keepdims=True)

    def update(c, s_ref, x_ref, masked):
        start = pl.multiple_of(c * tk, tk)
        st = s_ref[...]
        if masked:
            key = lax.broadcasted_iota(jnp.int32, st.shape, 0) + (c * tk - qi * tq)
            qry = lax.broadcasted_iota(jnp.int32, st.shape, 1)
            st = jnp.where(key <= qry, st, NEG)
            cmax = jnp.max(st, axis=0, keepdims=True)
        else:
            cmax = x_ref[...]
        m_prev = m_ref[...]
        m_new = jnp.maximum(m_prev, cmax)
        a = jnp.exp2(m_prev - m_new)
        p = jnp.exp2(st - m_new)
        pv = jnp.dot(vt_ref[:, pl.ds(start, tk)], p.astype(BF16), preferred_element_type=F32)
        acc_ref[...] = a * acc_ref[...] + pv
        m_ref[...] = m_new

    scores(0, sa_ref, xa_ref)

    def pair(i):
        c = 2 * i
        scores(c + 1, sb_ref, xb_ref)
        update(c, sa_ref, xa_ref, False)
        scores(c + 2, sa_ref, xa_ref)
        update(c + 1, sb_ref, xb_ref, False)

    def two_pairs(i, carry):
        pair(2 * i)
        pair(2 * i + 1)
        return carry

    lax.fori_loop(0, qi // 2, two_pairs, 0)

    @pl.when(qi % 2 == 1)
    def _():
        pair(qi - 1)

    c = 2 * qi
    late = pl.ds(tk, tq - tk)
    start = pl.multiple_of((c + 1) * tk, tk)
    sb_ref[:, late] = jnp.dot(k_ref[pl.ds(start, tk), :], q_ref[:, late], preferred_element_type=F32)
    update(c, sa_ref, xa_ref, True)
    st = sb_ref[:, late]
    key = lax.broadcasted_iota(jnp.int32, st.shape, 0)
    qry = lax.broadcasted_iota(jnp.int32, st.shape, 1)
    st = jnp.where(key <= qry, st, NEG)
    m_prev = m_ref[:, late]
    m_new = jnp.maximum(m_prev, jnp.max(st, axis=0, keepdims=True))
    a = jnp.exp2(m_prev - m_new)
    p = jnp.exp2(st - m_new)
    pv = jnp.dot(vt_ref[:, pl.ds(start, tk)], p.astype(BF16), preferred_element_type=F32)
    acc_ref[:, late] = a * acc_ref[:, late] + pv
    o_ref[...] = (acc_ref[:V_HEAD, :] / acc_ref[V_HEAD:V_HEAD + 1, :]).T.astype(o_ref.dtype)


def _mla_attention(qt, kn, krope, vt, B, S):
    T = qt.shape[1]
    tq = min(ROWS_WIDE, S)
    nq = S // tq
    kern = functools.partial(_mla_kernel, tq=tq)
    return pl.pallas_call(
        kern,
        out_shape=jax.ShapeDtypeStruct((T, B_HEADS * V_HEAD), BF16),
        grid=(B, B_HEADS, nq),
        in_specs=[
            pl.BlockSpec((MLA_HEAD_PAD, tq), lambda b, h, i: (h, b * nq + i)),
            pl.BlockSpec((S, QK_NOPE), lambda b, h, i: (b, h)),
            pl.BlockSpec((S, LANE), lambda b, h, i: (b, 0)),
            pl.BlockSpec((VT_ROWS, S), lambda b, h, i: (h, b)),
        ],
        out_specs=pl.BlockSpec((tq, V_HEAD), lambda b, h, i: (b * nq + i, h)),
        scratch_shapes=(
            [pltpu.VMEM((tq // 2, tq), F32)] * 2
            + [pltpu.VMEM((1, tq), F32)] * 3
            + [pltpu.VMEM((VT_ROWS, tq), F32)]
            + [pltpu.VMEM((S, MLA_HEAD_PAD), BF16), pltpu.VMEM((VT_ROWS, S), BF16)]
        ),
        compiler_params=_cparams("parallel", "parallel", "arbitrary"),
        name="mla_attention",
    )(qt, kn, krope, vt)


def _branch_kernel(ya_ref, yb_ref, wa_ref, wb_ref, ga_ref, gb_ref, o_ref):
    pa = jnp.dot(ya_ref[...], wa_ref[...], preferred_element_type=F32)
    pb = jnp.dot(yb_ref[...], wb_ref[...], preferred_element_type=F32)
    u = ga_ref[...].astype(F32) * pa + gb_ref[...].astype(F32) * pb
    o_ref[...] = u.astype(o_ref.dtype)


def _branch_mix(ya, yb, wa, wb, proj):
    T = ya.shape[0]
    tm, tn = min(ROWS_MEDIUM, T), COL_TILE
    return pl.pallas_call(
        _branch_kernel,
        out_shape=jax.ShapeDtypeStruct((T, D_MODEL), BF16),
        grid=(T // tm, D_MODEL // tn),
        in_specs=[
            pl.BlockSpec((tm, A_WIDTH), lambda i, j: (i, 0)),
            pl.BlockSpec((tm, B_HEADS * V_HEAD), lambda i, j: (i, 0)),
            pl.BlockSpec((A_WIDTH, tn), lambda i, j: (0, j)),
            pl.BlockSpec((B_HEADS * V_HEAD, tn), lambda i, j: (0, j)),
            pl.BlockSpec((tm, tn), lambda i, j: (i, COL_GA // tn + j)),
            pl.BlockSpec((tm, tn), lambda i, j: (i, COL_GB // tn + j)),
        ],
        out_specs=pl.BlockSpec((tm, tn), lambda i, j: (i, j)),
        compiler_params=_cparams("parallel", "arbitrary"),
        name="branch_mix",
    )(ya, yb, wa, wb, proj, proj)


def _outln_kernel(u_ref, w_ref, x_ref, g_ref, b_ref, h_ref, hb_ref):
    mix = jnp.dot(u_ref[...], w_ref[...], preferred_element_type=F32)
    h = _layer_norm(ALPHA * x_ref[...] + mix, g_ref[...], b_ref[...])
    h_ref[...] = h
    hb_ref[...] = h.astype(hb_ref.dtype)


def _out_projection_ln(u, w_out, x2, g, b):
    T = u.shape[0]
    tm = min(ROWS_MEDIUM, T)
    row = pl.BlockSpec((tm, D_MODEL), lambda i: (i, 0))
    vec = pl.BlockSpec((1, D_MODEL), lambda i: (0, 0))
    return pl.pallas_call(
        _outln_kernel,
        out_shape=(
            jax.ShapeDtypeStruct((T, D_MODEL), F32),
            jax.ShapeDtypeStruct((T, D_MODEL), BF16),
        ),
        grid=(T // tm,),
        in_specs=[row, pl.BlockSpec((D_MODEL, D_MODEL), lambda i: (0, 0)), row, vec, vec],
        out_specs=(row, row),
        compiler_params=_cparams("parallel"),
        name="out_projection_ln",
    )(u, w_out, x2, g, b)


def _topk_axis0(s, ids, k):
    big = jnp.int32(2 ** 30)
    vals, idxs = [], []
    for _ in range(k):
        m = jnp.max(s, axis=0, keepdims=True)
        idx = jnp.min(jnp.where(s == m, ids, big), axis=0, keepdims=True)
        vals.append(m)
        idxs.append(idx)
        s = jnp.where(ids == idx, -jnp.inf, s)
    return vals, idxs


def _sorting_network(n):
    def merge(lo, hi, r):
        step = r * 2
        if step < hi - lo:
            yield from merge(lo, hi, step)
            yield from merge(lo + r, hi, step)
            yield from ((i, i + r) for i in range(lo + r, hi - r, step))
        else:
            yield (lo, lo + r)

    def sort(lo, hi):
        if hi - lo >= 1:
            mid = lo + (hi - lo) // 2
            yield from sort(lo, mid)
            yield from sort(mid + 1, hi)
            yield from merge(lo, hi, 1)

    return tuple(sort(0, n - 1))


def _topk_keys(s, k):
    sub, nrow = 8, s.shape[0] // 8
    assert nrow == k
    lane_id = lax.broadcasted_iota(jnp.int32, (sub, s.shape[1]), 0)
    val = [s[sub * v:sub * (v + 1)] for v in range(nrow)]
    idx = [lane_id + sub * v for v in range(nrow)]
    for i, j in _sorting_network(nrow):
        swap = (val[j] > val[i]) | ((val[j] == val[i]) & (idx[j] < idx[i]))
        val[i], val[j] = jnp.where(swap, val[j], val[i]), jnp.where(swap, val[i], val[j])
        idx[i], idx[j] = jnp.where(swap, idx[j], idx[i]), jnp.where(swap, idx[i], idx[j])
    big = jnp.int32(2 ** 30)
    vals, idxs = [], []
    for t in range(k):
        m = jnp.max(val[0], axis=0, keepdims=True)
        win = jnp.min(jnp.where(val[0] == m, idx[0], big), axis=0, keepdims=True)
        vals.append(m)
        idxs.append(win)
        won = idx[0] == win
        for r in range(k - 1 - t):
            val[r] = jnp.where(won, val[r + 1], val[r])
            idx[r] = jnp.where(won, idx[r + 1], idx[r])
    return vals, idxs


def _select_rows(rows, sel):
    out = jnp.zeros(sel.shape, rows[0].dtype)
    for a, r in enumerate(rows):
        out = jnp.where(sel == a, r, out)
    return out


def _route_kernel(q_ref, k1_ref, k2_ref, i1_ref, i2_ref, g_ref):
    half = PEER_QDIM // 2
    k, tm = PEER_TOPK, q_ref.shape[0]
    sub = lax.broadcasted_iota(jnp.int32, (k // 2, tm), 0)
    cand_pos = jnp.concatenate([a * k + sub for a in range(k // 2)] + [k // 2 + sub, (k // 2 + sub) * k], axis=0)
    i1_all, i2_all, g_all = [], [], []
    for h in range(PEER_HEADS):
        q1 = q_ref[:, h * PEER_QDIM:h * PEER_QDIM + half]
        q2 = q_ref[:, h * PEER_QDIM + half:(h + 1) * PEER_QDIM]
        s1 = lax.dot_general(k1_ref[...], q1, _NT, preferred_element_type=F32)
        s2 = lax.dot_general(k2_ref[...], q2, _NT, preferred_element_type=F32)
        v1, i1 = _topk_keys(s1, k)
        v2, i2 = _topk_keys(s2, k)
        v1m = jnp.concatenate(v1, axis=0)
        v2m = jnp.concatenate(v2, axis=0)
        cand = jnp.concatenate(
            [v1[a] + v2m[:k // 2] for a in range(k // 2)] + [v1[0] + v2m[k // 2:], v1m[k // 2:] + v2[0]], axis=0)
        ts, pos = _topk_axis0(cand, cand_pos, k)
        top = jnp.concatenate(ts, axis=0)
        e = jnp.exp(top - ts[0])
        g_all.append(e / jnp.sum(e, axis=0, keepdims=True))
        posm = jnp.concatenate(pos, axis=0)
        i1_all.append(_select_rows(i1, posm >> int(math.log2(k))))
        i2_all.append(_select_rows(i2, posm & (k - 1)))
    i1_ref[...] = jnp.concatenate(i1_all, axis=0).T
    i2_ref[...] = jnp.concatenate(i2_all, axis=0).T
    g_ref[...] = jnp.concatenate(g_all, axis=0).T


def _peer_route(qp, k1, k2):
    T = qp.shape[0]
    tm = min(ROWS_NARROW, T)
    slot = pl.BlockSpec((tm, N_SLOTS), lambda i: (i, 0))
    keys = pl.BlockSpec((N_KEYS, PEER_QDIM // 2), lambda i: (0, 0))
    return pl.pallas_call(
        _route_kernel,
        out_shape=(
            jax.ShapeDtypeStruct((T, N_SLOTS), jnp.int32),
            jax.ShapeDtypeStruct((T, N_SLOTS), jnp.int32),
            jax.ShapeDtypeStruct((T, N_SLOTS), F32),
        ),
        grid=(T // tm,),
        in_specs=[pl.BlockSpec((tm, PEER_HEADS * PEER_QDIM), lambda i: (i, 0)), keys, keys],
        out_specs=(slot, slot, slot),
        compiler_params=_cparams("parallel"),
        name="peer_route",
    )(qp, k1, k2)


GATE_GROUP = 64


def _gate_matrix_kernel(i1_ref, i2_ref, g_ref, w_ref):
    key = lax.broadcasted_iota(jnp.int32, (N_KEYS, N_SLOTS), 0)

    def body(tg, carry):
        t0 = pl.multiple_of(tg * GATE_GROUP, GATE_GROUP)
        per_token = []
        for u in range(GATE_GROUP):
            r1 = i1_ref[pl.ds(t0 + u, 1), :]
            r2 = i2_ref[pl.ds(t0 + u, 1), :]
            g = g_ref[pl.ds(t0 + u, 1), :]
            a = jnp.where(key == r1, 1.0, 0.0).astype(BF16)
            b = jnp.where(key == r2, g, 0.0).astype(BF16)
            per_token.append(lax.dot_general(a, b, _NT, preferred_element_type=F32))
        w = pltpu.einshape("tid->itd", jnp.stack(per_token, axis=0))
        w_ref[:, pl.ds(t0, GATE_GROUP), :] = w.astype(w_ref.dtype)
        return carry

    lax.fori_loop(0, w_ref.shape[1] // GATE_GROUP, body, 0)


def _gate_matrix(i1, i2, g):
    T = i1.shape[0]
    tb = min(ROWS_NARROW, T)
    slot = pl.BlockSpec((tb, N_SLOTS), lambda i: (i, 0))
    return pl.pallas_call(
        _gate_matrix_kernel,
        out_shape=jax.ShapeDtypeStruct((N_KEYS, T, N_KEYS), BF16),
        grid=(T // tb,),
        in_specs=[slot, slot, slot],
        out_specs=pl.BlockSpec((N_KEYS, tb, N_KEYS), lambda i: (0, i, 0)),
        compiler_params=_cparams("parallel"),
        name="peer_gate_matrix",
    )(i1, i2, g)


EXPERT_TILE = COL_TILE


def _experts_kernel(hb_ref, dn_ref, up_ref, w_ref, y_ref, acc_ref):
    j = pl.program_id(1)

    @pl.when(j == 0)
    def _():
        acc_ref[...] = jnp.zeros(acc_ref.shape, F32)

    pre = jnp.dot(hb_ref[...], dn_ref[...], preferred_element_type=F32)
    act = 0.5 * pre * (1.0 + lax.erf(pre * (2.0 ** -0.5)))
    gates = jnp.concatenate([w_ref[m] for m in range(w_ref.shape[0])], axis=1)
    act = act * gates.astype(F32)
    acc_ref[...] += jnp.dot(act.astype(BF16), up_ref[...], preferred_element_type=F32)

    @pl.when(j == pl.num_programs(1) - 1)
    def _():
        y_ref[...] = acc_ref[...].astype(y_ref.dtype)


def _peer_experts(hb, down_tiles, up, w3):
    T = hb.shape[0]
    tm, te = min(ROWS_WIDE, T), EXPERT_TILE
    row = pl.BlockSpec((tm, D_MODEL), lambda i, j: (i, 0))
    return pl.pallas_call(
        _experts_kernel,
        out_shape=jax.ShapeDtypeStruct((T, D_MODEL), BF16),
        grid=(T // tm, N_EXPERTS // te),
        in_specs=[
            row,
            pl.BlockSpec((None, D_MODEL, te), lambda i, j: (j, 0, 0)),
            pl.BlockSpec((te, D_MODEL), lambda i, j: (j, 0)),
            pl.BlockSpec((te // N_KEYS, tm, N_KEYS), lambda i, j: (j, i, 0)),
        ],
        out_specs=row,
        scratch_shapes=[pltpu.VMEM((tm, D_MODEL), F32)],
        compiler_params=_cparams("parallel", "arbitrary"),
        name="peer_experts",
    )(hb, down_tiles, up, w3)


def _residual_ln_kernel(y_ref, h_ref, g_ref, b_ref, o_ref):
    o_ref[...] = _layer_norm(ALPHA * h_ref[...] + y_ref[...].astype(F32), g_ref[...], b_ref[...])


def _residual_ln(y, h, g, b):
    T = y.shape[0]
    tm = min(ROWS_MEDIUM, T)
    row = pl.BlockSpec((tm, D_MODEL), lambda i: (i, 0))
    vec = pl.BlockSpec((1, D_MODEL), lambda i: (0, 0))
    return pl.pallas_call(
        _residual_ln_kernel,
        out_shape=jax.ShapeDtypeStruct((T, D_MODEL), F32),
        grid=(T // tm,),
        in_specs=[row, row, vec, vec],
        out_specs=row,
        compiler_params=_cparams("parallel"),
        name="peer_residual_ln",
    )(y, h, g, b)


def _rope_tables(S):
    half = QK_ROPE // 2
    inv_freq = ROPE_THETA ** (-jnp.arange(half, dtype=F32) / half)
    ang = jnp.arange(S, dtype=jnp.int32).astype(F32)[:, None] * inv_freq[None, :]
    cos, sin = jnp.cos(ang), jnp.sin(ang)
    zeros = jnp.zeros((S, LANE - QK_ROPE), F32)
    lanes = (jnp.concatenate([cos, cos, zeros], axis=1), jnp.concatenate([-sin, sin, zeros], axis=1))
    return lanes, (cos.T, sin.T)


def _pack_input_weights(w_in, b_gates):
    wq, wk, wv, wcq, wckv, wkr, wg = jnp.split(
        w_in, (A_WIDTH, 2 * A_WIDTH, 3 * A_WIDTH, 3 * A_WIDTH + Q_LORA,
               3 * A_WIDTH + Q_LORA + KV_LORA, 3 * A_WIDTH + Q_LORA + KV_LORA + QK_ROPE), axis=1)
    pad = jnp.zeros((D_MODEL, PROJ_WIDTH - COL_KR - QK_ROPE), w_in.dtype)
    wq = wq * (A_HEAD_DIM ** -0.5)
    w_all = jnp.concatenate([wq, wk, wv, wg, wcq, wckv, wkr, pad], axis=1).astype(BF16)
    b_all = jnp.zeros((1, PROJ_WIDTH), F32).at[0, COL_GA:COL_CQ].set(b_gates)
    return w_all, b_all


def _pack_uq(w_uq):
    w = w_uq.reshape(Q_LORA, B_HEADS, QK_NOPE + QK_ROPE)
    w = jnp.pad(w, ((0, 0), (0, 0), (0, MLA_HEAD_PAD - QK_NOPE - QK_ROPE)))
    return w.reshape(Q_LORA, B_HEADS * MLA_HEAD_PAD).T.astype(BF16)


def kernel(x, w_in, b_gates, a_w_out, mla_q_norm, mla_w_uq, mla_kv_norm, mla_w_ukv, mla_w_out, w_out,
           ln1_g, ln1_b, peer_w_query, peer_sub_keys_1, peer_sub_keys_2, peer_expert_down,
           peer_expert_up, ln2_g, ln2_b):
    B, S, D = x.shape
    assert D == D_MODEL and w_in.shape[0] == DEPTH
    T = B * S
    (cos_t, sin_t), (cos_rt, sin_rt) = _rope_tables(S)
    h = x.reshape(T, D)
    for l in range(DEPTH):
        w_all, b_all = _pack_input_weights(w_in[l], b_gates[l])
        proj, qkv_s = _in_projection(h, w_all, b_all, B, S)

        (w1, d1), (w4, d4), (w16, d16) = A_PATTERNS
        assert (d1, d4, d16) == (1, 4, STREAMS)
        o1, l1 = _dilated_dense(proj, B, S, w1)
        o4, l4 = _dilated_streams(qkv_s, w4, d4)
        o16, l16 = _dilated_streams(qkv_s, w16, d16)
        ya = _combine_patterns(o1, l1, o4, l4, o16, l16, B, S)

        cqn, ckvn, krope = _latent_prep(proj, mla_q_norm[l][None], mla_kv_norm[l][None], cos_t, sin_t, S)
        qt = _q_up(cqn, _pack_uq(mla_w_uq[l]), cos_rt, sin_rt, S)
        w_ukv = mla_w_ukv[l].reshape(KV_LORA, B_HEADS, QK_NOPE + V_HEAD)
        wk = w_ukv[:, :, :QK_NOPE].reshape(KV_LORA, B_HEADS * QK_NOPE).astype(BF16)
        wv = jnp.pad(w_ukv[:, :, QK_NOPE:], ((0, 0), (0, 0), (0, VT_ROWS - V_HEAD)))
        wvt = wv.reshape(KV_LORA, B_HEADS * VT_ROWS).T.astype(BF16)
        ones_col = (jnp.arange(B_HEADS * VT_ROWS) % VT_ROWS == V_HEAD).astype(F32)[:, None]
        kn, vt = _kv_up(ckvn, wk, wvt, ones_col)
        yb = _mla_attention(qt, kn, krope, vt, B, S)

        u = _branch_mix(ya, yb, a_w_out[l].astype(BF16), mla_w_out[l].astype(BF16), proj)
        h1, h1b = _out_projection_ln(u, w_out[l].astype(BF16), h, ln1_g[l][None], ln1_b[l][None])

        qp = _matmul(h1b, peer_w_query[l].astype(BF16), "peer_query")
        i1, i2, g = _peer_route(qp, peer_sub_keys_1[l].astype(BF16), peer_sub_keys_2[l].astype(BF16))
        w3 = _gate_matrix(i1, i2, g)
        down_tiles = peer_expert_down[l].reshape(N_EXPERTS // EXPERT_TILE, EXPERT_TILE, D).transpose(0, 2, 1)
        yp = _peer_experts(h1b, down_tiles.astype(BF16), peer_expert_up[l].astype(BF16), w3)
        h = _residual_ln(yp, h1, ln2_g[l][None], ln2_b[l][None])
    return h.reshape(B, S, D)
```

```python
import functools
import math

import jax
import jax.numpy as jnp
from jax import lax
from jax.experimental import pallas as pl
from jax.experimental.pallas import tpu as pltpu

F32 = jnp.float32
BF16 = jnp.bfloat16

D_MODEL = 2048
A_HEADS = 16
A_HEAD_DIM = 128
A_PATTERNS = ((128, 1), (512, 4), (2048, 16))
A_BLOCK = 128
A_WIDTH = A_HEADS * A_HEAD_DIM
B_HEADS = 16
Q_LORA = 512
KV_LORA = 512
QK_NOPE = 128
QK_ROPE = 64
V_HEAD = 128
ROPE_THETA = 10000.0
N_KEYS = 128
PEER_HEADS = 8
PEER_QDIM = 256
PEER_TOPK = 16
N_EXPERTS = N_KEYS * N_KEYS
N_SLOTS = PEER_HEADS * PEER_TOPK
LN_EPS = 1e-5
RMS_EPS = 1e-6
DEPTH = 1
ALPHA = (2.0 * DEPTH) ** 0.25
NEG = -1e30

LANE = 128
MLA_HEAD_PAD = 256
VMEM_LIMIT = 56 * 1024 * 1024
ROWS_WIDE, ROWS_MEDIUM, ROWS_NARROW = 1024, 512, 256
COL_TILE = 1024

COL_Q, COL_K, COL_V = 0, A_WIDTH, 2 * A_WIDTH
COL_GA = 3 * A_WIDTH
COL_GB = COL_GA + D_MODEL
COL_CQ = COL_GB + D_MODEL
COL_CKV = COL_CQ + Q_LORA
COL_KR = COL_CKV + KV_LORA
PROJ_WIDTH = COL_CQ + 2048

_NT = (((1,), (1,)), ((), ()))


def _cparams(*sem):
    return pltpu.CompilerParams(dimension_semantics=sem, vmem_limit_bytes=VMEM_LIMIT)


def _layer_norm(z, g, b):
    mu = jnp.mean(z, axis=-1, keepdims=True)
    zc = z - mu
    var = jnp.mean(zc * zc, axis=-1, keepdims=True)
    return zc * lax.rsqrt(var + LN_EPS) * g + b


STREAMS = max(d for _, d in A_PATTERNS)
PERM_ROWS = 16 * STREAMS


MID_DILATION = A_PATTERNS[1][1]


def _stream_of_residue(r):
    return (r % MID_DILATION) * (STREAMS // MID_DILATION) + r // MID_DILATION


def _stream_permutation():
    tok = jnp.arange(PERM_ROWS, dtype=jnp.int32)
    out_row = _stream_of_residue(tok % STREAMS) * (PERM_ROWS // STREAMS) + tok // STREAMS
    return (out_row[None, :] == jnp.arange(PERM_ROWS, dtype=jnp.int32)[:, None]).astype(BF16)


def _inproj_kernel(x_ref, w_ref, b_ref, perm_ref, o_ref, qs_ref, *, qkv_hi, gate_lo, gate_hi):
    j = pl.program_id(1)
    acc = jnp.dot(x_ref[...].astype(BF16), w_ref[...], preferred_element_type=F32)
    is_gate = jnp.logical_and(j >= gate_lo, j < gate_hi)

    @pl.when(is_gate)
    def _():
        o_ref[...] = jax.nn.sigmoid(acc + b_ref[...]).astype(o_ref.dtype)

    @pl.when(jnp.logical_not(is_gate))
    def _():
        o_ref[...] = acc.astype(o_ref.dtype)

    @pl.when(j < qkv_hi)
    def _():
        rows = acc.astype(BF16)
        per = PERM_ROWS // STREAMS
        for g in range(rows.shape[0] // PERM_ROWS):
            grouped = jnp.dot(perm_ref[...], rows[g * PERM_ROWS:(g + 1) * PERM_ROWS],
                              preferred_element_type=F32).astype(qs_ref.dtype)
            for s in range(STREAMS):
                qs_ref[s, g * per:(g + 1) * per, :] = grouped[s * per:(s + 1) * per]


def _in_projection(x2, w_all, b_all, B, S):
    T = x2.shape[0]
    tm, tn = min(ROWS_WIDE, S), COL_TILE
    nt = S // tm
    qkv_hi = COL_GA // tn
    kern = functools.partial(_inproj_kernel, qkv_hi=qkv_hi, gate_lo=COL_GA // tn, gate_hi=COL_CQ // tn)
    return pl.pallas_call(
        kern,
        out_shape=(
            jax.ShapeDtypeStruct((T, PROJ_WIDTH), BF16),
            jax.ShapeDtypeStruct((B, STREAMS, S // STREAMS, 3 * A_WIDTH), BF16),
        ),
        grid=(T // tm, PROJ_WIDTH // tn),
        in_specs=[
            pl.BlockSpec((tm, D_MODEL), lambda i, j: (i, 0)),
            pl.BlockSpec((D_MODEL, tn), lambda i, j: (0, j)),
            pl.BlockSpec((1, tn), lambda i, j: (0, j)),
            pl.BlockSpec((PERM_ROWS, PERM_ROWS), lambda i, j: (0, 0)),
        ],
        out_specs=(
            pl.BlockSpec((tm, tn), lambda i, j: (i, j)),
            pl.BlockSpec((None, STREAMS, tm // STREAMS, tn),
                         lambda i, j: (i // nt, 0, i % nt, jnp.minimum(j, qkv_hi - 1))),
        ),
        compiler_params=_cparams("parallel", "arbitrary"),
        name="in_projection",
    )(x2, w_all, b_all, _stream_permutation())


def _block_pos(idx, groups):
    if groups == 1:
        return idx
    per = A_BLOCK // groups
    return groups * (idx % per) + idx // per


def _dilated_bias(dilation, steps, groups):
    blk = A_BLOCK
    i = jnp.arange(blk, dtype=jnp.int32)[:, None]
    c = jnp.arange(2 * blk, dtype=jnp.int32)[None, :]
    rel = blk + _block_pos(i, groups) - (blk * (c // blk) + _block_pos(c % blk, groups))
    band = (rel >= 0) & (rel <= steps)
    slopes = jnp.asarray([2.0 ** (-8.0 * (h + 1) / A_HEADS) for h in range(A_HEADS)], F32)
    bias = -slopes[:, None, None] * (dilation * rel).astype(F32)[None]
    first = jnp.where(band & (c >= blk), bias, NEG)
    later = jnp.where(band, bias, NEG)
    return jnp.stack([first, later], axis=0)


DILATED_SUB = 2


def _dilated_kernel(q_ref, kp_ref, ko_ref, vp_ref, vo_ref, bias_ref, o_ref, lse_ref):
    n = pl.program_id(2)
    blk = A_BLOCK
    per = kp_ref.shape[-2]
    lane = lax.broadcasted_iota(jnp.int32, (blk, LANE), 1)

    def rows(ref, u, sl):
        return ref[..., u * per:(u + 1) * per, sl].reshape(blk, A_HEAD_DIM)

    for u in range(DILATED_SUB):
        table = jnp.minimum(n, 1) if u == 0 else 1
        lse_all = jnp.zeros((blk, LANE), F32)
        for h in range(A_HEADS):
            sl = slice(h * A_HEAD_DIM, (h + 1) * A_HEAD_DIM)
            q = rows(q_ref, u, sl)
            k_prev = rows(kp_ref, 0, sl) if u == 0 else rows(ko_ref, u - 1, sl)
            v_prev = rows(vp_ref, 0, sl) if u == 0 else rows(vo_ref, u - 1, sl)
            k = jnp.concatenate([k_prev, rows(ko_ref, u, sl)], axis=0)
            v = jnp.concatenate([v_prev, rows(vo_ref, u, sl)], axis=0)
            logits = lax.dot_general(q, k, _NT, preferred_element_type=F32) + bias_ref[table, h]
            m = jnp.max(logits, axis=-1, keepdims=True)
            p = jnp.exp(logits - m)
            z = jnp.sum(p, axis=-1, keepdims=True)
            o = jnp.dot(p.astype(BF16), v, preferred_element_type=F32) / z
            o_ref[..., u * per:(u + 1) * per, sl] = o.astype(o_ref.dtype).reshape(kp_ref.shape[:-1] + (A_HEAD_DIM,))
            lse_all = jnp.where(lane == h, m + jnp.log(z), lse_all)
        lse_ref[..., u * per:(u + 1) * per, :] = lse_all.reshape(kp_ref.shape[:-1] + (LANE,))


def _dilated_dense(proj, B, S, window):
    T = B * S
    sub = DILATED_SUB
    nb = S // (A_BLOCK * sub)
    prev, own = (A_BLOCK, A_WIDTH), (A_BLOCK * sub, A_WIDTH)
    before = lambda b, n: b * nb * sub + jnp.maximum(n * sub - 1, 0)
    return pl.pallas_call(
        _dilated_kernel,
        out_shape=(jax.ShapeDtypeStruct((T, A_WIDTH), BF16), jax.ShapeDtypeStruct((T, LANE), F32)),
        grid=(B, 1, nb),
        in_specs=[
            pl.BlockSpec(own, lambda b, r, n: (b * nb + n, COL_Q // A_WIDTH)),
            pl.BlockSpec(prev, lambda b, r, n: (before(b, n), COL_K // A_WIDTH)),
            pl.BlockSpec(own, lambda b, r, n: (b * nb + n, COL_K // A_WIDTH)),
            pl.BlockSpec(prev, lambda b, r, n: (before(b, n), COL_V // A_WIDTH)),
            pl.BlockSpec(own, lambda b, r, n: (b * nb + n, COL_V // A_WIDTH)),
            pl.BlockSpec((2, A_HEADS, A_BLOCK, 2 * A_BLOCK), lambda b, r, n: (0, 0, 0, 0)),
        ],
        out_specs=(
            pl.BlockSpec(own, lambda b, r, n: (b * nb + n, 0)),
            pl.BlockSpec((A_BLOCK * sub, LANE), lambda b, r, n: (b * nb + n, 0)),
        ),
        compiler_params=_cparams("parallel", "parallel", "arbitrary"),
        name="dilated_attention_d1",
    )(proj, proj, proj, proj, proj, _dilated_bias(1, window, 1))


def _dilated_streams(qkv_s, window, dilation):
    B, ns, Ls, _ = qkv_s.shape
    groups = STREAMS // dilation
    per = A_BLOCK // groups
    sub = DILATED_SUB
    assert ns == STREAMS and Ls % (per * sub) == 0
    prev, own = (None, groups, per, A_WIDTH), (None, groups, per * sub, A_WIDTH)
    before = lambda n: jnp.maximum(n * sub - 1, 0)
    return pl.pallas_call(
        _dilated_kernel,
        out_shape=(
            jax.ShapeDtypeStruct((B, STREAMS, Ls, A_WIDTH), BF16),
            jax.ShapeDtypeStruct((B, STREAMS, Ls, LANE), F32),
        ),
        grid=(B, STREAMS // groups, Ls // (per * sub)),
        in_specs=[
            pl.BlockSpec(own, lambda b, r, n: (b, r, n, 0)),
            pl.BlockSpec(prev, lambda b, r, n: (b, r, before(n), 1)),
            pl.BlockSpec(own, lambda b, r, n: (b, r, n, 1)),
            pl.BlockSpec(prev, lambda b, r, n: (b, r, before(n), 2)),
            pl.BlockSpec(own, lambda b, r, n: (b, r, n, 2)),
            pl.BlockSpec((2, A_HEADS, A_BLOCK, 2 * A_BLOCK), lambda b, r, n: (0, 0, 0, 0)),
        ],
        out_specs=(
            pl.BlockSpec(own, lambda b, r, n: (b, r, n, 0)),
            pl.BlockSpec((None, groups, per * sub, LANE), lambda b, r, n: (b, r, n, 0)),
        ),
        compiler_params=_cparams("parallel", "parallel", "arbitrary"),
        name=f"dilated_attention_d{dilation}",
    )(qkv_s, qkv_s, qkv_s, qkv_s, qkv_s, _dilated_bias(dilation, window // dilation, groups))


COMBINE_STEPS = 16


def _combine_kernel(o1_ref, o2_ref, o3_ref, l1_ref, l2_ref, l3_ref, y_ref, ob_ref, oc_ref, lb_ref, lc_ref):
    for r in range(STREAMS):
        s = _stream_of_residue(r)
        tok = pl.ds(r, COMBINE_STEPS, stride=STREAMS)
        lb_ref[tok, :] = l2_ref[s]
        lc_ref[tok, :] = l3_ref[s]
        for h in range(A_HEADS):
            sl = slice(h * A_HEAD_DIM, (h + 1) * A_HEAD_DIM)
            ob_ref[h, tok, :] = o2_ref[s, :, sl].astype(F32)
            oc_ref[h, tok, :] = o3_ref[s, :, sl].astype(F32)
    a, b, c = l1_ref[...], lb_ref[...], lc_ref[...]
    m = jnp.maximum(jnp.maximum(a, b), c)
    ea, eb, ec = jnp.exp(a - m), jnp.exp(b - m), jnp.exp(c - m)
    inv = 1.0 / (ea + eb + ec)
    wa, wb, wc = ea * inv, eb * inv, ec * inv
    for h in range(A_HEADS):
        sl = slice(h * A_HEAD_DIM, (h + 1) * A_HEAD_DIM)
        y = (wa[:, h:h + 1] * o1_ref[:, sl].astype(F32)
             + wb[:, h:h + 1] * ob_ref[h]
             + wc[:, h:h + 1] * oc_ref[h])
        y_ref[:, sl] = y.astype(y_ref.dtype)


def _combine_patterns(o1, l1, o4, l4, o16, l16, B, S):
    T = B * S
    tm = COMBINE_STEPS * STREAMS
    nt = S // tm
    tok_o = pl.BlockSpec((tm, A_WIDTH), lambda b, i: (b * nt + i, 0))
    tok_l = pl.BlockSpec((tm, LANE), lambda b, i: (b * nt + i, 0))
    str_o = pl.BlockSpec((None, STREAMS, COMBINE_STEPS, A_WIDTH), lambda b, i: (b, 0, i, 0))
    str_l = pl.BlockSpec((None, STREAMS, COMBINE_STEPS, LANE), lambda b, i: (b, 0, i, 0))
    return pl.pallas_call(
        _combine_kernel,
        out_shape=jax.ShapeDtypeStruct((T, A_WIDTH), BF16),
        grid=(B, nt),
        in_specs=[tok_o, str_o, str_o, tok_l, str_l, str_l],
        out_specs=tok_o,
        scratch_shapes=[
            pltpu.VMEM((A_HEADS, tm, A_HEAD_DIM), F32),
            pltpu.VMEM((A_HEADS, tm, A_HEAD_DIM), F32),
            pltpu.VMEM((tm, LANE), F32),
            pltpu.VMEM((tm, LANE), F32),
        ],
        compiler_params=_cparams("parallel", "parallel"),
        name="combine_patterns",
    )(o1, o4, o16, l1, l4, l16)


def _rope_lanes(t, cos, sin):
    lane = lax.broadcasted_iota(jnp.int32, t.shape, 1)
    half = QK_ROPE // 2
    rot = jnp.where(lane < half, pltpu.roll(t, LANE - half, 1), pltpu.roll(t, half, 1))
    return t * cos + rot * sin


def _rms_norm(x, g):
    ms = jnp.mean(x * x, axis=-1, keepdims=True)
    return x * lax.rsqrt(ms + RMS_EPS) * g


def _latent_kernel(cq_ref, ckv_ref, kr_ref, gq_ref, gkv_ref, cos_ref, sin_ref, cqn_ref, ckvn_ref, krope_ref):
    cqn_ref[...] = _rms_norm(cq_ref[...].astype(F32), gq_ref[...]).astype(cqn_ref.dtype)
    ckvn_ref[...] = _rms_norm(ckv_ref[...].astype(F32), gkv_ref[...]).astype(ckvn_ref.dtype)
    krope_ref[...] = _rope_lanes(kr_ref[...].astype(F32), cos_ref[...], sin_ref[...]).astype(krope_ref.dtype)


def _latent_prep(proj, gq, gkv, cos_t, sin_t, S):
    T = proj.shape[0]
    tm = min(ROWS_MEDIUM, S)
    ns = S // tm
    return pl.pallas_call(
        _latent_kernel,
        out_shape=(
            jax.ShapeDtypeStruct((T, Q_LORA), BF16),
            jax.ShapeDtypeStruct((T, KV_LORA), BF16),
            jax.ShapeDtypeStruct((T, LANE), BF16),
        ),
        grid=(T // tm,),
        in_specs=[
            pl.BlockSpec((tm, Q_LORA), lambda i: (i, COL_CQ // Q_LORA)),
            pl.BlockSpec((tm, KV_LORA), lambda i: (i, COL_CKV // KV_LORA)),
            pl.BlockSpec((tm, LANE), lambda i: (i, COL_KR // LANE)),
            pl.BlockSpec((1, Q_LORA), lambda i: (0, 0)),
            pl.BlockSpec((1, KV_LORA), lambda i: (0, 0)),
            pl.BlockSpec((tm, LANE), lambda i: (i % ns, 0)),
            pl.BlockSpec((tm, LANE), lambda i: (i % ns, 0)),
        ],
        out_specs=(
            pl.BlockSpec((tm, Q_LORA), lambda i: (i, 0)),
            pl.BlockSpec((tm, KV_LORA), lambda i: (i, 0)),
            pl.BlockSpec((tm, LANE), lambda i: (i, 0)),
        ),
        compiler_params=_cparams("parallel"),
        name="latent_prep",
    )(proj, proj, proj, gq, gkv, cos_t, sin_t)


def _qup_kernel(c_ref, wt_ref, cos_ref, sin_ref, o_ref, *, scale):
    acc = lax.dot_general(wt_ref[...], c_ref[...], _NT, preferred_element_type=F32) * scale
    cos, sin = cos_ref[...], sin_ref[...]
    half = QK_ROPE // 2
    for hb in range(acc.shape[0] // MLA_HEAD_PAD):
        lo = hb * MLA_HEAD_PAD
        r1 = acc[lo + QK_NOPE:lo + QK_NOPE + half]
        r2 = acc[lo + QK_NOPE + half:lo + QK_NOPE + QK_ROPE]
        o_ref[lo:lo + QK_NOPE] = acc[lo:lo + QK_NOPE].astype(o_ref.dtype)
        o_ref[lo + QK_NOPE:lo + QK_NOPE + half] = (r1 * cos - r2 * sin).astype(o_ref.dtype)
        o_ref[lo + QK_NOPE + half:lo + QK_NOPE + QK_ROPE] = (r2 * cos + r1 * sin).astype(o_ref.dtype)
        o_ref[lo + QK_NOPE + QK_ROPE:lo + MLA_HEAD_PAD] = acc[lo + QK_NOPE + QK_ROPE:lo + MLA_HEAD_PAD].astype(o_ref.dtype)


def _q_up(cqn, w_uq_pt, cos_rt, sin_rt, S):
    T = cqn.shape[0]
    N = w_uq_pt.shape[0]
    tm, tn = min(ROWS_WIDE, S), COL_TILE
    ns = S // tm
    half = QK_ROPE // 2
    kern = functools.partial(_qup_kernel, scale=(QK_NOPE + QK_ROPE) ** -0.5 * math.log2(math.e))
    return pl.pallas_call(
        kern,
        out_shape=jax.ShapeDtypeStruct((N, T), BF16),
        grid=(T // tm, N // tn),
        in_specs=[
            pl.BlockSpec((tm, Q_LORA), lambda i, j: (i, 0)),
            pl.BlockSpec((tn, Q_LORA), lambda i, j: (j, 0)),
            pl.BlockSpec((half, tm), lambda i, j: (0, i % ns)),
            pl.BlockSpec((half, tm), lambda i, j: (0, i % ns)),
        ],
        out_specs=pl.BlockSpec((tn, tm), lambda i, j: (j, i)),
        compiler_params=_cparams("parallel", "arbitrary"),
        name="mla_q_up",
    )(cqn, w_uq_pt, cos_rt, sin_rt)


def _mm_kernel(a_ref, w_ref, o_ref):
    o_ref[...] = jnp.dot(a_ref[...], w_ref[...], preferred_element_type=F32).astype(o_ref.dtype)


def _matmul(a, w, name, tm=ROWS_WIDE, tn=COL_TILE):
    M, K = a.shape
    N = w.shape[1]
    tm, tn = min(tm, M), min(tn, N)
    return pl.pallas_call(
        _mm_kernel,
        out_shape=jax.ShapeDtypeStruct((M, N), BF16),
        grid=(M // tm, N // tn),
        in_specs=[
            pl.BlockSpec((tm, K), lambda i, j: (i, 0)),
            pl.BlockSpec((K, tn), lambda i, j: (0, j)),
        ],
        out_specs=pl.BlockSpec((tm, tn), lambda i, j: (i, j)),
        compiler_params=_cparams("parallel", "arbitrary"),
        name=name,
    )(a, w)


VT_ROWS = V_HEAD + 16


def _kvup_kernel(c_ref, wk_ref, wvt_ref, ones_ref, kn_ref, vt_ref):
    c = c_ref[...]
    kn_ref[...] = jnp.dot(c, wk_ref[...], preferred_element_type=F32).astype(kn_ref.dtype)
    vt = lax.dot_general(wvt_ref[...], c, _NT, preferred_element_type=F32) + ones_ref[...]
    vt_ref[...] = vt.astype(vt_ref.dtype)


def _kv_up(ckvn, wk, wvt, ones_col):
    T = ckvn.shape[0]
    tm = min(ROWS_WIDE, T)
    n = B_HEADS * QK_NOPE
    nv = B_HEADS * VT_ROWS
    return pl.pallas_call(
        _kvup_kernel,
        out_shape=(jax.ShapeDtypeStruct((T, n), BF16), jax.ShapeDtypeStruct((nv, T), BF16)),
        grid=(T // tm,),
        in_specs=[
            pl.BlockSpec((tm, KV_LORA), lambda i: (i, 0)),
            pl.BlockSpec((KV_LORA, n), lambda i: (0, 0)),
            pl.BlockSpec((nv, KV_LORA), lambda i: (0, 0)),
            pl.BlockSpec((nv, 1), lambda i: (0, 0)),
        ],
        out_specs=(pl.BlockSpec((tm, n), lambda i: (i, 0)), pl.BlockSpec((nv, tm), lambda i: (0, i))),
        compiler_params=_cparams("parallel"),
        name="mla_kv_up",
    )(ckvn, wk, wvt, ones_col)


def _mla_kernel(q_ref, kn_ref, kr_ref, vt_ref, o_ref, sa_ref, sb_ref, xa_ref, xb_ref, m_ref, acc_ref, *, tq):
    qi = pl.program_id(2)
    tk = tq // 2
    q = q_ref[...]
    m_ref[...] = jnp.full(m_ref.shape, NEG, F32)
    acc_ref[...] = jnp.zeros(acc_ref.shape, F32)

    def scores(c, s_ref, x_ref):
        start = pl.multiple_of(c * tk, tk)
        k = jnp.concatenate([kn_ref[pl.ds(start, tk), :], kr_ref[pl.ds(start, tk), :]], axis=1)
        st = jnp.dot(k, q, preferred_element_type=F32)
        s_ref[...] = st
        x_ref[...] = jnp.max(st, axis=0, keepdims=True)

    def update(c, s_ref, x_ref, masked):
        start = pl.multiple_of(c * tk, tk)
        st = s_ref[...]
        if masked:
            key = lax.broadcasted_iota(jnp.int32, st.shape, 0) + (c * tk - qi * tq)
            qry = lax.broadcasted_iota(jnp.int32, st.shape, 1)
            st = jnp.where(key <= qry, st, NEG)
            cmax = jnp.max(st, axis=0, keepdims=True)
        else:
            cmax = x_ref[...]
        m_prev = m_ref[...]
        m_new = jnp.maximum(m_prev, cmax)
        a = jnp.exp2(m_prev - m_new)
        p = jnp.exp2(st - m_new)
        pv = jnp.dot(vt_ref[:, pl.ds(start, tk)], p.astype(BF16), preferred_element_type=F32)
        acc_ref[...] = a * acc_ref[...] + pv
        m_ref[...] = m_new

    scores(0, sa_ref, xa_ref)

    def pair(i):
        c = 2 * i
        scores(c + 1, sb_ref, xb_ref)
        update(c, sa_ref, xa_ref, False)
        scores(c + 2, sa_ref, xa_ref)
        update(c + 1, sb_ref, xb_ref, False)

    def two_pairs(i, carry):
        pair(2 * i)
        pair(2 * i + 1)
        return carry

    lax.fori_loop(0, qi // 2, two_pairs, 0)

    @pl.when(qi % 2 == 1)
    def _():
        pair(qi - 1)

    c = 2 * qi
    late = pl.ds(tk, tq - tk)
    start = pl.multiple_of((c + 1) * tk, tk)
    k = jnp.concatenate([kn_ref[pl.ds(start, tk), :], kr_ref[pl.ds(start, tk), :]], axis=1)
    sb_ref[:, late] = jnp.dot(k, q_ref[:, late], preferred_element_type=F32)
    update(c, sa_ref, xa_ref, True)
    st = sb_ref[:, late]
    key = lax.broadcasted_iota(jnp.int32, st.shape, 0)
    qry = lax.broadcasted_iota(jnp.int32, st.shape, 1)
    st = jnp.where(key <= qry, st, NEG)
    m_prev = m_ref[:, late]
    m_new = jnp.maximum(m_prev, jnp.max(st, axis=0, keepdims=True))
    a = jnp.exp2(m_prev - m_new)
    p = jnp.exp2(st - m_new)
    pv = jnp.dot(vt_ref[:, pl.ds(start, tk)], p.astype(BF16), preferred_element_type=F32)
    acc_ref[:, late] = a * acc_ref[:, late] + pv
    o_ref[...] = (acc_ref[:V_HEAD, :] / acc_ref[V_HEAD:V_HEAD + 1, :]).T.astype(o_ref.dtype)


def _mla_attention(qt, kn, krope, vt, B, S):
    T = qt.shape[1]
    tq = min(ROWS_WIDE, S)
    nq = S // tq
    kern = functools.partial(_mla_kernel, tq=tq)
    return pl.pallas_call(
        kern,
        out_shape=jax.ShapeDtypeStruct((T, B_HEADS * V_HEAD), BF16),
        grid=(B, B_HEADS, nq),
        in_specs=[
            pl.BlockSpec((MLA_HEAD_PAD, tq), lambda b, h, i: (h, b * nq + i)),
            pl.BlockSpec((S, QK_NOPE), lambda b, h, i: (b, h), pipeline_mode=pl.Buffered(1)),
            pl.BlockSpec((S, LANE), lambda b, h, i: (b, 0), pipeline_mode=pl.Buffered(1)),
            pl.BlockSpec((VT_ROWS, S), lambda b, h, i: (h, b), pipeline_mode=pl.Buffered(1)),
        ],
        out_specs=pl.BlockSpec((tq, V_HEAD), lambda b, h, i: (b * nq + i, h)),
        scratch_shapes=(
            [pltpu.VMEM((tq // 2, tq), F32)] * 2
            + [pltpu.VMEM((1, tq), F32)] * 3
            + [pltpu.VMEM((VT_ROWS, tq), F32)]
        ),
        compiler_params=_cparams("parallel", "parallel", "arbitrary"),
        name="mla_attention",
    )(qt, kn, krope, vt)


def _branch_kernel(ya_ref, yb_ref, wa_ref, wb_ref, ga_ref, gb_ref, o_ref):
    pa = jnp.dot(ya_ref[...], wa_ref[...], preferred_element_type=F32)
    pb = jnp.dot(yb_ref[...], wb_ref[...], preferred_element_type=F32)
    u = ga_ref[...].astype(F32) * pa + gb_ref[...].astype(F32) * pb
    o_ref[...] = u.astype(o_ref.dtype)


def _branch_mix(ya, yb, wa, wb, proj):
    T = ya.shape[0]
    tm, tn = min(ROWS_MEDIUM, T), COL_TILE
    return pl.pallas_call(
        _branch_kernel,
        out_shape=jax.ShapeDtypeStruct((T, D_MODEL), BF16),
        grid=(T // tm, D_MODEL // tn),
        in_specs=[
            pl.BlockSpec((tm, A_WIDTH), lambda i, j: (i, 0)),
            pl.BlockSpec((tm, B_HEADS * V_HEAD), lambda i, j: (i, 0)),
            pl.BlockSpec((A_WIDTH, tn), lambda i, j: (0, j)),
            pl.BlockSpec((B_HEADS * V_HEAD, tn), lambda i, j: (0, j)),
            pl.BlockSpec((tm, tn), lambda i, j: (i, COL_GA // tn + j)),
            pl.BlockSpec((tm, tn), lambda i, j: (i, COL_GB // tn + j)),
        ],
        out_specs=pl.BlockSpec((tm, tn), lambda i, j: (i, j)),
        compiler_params=_cparams("parallel", "arbitrary"),
        name="branch_mix",
    )(ya, yb, wa, wb, proj, proj)


def _outln_kernel(u_ref, w_ref, x_ref, g_ref, b_ref, h_ref, hb_ref):
    mix = jnp.dot(u_ref[...], w_ref[...], preferred_element_type=F32)
    h = _layer_norm(ALPHA * x_ref[...] + mix, g_ref[...], b_ref[...])
    h_ref[...] = h
    hb_ref[...] = h.astype(hb_ref.dtype)


def _out_projection_ln(u, w_out, x2, g, b):
    T = u.shape[0]
    tm = min(ROWS_MEDIUM, T)
    row = pl.BlockSpec((tm, D_MODEL), lambda i: (i, 0))
    vec = pl.BlockSpec((1, D_MODEL), lambda i: (0, 0))
    return pl.pallas_call(
        _outln_kernel,
        out_shape=(
            jax.ShapeDtypeStruct((T, D_MODEL), F32),
            jax.ShapeDtypeStruct((T, D_MODEL), BF16),
        ),
        grid=(T // tm,),
        in_specs=[row, pl.BlockSpec((D_MODEL, D_MODEL), lambda i: (0, 0)), row, vec, vec],
        out_specs=(row, row),
        compiler_params=_cparams("parallel"),
        name="out_projection_ln",
    )(u, w_out, x2, g, b)


def _topk_axis0(s, ids, k):
    big = jnp.int32(2 ** 30)
    vals, idxs = [], []
    for _ in range(k):
        m = jnp.max(s, axis=0, keepdims=True)
        idx = jnp.min(jnp.where(s == m, ids, big), axis=0, keepdims=True)
        vals.append(m)
        idxs.append(idx)
        s = jnp.where(ids == idx, -jnp.inf, s)
    return vals, idxs


def _sorting_network(n):
    def merge(lo, hi, r):
        step = r * 2
        if step < hi - lo:
            yield from merge(lo, hi, step)
            yield from merge(lo + r, hi, step)
            yield from ((i, i + r) for i in range(lo + r, hi - r, step))
        else:
            yield (lo, lo + r)

    def sort(lo, hi):
        if hi - lo >= 1:
            mid = lo + (hi - lo) // 2
            yield from sort(lo, mid)
            yield from sort(mid + 1, hi)
            yield from merge(lo, hi, 1)

    return tuple(sort(0, n - 1))


def _topk_keys(s, k):
    sub, nrow = 8, s.shape[0] // 8
    assert nrow == k
    lane_id = lax.broadcasted_iota(jnp.int32, (sub, s.shape[1]), 0)
    val = [s[sub * v:sub * (v + 1)] for v in range(nrow)]
    idx = [lane_id + sub * v for v in range(nrow)]
    for i, j in _sorting_network(nrow):
        swap = (val[j] > val[i]) | ((val[j] == val[i]) & (idx[j] < idx[i]))
        val[i], val[j] = jnp.where(swap, val[j], val[i]), jnp.where(swap, val[i], val[j])
        idx[i], idx[j] = jnp.where(swap, idx[j], idx[i]), jnp.where(swap, idx[i], idx[j])
    big = jnp.int32(2 ** 30)
    vals, idxs = [], []
    for t in range(k):
        m = jnp.max(val[0], axis=0, keepdims=True)
        win = jnp.min(jnp.where(val[0] == m, idx[0], big), axis=0, keepdims=True)
        vals.append(m)
        idxs.append(win)
        won = idx[0] == win
        for r in range(k - 1 - t):
            val[r] = jnp.where(won, val[r + 1], val[r])
            idx[r] = jnp.where(won, idx[r + 1], idx[r])
    return vals, idxs


def _select_rows(rows, sel):
    out = jnp.zeros(sel.shape, rows[0].dtype)
    for a, r in enumerate(rows):
        out = jnp.where(sel == a, r, out)
    return out


def _route_kernel(q_ref, k1_ref, k2_ref, i1_ref, i2_ref, g_ref):
    half = PEER_QDIM // 2
    k, tm = PEER_TOPK, q_ref.shape[0]
    sub = lax.broadcasted_iota(jnp.int32, (k // 2, tm), 0)
    cand_pos = jnp.concatenate([a * k + sub for a in range(k // 2)] + [k // 2 + sub, (k // 2 + sub) * k], axis=0)
    i1_all, i2_all, g_all = [], [], []
    for h in range(PEER_HEADS):
        q1 = q_ref[:, h * PEER_QDIM:h * PEER_QDIM + half]
        q2 = q_ref[:, h * PEER_QDIM + half:(h + 1) * PEER_QDIM]
        s1 = lax.dot_general(k1_ref[...], q1, _NT, preferred_element_type=F32)
        s2 = lax.dot_general(k2_ref[...], q2, _NT, preferred_element_type=F32)
        v1, i1 = _topk_keys(s1, k)
        v2, i2 = _topk_keys(s2, k)
        v1m = jnp.concatenate(v1, axis=0)
        v2m = jnp.concatenate(v2, axis=0)
        cand = jnp.concatenate(
            [v1[a] + v2m[:k // 2] for a in range(k // 2)] + [v1[0] + v2m[k // 2:], v1m[k // 2:] + v2[0]], axis=0)
        ts, pos = _topk_axis0(cand, cand_pos, k)
        top = jnp.concatenate(ts, axis=0)
        e = jnp.exp(top - ts[0])
        g_all.append(e / jnp.sum(e, axis=0, keepdims=True))
        posm = jnp.concatenate(pos, axis=0)
        i1_all.append(_select_rows(i1, posm >> int(math.log2(k))))
        i2_all.append(_select_rows(i2, posm & (k - 1)))
    i1_ref[...] = jnp.concatenate(i1_all, axis=0).T
    i2_ref[...] = jnp.concatenate(i2_all, axis=0).T
    g_ref[...] = jnp.concatenate(g_all, axis=0).T


def _peer_route(qp, k1, k2):
    T = qp.shape[0]
    tm = min(ROWS_NARROW, T)
    slot = pl.BlockSpec((tm, N_SLOTS), lambda i: (i, 0))
    keys = pl.BlockSpec((N_KEYS, PEER_QDIM // 2), lambda i: (0, 0))
    return pl.pallas_call(
        _route_kernel,
        out_shape=(
            jax.ShapeDtypeStruct((T, N_SLOTS), jnp.int32),
            jax.ShapeDtypeStruct((T, N_SLOTS), jnp.int32),
            jax.ShapeDtypeStruct((T, N_SLOTS), F32),
        ),
        grid=(T // tm,),
        in_specs=[pl.BlockSpec((tm, PEER_HEADS * PEER_QDIM), lambda i: (i, 0)), keys, keys],
        out_specs=(slot, slot, slot),
        compiler_params=_cparams("parallel"),
        name="peer_route",
    )(qp, k1, k2)


GATE_GROUP = 64


def _gate_matrix_kernel(i1_ref, i2_ref, g_ref, w_ref):
    key = lax.broadcasted_iota(jnp.int32, (N_KEYS, N_SLOTS), 0)

    def body(tg, carry):
        t0 = pl.multiple_of(tg * GATE_GROUP, GATE_GROUP)
        per_token = []
        for u in range(GATE_GROUP):
            r1 = i1_ref[pl.ds(t0 + u, 1), :]
            r2 = i2_ref[pl.ds(t0 + u, 1), :]
            g = g_ref[pl.ds(t0 + u, 1), :]
            a = jnp.where(key == r1, 1.0, 0.0).astype(BF16)
            b = jnp.where(key == r2, g, 0.0).astype(BF16)
            per_token.append(lax.dot_general(a, b, _NT, preferred_element_type=F32))
        w = pltpu.einshape("tid->itd", jnp.stack(per_token, axis=0))
        w_ref[:, pl.ds(t0, GATE_GROUP), :] = w.astype(w_ref.dtype)
        return carry

    lax.fori_loop(0, w_ref.shape[1] // GATE_GROUP, body, 0)


def _gate_matrix(i1, i2, g):
    T = i1.shape[0]
    tb = min(ROWS_NARROW, T)
    slot = pl.BlockSpec((tb, N_SLOTS), lambda i: (i, 0))
    return pl.pallas_call(
        _gate_matrix_kernel,
        out_shape=jax.ShapeDtypeStruct((N_KEYS, T, N_KEYS), BF16),
        grid=(T // tb,),
        in_specs=[slot, slot, slot],
        out_specs=pl.BlockSpec((N_KEYS, tb, N_KEYS), lambda i: (0, i, 0)),
        compiler_params=_cparams("parallel"),
        name="peer_gate_matrix",
    )(i1, i2, g)


EXPERT_TILE = COL_TILE


def _experts_kernel(hb_ref, dn_ref, up_ref, w_ref, y_ref, acc_ref):
    j = pl.program_id(1)

    @pl.when(j == 0)
    def _():
        acc_ref[...] = jnp.zeros(acc_ref.shape, F32)

    pre = jnp.dot(hb_ref[...], dn_ref[...], preferred_element_type=F32)
    act = 0.5 * pre * (1.0 + lax.erf(pre * (2.0 ** -0.5)))
    gates = jnp.concatenate([w_ref[m] for m in range(w_ref.shape[0])], axis=1)
    act = act * gates.astype(F32)
    acc_ref[...] += jnp.dot(act.astype(BF16), up_ref[...], preferred_element_type=F32)

    @pl.when(j == pl.num_programs(1) - 1)
    def _():
        y_ref[...] = acc_ref[...].astype(y_ref.dtype)


def _peer_experts(hb, down_tiles, up, w3):
    T = hb.shape[0]
    tm, te = min(ROWS_WIDE, T), EXPERT_TILE
    row = pl.BlockSpec((tm, D_MODEL), lambda i, j: (i, 0))
    return pl.pallas_call(
        _experts_kernel,
        out_shape=jax.ShapeDtypeStruct((T, D_MODEL), BF16),
        grid=(T // tm, N_EXPERTS // te),
        in_specs=[
            row,
            pl.BlockSpec((None, D_MODEL, te), lambda i, j: (j, 0, 0)),
            pl.BlockSpec((te, D_MODEL), lambda i, j: (j, 0)),
            pl.BlockSpec((te // N_KEYS, tm, N_KEYS), lambda i, j: (j, i, 0)),
        ],
        out_specs=row,
        scratch_shapes=[pltpu.VMEM((tm, D_MODEL), F32)],
        compiler_params=_cparams("parallel", "arbitrary"),
        name="peer_experts",
    )(hb, down_tiles, up, w3)


def _residual_ln_kernel(y_ref, h_ref, g_ref, b_ref, o_ref):
    o_ref[...] = _layer_norm(ALPHA * h_ref[...] + y_ref[...].astype(F32), g_ref[...], b_ref[...])


def _residual_ln(y, h, g, b):
    T = y.shape[0]
    tm = min(ROWS_MEDIUM, T)
    row = pl.BlockSpec((tm, D_MODEL), lambda i: (i, 0))
    vec = pl.BlockSpec((1, D_MODEL), lambda i: (0, 0))
    return pl.pallas_call(
        _residual_ln_kernel,
        out_shape=jax.ShapeDtypeStruct((T, D_MODEL), F32),
        grid=(T // tm,),
        in_specs=[row, row, vec, vec],
        out_specs=row,
        compiler_params=_cparams("parallel"),
        name="peer_residual_ln",
    )(y, h, g, b)


def _rope_tables(S):
    half = QK_ROPE // 2
    inv_freq = ROPE_THETA ** (-jnp.arange(half, dtype=F32) / half)
    ang = jnp.arange(S, dtype=jnp.int32).astype(F32)[:, None] * inv_freq[None, :]
    cos, sin = jnp.cos(ang), jnp.sin(ang)
    zeros = jnp.zeros((S, LANE - QK_ROPE), F32)
    lanes = (jnp.concatenate([cos, cos, zeros], axis=1), jnp.concatenate([-sin, sin, zeros], axis=1))
    return lanes, (cos.T, sin.T)


def _pack_input_weights(w_in, b_gates):
    wq, wk, wv, wcq, wckv, wkr, wg = jnp.split(
        w_in, (A_WIDTH, 2 * A_WIDTH, 3 * A_WIDTH, 3 * A_WIDTH + Q_LORA,
               3 * A_WIDTH + Q_LORA + KV_LORA, 3 * A_WIDTH + Q_LORA + KV_LORA + QK_ROPE), axis=1)
    pad = jnp.zeros((D_MODEL, PROJ_WIDTH - COL_KR - QK_ROPE), w_in.dtype)
    wq = wq * (A_HEAD_DIM ** -0.5)
    w_all = jnp.concatenate([wq, wk, wv, wg, wcq, wckv, wkr, pad], axis=1).astype(BF16)
    b_all = jnp.zeros((1, PROJ_WIDTH), F32).at[0, COL_GA:COL_CQ].set(b_gates)
    return w_all, b_all


def _pack_uq(w_uq):
    w = w_uq.reshape(Q_LORA, B_HEADS, QK_NOPE + QK_ROPE)
    w = jnp.pad(w, ((0, 0), (0, 0), (0, MLA_HEAD_PAD - QK_NOPE - QK_ROPE)))
    return w.reshape(Q_LORA, B_HEADS * MLA_HEAD_PAD).T.astype(BF16)


def kernel(x, w_in, b_gates, a_w_out, mla_q_norm, mla_w_uq, mla_kv_norm, mla_w_ukv, mla_w_out, w_out,
           ln1_g, ln1_b, peer_w_query, peer_sub_keys_1, peer_sub_keys_2, peer_expert_down,
           peer_expert_up, ln2_g, ln2_b):
    B, S, D = x.shape
    assert D == D_MODEL and w_in.shape[0] == DEPTH
    T = B * S
    (cos_t, sin_t), (cos_rt, sin_rt) = _rope_tables(S)
    h = x.reshape(T, D)
    for l in range(DEPTH):
        w_all, b_all = _pack_input_weights(w_in[l], b_gates[l])
        proj, qkv_s = _in_projection(h, w_all, b_all, B, S)

        (w1, d1), (w4, d4), (w16, d16) = A_PATTERNS
        assert (d1, d4, d16) == (1, 4, STREAMS)
        o1, l1 = _dilated_dense(proj, B, S, w1)
        o4, l4 = _dilated_streams(qkv_s, w4, d4)
        o16, l16 = _dilated_streams(qkv_s, w16, d16)
        ya = _combine_patterns(o1, l1, o4, l4, o16, l16, B, S)

        cqn, ckvn, krope = _latent_prep(proj, mla_q_norm[l][None], mla_kv_norm[l][None], cos_t, sin_t, S)
        qt = _q_up(cqn, _pack_uq(mla_w_uq[l]), cos_rt, sin_rt, S)
        w_ukv = mla_w_ukv[l].reshape(KV_LORA, B_HEADS, QK_NOPE + V_HEAD)
        wk = w_ukv[:, :, :QK_NOPE].reshape(KV_LORA, B_HEADS * QK_NOPE).astype(BF16)
        wv = jnp.pad(w_ukv[:, :, QK_NOPE:], ((0, 0), (0, 0), (0, VT_ROWS - V_HEAD)))
        wvt = wv.reshape(KV_LORA, B_HEADS * VT_ROWS).T.astype(BF16)
        ones_col = (jnp.arange(B_HEADS * VT_ROWS) % VT_ROWS == V_HEAD).astype(F32)[:, None]
        kn, vt = _kv_up(ckvn, wk, wvt, ones_col)
        yb = _mla_attention(qt, kn, krope, vt, B, S)

        u = _branch_mix(ya, yb, a_w_out[l].astype(BF16), mla_w_out[l].astype(BF16), proj)
        h1, h1b = _out_projection_ln(u, w_out[l].astype(BF16), h, ln1_g[l][None], ln1_b[l][None])

        qp = _matmul(h1b, peer_w_query[l].astype(BF16), "peer_query")
        i1, i2, g = _peer_route(qp, peer_sub_keys_1[l].astype(BF16), peer_sub_keys_2[l].astype(BF16))
        w3 = _gate_matrix(i1, i2, g)
        down_tiles = peer_expert_down[l].reshape(N_EXPERTS // EXPERT_TILE, EXPERT_TILE, D).transpose(0, 2, 1)
        yp = _peer_experts(h1b, down_tiles.astype(BF16), peer_expert_up[l].astype(BF16), w3)
        h = _residual_ln(yp, h1, ln2_g[l][None], ln2_b[l][None])
    return h.reshape(B, S, D)
```
